```python
import math
import jax, jax.numpy as jnp
from jax import lax
import numpy as np

D_MODEL = 1024
BATCH = 16
SEQ = 256
DEPTH = 2
DEC_BATCH = 2
DEC_SEQ = 4096
PAST_LEN = 256

GRID_W = 64
N_ATT = (DEPTH + 1) // 2
N_CONV = DEPTH // 2
H_DIFF = 4
DH_DIFF = 64
DIFF_QK = H_DIFF * 2 * DH_DIFF
DIFF_V = H_DIFF * 2 * DH_DIFF
H_MLA = 8
Q_LORA = 256
KV_LORA = 128
QK_NOPE = 64
QK_ROPE = 32
V_MLA = 64
ATT_IN = 2 * DIFF_QK + DIFF_V + Q_LORA + KV_LORA + QK_ROPE
ATT_SPLITS = [DIFF_QK, 2 * DIFF_QK, 2 * DIFF_QK + DIFF_V, 2 * DIFF_QK + DIFF_V + Q_LORA,
              2 * DIFF_QK + DIFF_V + Q_LORA + KV_LORA]
ATT_OUT = DIFF_V + H_MLA * V_MLA
CONV_CH = 512
CONV_WIDTH = 31
POOL_CH = 512
POOL_WINDOWS = (2, 4, 8, 16)
POOL_GROUPS = len(POOL_WINDOWS)
POOL_GC = POOL_CH // POOL_GROUPS
CONV_IN = 2 * CONV_CH + POOL_CH
CONV_OUT = CONV_CH + POOL_CH
D_FF = 4 * D_MODEL
ROPE_BASE = 10000.0
Q_BLOCK = 128
NORM_EPS = 1e-5
DEEPNORM_ALPHA = (2 * DEPTH) ** 0.25
DEEPNORM_BETA = (8 * DEPTH) ** -0.25

kernel_name = 'hybrid_diffattn_mla_conformer_pool_dit'


def layer_norm(x, g, b):
    xf = x.astype(jnp.float32)
    mu = jnp.mean(xf, axis=-1, keepdims=True)
    var = jnp.mean(jnp.square(xf - mu), axis=-1, keepdims=True)
    return ((xf - mu) * lax.rsqrt(var + NORM_EPS) * g.astype(jnp.float32) + b.astype(jnp.float32)).astype(x.dtype)


def rms_norm(x, g):
    xf = x.astype(jnp.float32)
    ms = jnp.mean(jnp.square(xf), axis=-1, keepdims=True)
    return (xf * lax.rsqrt(ms + NORM_EPS) * g.astype(jnp.float32)).astype(x.dtype)


def rope_1d(x, pos):
    half = x.shape[-1] // 2
    inv = ROPE_BASE ** (-jnp.arange(half, dtype=jnp.float32) / half)
    ang = pos.astype(jnp.float32)[:, None] * inv[None, :]
    ang = ang.reshape(ang.shape[0], *([1] * (x.ndim - 3)), half)
    cos, sin = jnp.cos(ang), jnp.sin(ang)
    xf = x.astype(jnp.float32)
    x1, x2 = xf[..., :half], xf[..., half:]
    return jnp.concatenate([x1 * cos - x2 * sin, x1 * sin + x2 * cos], axis=-1).astype(x.dtype)


def rope_2d(x, row, col):
    n = x.shape[-1] // 2
    return jnp.concatenate([rope_1d(x[..., :n], row), rope_1d(x[..., n:], col)], axis=-1)


def sweep_query_blocks(fn, qs):
    b, t = qs[0].shape[:2]
    qb = Q_BLOCK if t % Q_BLOCK == 0 else t
    nb = t // qb
    blocks = tuple(jnp.moveaxis(q.reshape(b, nb, qb, *q.shape[2:]), 1, 0) for q in qs)
    out = lax.map(lambda blk: fn(*blk), blocks)
    out = jnp.moveaxis(out, 0, 1)
    return out.reshape(b, t, *out.shape[3:])


def diff_attend(q_segs, k_segs, v, lam):
    scale = DH_DIFF ** -0.5

    def blk(*qs):
        s = jnp.concatenate([jnp.einsum('bqhmd,bkhmd->bhmqk', q, k) for q, k in zip(qs, k_segs)], axis=-1)
        p = jax.nn.softmax(s.astype(jnp.float32) * scale, axis=-1)
        a = p[:, :, 0] - lam * p[:, :, 1]
        return jnp.einsum('bhqk,bkhe->bqhe', a, v.astype(jnp.float32)).astype(v.dtype)

    return sweep_query_blocks(blk, tuple(q_segs))


def mla_attend(q_nope, pe_q, k_nope, pe_k, v):
    scale = (QK_NOPE + QK_ROPE) ** -0.5

    def blk(qn, *qps):
        s_pe = jnp.concatenate([jnp.einsum('bqhr,bkr->bhqk', qp, kp) for qp, kp in zip(qps, pe_k)], axis=-1)
        s = jnp.einsum('bqhd,bkhd->bhqk', qn, k_nope) + s_pe
        p = jax.nn.softmax(s.astype(jnp.float32) * scale, axis=-1)
        return jnp.einsum('bhqk,bkhe->bqhe', p, v.astype(jnp.float32)).astype(v.dtype)

    return sweep_query_blocks(blk, (q_nope,) + tuple(pe_q))


def attention_mixer(h, w_in, w_uq, w_ukv, q_norm_g, kv_norm_g, lam, lam_init, diff_norm_g, w_out, ctx, pos):
    b, t, _ = h.shape
    q_d, k_d, v_d, cq, ckv, kpe = jnp.split(h @ w_in, ATT_SPLITS, axis=-1)
    q_d = q_d.reshape(b, t, H_DIFF, 2, DH_DIFF)
    k_d = k_d.reshape(b, t, H_DIFF, 2, DH_DIFF)
    v_d = v_d.reshape(b, t, H_DIFF, 2 * DH_DIFF)
    q_m = (rms_norm(cq, q_norm_g) @ w_uq).reshape(b, t, H_MLA, QK_NOPE + QK_ROPE)
    q_nope, q_pe = q_m[..., :QK_NOPE], q_m[..., QK_NOPE:]
    ckv = rms_norm(ckv, kv_norm_g)
    if ctx is None:
        v_all, ckv_all = v_d, ckv
        diff_q, diff_k = (q_d,), (k_d,)
        pe_q, pe_k = (q_pe,), (kpe,)
    else:
        row, col = pos
        k_ctx = ctx[0].reshape(b, -1, H_DIFF, 2, DH_DIFF)
        v_all = jnp.concatenate([ctx[1], v_d], axis=1)
        ckv_all = jnp.concatenate([ctx[2], ckv], axis=1)
        diff_q, diff_k = (q_d, rope_2d(q_d, row, col)), (k_ctx, rope_2d(k_d, row, col))
        pe_q, pe_k = (q_pe, rope_2d(q_pe, row, col)), (ctx[3], rope_2d(kpe, row, col))
    o_d = diff_attend(diff_q, diff_k, v_all, lam)
    kv = (ckv_all @ w_ukv).reshape(b, -1, H_MLA, QK_NOPE + V_MLA)
    o_m = mla_attend(q_nope, pe_q, kv[..., :QK_NOPE], pe_k, kv[..., QK_NOPE:])
    o_d = (rms_norm(o_d, diff_norm_g) * (1.0 - lam_init)).reshape(b, t, DIFF_V)
    y = jnp.concatenate([o_d, o_m.reshape(b, t, H_MLA * V_MLA)], axis=-1) @ w_out
    new_ctx = (k_d.reshape(b, t, H_DIFF, 2 * DH_DIFF), v_d, ckv, kpe)
    return y, new_ctx


def depthwise_conv(u, w, bias):
    out = lax.conv_general_dilated(u, w[:, None, :].astype(u.dtype), window_strides=(1,),
                                   padding=[(CONV_WIDTH // 2, CONV_WIDTH // 2)],
                                   dimension_numbers=('NWC', 'WIO', 'NWC'),
                                   feature_group_count=u.shape[-1])
    return out + bias


def multi_scale_pool(x):
    t_len = x.shape[1]
    t = jnp.arange(t_len)
    xf = x.astype(jnp.float32)
    cs = jnp.concatenate([jnp.zeros_like(xf[:, :1]), lax.cumsum(xf, axis=1)], axis=1)
    outs = []
    for g, w in enumerate(POOL_WINDOWS):
        lo = jnp.maximum(t - w // 2, 0)
        hi = jnp.minimum(t + w // 2 - 1, t_len - 1)
        csg = cs[..., g * POOL_GC:(g + 1) * POOL_GC]
        cnt = (hi - lo + 1).astype(jnp.float32)[None, :, None]
        outs.append((csg[:, hi + 1] - csg[:, lo]) / cnt)
    return jnp.concatenate(outs, axis=-1).astype(x.dtype)


def conv_pool_mixer(h, w_in, conv_w, conv_b, cn_g, cn_b, w_pool, pool_scale, w_out):
    b, t, _ = h.shape
    a, gate, pz = jnp.split(h @ w_in, [CONV_CH, 2 * CONV_CH], axis=-1)
    u = a * jax.nn.sigmoid(gate)
    u = jax.nn.silu(layer_norm(depthwise_conv(u, conv_w, conv_b), cn_g, cn_b))
    d = (multi_scale_pool(pz) - pz).reshape(b, t, POOL_GROUPS, POOL_GC)
    d = jnp.einsum('btgc,gce->btge', d, w_pool).reshape(b, t, POOL_CH) * pool_scale
    return jnp.concatenate([u, d], axis=-1) @ w_out


def squared_relu_mlp(h, w1, w2):
    return jnp.square(jax.nn.relu(h @ w1)) @ w2


def setup_inputs(seed: int = 0) -> dict:
    key = jax.random.key(seed)
    ks = jax.random.split(key, 40)
    cnt = [0]

    def nrm(shape, scale):
        k_ = ks[cnt[0]]
        cnt[0] += 1
        return scale * jax.random.normal(k_, shape, jnp.float32)

    d = D_MODEL
    return {
        'x_prompt': nrm((BATCH, SEQ, d), 1.0),
        'x_sample': nrm((DEC_BATCH, DEC_SEQ, d), 1.0),
        'cache_diff_k': nrm((DEC_BATCH, N_ATT, PAST_LEN, H_DIFF, 2 * DH_DIFF), 1.0),
        'cache_diff_v': nrm((DEC_BATCH, N_ATT, PAST_LEN, H_DIFF, 2 * DH_DIFF), 1.0),
        'cache_mla_ckv': nrm((DEC_BATCH, N_ATT, PAST_LEN, KV_LORA), 1.0),
        'cache_mla_krope': nrm((DEC_BATCH, N_ATT, PAST_LEN, QK_ROPE), 1.0),
        'c': nrm((DEC_BATCH, d), 1.0),
        'c_ctx': nrm((d,), 1.0),
        'w_ada': nrm((DEPTH, d, 6 * d), d ** -0.5),
        'b_ada': nrm((DEPTH, 6 * d), 0.02),
        'ln_mix_g': 1.0 + nrm((DEPTH, d), 0.02),
        'ln_mix_b': nrm((DEPTH, d), 0.02),
        'ln_mlp_g': 1.0 + nrm((DEPTH, d), 0.02),
        'ln_mlp_b': nrm((DEPTH, d), 0.02),
        'w_mlp_in': nrm((DEPTH, d, D_FF), d ** -0.5),
        'w_mlp_out': nrm((DEPTH, D_FF, d), D_FF ** -0.5 * DEEPNORM_BETA),
        'w_att_in': nrm((N_ATT, d, ATT_IN), d ** -0.5),
        'w_uq': nrm((N_ATT, Q_LORA, H_MLA * (QK_NOPE + QK_ROPE)), Q_LORA ** -0.5),
        'w_ukv': nrm((N_ATT, KV_LORA, H_MLA * (QK_NOPE + V_MLA)), KV_LORA ** -0.5),
        'q_norm_g': 1.0 + nrm((N_ATT, Q_LORA), 0.02),
        'kv_norm_g': 1.0 + nrm((N_ATT, KV_LORA), 0.02),
        'lam_q1': nrm((N_ATT, DH_DIFF), 0.1),
        'lam_k1': nrm((N_ATT, DH_DIFF), 0.1),
        'lam_q2': nrm((N_ATT, DH_DIFF), 0.1),
        'lam_k2': nrm((N_ATT, DH_DIFF), 0.1),
        'diff_norm_g': 1.0 + nrm((N_ATT, 2 * DH_DIFF), 0.02),
        'w_att_out': nrm((N_ATT, ATT_OUT, d), ATT_OUT ** -0.5 * DEEPNORM_BETA),
        'w_conv_in': nrm((N_CONV, d, CONV_IN), d ** -0.5),
        'conv_w': nrm((N_CONV, CONV_WIDTH, CONV_CH), CONV_WIDTH ** -0.5),
        'conv_b': nrm((N_CONV, CONV_CH), 0.02),
        'conv_norm_g': 1.0 + nrm((N_CONV, CONV_CH), 0.02),
        'conv_norm_b': nrm((N_CONV, CONV_CH), 0.02),
        'w_pool': nrm((N_CONV, POOL_GROUPS, POOL_GC, POOL_GC), POOL_GC ** -0.5),
        'pool_scale': 1.0 + nrm((N_CONV, POOL_CH), 0.1),
        'w_conv_out': nrm((N_CONV, CONV_OUT, d), CONV_OUT ** -0.5 * DEEPNORM_BETA),
    }


def reference(x_prompt, x_sample, cache_diff_k, cache_diff_v, cache_mla_ckv, cache_mla_krope, c, c_ctx,
              w_ada, b_ada, ln_mix_g, ln_mix_b, ln_mlp_g, ln_mlp_b, w_mlp_in, w_mlp_out,
              w_att_in, w_uq, w_ukv, q_norm_g, kv_norm_g, lam_q1, lam_k1, lam_q2, lam_k2, diff_norm_g, w_att_out,
              w_conv_in, conv_w, conv_b, conv_norm_g, conv_norm_b, w_pool, pool_scale, w_conv_out):
    t_lat = x_sample.shape[1]
    rows = t_lat // GRID_W
    row = jnp.repeat(jnp.arange(rows, dtype=jnp.int32), GRID_W)
    col = jnp.tile(jnp.arange(GRID_W, dtype=jnp.int32), rows)

    def mixer(l, h, ctx, pos):
        i = l // 2
        if l % 2 == 0:
            lam_init = 0.8 - 0.6 * math.exp(-0.3 * l)
            lam = (jnp.exp(jnp.sum(lam_q1[i].astype(jnp.float32) * lam_k1[i].astype(jnp.float32)))
                   - jnp.exp(jnp.sum(lam_q2[i].astype(jnp.float32) * lam_k2[i].astype(jnp.float32)))
                   + lam_init)
            return attention_mixer(h, w_att_in[i], w_uq[i], w_ukv[i], q_norm_g[i], kv_norm_g[i], lam, lam_init,
                                   diff_norm_g[i], w_att_out[i], ctx, pos)
        y = conv_pool_mixer(h, w_conv_in[i], conv_w[i], conv_b[i], conv_norm_g[i], conv_norm_b[i],
                            w_pool[i], pool_scale[i], w_conv_out[i])
        return y, None

    def run_layer(l, x, cond, ctx, pos):
        mods = jax.nn.silu(cond) @ w_ada[l] + b_ada[l]
        sh_m, sc_m, g_m, sh_f, sc_f, g_f = jnp.split(mods, 6, axis=-1)
        y, new_ctx = mixer(l, x * (1.0 + sc_m) + sh_m, ctx, pos)
        x = layer_norm(DEEPNORM_ALPHA * x + g_m * y, ln_mix_g[l], ln_mix_b[l])
        f = squared_relu_mlp(x * (1.0 + sc_f) + sh_f, w_mlp_in[l], w_mlp_out[l])
        x = layer_norm(DEEPNORM_ALPHA * x + g_f * f, ln_mlp_g[l], ln_mlp_b[l])
        return x, new_ctx

    xp = x_prompt
    ks_, vs_, ckvs_, kpes_ = [], [], [], []
    for l in range(DEPTH):
        xp, st = run_layer(l, xp, c_ctx, None, None)
        if st is not None:
            ks_.append(st[0])
            vs_.append(st[1])
            ckvs_.append(st[2])
            kpes_.append(st[3])
    new_diff_k = jnp.stack(ks_, axis=1)
    new_diff_v = jnp.stack(vs_, axis=1)
    new_mla_ckv = jnp.stack(ckvs_, axis=1)
    new_mla_krope = jnp.stack(kpes_, axis=1)

    xs = x_sample
    cond = c[:, None, :]
    for l in range(DEPTH):
        i = l // 2
        ctx = (cache_diff_k[:, i], cache_diff_v[:, i], cache_mla_ckv[:, i], cache_mla_krope[:, i]) if l % 2 == 0 else None
        xs, _ = run_layer(l, xs, cond, ctx, (row, col))

    return (xp, xs, new_diff_k, new_diff_v, new_mla_ckv, new_mla_krope)
```

```python
import functools
import math

import jax
import jax.numpy as jnp
import numpy as np
from jax import lax
from jax.experimental import pallas as pl
from jax.experimental.pallas import tpu as pltpu

D_MODEL = 1024
DEPTH = 2
GRID_W = 64
H_DIFF = 4
DH_DIFF = 64
DIFF_QK = H_DIFF * 2 * DH_DIFF
DIFF_V = H_DIFF * 2 * DH_DIFF
H_MLA = 8
Q_LORA = 256
KV_LORA = 128
QK_NOPE = 64
QK_ROPE = 32
V_MLA = 64
ATT_IN = 2 * DIFF_QK + DIFF_V + Q_LORA + KV_LORA + QK_ROPE
CONV_CH = 512
CONV_WIDTH = 31
POOL_CH = 512
POOL_WINDOWS = (2, 4, 8, 16)
POOL_GC = POOL_CH // len(POOL_WINDOWS)
CONV_IN = 2 * CONV_CH + POOL_CH
D_FF = 4 * D_MODEL
ROPE_BASE = 10000.0
NORM_EPS = 1e-5
DEEPNORM_ALPHA = (2 * DEPTH) ** 0.25

LANES = 128
ATT_IN_PAD = 2048
MLA_HEAD_PAD = LANES
VMEM_LIMIT = 56 * 1024 * 1024

TOKEN_TILE = 512
Q_TILE = 256
KEY_CHUNK = 512
CONV_TILE = 512
CONV_HALO = 16
FF_CHUNK = 1024

F32 = jnp.float32
BF16 = jnp.bfloat16


def _dot(a, b):
    return jnp.dot(a, b, preferred_element_type=F32)


def _dot_nt(a, b):
    return lax.dot_general(a, b, (((1,), (1,)), ((), ())), preferred_element_type=F32)


def _layer_norm(x, g, b):
    mu = jnp.mean(x, axis=-1, keepdims=True)
    xc = x - mu
    var = jnp.mean(xc * xc, axis=-1, keepdims=True)
    return xc * lax.rsqrt(var + NORM_EPS) * g + b


def _rms_norm(x, g):
    ms = jnp.mean(x * x, axis=-1, keepdims=True)
    return x * lax.rsqrt(ms + NORM_EPS) * g


def _const_spec(shape):
    nd = len(shape)
    return pl.BlockSpec(shape, lambda *_: (0,) * nd, pipeline_mode=pl.Buffered(1))


def _params(*sem):
    return pltpu.CompilerParams(dimension_semantics=sem, vmem_limit_bytes=VMEM_LIMIT)


ADA_TILE = 1536


def _ada_kernel(cond_ref, w_ref, b_ref, out_ref):
    cond = cond_ref[...]
    act = (cond * jax.nn.sigmoid(cond)).astype(BF16)
    out_ref[0] = _dot(act, w_ref[0]) + b_ref[0]


def _ada_mods(cond8, w_ada, b_ada):
    n = 6 * D_MODEL
    return pl.pallas_call(
        _ada_kernel,
        grid=(DEPTH, n // ADA_TILE),
        in_specs=[
            pl.BlockSpec((8, D_MODEL), lambda l, j: (0, 0)),
            pl.BlockSpec((1, D_MODEL, ADA_TILE), lambda l, j: (l, 0, j)),
            pl.BlockSpec((1, 1, ADA_TILE), lambda l, j: (l, 0, j)),
        ],
        out_specs=pl.BlockSpec((1, 8, ADA_TILE), lambda l, j: (l, 0, j)),
        out_shape=jax.ShapeDtypeStruct((DEPTH, 8, n), F32),
        compiler_params=_params("arbitrary", "arbitrary"),
        name="ada_mods",
    )(cond8, w_ada, b_ada.reshape(DEPTH, 1, n))


def _rope_tables(t_lat):
    pos = np.arange(t_lat)
    row = (pos // GRID_W).astype(np.float64)
    col = (pos % GRID_W).astype(np.float64)

    def tables(lane_kind):
        cos = np.ones((t_lat, LANES))
        sa = np.zeros((t_lat, LANES))
        sb = np.zeros((t_lat, LANES))
        for lane, kind in enumerate(lane_kind):
            if kind is None:
                continue
            axis, half, j, upper = kind
            inv = ROPE_BASE ** (-float(j) / half)
            ang = (row if axis == 0 else col) * inv
            cos[:, lane] = np.cos(ang)
            if upper:
                sb[:, lane] = np.sin(ang)
            else:
                sa[:, lane] = -np.sin(ang)
        return [cos, sa, sb]

    def rot_kinds(n):
        half = n // 4
        kinds = []
        for i in range(n):
            axis, r = divmod(i, n // 2)
            kinds.append((axis, half, r % half, r >= half))
        return kinds

    diff = (rot_kinds(DH_DIFF) * 2)
    mla_q = [None] * QK_NOPE + rot_kinds(QK_ROPE) + [None] * (LANES - QK_NOPE - QK_ROPE)
    mla_k = rot_kinds(QK_ROPE) + [None] * (LANES - QK_ROPE)
    tabs = tables(diff) + tables(mla_q) + tables(mla_k)
    return jnp.asarray(np.stack(tabs).astype(np.float32))


def _rope(x, cos, sa, sb, shift):
    return x * cos + pltpu.roll(x, LANES - shift, 1) * sa + pltpu.roll(x, shift, 1) * sb


def _rope_wide(x, cos, sa, sb, shift):
    n = x.shape[1] // LANES
    return jnp.concatenate(
        [_rope(x[:, i * LANES:(i + 1) * LANES], cos, sa, sb, shift) for i in range(n)], axis=1)


def _att_in_body(x_ref, mods_ref, w_in_ref, qg_ref, kvg_ref, wq_ref, wk_ref, ek_ref, wv_ref):
    mods = mods_ref[0, 0]
    sh, sc = mods[0:1], mods[1:2]
    h = (x_ref[...] * (1.0 + sc) + sh).astype(BF16)
    proj = _dot(h, w_in_ref[...])
    q_d = proj[:, 0:DIFF_QK] * (DH_DIFF ** -0.5)
    k_d = proj[:, DIFF_QK:2 * DIFF_QK]
    v_d = proj[:, 2 * DIFF_QK:2 * DIFF_QK + DIFF_V]
    o = 2 * DIFF_QK + DIFF_V
    cq = proj[:, o:o + Q_LORA]
    ckv = proj[:, o + Q_LORA:o + Q_LORA + KV_LORA]
    kpe = proj[:, o + Q_LORA + KV_LORA:o + Q_LORA + KV_LORA + LANES]
    q_m = _dot(_rms_norm(cq, qg_ref[...]).astype(BF16), wq_ref[...])
    ckv_n = _rms_norm(ckv, kvg_ref[...])
    ckv_b = ckv_n.astype(BF16)
    k_nope = _dot(ckv_b, wk_ref[...])
    v_m = _dot(ckv_b, wv_ref[...])
    return q_d, k_d, v_d, q_m, ckv_n, kpe, k_nope, v_m


def _att_in_ctx_kernel(x_ref, mods_ref, w_in_ref, qg_ref, kvg_ref, wq_ref, wk_ref, ek_ref, wv_ref,
                       qd_ref, kd_ref, vd_ref, qm_ref, km_ref, vm_ref,
                       kdf_ref, vdf_ref, ckvf_ref, kpef_ref):
    q_d, k_d, v_d, q_m, ckv_n, kpe, k_nope, v_m = _att_in_body(
        x_ref, mods_ref, w_in_ref, qg_ref, kvg_ref, wq_ref, wk_ref, ek_ref, wv_ref)
    qd_ref[...] = q_d.astype(BF16)
    kd_ref[...] = k_d.astype(BF16)
    vd_ref[...] = v_d.astype(BF16)
    qm_ref[...] = q_m.astype(BF16)
    km_ref[...] = (k_nope + _dot(kpe.astype(BF16), ek_ref[...])).astype(BF16)
    vm_ref[...] = v_m.astype(BF16)
    kdf_ref[...] = k_d
    vdf_ref[...] = v_d
    ckvf_ref[...] = ckv_n
    kpef_ref[...] = kpe[:, 0:QK_ROPE]


def _att_in_lat_kernel(x_ref, mods_ref, w_in_ref, qg_ref, kvg_ref, wq_ref, wk_ref, ek_ref, wv_ref,
                       tab_ref,
                       qdu_ref, qdr_ref, kdr_ref, vd_ref, qmu_ref, qmr_ref, kmr_ref, vm_ref):
    q_d, k_d, v_d, q_m, ckv_n, kpe, k_nope, v_m = _att_in_body(
        x_ref, mods_ref, w_in_ref, qg_ref, kvg_ref, wq_ref, wk_ref, ek_ref, wv_ref)
    qdu_ref[...] = q_d.astype(BF16)
    qdr_ref[...] = _rope_wide(q_d, tab_ref[0], tab_ref[1], tab_ref[2], DH_DIFF // 4).astype(BF16)
    kdr_ref[...] = _rope_wide(k_d, tab_ref[0], tab_ref[1], tab_ref[2], DH_DIFF // 4).astype(BF16)
    vd_ref[...] = v_d.astype(BF16)
    qmu_ref[...] = q_m.astype(BF16)
    qmr_ref[...] = _rope_wide(q_m, tab_ref[3], tab_ref[4], tab_ref[5], QK_ROPE // 4).astype(BF16)
    kpe_r = _rope(kpe, tab_ref[6], tab_ref[7], tab_ref[8], QK_ROPE // 4)
    kmr_ref[...] = (k_nope + _dot(kpe_r.astype(BF16), ek_ref[...])).astype(BF16)
    vm_ref[...] = v_m.astype(BF16)


def _att_in(x2d, mods, mod_row_fn, wts, tables, tm):
    t = x2d.shape[0]
    w_in, qg, kvg, wq, wk, ek, wv = wts
    tok = lambda n: pl.BlockSpec((tm, n), lambda i: (i, 0))
    in_specs = [
        tok(D_MODEL),
        pl.BlockSpec((1, 1, 6, D_MODEL), lambda i: (0, mod_row_fn(i), 0, 0)),
        _const_spec(w_in.shape), _const_spec(qg.shape), _const_spec(kvg.shape),
        _const_spec(wq.shape), _const_spec(wk.shape), _const_spec(ek.shape), _const_spec(wv.shape),
    ]
    wide = H_MLA * MLA_HEAD_PAD
    bf = lambda n: jax.ShapeDtypeStruct((t, n), BF16)
    fl = lambda n: jax.ShapeDtypeStruct((t, n), F32)
    if tables is None:
        widths = [DIFF_QK, DIFF_QK, DIFF_V, wide, wide, H_MLA * V_MLA]
        out_shape = [bf(n) for n in widths] + [fl(DIFF_QK), fl(DIFF_V), fl(KV_LORA), fl(QK_ROPE)]
        out_specs = [tok(n) for n in widths] + [tok(DIFF_QK), tok(DIFF_V), tok(KV_LORA), tok(QK_ROPE)]
        kern, args, name = _att_in_ctx_kernel, (), "att_in_ctx"
    else:
        t_lat = tables.shape[1]
        in_specs.append(pl.BlockSpec((9, tm, LANES), lambda i: (0, i % (t_lat // tm), 0)))
        widths = [DIFF_QK, DIFF_QK, DIFF_QK, DIFF_V, wide, wide, wide, H_MLA * V_MLA]
        out_shape = [bf(n) for n in widths]
        out_specs = [tok(n) for n in widths]
        kern, args, name = _att_in_lat_kernel, (tables,), "att_in_lat"
    return pl.pallas_call(
        kern, grid=(t // tm,), in_specs=in_specs, out_specs=out_specs, out_shape=out_shape,
        compiler_params=_params("arbitrary"), name=name,
    )(x2d, mods, w_in, qg, kvg, wq, wk, ek, wv, *args)


def _cache_kv_kernel(ckv_ref, kpe_ref, wk_ref, ek_ref, wv_ref, km_ref, vm_ref):
    ckv_b = ckv_ref[...].astype(BF16)
    km_ref[...] = (_dot(ckv_b, wk_ref[...]) + _dot(kpe_ref[...].astype(BF16), ek_ref[...])).astype(BF16)
    vm_ref[...] = _dot(ckv_b, wv_ref[...]).astype(BF16)


def _cache_kv(ckv2d, kpe_slab2d, wk, ek, wv):
    t = ckv2d.shape[0]
    full = lambda a: pl.BlockSpec(a.shape, lambda i: (0,) * a.ndim)
    return pl.pallas_call(
        _cache_kv_kernel, grid=(1,),
        in_specs=[full(ckv2d), full(kpe_slab2d), full(wk), full(ek), full(wv)],
        out_specs=[pl.BlockSpec((t, H_MLA * MLA_HEAD_PAD), lambda i: (0, 0)),
                   pl.BlockSpec((t, H_MLA * V_MLA), lambda i: (0, 0))],
        out_shape=[jax.ShapeDtypeStruct((t, H_MLA * MLA_HEAD_PAD), BF16),
                   jax.ShapeDtypeStruct((t, H_MLA * V_MLA), BF16)],
        compiler_params=_params("arbitrary"), name="cache_kv",
    )(ckv2d, kpe_slab2d, wk, ek, wv)


def _softmax_step(s, v, m, l, acc):
    m_new = jnp.maximum(m, jnp.max(s, axis=-1, keepdims=True))
    alpha = jnp.exp(m - m_new)
    p = jnp.exp(s - m_new)
    l = alpha * l + jnp.sum(p, axis=-1, keepdims=True)
    acc = alpha * acc + _dot(p.astype(BF16), v)
    return m_new, l, acc


def _attend(q_first, k_first, v_first, q_rest, k_rest_ref, v_rest_ref, scale):
    s = _dot_nt(q_first, k_first)
    if scale is not None:
        s = s * scale
    m = jnp.max(s, axis=-1, keepdims=True)
    p = jnp.exp(s - m)
    l = jnp.sum(p, axis=-1, keepdims=True)
    acc = _dot(p.astype(BF16), v_first)
    if q_rest is not None:
        n_chunks = k_rest_ref.shape[1] // KEY_CHUNK

        def body(c, carry):
            off = pl.multiple_of(c * KEY_CHUNK, KEY_CHUNK)
            k = k_rest_ref[0, pl.ds(off, KEY_CHUNK), :]
            v = v_rest_ref[0, pl.ds(off, KEY_CHUNK), :]
            s = _dot_nt(q_rest, k)
            if scale is not None:
                s = s * scale
            return _softmax_step(s, v, *carry)

        m, l, acc = lax.fori_loop(0, n_chunks, body, (m, l, acc))
    return acc / l


def _split_maps(q):
    lane = lax.broadcasted_iota(jnp.int32, q.shape, 1)
    zero = jnp.zeros_like(q)
    return jnp.concatenate([jnp.where(lane < DH_DIFF, q, zero), jnp.where(lane >= DH_DIFF, q, zero)], axis=0)


def _lambda(lam_ref, lam_init):
    lv = lam_ref[...]
    a = jnp.sum(lv[0:1] * lv[1:2], axis=-1, keepdims=True)
    b = jnp.sum(lv[2:3] * lv[3:4], axis=-1, keepdims=True)
    return jnp.exp(a) - jnp.exp(b) + lam_init


def _diff_finish(o2, lam_ref, g_ref, lam_init, out_ref):
    rows = o2.shape[0] // 2
    lam = _lambda(lam_ref, lam_init)
    o = o2[:rows] - lam * o2[rows:]
    out_ref[0] = (_rms_norm(o, g_ref[...]) * (1.0 - lam_init)).astype(out_ref.dtype)


def _diff_ctx_kernel(q_ref, k_ref, v_ref, lam_ref, g_ref, out_ref, *, lam_init):
    o2 = _attend(_split_maps(q_ref[0]), k_ref[0], v_ref[0], None, None, None, None)
    _diff_finish(o2, lam_ref, g_ref, lam_init, out_ref)


def _diff_lat_kernel(qu_ref, qr_ref, kc_ref, vc_ref, k_ref, v_ref, lam_ref, g_ref, out_ref, *, lam_init):
    o2 = _attend(_split_maps(qu_ref[0]), kc_ref[0].astype(BF16), vc_ref[0].astype(BF16),
                 _split_maps(qr_ref[0]), k_ref, v_ref, None)
    _diff_finish(o2, lam_ref, g_ref, lam_init, out_ref)


def _mla_pick(o_a, o_b):
    lane = lax.broadcasted_iota(jnp.int32, o_a.shape, 1)
    return jnp.where(lane < V_MLA, o_a, o_b)


def _mla_ctx_kernel(qa_ref, qb_ref, ka_ref, kb_ref, v_ref, out_ref, *, scale):
    v = v_ref[0]
    o_a = _attend(qa_ref[0], ka_ref[0], v, None, None, None, scale)
    o_b = _attend(qb_ref[0], kb_ref[0], v, None, None, None, scale)
    out_ref[0] = _mla_pick(o_a, o_b).astype(out_ref.dtype)


def _mla_lat_kernel(qua_ref, qub_ref, qra_ref, qrb_ref, kca_ref, kcb_ref, vc_ref, ka_ref, kb_ref, v_ref,
                    out_ref, *, scale):
    vc = vc_ref[0]
    o_a = _attend(qua_ref[0], kca_ref[0], vc, qra_ref[0], ka_ref, v_ref, scale)
    o_b = _attend(qub_ref[0], kcb_ref[0], vc, qrb_ref[0], kb_ref, v_ref, scale)
    out_ref[0] = _mla_pick(o_a, o_b).astype(out_ref.dtype)


def _attention_ctx(streams, lam4, diff_g, lam_init, b, t):
    qd, kd, vd, qm, km, vm = [a.reshape(b, t, a.shape[-1]) for a in streams]
    head = lambda w: pl.BlockSpec((1, t, w), lambda i, h: (i, 0, h))
    o_d = pl.pallas_call(
        functools.partial(_diff_ctx_kernel, lam_init=lam_init),
        grid=(b, H_DIFF),
        in_specs=[head(LANES), head(LANES), head(LANES),
                  pl.BlockSpec(lam4.shape, lambda i, h: (0, 0)),
                  pl.BlockSpec(diff_g.shape, lambda i, h: (0, 0))],
        out_specs=head(LANES),
        out_shape=jax.ShapeDtypeStruct((b, t, DIFF_V), BF16),
        compiler_params=_params("arbitrary", "arbitrary"), name="diff_attn_ctx",
    )(qd, kd, vd, lam4, diff_g)
    head_a = pl.BlockSpec((1, t, LANES), lambda i, h: (i, 0, 2 * h))
    head_b = pl.BlockSpec((1, t, LANES), lambda i, h: (i, 0, 2 * h + 1))
    o_m = pl.pallas_call(
        functools.partial(_mla_ctx_kernel, scale=(QK_NOPE + QK_ROPE) ** -0.5),
        grid=(b, H_MLA // 2),
        in_specs=[head_a, head_b, head_a, head_b, head(LANES)],
        out_specs=head(LANES),
        out_shape=jax.ShapeDtypeStruct((b, t, H_MLA * V_MLA), BF16),
        compiler_params=_params("arbitrary", "arbitrary"), name="mla_attn_ctx",
    )(qm, qm, km, km, vm)
    return o_d.reshape(b * t, DIFF_V), o_m.reshape(b * t, H_MLA * V_MLA)


def _attention_lat(streams, cache, lam4, diff_g, lam_init, b, t):
    qdu, qdr, kdr, vd, qmu, qmr, kmr, vm = [a.reshape(b, t, a.shape[-1]) for a in streams]
    kd_c, vd_c, km_c, vm_c = cache
    past = kd_c.shape[1]
    qblk = lambda w: pl.BlockSpec((1, Q_TILE, w), lambda i, h, j: (i, j, h))
    keys = lambda n, w: pl.BlockSpec((1, n, w), lambda i, h, j: (i, 0, h))
    o_d = pl.pallas_call(
        functools.partial(_diff_lat_kernel, lam_init=lam_init),
        grid=(b, H_DIFF, t // Q_TILE),
        in_specs=[qblk(LANES), qblk(LANES), keys(past, LANES), keys(past, LANES),
                  keys(t, LANES), keys(t, LANES),
                  pl.BlockSpec(lam4.shape, lambda i, h, j: (0, 0)),
                  pl.BlockSpec(diff_g.shape, lambda i, h, j: (0, 0))],
        out_specs=qblk(LANES),
        out_shape=jax.ShapeDtypeStruct((b, t, DIFF_V), BF16),
        compiler_params=_params("arbitrary", "arbitrary", "arbitrary"), name="diff_attn_lat",
    )(qdu, qdr, kd_c, vd_c, kdr, vd, lam4, diff_g)
    q_a = pl.BlockSpec((1, Q_TILE, LANES), lambda i, h, j: (i, j, 2 * h))
    q_b = pl.BlockSpec((1, Q_TILE, LANES), lambda i, h, j: (i, j, 2 * h + 1))
    keys_a = lambda n: pl.BlockSpec((1, n, LANES), lambda i, h, j: (i, 0, 2 * h))
    keys_b = lambda n: pl.BlockSpec((1, n, LANES), lambda i, h, j: (i, 0, 2 * h + 1))
    o_m = pl.pallas_call(
        functools.partial(_mla_lat_kernel, scale=(QK_NOPE + QK_ROPE) ** -0.5),
        grid=(b, H_MLA // 2, t // Q_TILE),
        in_specs=[q_a, q_b, q_a, q_b, keys_a(past), keys_b(past), keys(past, LANES),
                  keys_a(t), keys_b(t), keys(t, LANES)],
        out_specs=qblk(LANES),
        out_shape=jax.ShapeDtypeStruct((b, t, H_MLA * V_MLA), BF16),
        compiler_params=_params("arbitrary", "arbitrary", "arbitrary"), name="mla_attn_lat",
    )(qmu, qmu, qmr, qmr, km_c, km_c, vm_c, kmr, kmr, vm)
    return o_d.reshape(b * t, DIFF_V), o_m.reshape(b * t, H_MLA * V_MLA)


def _post_kernel(x_ref, ma_ref, mb_ref, mods_ref, wo_ref, w1_ref, w2_ref, ln_ref, out_ref):
    mods = mods_ref[0, 0]
    g_m, sh_f, sc_f, g_f = mods[2:3], mods[3:4], mods[4:5], mods[5:6]
    ln = ln_ref[...]
    mix = jnp.concatenate([ma_ref[...], mb_ref[...]], axis=1)
    y = _dot(mix, wo_ref[...])
    x1 = _layer_norm(DEEPNORM_ALPHA * x_ref[...] + g_m * y, ln[0:1], ln[1:2])
    h = (x1 * (1.0 + sc_f) + sh_f).astype(BF16)
    f = jnp.zeros(x1.shape, F32)
    for c in range(D_FF // FF_CHUNK):
        a = jnp.maximum(_dot(h, w1_ref[:, c * FF_CHUNK:(c + 1) * FF_CHUNK]), 0.0)
        f = f + _dot((a * a).astype(BF16), w2_ref[c * FF_CHUNK:(c + 1) * FF_CHUNK, :])
    out_ref[...] = _layer_norm(DEEPNORM_ALPHA * x1 + g_f * f, ln[2:3], ln[3:4])


def _post(x2d, mix_a, mix_b, mods, layer, mod_row_fn, wo, w1, w2, ln4, tm):
    t = x2d.shape[0]
    tok = lambda n: pl.BlockSpec((tm, n), lambda i: (i, 0))
    return pl.pallas_call(
        _post_kernel, grid=(t // tm,),
        in_specs=[tok(D_MODEL), tok(mix_a.shape[1]), tok(mix_b.shape[1]),
                  pl.BlockSpec((1, 1, 6, D_MODEL), lambda i: (layer, mod_row_fn(i), 0, 0)),
                  _const_spec(wo.shape), _const_spec(w1.shape), _const_spec(w2.shape),
                  _const_spec(ln4.shape)],
        out_specs=tok(D_MODEL),
        out_shape=jax.ShapeDtypeStruct((t, D_MODEL), F32),
        compiler_params=_params("arbitrary"), name=f"post_mlp_l{layer}",
    )(x2d, mix_a, mix_b, mods, wo, w1, w2, ln4)


def _conv_in_kernel(x_ref, mods_ref, w_ref, u_ref, pz_ref):
    mods = mods_ref[0, 0]
    sh, sc = mods[0:1], mods[1:2]
    h = (x_ref[...] * (1.0 + sc) + sh).astype(BF16)
    proj = _dot(h, w_ref[...])
    a, gate = proj[:, 0:CONV_CH], proj[:, CONV_CH:2 * CONV_CH]
    u_ref[...] = a * jax.nn.sigmoid(gate)
    pz_ref[...] = proj[:, 2 * CONV_CH:]


def _conv_in(x2d, mods, layer, mod_row_fn, w, tm):
    t = x2d.shape[0]
    tok = lambda n: pl.BlockSpec((tm, n), lambda i: (i, 0))
    return pl.pallas_call(
        _conv_in_kernel, grid=(t // tm,),
        in_specs=[tok(D_MODEL),
                  pl.BlockSpec((1, 1, 6, D_MODEL), lambda i: (layer, mod_row_fn(i), 0, 0)),
                  _const_spec(w.shape)],
        out_specs=[tok(CONV_CH), tok(POOL_CH)],
        out_shape=[jax.ShapeDtypeStruct((t, CONV_CH), F32), jax.ShapeDtypeStruct((t, POOL_CH), F32)],
        compiler_params=_params("arbitrary"), name="conv_in",
    )(x2d, mods, w)


def _conv_pool_kernel(*refs, n_tiles, seq_len):
    if n_tiles > 1:
        (u_ref, ul_ref, ur_ref, pz_ref, pl_ref, pr_ref, cw_ref, cp_ref, wp_ref, ps_ref,
         uo_ref, do_ref, ubuf, pbuf) = refs
    else:
        u_ref, pz_ref, cw_ref, cp_ref, wp_ref, ps_ref, uo_ref, do_ref, ubuf, pbuf = refs
    tm = u_ref.shape[1]
    j = pl.program_id(1)
    halo_zero = jnp.zeros((CONV_HALO, CONV_CH), F32)
    for buf, mid, sides in ((ubuf, u_ref, (ul_ref, ur_ref) if n_tiles > 1 else None),
                            (pbuf, pz_ref, (pl_ref, pr_ref) if n_tiles > 1 else None)):
        buf[CONV_HALO:CONV_HALO + tm, :] = mid[0]
        if sides is None:
            buf[0:CONV_HALO, :] = halo_zero
            buf[CONV_HALO + tm:, :] = halo_zero
        else:
            buf[0:CONV_HALO, :] = jnp.where(j > 0, sides[0][0], halo_zero)
            buf[CONV_HALO + tm:, :] = jnp.where(j < n_tiles - 1, sides[1][0], halo_zero)

    acc = jnp.zeros((tm, CONV_CH), F32)
    base = CONV_HALO - CONV_WIDTH // 2
    for k in range(CONV_WIDTH):
        acc = acc + ubuf[base + k:base + k + tm, :] * cw_ref[k:k + 1, :]
    cp = cp_ref[...]
    z = _layer_norm(acc + cp[0:1], cp[1:2], cp[2:3])
    uo_ref[0] = (z * jax.nn.sigmoid(z)).astype(uo_ref.dtype)

    pos = j * tm + lax.broadcasted_iota(jnp.int32, (tm, 1), 0)
    outs = []
    for g, w in enumerate(POOL_WINDOWS):
        cols = slice(g * POOL_GC, (g + 1) * POOL_GC)
        s = jnp.zeros((tm, POOL_GC), F32)
        for d in range(-(w // 2), w // 2):
            s = s + pbuf[CONV_HALO + d:CONV_HALO + d + tm, cols]
        lo = jnp.maximum(pos - w // 2, 0)
        hi = jnp.minimum(pos + w // 2 - 1, seq_len - 1)
        cnt = (hi - lo + 1).astype(F32)
        dgrp = s / cnt - pz_ref[0][:, cols]
        outs.append(_dot(dgrp.astype(BF16), wp_ref[g]))
    do_ref[0] = (jnp.concatenate(outs, axis=1) * ps_ref[...]).astype(do_ref.dtype)


def _conv_pool(u2d, pz2d, b, t, tm, cw, cp, wp, ps):
    n_tiles = t // tm
    u3, p3 = u2d.reshape(b, t, CONV_CH), pz2d.reshape(b, t, POOL_CH)
    mid = pl.BlockSpec((1, tm, CONV_CH), lambda i, j: (i, j, 0))
    r = tm // CONV_HALO
    left = pl.BlockSpec((1, CONV_HALO, CONV_CH), lambda i, j: (i, jnp.maximum(j * r - 1, 0), 0))
    right = pl.BlockSpec((1, CONV_HALO, CONV_CH),
                         lambda i, j: (i, jnp.minimum((j + 1) * r, t // CONV_HALO - 1), 0))
    const = lambda a: pl.BlockSpec(a.shape, lambda i, j: (0,) * a.ndim)
    if n_tiles > 1:
        in_specs = [mid, left, right, mid, left, right]
        args = (u3, u3, u3, p3, p3, p3)
    else:
        in_specs = [mid, mid]
        args = (u3, p3)
    in_specs += [const(cw), const(cp), const(wp), const(ps)]
    uo, do = pl.pallas_call(
        functools.partial(_conv_pool_kernel, n_tiles=n_tiles, seq_len=t),
        grid=(b, n_tiles), in_specs=in_specs,
        out_specs=[mid, mid],
        out_shape=[jax.ShapeDtypeStruct((b, t, CONV_CH), BF16), jax.ShapeDtypeStruct((b, t, POOL_CH), BF16)],
        scratch_shapes=[pltpu.VMEM((tm + 2 * CONV_HALO, CONV_CH), F32),
                        pltpu.VMEM((tm + 2 * CONV_HALO, POOL_CH), F32)],
        compiler_params=_params("arbitrary", "arbitrary"),
        name=f"conv_pool_{'lat' if n_tiles > 1 else 'ctx'}",
    )(*args, cw, cp, wp, ps)
    return uo.reshape(b * t, CONV_CH), do.reshape(b * t, POOL_CH)


def _mla_weight_layouts(w_uq, w_ukv):
    pad_q = MLA_HEAD_PAD - QK_NOPE - QK_ROPE
    wq = w_uq.reshape(Q_LORA, H_MLA, QK_NOPE + QK_ROPE)
    wq = jnp.pad(wq, ((0, 0), (0, 0), (0, pad_q))).reshape(Q_LORA, H_MLA * MLA_HEAD_PAD)
    wkv = w_ukv.reshape(KV_LORA, H_MLA, QK_NOPE + V_MLA)
    wk = jnp.pad(wkv[:, :, :QK_NOPE], ((0, 0), (0, 0), (0, MLA_HEAD_PAD - QK_NOPE)))
    wk = wk.reshape(KV_LORA, H_MLA * MLA_HEAD_PAD)
    wv = wkv[:, :, QK_NOPE:].reshape(KV_LORA, H_MLA * V_MLA)
    place = np.zeros((LANES, H_MLA, MLA_HEAD_PAD), np.float32)
    for h in range(H_MLA):
        for r in range(QK_ROPE):
            place[r, h, QK_NOPE + r] = 1.0
    ek = jnp.asarray(place.reshape(LANES, H_MLA * MLA_HEAD_PAD))
    return wq.astype(BF16), wk.astype(BF16), ek.astype(BF16), wv.astype(BF16)


def kernel(x_prompt, x_sample, cache_diff_k, cache_diff_v, cache_mla_ckv, cache_mla_krope, c, c_ctx,
           w_ada, b_ada, ln_mix_g, ln_mix_b, ln_mlp_g, ln_mlp_b, w_mlp_in, w_mlp_out,
           w_att_in, w_uq, w_ukv, q_norm_g, kv_norm_g, lam_q1, lam_k1, lam_q2, lam_k2, diff_norm_g, w_att_out,
           w_conv_in, conv_w, conv_b, conv_norm_g, conv_norm_b, w_pool, pool_scale, w_conv_out):
    bc, tc, d = x_prompt.shape
    bl, tl, _ = x_sample.shape
    past = cache_diff_k.shape[2]

    cond8 = jnp.zeros((8, d), F32).at[0].set(c_ctx).at[1:1 + bl].set(c)
    mods = _ada_mods(cond8, w_ada.astype(BF16), b_ada).reshape(DEPTH, 8, 6, d)
    ctx_row = lambda i: 0
    lat_row = lambda i: 1 + i // (tl // TOKEN_TILE)

    w_in = jnp.pad(w_att_in[0], ((0, 0), (0, ATT_IN_PAD - ATT_IN))).astype(BF16)
    wq, wk, ek, wv = _mla_weight_layouts(w_uq[0], w_ukv[0])
    att_w = (w_in, q_norm_g[0].reshape(1, Q_LORA), kv_norm_g[0].reshape(1, KV_LORA), wq, wk, ek, wv)
    lam4 = jnp.stack([lam_q1[0], lam_k1[0], lam_q2[0], lam_k2[0]])
    diff_g = diff_norm_g[0].reshape(1, 2 * DH_DIFF)
    lam_init = 0.8 - 0.6 * math.exp(-0.3 * 0)
    ln4 = [jnp.stack([ln_mix_g[l], ln_mix_b[l], ln_mlp_g[l], ln_mlp_b[l]]) for l in range(DEPTH)]
    w1 = [w_mlp_in[l].astype(BF16) for l in range(DEPTH)]
    w2 = [w_mlp_out[l].astype(BF16) for l in range(DEPTH)]
    w_ao = w_att_out[0].astype(BF16)
    w_ci = w_conv_in[0].astype(BF16)
    w_co = w_conv_out[0].astype(BF16)
    conv_p = jnp.stack([conv_b[0], conv_norm_g[0], conv_norm_b[0]])
    wp = w_pool[0].astype(BF16)
    ps = pool_scale[0].reshape(1, POOL_CH)

    xp = x_prompt.reshape(bc * tc, d)
    outs = _att_in(xp, mods, ctx_row, att_w, None, TOKEN_TILE)
    kd_f, vd_f, ckv_f, kpe_f = outs[6:]
    o_d, o_m = _attention_ctx(outs[:6], lam4, diff_g, lam_init, bc, tc)
    xp = _post(xp, o_d, o_m, mods, 0, ctx_row, w_ao, w1[0], w2[0], ln4[0], TOKEN_TILE)
    u, pz = _conv_in(xp, mods, 1, ctx_row, w_ci, TOKEN_TILE)
    uo, do = _conv_pool(u, pz, bc, tc, tc, conv_w[0], conv_p, wp, ps)
    xp = _post(xp, uo, do, mods, 1, ctx_row, w_co, w1[1], w2[1], ln4[1], TOKEN_TILE)

    tables = _rope_tables(tl)
    xs = x_sample.reshape(bl * tl, d)
    outs = _att_in(xs, mods, lat_row, att_w, tables, TOKEN_TILE)
    km_c, vm_c = _cache_kv(cache_mla_ckv[:, 0].reshape(bl * past, KV_LORA),
                           jnp.pad(cache_mla_krope[:, 0].reshape(bl * past, QK_ROPE),
                                   ((0, 0), (0, LANES - QK_ROPE))), wk, ek, wv)
    cache = (cache_diff_k[:, 0].reshape(bl, past, DIFF_QK), cache_diff_v[:, 0].reshape(bl, past, DIFF_V),
             km_c.reshape(bl, past, -1), vm_c.reshape(bl, past, -1))
    o_d, o_m = _attention_lat(outs, cache, lam4, diff_g, lam_init, bl, tl)
    xs = _post(xs, o_d, o_m, mods, 0, lat_row, w_ao, w1[0], w2[0], ln4[0], TOKEN_TILE)
    u, pz = _conv_in(xs, mods, 1, lat_row, w_ci, TOKEN_TILE)
    uo, do = _conv_pool(u, pz, bl, tl, CONV_TILE, conv_w[0], conv_p, wp, ps)
    xs = _post(xs, uo, do, mods, 1, lat_row, w_co, w1[1], w2[1], ln4[1], TOKEN_TILE)

    return (xp.reshape(bc, tc, d), xs.reshape(bl, tl, d),
            kd_f.reshape(bc, 1, tc, H_DIFF, 2 * DH_DIFF), vd_f.reshape(bc, 1, tc, H_DIFF, 2 * DH_DIFF),
            ckv_f.reshape(bc, 1, tc, KV_LORA), kpe_f.reshape(bc, 1, tc, QK_ROPE))
```

```python
import functools
import math

import jax
import jax.numpy as jnp
import numpy as np
from jax import lax
from jax.experimental import pallas as pl
from jax.experimental.pallas import tpu as pltpu

D_MODEL = 1024
DEPTH = 2
GRID_W = 64
H_DIFF = 4
DH_DIFF = 64
DIFF_QK = H_DIFF * 2 * DH_DIFF
DIFF_V = H_DIFF * 2 * DH_DIFF
H_MLA = 8
Q_LORA = 256
KV_LORA = 128
QK_NOPE = 64
QK_ROPE = 32
V_MLA = 64
ATT_IN = 2 * DIFF_QK + DIFF_V + Q_LORA + KV_LORA + QK_ROPE
CONV_CH = 512
CONV_WIDTH = 31
POOL_CH = 512
POOL_WINDOWS = (2, 4, 8, 16)
POOL_GC = POOL_CH // len(POOL_WINDOWS)
CONV_IN = 2 * CONV_CH + POOL_CH
D_FF = 4 * D_MODEL
ROPE_BASE = 10000.0
NORM_EPS = 1e-5
DEEPNORM_ALPHA = (2 * DEPTH) ** 0.25
LOG2E = math.log2(math.e)

LANES = 128
MLA_SLAB = LANES
MLA_WIDE = H_MLA * MLA_SLAB
VMEM_LIMIT = 56 * 1024 * 1024

TOKEN_TILE = 512
KEY_CHUNK = TOKEN_TILE
Q_TILE = 256
CONV_TILE = 512
CONV_HALO = 16
FF_CHUNK = 1024

F32 = jnp.float32
BF16 = jnp.bfloat16


def _dot(a, b):
    return jnp.dot(a, b, preferred_element_type=F32)


def _dot_nt(a, b):
    return lax.dot_general(a, b, (((1,), (1,)), ((), ())), preferred_element_type=F32)


def _layer_norm(x, g, b):
    mu = jnp.mean(x, axis=-1, keepdims=True)
    xc = x - mu
    var = jnp.mean(xc * xc, axis=-1, keepdims=True)
    return xc * lax.rsqrt(var + NORM_EPS) * g + b


def _rms_norm(x, g, axis=-1):
    ms = jnp.mean(x * x, axis=axis, keepdims=True)
    return x * lax.rsqrt(ms + NORM_EPS) * g


def _const_spec(shape):
    nd = len(shape)
    return pl.BlockSpec(shape, lambda *_: (0,) * nd, pipeline_mode=pl.Buffered(1))


def _params(*sem):
    return pltpu.CompilerParams(dimension_semantics=sem, vmem_limit_bytes=VMEM_LIMIT)


ADA_TILE = 1536


def _ada_kernel(cond_ref, w_ref, b_ref, out_ref):
    cond = cond_ref[...]
    act = (cond * jax.nn.sigmoid(cond)).astype(BF16)
    out_ref[0] = _dot(act, w_ref[0]) + b_ref[0]


def _ada_mods(cond8, w_ada, b_ada):
    n = 6 * D_MODEL
    return pl.pallas_call(
        _ada_kernel,
        grid=(DEPTH, n // ADA_TILE),
        in_specs=[
            pl.BlockSpec((8, D_MODEL), lambda l, j: (0, 0)),
            pl.BlockSpec((1, D_MODEL, ADA_TILE), lambda l, j: (l, 0, j)),
            pl.BlockSpec((1, 1, ADA_TILE), lambda l, j: (l, 0, j)),
        ],
        out_specs=pl.BlockSpec((1, 8, ADA_TILE), lambda l, j: (l, 0, j)),
        out_shape=jax.ShapeDtypeStruct((DEPTH, 8, n), F32),
        compiler_params=_params("arbitrary", "arbitrary"),
        name="ada_mods",
    )(cond8, w_ada, b_ada.reshape(DEPTH, 1, n))


def _rope_tables(t_lat):
    pos = np.arange(t_lat)
    row = (pos // GRID_W).astype(np.float64)
    col = (pos % GRID_W).astype(np.float64)

    def tables(kinds):
        cos = np.ones((t_lat, LANES))
        sa = np.zeros((t_lat, LANES))
        sb = np.zeros((t_lat, LANES))
        for lane, kind in enumerate(kinds):
            if kind is None:
                continue
            axis, half, j, upper = kind
            ang = (row if axis == 0 else col) * ROPE_BASE ** (-float(j) / half)
            cos[:, lane] = np.cos(ang)
            if upper:
                sb[:, lane] = np.sin(ang)
            else:
                sa[:, lane] = -np.sin(ang)
        return [cos, sa, sb]

    def rot_kinds(n):
        half = n // 4
        kinds = []
        for i in range(n):
            axis, r = divmod(i, n // 2)
            kinds.append((axis, half, r % half, r >= half))
        return kinds

    diff = rot_kinds(DH_DIFF) * 2
    mla_q = [None] * QK_NOPE + rot_kinds(QK_ROPE) + [None] * (LANES - QK_NOPE - QK_ROPE)
    mla_k = rot_kinds(QK_ROPE) + [None] * (LANES - QK_ROPE)
    chan = np.stack([t.T for t in tables(diff) + tables(mla_q)]).astype(np.float32)
    tok = np.stack(tables(diff) + tables(mla_k)).astype(np.float32)
    return jnp.asarray(chan), jnp.asarray(tok)


def _rope_tok(x, cos, sa, sb, shift):
    return x * cos + pltpu.roll(x, LANES - shift, 1) * sa + pltpu.roll(x, shift, 1) * sb


def _rope_tok_wide(x, cos, sa, sb, shift):
    n = x.shape[1] // LANES
    return jnp.concatenate(
        [_rope_tok(x[:, i * LANES:(i + 1) * LANES], cos, sa, sb, shift) for i in range(n)], axis=1)


def _rope_chan(x, cos, sa, sb, shift):
    up = jnp.concatenate([x[shift:], x[:shift]], axis=0)
    down = jnp.concatenate([x[-shift:], x[:-shift]], axis=0)
    return x * cos + up * sa + down * sb


def _rope_chan_wide(x, cos, sa, sb, shift):
    n = x.shape[0] // LANES
    return jnp.concatenate(
        [_rope_chan(x[i * LANES:(i + 1) * LANES], cos, sa, sb, shift) for i in range(n)], axis=0)


_T_QD, _T_VD, _T_CQ, _T_CKV = 0, DIFF_QK, DIFF_QK + DIFF_V, DIFF_QK + DIFF_V + Q_LORA
_T_ROWS = _T_CKV + KV_LORA
_R_KD, _R_CKV, _R_KPE, _R_VD = 0, DIFF_QK, DIFF_QK + KV_LORA, DIFF_QK + KV_LORA + LANES

DIFF_QSCALE = DH_DIFF ** -0.5 * LOG2E
MLA_QSCALE = (QK_NOPE + QK_ROPE) ** -0.5 * LOG2E


def _att_in_body(x_ref, mods_ref, wt_ref, wr_ref, qg_ref, kvg_row_ref, kvg_col_ref, wqt_ref, wk_ref, wvt_ref):
    mods = mods_ref[0, 0]
    sh, sc = mods[0:1], mods[1:2]
    h = (x_ref[...] * (1.0 + sc) + sh).astype(BF16)
    pt = _dot_nt(wt_ref[...], h)
    pr = _dot(h, wr_ref[...])
    qd_t = pt[_T_QD:_T_QD + DIFF_QK] * DIFF_QSCALE
    vd_t = pt[_T_VD:_T_VD + DIFF_V]
    cqn_t = _rms_norm(pt[_T_CQ:_T_CQ + Q_LORA], qg_ref[...], axis=0)
    qm_t = _dot(wqt_ref[...], cqn_t.astype(BF16)) * MLA_QSCALE
    ckvn_t = _rms_norm(pt[_T_CKV:_T_CKV + KV_LORA], kvg_col_ref[...], axis=0)
    vm_t = _dot(wvt_ref[...], ckvn_t.astype(BF16))
    k_d = pr[:, _R_KD:_R_KD + DIFF_QK]
    ckv_n = _rms_norm(pr[:, _R_CKV:_R_CKV + KV_LORA], kvg_row_ref[...])
    kpe = pr[:, _R_KPE:_R_KPE + LANES]
    k_nope = _dot(ckv_n.astype(BF16), wk_ref[...])
    return pr, qd_t, vd_t, qm_t, vm_t, k_d, ckv_n, kpe, k_nope


def _att_in_ctx_kernel(x_ref, mods_ref, wt_ref, wr_ref, qg_ref, kvg_row_ref, kvg_col_ref, wqt_ref, wk_ref,
                       ek_ref, wvt_ref,
                       qdt_ref, kd_ref, vdt_ref, qmt_ref, km_ref, vmt_ref,
                       kdf_ref, vdf_ref, ckvf_ref, kpef_ref):
    pr, qd_t, vd_t, qm_t, vm_t, k_d, ckv_n, kpe, k_nope = _att_in_body(
        x_ref, mods_ref, wt_ref, wr_ref, qg_ref, kvg_row_ref, kvg_col_ref, wqt_ref, wk_ref, wvt_ref)
    qdt_ref[...] = qd_t.astype(BF16)
    kd_ref[...] = k_d.astype(BF16)
    vdt_ref[0] = vd_t.astype(BF16)
    qmt_ref[...] = qm_t.astype(BF16)
    km_ref[...] = (k_nope + _dot(kpe.astype(BF16), ek_ref[...])).astype(BF16)
    vmt_ref[0] = vm_t.astype(BF16)
    kdf_ref[...] = k_d
    vdf_ref[...] = pr[:, _R_VD:_R_VD + DIFF_V]
    ckvf_ref[...] = ckv_n
    kpef_ref[...] = kpe[:, 0:QK_ROPE]


def _att_in_lat_kernel(x_ref, mods_ref, wt_ref, wr_ref, qg_ref, kvg_row_ref, kvg_col_ref, wqt_ref, wk_ref,
                       ek_ref, wvt_ref, tabc_ref, tabt_ref,
                       qdut_ref, qdrt_ref, kdr_ref, vdt_ref, qmut_ref, qmrt_ref, kmr_ref, vmt_ref):
    _, qd_t, vd_t, qm_t, vm_t, k_d, ckv_n, kpe, k_nope = _att_in_body(
        x_ref, mods_ref, wt_ref, wr_ref, qg_ref, kvg_row_ref, kvg_col_ref, wqt_ref, wk_ref, wvt_ref)
    qdut_ref[...] = qd_t.astype(BF16)
    qdrt_ref[...] = _rope_chan_wide(qd_t, tabc_ref[0], tabc_ref[1], tabc_ref[2], DH_DIFF // 4).astype(BF16)
    kdr_ref[...] = _rope_tok_wide(k_d, tabt_ref[0], tabt_ref[1], tabt_ref[2], DH_DIFF // 4).astype(BF16)
    vdt_ref[0] = vd_t.astype(BF16)
    qmut_ref[...] = qm_t.astype(BF16)
    qmrt_ref[...] = _rope_chan_wide(qm_t, tabc_ref[3], tabc_ref[4], tabc_ref[5], QK_ROPE // 4).astype(BF16)
    kpe_r = _rope_tok(kpe, tabt_ref[3], tabt_ref[4], tabt_ref[5], QK_ROPE // 4)
    kmr_ref[...] = (k_nope + _dot(kpe_r.astype(BF16), ek_ref[...])).astype(BF16)
    vmt_ref[0] = vm_t.astype(BF16)


def _att_in(x2d, mods, mod_row_fn, wts, tables, tm):
    t = x2d.shape[0]
    nt = t // tm
    tok = lambda n: pl.BlockSpec((tm, n), lambda i: (i, 0))
    chan = lambda n: pl.BlockSpec((n, tm), lambda i: (0, i))
    chunk = lambda n: pl.BlockSpec((1, n, tm), lambda i: (i, 0, 0))
    in_specs = [tok(D_MODEL), pl.BlockSpec((1, 1, 6, D_MODEL), lambda i: (0, mod_row_fn(i), 0, 0))]
    in_specs += [_const_spec(w.shape) for w in wts]
    s_tok = lambda n, dt=BF16: jax.ShapeDtypeStruct((t, n), dt)
    s_chan = lambda n: jax.ShapeDtypeStruct((n, t), BF16)
    s_chunk = lambda n: jax.ShapeDtypeStruct((nt, n, tm), BF16)
    if tables is None:
        out_shape = [s_chan(DIFF_QK), s_tok(DIFF_QK), s_chunk(DIFF_V), s_chan(MLA_WIDE), s_tok(MLA_WIDE),
                     s_chunk(H_MLA * V_MLA),
                     s_tok(DIFF_QK, F32), s_tok(DIFF_V, F32), s_tok(KV_LORA, F32), s_tok(QK_ROPE, F32)]
        out_specs = [chan(DIFF_QK), tok(DIFF_QK), chunk(DIFF_V), chan(MLA_WIDE), tok(MLA_WIDE),
                     chunk(H_MLA * V_MLA), tok(DIFF_QK), tok(DIFF_V), tok(KV_LORA), tok(QK_ROPE)]
        kern, args, name = _att_in_ctx_kernel, (), "att_in_ctx"
    else:
        tab_c, tab_t = tables
        t_lat = tab_t.shape[1]
        in_specs += [pl.BlockSpec((6, LANES, tm), lambda i: (0, 0, i % (t_lat // tm))),
                     pl.BlockSpec((6, tm, LANES), lambda i: (0, i % (t_lat // tm), 0))]
        out_shape = [s_chan(DIFF_QK), s_chan(DIFF_QK), s_tok(DIFF_QK), s_chunk(DIFF_V),
                     s_chan(MLA_WIDE), s_chan(MLA_WIDE), s_tok(MLA_WIDE), s_chunk(H_MLA * V_MLA)]
        out_specs = [chan(DIFF_QK), chan(DIFF_QK), tok(DIFF_QK), chunk(DIFF_V),
                     chan(MLA_WIDE), chan(MLA_WIDE), tok(MLA_WIDE), chunk(H_MLA * V_MLA)]
        kern, args, name = _att_in_lat_kernel, (tab_c, tab_t), "att_in_lat"
    return pl.pallas_call(
        kern, grid=(nt,), in_specs=in_specs, out_specs=out_specs, out_shape=out_shape,
        compiler_params=_params("arbitrary"), name=name,
    )(x2d, mods, *wts, *args)


def _cache_kv_kernel(ckv_ref, kpe_ref, wk_ref, ek_ref, wvt_ref, km_ref, vmt_ref):
    ckv_b = ckv_ref[...].astype(BF16)
    km_ref[...] = (_dot(ckv_b, wk_ref[...]) + _dot(kpe_ref[...].astype(BF16), ek_ref[...])).astype(BF16)
    past = vmt_ref.shape[2]
    vt = _dot_nt(wvt_ref[...], ckv_b)
    for b in range(vmt_ref.shape[0]):
        vmt_ref[b] = vt[:, b * past:(b + 1) * past].astype(BF16)


def _cache_kv(ckv2d, kpe_slab2d, wk, ek, wvt, bl):
    t = ckv2d.shape[0]
    full = lambda a: pl.BlockSpec(a.shape, lambda i: (0,) * a.ndim)
    return pl.pallas_call(
        _cache_kv_kernel, grid=(1,),
        in_specs=[full(ckv2d), full(kpe_slab2d), full(wk), full(ek), full(wvt)],
        out_specs=[pl.BlockSpec((t, MLA_WIDE), lambda i: (0, 0)),
                   pl.BlockSpec((bl, H_MLA * V_MLA, t // bl), lambda i: (0, 0, 0))],
        out_shape=[jax.ShapeDtypeStruct((t, MLA_WIDE), BF16),
                   jax.ShapeDtypeStruct((bl, H_MLA * V_MLA, t // bl), BF16)],
        compiler_params=_params("arbitrary"), name="cache_kv",
    )(ckv2d, kpe_slab2d, wk, ek, wvt)


def _softmax_pv(s, vt, state):
    m_c = jnp.max(s, axis=0, keepdims=True)
    if state is None:
        p = jnp.exp2(s - m_c)
        return m_c, jnp.sum(p, axis=0, keepdims=True), _dot(vt, p.astype(BF16))
    m, l, acc = state
    m_new = jnp.maximum(m, m_c)
    alpha = jnp.exp2(m - m_new)
    p = jnp.exp2(s - m_new)
    l = alpha * l + jnp.sum(p, axis=0, keepdims=True)
    return m_new, l, alpha * acc + _dot(vt, p.astype(BF16))


def _attend(first, rest):
    items = [(si, qt, (lambda k=k: k), (lambda vt=vt: vt)) for si, (qt, k, vt) in enumerate(first)]
    if rest is not None:
        for c in range(rest[0][2].shape[0]):
            for si, (qt, k_ref, vt_ref) in enumerate(rest):
                items.append((si, qt, (lambda r=k_ref, c=c: r[c * KEY_CHUNK:(c + 1) * KEY_CHUNK, :]),
                              (lambda r=vt_ref, c=c: r[c])))
    ns = len(first)
    groups = [items[i:i + ns] for i in range(0, len(items), ns)]
    scores = lambda grp: [_dot(k_fn(), qt) for (_, qt, k_fn, _) in grp]
    state = [None] * ns
    s_next = scores(groups[0])
    for g, grp in enumerate(groups):
        s_cur = s_next
        if g + 1 < len(groups):
            s_next = scores(groups[g + 1])
        for s, (si, _, _, vt_fn) in zip(s_cur, grp):
            state[si] = _softmax_pv(s, vt_fn(), state[si])
    return [acc / l for (_, l, acc) in state]


def _split_maps(qt):
    row = lax.broadcasted_iota(jnp.int32, qt.shape, 0)
    zero = jnp.zeros_like(qt)
    return jnp.where(row < DH_DIFF, qt, zero), jnp.where(row >= DH_DIFF, qt, zero)


def _lambda(lam_ref, lam_init):
    lv = lam_ref[...]
    a = jnp.sum(lv[0:1] * lv[1:2], axis=-1, keepdims=True)
    b = jnp.sum(lv[2:3] * lv[3:4], axis=-1, keepdims=True)
    return jnp.exp(a) - jnp.exp(b) + lam_init


def _diff_finish(o1_t, o2_t, lam_ref, g_ref, lam_init, out_ref):
    o = (o1_t - _lambda(lam_ref, lam_init) * o2_t).T
    out_ref[...] = (_rms_norm(o, g_ref[...]) * (1.0 - lam_init)).astype(out_ref.dtype)


def _diff_ctx_kernel(qt_ref, k_ref, vt_ref, lam_ref, g_ref, out_ref, *, lam_init):
    k, vt = k_ref[...], vt_ref[0]
    o1, o2 = _attend([(q, k, vt) for q in _split_maps(qt_ref[...])], None)
    _diff_finish(o1, o2, lam_ref, g_ref, lam_init, out_ref)


def _diff_lat_kernel(qut_ref, qrt_ref, kc_ref, vc_ref, k_ref, vt_ref, lam_ref, g_ref, out_ref, *, lam_init):
    kc, vct = kc_ref[...].astype(BF16), vc_ref[...].T.astype(BF16)
    first = [(q, kc, vct) for q in _split_maps(qut_ref[...])]
    rest = [(q, k_ref, vt_ref) for q in _split_maps(qrt_ref[...])]
    o1, o2 = _attend(first, rest)
    _diff_finish(o1, o2, lam_ref, g_ref, lam_init, out_ref)


def _mla_finish(oa_t, ob_t, out_ref):
    row = lax.broadcasted_iota(jnp.int32, oa_t.shape, 0)
    out_ref[...] = jnp.where(row < V_MLA, oa_t, ob_t).T.astype(out_ref.dtype)


def _mla_ctx_kernel(qa_ref, qb_ref, ka_ref, kb_ref, vt_ref, out_ref):
    vt = vt_ref[0]
    oa, ob = _attend([(qa_ref[...], ka_ref[...], vt), (qb_ref[...], kb_ref[...], vt)], None)
    _mla_finish(oa, ob, out_ref)


def _mla_lat_kernel(qua_ref, qub_ref, qra_ref, qrb_ref, kca_ref, kcb_ref, vct_ref, ka_ref, kb_ref, vt_ref,
                    out_ref):
    vct = vct_ref[0]
    first = [(qua_ref[...], kca_ref[...], vct), (qub_ref[...], kcb_ref[...], vct)]
    rest = [(qra_ref[...], ka_ref, vt_ref), (qrb_ref[...], kb_ref, vt_ref)]
    oa, ob = _attend(first, rest)
    _mla_finish(oa, ob, out_ref)


def _attention_ctx(streams, lam4, diff_g, lam_init, b, t):
    qd_t, kd, vd_t, qm_t, km, vm_t = streams
    per_chunk = vd_t.shape[2] // t
    sem = _params("arbitrary", "arbitrary")
    qt = lambda f: pl.BlockSpec((LANES, t), lambda i, h: (f(h), i))
    keys = lambda f: pl.BlockSpec((t, LANES), lambda i, h: (i, f(h)))
    vals = pl.BlockSpec((1, LANES, t), lambda i, h: (i // per_chunk, h, i % per_chunk))
    out = pl.BlockSpec((t, LANES), lambda i, h: (i, h))
    same, even, odd = (lambda h: h), (lambda h: 2 * h), (lambda h: 2 * h + 1)
    o_d = pl.pallas_call(
        functools.partial(_diff_ctx_kernel, lam_init=lam_init),
        grid=(b, H_DIFF),
        in_specs=[qt(same), keys(same), vals,
                  pl.BlockSpec(lam4.shape, lambda i, h: (0, 0)),
                  pl.BlockSpec(diff_g.shape, lambda i, h: (0, 0))],
        out_specs=out,
        out_shape=jax.ShapeDtypeStruct((b * t, DIFF_V), BF16),
        compiler_params=sem, name="diff_attn_ctx",
    )(qd_t, kd, vd_t, lam4, diff_g)
    o_m = pl.pallas_call(
        _mla_ctx_kernel,
        grid=(b, H_MLA // 2),
        in_specs=[qt(even), qt(odd), keys(even), keys(odd), vals],
        out_specs=out,
        out_shape=jax.ShapeDtypeStruct((b * t, H_MLA * V_MLA), BF16),
        compiler_params=sem, name="mla_attn_ctx",
    )(qm_t, qm_t, km, km, vm_t)
    return o_d, o_m


def _attention_lat(streams, cache, lam4, diff_g, lam_init, b, t):
    qdu_t, qdr_t, kdr, vd_t, qmu_t, qmr_t, kmr, vm_t = streams
    kd_c, vd_c, km_c, vmc_t = cache
    past = kd_c.shape[0] // b
    nq = t // Q_TILE
    n_chunks = t // KEY_CHUNK
    sem = _params("arbitrary", "arbitrary", "arbitrary")
    qt = lambda f: pl.BlockSpec((LANES, Q_TILE), lambda i, h, j: (f(h), i * nq + j))
    keys = lambda n, f: pl.BlockSpec((n, LANES), lambda i, h, j: (i, f(h)))
    vals = pl.BlockSpec((n_chunks, LANES, KEY_CHUNK), lambda i, h, j: (i, h, 0))
    out = pl.BlockSpec((Q_TILE, LANES), lambda i, h, j: (i * nq + j, h))
    same, even, odd = (lambda h: h), (lambda h: 2 * h), (lambda h: 2 * h + 1)
    o_d = pl.pallas_call(
        functools.partial(_diff_lat_kernel, lam_init=lam_init),
        grid=(b, H_DIFF, nq),
        in_specs=[qt(same), qt(same), keys(past, same), keys(past, same), keys(t, same), vals,
                  pl.BlockSpec(lam4.shape, lambda i, h, j: (0, 0)),
                  pl.BlockSpec(diff_g.shape, lambda i, h, j: (0, 0))],
        out_specs=out,
        out_shape=jax.ShapeDtypeStruct((b * t, DIFF_V), BF16),
        compiler_params=sem, name="diff_attn_lat",
    )(qdu_t, qdr_t, kd_c, vd_c, kdr, vd_t, lam4, diff_g)
    o_m = pl.pallas_call(
        _mla_lat_kernel,
        grid=(b, H_MLA // 2, nq),
        in_specs=[qt(even), qt(odd), qt(even), qt(odd), keys(past, even), keys(past, odd),
                  pl.BlockSpec((1, LANES, past), lambda i, h, j: (i, h, 0)),
                  keys(t, even), keys(t, odd), vals],
        out_specs=out,
        out_shape=jax.ShapeDtypeStruct((b * t, H_MLA * V_MLA), BF16),
        compiler_params=sem, name="mla_attn_lat",
    )(qmu_t, qmu_t, qmr_t, qmr_t, km_c, km_c, vmc_t, kmr, kmr, vm_t)
    return o_d, o_m


def _post_kernel(x_ref, ma_ref, mb_ref, mods_ref, wo_ref, w1_ref, w2_ref, ln_ref, out_ref):
    mods = mods_ref[0, 0]
    g_m, sh_f, sc_f, g_f = mods[2:3], mods[3:4], mods[4:5], mods[5:6]
    ln = ln_ref[...]
    mix = jnp.concatenate([ma_ref[...], mb_ref[...]], axis=1)
    y = _dot(mix, wo_ref[...])
    x1 = _layer_norm(DEEPNORM_ALPHA * x_ref[...] + g_m * y, ln[0:1], ln[1:2])
    h = (x1 * (1.0 + sc_f) + sh_f).astype(BF16)
    f = jnp.zeros(x1.shape, F32)
    for c in range(D_FF // FF_CHUNK):
        a = jnp.maximum(_dot(h, w1_ref[:, c * FF_CHUNK:(c + 1) * FF_CHUNK]), 0.0)
        f = f + _dot((a * a).astype(BF16), w2_ref[c * FF_CHUNK:(c + 1) * FF_CHUNK, :])
    out_ref[...] = _layer_norm(DEEPNORM_ALPHA * x1 + g_f * f, ln[2:3], ln[3:4])


def _post(x2d, mix_a, mix_b, mods, layer, mod_row_fn, wo, w1, w2, ln4, tm):
    t = x2d.shape[0]
    tok = lambda n: pl.BlockSpec((tm, n), lambda i: (i, 0))
    return pl.pallas_call(
        _post_kernel, grid=(t // tm,),
        in_specs=[tok(D_MODEL), tok(mix_a.shape[1]), tok(mix_b.shape[1]),
                  pl.BlockSpec((1, 1, 6, D_MODEL), lambda i: (layer, mod_row_fn(i), 0, 0)),
                  _const_spec(wo.shape), _const_spec(w1.shape), _const_spec(w2.shape),
                  _const_spec(ln4.shape)],
        out_specs=tok(D_MODEL),
        out_shape=jax.ShapeDtypeStruct((t, D_MODEL), F32),
        compiler_params=_params("arbitrary"), name=f"post_mlp_l{layer}",
    )(x2d, mix_a, mix_b, mods, wo, w1, w2, ln4)


def _conv_in_kernel(x_ref, mods_ref, w_ref, u_ref, pz_ref):
    mods = mods_ref[0, 0]
    sh, sc = mods[0:1], mods[1:2]
    h = (x_ref[...] * (1.0 + sc) + sh).astype(BF16)
    proj = _dot(h, w_ref[...])
    a, gate = proj[:, 0:CONV_CH], proj[:, CONV_CH:2 * CONV_CH]
    u_ref[...] = a * jax.nn.sigmoid(gate)
    pz_ref[...] = proj[:, 2 * CONV_CH:]


def _conv_in(x2d, mods, layer, mod_row_fn, w, tm):
    t = x2d.shape[0]
    tok = lambda n: pl.BlockSpec((tm, n), lambda i: (i, 0))
    return pl.pallas_call(
        _conv_in_kernel, grid=(t // tm,),
        in_specs=[tok(D_MODEL),
                  pl.BlockSpec((1, 1, 6, D_MODEL), lambda i: (layer, mod_row_fn(i), 0, 0)),
                  _const_spec(w.shape)],
        out_specs=[tok(CONV_CH), tok(POOL_CH)],
        out_shape=[jax.ShapeDtypeStruct((t, CONV_CH), F32), jax.ShapeDtypeStruct((t, POOL_CH), F32)],
        compiler_params=_params("arbitrary"), name="conv_in",
    )(x2d, mods, w)


def _conv_pool_kernel(*refs, n_tiles, seq_len):
    if n_tiles > 1:
        (u_ref, ul_ref, ur_ref, pz_ref, pl_ref, pr_ref, cw_ref, cp_ref, wp_ref, ps_ref,
         uo_ref, do_ref, ubuf, pbuf) = refs
    else:
        u_ref, pz_ref, cw_ref, cp_ref, wp_ref, ps_ref, uo_ref, do_ref, ubuf, pbuf = refs
    tm = u_ref.shape[1]
    j = pl.program_id(1)
    halo_zero = jnp.zeros((CONV_HALO, CONV_CH), F32)
    for buf, mid, sides in ((ubuf, u_ref, (ul_ref, ur_ref) if n_tiles > 1 else None),
                            (pbuf, pz_ref, (pl_ref, pr_ref) if n_tiles > 1 else None)):
        buf[CONV_HALO:CONV_HALO + tm, :] = mid[0]
        if sides is None:
            buf[0:CONV_HALO, :] = halo_zero
            buf[CONV_HALO + tm:, :] = halo_zero
        else:
            buf[0:CONV_HALO, :] = jnp.where(j > 0, sides[0][0], halo_zero)
            buf[CONV_HALO + tm:, :] = jnp.where(j < n_tiles - 1, sides[1][0], halo_zero)

    acc = jnp.zeros((tm, CONV_CH), F32)
    base = CONV_HALO - CONV_WIDTH // 2
    for k in range(CONV_WIDTH):
        acc = acc + ubuf[base + k:base + k + tm, :] * cw_ref[k:k + 1, :]
    cp = cp_ref[...]
    z = _layer_norm(acc + cp[0:1], cp[1:2], cp[2:3])
    uo_ref[0] = (z * jax.nn.sigmoid(z)).astype(uo_ref.dtype)

    pos = j * tm + lax.broadcasted_iota(jnp.int32, (tm, 1), 0)
    outs = []
    for g, w in enumerate(POOL_WINDOWS):
        cols = slice(g * POOL_GC, (g + 1) * POOL_GC)
        s = jnp.zeros((tm, POOL_GC), F32)
        for d in range(-(w // 2), w // 2):
            s = s + pbuf[CONV_HALO + d:CONV_HALO + d + tm, cols]
        lo = jnp.maximum(pos - w // 2, 0)
        hi = jnp.minimum(pos + w // 2 - 1, seq_len - 1)
        cnt = (hi - lo + 1).astype(F32)
        dgrp = s / cnt - pz_ref[0][:, cols]
        outs.append(_dot(dgrp.astype(BF16), wp_ref[g]))
    do_ref[0] = (jnp.concatenate(outs, axis=1) * ps_ref[...]).astype(do_ref.dtype)


def _conv_pool(u2d, pz2d, b, t, tm, cw, cp, wp, ps):
    n_tiles = t // tm
    u3, p3 = u2d.reshape(b, t, CONV_CH), pz2d.reshape(b, t, POOL_CH)
    mid = pl.BlockSpec((1, tm, CONV_CH), lambda i, j: (i, j, 0))
    r = tm // CONV_HALO
    left = pl.BlockSpec((1, CONV_HALO, CONV_CH), lambda i, j: (i, jnp.maximum(j * r - 1, 0), 0))
    right = pl.BlockSpec((1, CONV_HALO, CONV_CH),
                         lambda i, j: (i, jnp.minimum((j + 1) * r, t // CONV_HALO - 1), 0))
    const = lambda a: pl.BlockSpec(a.shape, lambda i, j: (0,) * a.ndim)
    if n_tiles > 1:
        in_specs = [mid, left, right, mid, left, right]
        args = (u3, u3, u3, p3, p3, p3)
    else:
        in_specs = [mid, mid]
        args = (u3, p3)
    in_specs += [const(cw), const(cp), const(wp), const(ps)]
    uo, do = pl.pallas_call(
        functools.partial(_conv_pool_kernel, n_tiles=n_tiles, seq_len=t),
        grid=(b, n_tiles), in_specs=in_specs,
        out_specs=[mid, mid],
        out_shape=[jax.ShapeDtypeStruct((b, t, CONV_CH), BF16), jax.ShapeDtypeStruct((b, t, POOL_CH), BF16)],
        scratch_shapes=[pltpu.VMEM((tm + 2 * CONV_HALO, CONV_CH), F32),
                        pltpu.VMEM((tm + 2 * CONV_HALO, POOL_CH), F32)],
        compiler_params=_params("arbitrary", "arbitrary"),
        name=f"conv_pool_{'lat' if n_tiles > 1 else 'ctx'}",
    )(*args, cw, cp, wp, ps)
    return uo.reshape(b * t, CONV_CH), do.reshape(b * t, POOL_CH)


def _att_in_weights(w_att_in, with_v_tok):
    o = 2 * DIFF_QK + DIFF_V
    q_d, k_d, v_d = w_att_in[:, :DIFF_QK], w_att_in[:, DIFF_QK:2 * DIFF_QK], w_att_in[:, 2 * DIFF_QK:o]
    cq, ckv = w_att_in[:, o:o + Q_LORA], w_att_in[:, o + Q_LORA:o + Q_LORA + KV_LORA]
    kpe = jnp.pad(w_att_in[:, o + Q_LORA + KV_LORA:], ((0, 0), (0, LANES - QK_ROPE)))
    w_t = jnp.concatenate([q_d, v_d, cq, ckv], axis=1).T.astype(BF16)
    w_r = jnp.concatenate([k_d, ckv, kpe] + ([v_d] if with_v_tok else []), axis=1).astype(BF16)
    return w_t, w_r


def _mla_weights(w_uq, w_ukv):
    wq = w_uq.reshape(Q_LORA, H_MLA, QK_NOPE + QK_ROPE)
    wq = jnp.pad(wq, ((0, 0), (0, 0), (0, MLA_SLAB - QK_NOPE - QK_ROPE))).reshape(Q_LORA, MLA_WIDE)
    wkv = w_ukv.reshape(KV_LORA, H_MLA, QK_NOPE + V_MLA)
    wk = jnp.pad(wkv[:, :, :QK_NOPE], ((0, 0), (0, 0), (0, MLA_SLAB - QK_NOPE))).reshape(KV_LORA, MLA_WIDE)
    wv = wkv[:, :, QK_NOPE:].reshape(KV_LORA, H_MLA * V_MLA)
    place = np.zeros((LANES, H_MLA, MLA_SLAB), np.float32)
    for h in range(H_MLA):
        for r in range(QK_ROPE):
            place[r, h, QK_NOPE + r] = 1.0
    ek = jnp.asarray(place.reshape(LANES, MLA_WIDE))
    return wq.T.astype(BF16), wk.astype(BF16), ek.astype(BF16), wv.T.astype(BF16)


def kernel(x_prompt, x_sample, cache_diff_k, cache_diff_v, cache_mla_ckv, cache_mla_krope, c, c_ctx,
           w_ada, b_ada, ln_mix_g, ln_mix_b, ln_mlp_g, ln_mlp_b, w_mlp_in, w_mlp_out,
           w_att_in, w_uq, w_ukv, q_norm_g, kv_norm_g, lam_q1, lam_k1, lam_q2, lam_k2, diff_norm_g, w_att_out,
           w_conv_in, conv_w, conv_b, conv_norm_g, conv_norm_b, w_pool, pool_scale, w_conv_out):
    bc, tc, d = x_prompt.shape
    bl, tl, _ = x_sample.shape
    past = cache_diff_k.shape[2]

    cond8 = jnp.zeros((8, d), F32).at[0].set(c_ctx).at[1:1 + bl].set(c)
    mods = _ada_mods(cond8, w_ada.astype(BF16), b_ada).reshape(DEPTH, 8, 6, d)
    ctx_row = lambda i: 0
    lat_row = lambda i: 1 + i // (tl // TOKEN_TILE)

    wqt, wk, ek, wvt = _mla_weights(w_uq[0], w_ukv[0])
    norm_w = (q_norm_g[0].reshape(Q_LORA, 1), kv_norm_g[0].reshape(1, KV_LORA), kv_norm_g[0].reshape(KV_LORA, 1))
    att_w = lambda with_v: _att_in_weights(w_att_in[0], with_v) + norm_w + (wqt, wk, ek, wvt)
    lam4 = jnp.stack([lam_q1[0], lam_k1[0], lam_q2[0], lam_k2[0]])
    diff_g = diff_norm_g[0].reshape(1, 2 * DH_DIFF)
    lam_init = 0.8 - 0.6 * math.exp(-0.3 * 0)
    ln4 = [jnp.stack([ln_mix_g[l], ln_mix_b[l], ln_mlp_g[l], ln_mlp_b[l]]) for l in range(DEPTH)]
    w1 = [w_mlp_in[l].astype(BF16) for l in range(DEPTH)]
    w2 = [w_mlp_out[l].astype(BF16) for l in range(DEPTH)]
    w_ao = w_att_out[0].astype(BF16)
    w_ci = w_conv_in[0].astype(BF16)
    w_co = w_conv_out[0].astype(BF16)
    conv_p = jnp.stack([conv_b[0], conv_norm_g[0], conv_norm_b[0]])
    wp = w_pool[0].astype(BF16)
    ps = pool_scale[0].reshape(1, POOL_CH)

    xp = x_prompt.reshape(bc * tc, d)
    outs = _att_in(xp, mods, ctx_row, att_w(True), None, TOKEN_TILE)
    kd_f, vd_f, ckv_f, kpe_f = outs[6:]
    o_d, o_m = _attention_ctx(outs[:6], lam4, diff_g, lam_init, bc, tc)
    xp = _post(xp, o_d, o_m, mods, 0, ctx_row, w_ao, w1[0], w2[0], ln4[0], TOKEN_TILE)
    u, pz = _conv_in(xp, mods, 1, ctx_row, w_ci, TOKEN_TILE)
    uo, do = _conv_pool(u, pz, bc, tc, tc, conv_w[0], conv_p, wp, ps)
    xp = _post(xp, uo, do, mods, 1, ctx_row, w_co, w1[1], w2[1], ln4[1], TOKEN_TILE)

    xs = x_sample.reshape(bl * tl, d)
    outs = _att_in(xs, mods, lat_row, att_w(False), _rope_tables(tl), TOKEN_TILE)
    km_c, vmc_t = _cache_kv(cache_mla_ckv[:, 0].reshape(bl * past, KV_LORA),
                            jnp.pad(cache_mla_krope[:, 0].reshape(bl * past, QK_ROPE),
                                    ((0, 0), (0, LANES - QK_ROPE))), wk, ek, wvt, bl)
    cache = (cache_diff_k[:, 0].reshape(bl * past, DIFF_QK), cache_diff_v[:, 0].reshape(bl * past, DIFF_V),
             km_c, vmc_t)
    o_d, o_m = _attention_lat(outs, cache, lam4, diff_g, lam_init, bl, tl)
    xs = _post(xs, o_d, o_m, mods, 0, lat_row, w_ao, w1[0], w2[0], ln4[0], TOKEN_TILE)
    u, pz = _conv_in(xs, mods, 1, lat_row, w_ci, TOKEN_TILE)
    uo, do = _conv_pool(u, pz, bl, tl, CONV_TILE, conv_w[0], conv_p, wp, ps)
    xs = _post(xs, uo, do, mods, 1, lat_row, w_co, w1[1], w2[1], ln4[1], TOKEN_TILE)

    return (xp.reshape(bc, tc, d), xs.reshape(bl, tl, d),
            kd_f.reshape(bc, 1, tc, H_DIFF, 2 * DH_DIFF), vd_f.reshape(bc, 1, tc, H_DIFF, 2 * DH_DIFF),
            ckv_f.reshape(bc, 1, tc, KV_LORA), kpe_f.reshape(bc, 1, tc, QK_ROPE))
```

```python
import functools
import math

import jax
import jax.numpy as jnp
import numpy as np
from jax import lax
from jax.experimental import pallas as pl
from jax.experimental.pallas import tpu as pltpu

D_MODEL = 1024
DEPTH = 2
GRID_W = 64
H_DIFF = 4
DH_DIFF = 64
DIFF_QK = H_DIFF * 2 * DH_DIFF
DIFF_V = H_DIFF * 2 * DH_DIFF
H_MLA = 8
Q_LORA = 256
KV_LORA = 128
QK_NOPE = 64
QK_ROPE = 32
V_MLA = 64
ATT_IN = 2 * DIFF_QK + DIFF_V + Q_LORA + KV_LORA + QK_ROPE
CONV_CH = 512
CONV_WIDTH = 31
POOL_CH = 512
POOL_WINDOWS = (2, 4, 8, 16)
POOL_GC = POOL_CH // len(POOL_WINDOWS)
CONV_IN = 2 * CONV_CH + POOL_CH
D_FF = 4 * D_MODEL
ROPE_BASE = 10000.0
NORM_EPS = 1e-5
DEEPNORM_ALPHA = (2 * DEPTH) ** 0.25
LOG2E = math.log2(math.e)

LANES = 128
SUBLANES = 8
MLA_SLAB = LANES
MLA_WIDE = H_MLA * MLA_SLAB
VMEM_LIMIT = 56 * 1024 * 1024

TOKEN_TILE = 512
KEY_CHUNK = TOKEN_TILE
Q_TILE = 512
CONV_TILE = 512
CONV_HALO = 16
CONV_STRIP = 32
FF_CHUNK = 1024

F32 = jnp.float32
BF16 = jnp.bfloat16


def _dot(a, b):
    return jnp.dot(a, b, preferred_element_type=F32)


def _dot_nt(a, b):
    return lax.dot_general(a, b, (((1,), (1,)), ((), ())), preferred_element_type=F32)


def _layer_norm(x, g, b):
    mu = jnp.mean(x, axis=-1, keepdims=True)
    xc = x - mu
    var = jnp.mean(xc * xc, axis=-1, keepdims=True)
    return xc * lax.rsqrt(var + NORM_EPS) * g + b


def _rms_norm(x, g, axis=-1):
    ms = jnp.mean(x * x, axis=axis, keepdims=True)
    return x * lax.rsqrt(ms + NORM_EPS) * g


def _const_spec(shape):
    nd = len(shape)
    return pl.BlockSpec(shape, lambda *_: (0,) * nd, pipeline_mode=pl.Buffered(1))


def _params(*sem):
    return pltpu.CompilerParams(dimension_semantics=sem, vmem_limit_bytes=VMEM_LIMIT)


ADA_TILE = 1536


def _ada_kernel(cond_ref, w_ref, b_ref, out_ref):
    cond = cond_ref[...]
    act = (cond * jax.nn.sigmoid(cond)).astype(BF16)
    out_ref[0] = _dot(act, w_ref[0]) + b_ref[0]


def _ada_mods(cond8, w_ada, b_ada):
    n = 6 * D_MODEL
    return pl.pallas_call(
        _ada_kernel,
        grid=(DEPTH, n // ADA_TILE),
        in_specs=[
            pl.BlockSpec((8, D_MODEL), lambda l, j: (0, 0)),
            pl.BlockSpec((1, D_MODEL, ADA_TILE), lambda l, j: (l, 0, j)),
            pl.BlockSpec((1, 1, ADA_TILE), lambda l, j: (l, 0, j)),
        ],
        out_specs=pl.BlockSpec((1, 8, ADA_TILE), lambda l, j: (l, 0, j)),
        out_shape=jax.ShapeDtypeStruct((DEPTH, 8, n), F32),
        compiler_params=_params("arbitrary", "arbitrary"),
        name="ada_mods",
    )(cond8, w_ada, b_ada.reshape(DEPTH, 1, n))


def _rope_tables(t_lat):
    pos = np.arange(t_lat)
    row = (pos // GRID_W).astype(np.float64)
    col = (pos % GRID_W).astype(np.float64)

    def tables(kinds):
        cos = np.ones((t_lat, LANES))
        sa = np.zeros((t_lat, LANES))
        sb = np.zeros((t_lat, LANES))
        for lane, kind in enumerate(kinds):
            if kind is None:
                continue
            axis, half, j, upper = kind
            ang = (row if axis == 0 else col) * ROPE_BASE ** (-float(j) / half)
            cos[:, lane] = np.cos(ang)
            if upper:
                sb[:, lane] = np.sin(ang)
            else:
                sa[:, lane] = -np.sin(ang)
        return [cos, sa, sb]

    def rot_kinds(n):
        half = n // 4
        kinds = []
        for i in range(n):
            axis, r = divmod(i, n // 2)
            kinds.append((axis, half, r % half, r >= half))
        return kinds

    diff = rot_kinds(DH_DIFF) * 2
    mla_q = [None] * QK_NOPE + rot_kinds(QK_ROPE) + [None] * (LANES - QK_NOPE - QK_ROPE)
    mla_k = rot_kinds(QK_ROPE) + [None] * (LANES - QK_ROPE)
    chan = np.stack([t.T for t in tables(diff) + tables(mla_q)]).astype(np.float32)
    tok = np.stack(tables(diff) + tables(mla_k)).astype(np.float32)
    return jnp.asarray(chan), jnp.asarray(tok)


def _rope_tok(x, cos, sa, sb, shift):
    return x * cos + pltpu.roll(x, LANES - shift, 1) * sa + pltpu.roll(x, shift, 1) * sb


def _rope_tok_wide(x, cos, sa, sb, shift):
    n = x.shape[1] // LANES
    return jnp.concatenate(
        [_rope_tok(x[:, i * LANES:(i + 1) * LANES], cos, sa, sb, shift) for i in range(n)], axis=1)


def _rope_chan(x, cos, sa, sb, shift):
    up = jnp.concatenate([x[shift:], x[:shift]], axis=0)
    down = jnp.concatenate([x[-shift:], x[:-shift]], axis=0)
    return x * cos + up * sa + down * sb


def _rope_chan_wide(x, cos, sa, sb, shift):
    n = x.shape[0] // LANES
    return jnp.concatenate(
        [_rope_chan(x[i * LANES:(i + 1) * LANES], cos, sa, sb, shift) for i in range(n)], axis=0)


_T_QD, _T_VD, _T_CQ, _T_CKV = 0, DIFF_QK, DIFF_QK + DIFF_V, DIFF_QK + DIFF_V + Q_LORA
_T_ROWS = _T_CKV + KV_LORA
_R_KD, _R_CKV, _R_KPE, _R_VD = 0, DIFF_QK, DIFF_QK + KV_LORA, DIFF_QK + KV_LORA + LANES

DIFF_QSCALE = DH_DIFF ** -0.5 * LOG2E
MLA_QSCALE = (QK_NOPE + QK_ROPE) ** -0.5 * LOG2E


def _att_in_body(x_ref, mods_ref, wt_ref, wr_ref, qg_ref, kvg_row_ref, kvg_col_ref, wqt_ref, wk_ref, wvt_ref):
    mods = mods_ref[0, 0]
    sh, sc = mods[0:1], mods[1:2]
    h = (x_ref[...] * (1.0 + sc) + sh).astype(BF16)
    pt = _dot_nt(wt_ref[...], h)
    pr = _dot(h, wr_ref[...])
    qd_t = pt[_T_QD:_T_QD + DIFF_QK] * DIFF_QSCALE
    vd_t = pt[_T_VD:_T_VD + DIFF_V]
    cqn_t = _rms_norm(pt[_T_CQ:_T_CQ + Q_LORA], qg_ref[...], axis=0)
    qm_t = _dot(wqt_ref[...], cqn_t.astype(BF16)) * MLA_QSCALE
    ckvn_t = _rms_norm(pt[_T_CKV:_T_CKV + KV_LORA], kvg_col_ref[...], axis=0)
    vm_t = _dot(wvt_ref[...], ckvn_t.astype(BF16))
    k_d = pr[:, _R_KD:_R_KD + DIFF_QK]
    ckv_n = _rms_norm(pr[:, _R_CKV:_R_CKV + KV_LORA], kvg_row_ref[...])
    kpe = pr[:, _R_KPE:_R_KPE + LANES]
    k_nope = _dot(ckv_n.astype(BF16), wk_ref[...])
    return pr, qd_t, vd_t, qm_t, vm_t, k_d, ckv_n, kpe, k_nope


def _att_in_ctx_kernel(x_ref, mods_ref, wt_ref, wr_ref, qg_ref, kvg_row_ref, kvg_col_ref, wqt_ref, wk_ref,
                       ek_ref, wvt_ref,
                       qdt_ref, kd_ref, vdt_ref, qmt_ref, km_ref, vmt_ref,
                       kdf_ref, vdf_ref, ckvf_ref, kpef_ref):
    pr, qd_t, vd_t, qm_t, vm_t, k_d, ckv_n, kpe, k_nope = _att_in_body(
        x_ref, mods_ref, wt_ref, wr_ref, qg_ref, kvg_row_ref, kvg_col_ref, wqt_ref, wk_ref, wvt_ref)
    qdt_ref[...] = qd_t.astype(BF16)
    kd_ref[...] = k_d.astype(BF16)
    vdt_ref[0] = vd_t.astype(BF16)
    qmt_ref[...] = qm_t.astype(BF16)
    km_ref[...] = (k_nope + _dot(kpe.astype(BF16), ek_ref[...])).astype(BF16)
    vmt_ref[0] = vm_t.astype(BF16)
    kdf_ref[...] = k_d
    vdf_ref[...] = pr[:, _R_VD:_R_VD + DIFF_V]
    ckvf_ref[...] = ckv_n
    kpef_ref[...] = kpe[:, 0:QK_ROPE]


def _att_in_lat_kernel(x_ref, mods_ref, wt_ref, wr_ref, qg_ref, kvg_row_ref, kvg_col_ref, wqt_ref, wk_ref,
                       ek_ref, wvt_ref, tabc_ref, tabt_ref,
                       qdut_ref, qdrt_ref, kdr_ref, vdt_ref, qmut_ref, qmrt_ref, kmr_ref, vmt_ref):
    _, qd_t, vd_t, qm_t, vm_t, k_d, ckv_n, kpe, k_nope = _att_in_body(
        x_ref, mods_ref, wt_ref, wr_ref, qg_ref, kvg_row_ref, kvg_col_ref, wqt_ref, wk_ref, wvt_ref)
    qdut_ref[...] = qd_t.astype(BF16)
    qdrt_ref[...] = _rope_chan_wide(qd_t, tabc_ref[0], tabc_ref[1], tabc_ref[2], DH_DIFF // 4).astype(BF16)
    kdr_ref[...] = _rope_tok_wide(k_d, tabt_ref[0], tabt_ref[1], tabt_ref[2], DH_DIFF // 4).astype(BF16)
    vdt_ref[0] = vd_t.astype(BF16)
    qmut_ref[...] = qm_t.astype(BF16)
    qmrt_ref[...] = _rope_chan_wide(qm_t, tabc_ref[3], tabc_ref[4], tabc_ref[5], QK_ROPE // 4).astype(BF16)
    kpe_r = _rope_tok(kpe, tabt_ref[3], tabt_ref[4], tabt_ref[5], QK_ROPE // 4)
    kmr_ref[...] = (k_nope + _dot(kpe_r.astype(BF16), ek_ref[...])).astype(BF16)
    vmt_ref[0] = vm_t.astype(BF16)


def _att_in(x2d, mods, mod_row_fn, wts, tables, tm):
    t = x2d.shape[0]
    nt = t // tm
    tok = lambda n: pl.BlockSpec((tm, n), lambda i: (i, 0))
    chan = lambda n: pl.BlockSpec((n, tm), lambda i: (0, i))
    chunk = lambda n: pl.BlockSpec((1, n, tm), lambda i: (i, 0, 0))
    in_specs = [tok(D_MODEL), pl.BlockSpec((1, 1, 6, D_MODEL), lambda i: (0, mod_row_fn(i), 0, 0))]
    in_specs += [_const_spec(w.shape) for w in wts]
    s_tok = lambda n, dt=BF16: jax.ShapeDtypeStruct((t, n), dt)
    s_chan = lambda n: jax.ShapeDtypeStruct((n, t), BF16)
    s_chunk = lambda n: jax.ShapeDtypeStruct((nt, n, tm), BF16)
    if tables is None:
        out_shape = [s_chan(DIFF_QK), s_tok(DIFF_QK), s_chunk(DIFF_V), s_chan(MLA_WIDE), s_tok(MLA_WIDE),
                     s_chunk(H_MLA * V_MLA),
                     s_tok(DIFF_QK, F32), s_tok(DIFF_V, F32), s_tok(KV_LORA, F32), s_tok(QK_ROPE, F32)]
        out_specs = [chan(DIFF_QK), tok(DIFF_QK), chunk(DIFF_V), chan(MLA_WIDE), tok(MLA_WIDE),
                     chunk(H_MLA * V_MLA), tok(DIFF_QK), tok(DIFF_V), tok(KV_LORA), tok(QK_ROPE)]
        kern, args, name = _att_in_ctx_kernel, (), "att_in_ctx"
    else:
        tab_c, tab_t = tables
        t_lat = tab_t.shape[1]
        in_specs += [pl.BlockSpec((6, LANES, tm), lambda i: (0, 0, i % (t_lat // tm))),
                     pl.BlockSpec((6, tm, LANES), lambda i: (0, i % (t_lat // tm), 0))]
        out_shape = [s_chan(DIFF_QK), s_chan(DIFF_QK), s_tok(DIFF_QK), s_chunk(DIFF_V),
                     s_chan(MLA_WIDE), s_chan(MLA_WIDE), s_tok(MLA_WIDE), s_chunk(H_MLA * V_MLA)]
        out_specs = [chan(DIFF_QK), chan(DIFF_QK), tok(DIFF_QK), chunk(DIFF_V),
                     chan(MLA_WIDE), chan(MLA_WIDE), tok(MLA_WIDE), chunk(H_MLA * V_MLA)]
        kern, args, name = _att_in_lat_kernel, (tab_c, tab_t), "att_in_lat"
    return pl.pallas_call(
        kern, grid=(nt,), in_specs=in_specs, out_specs=out_specs, out_shape=out_shape,
        compiler_params=_params("arbitrary"), name=name,
    )(x2d, mods, *wts, *args)


def _cache_kv_kernel(ckv_ref, kpe_ref, wk_ref, ek_ref, wvt_ref, km_ref, vmt_ref):
    ckv_b = ckv_ref[...].astype(BF16)
    km_ref[...] = (_dot(ckv_b, wk_ref[...]) + _dot(kpe_ref[...].astype(BF16), ek_ref[...])).astype(BF16)
    past = vmt_ref.shape[2]
    vt = _dot_nt(wvt_ref[...], ckv_b)
    for b in range(vmt_ref.shape[0]):
        vmt_ref[b] = vt[:, b * past:(b + 1) * past].astype(BF16)


def _cache_kv(ckv2d, kpe_slab2d, wk, ek, wvt, bl):
    t = ckv2d.shape[0]
    full = lambda a: pl.BlockSpec(a.shape, lambda i: (0,) * a.ndim)
    return pl.pallas_call(
        _cache_kv_kernel, grid=(1,),
        in_specs=[full(ckv2d), full(kpe_slab2d), full(wk), full(ek), full(wvt)],
        out_specs=[pl.BlockSpec((t, MLA_WIDE), lambda i: (0, 0)),
                   pl.BlockSpec((bl, H_MLA * V_MLA, t // bl), lambda i: (0, 0, 0))],
        out_shape=[jax.ShapeDtypeStruct((t, MLA_WIDE), BF16),
                   jax.ShapeDtypeStruct((bl, H_MLA * V_MLA, t // bl), BF16)],
        compiler_params=_params("arbitrary"), name="cache_kv",
    )(ckv2d, kpe_slab2d, wk, ek, wvt)


def _softmax_pv(s, vt, state):
    m_c = jnp.max(s, axis=0, keepdims=True)
    if state is None:
        p = jnp.exp2(s - m_c)
        return m_c, jnp.sum(p, axis=0, keepdims=True), _dot(vt, p.astype(BF16))
    m, l, acc = state
    m_new = jnp.maximum(m, m_c)
    alpha = jnp.exp2(m - m_new)
    p = jnp.exp2(s - m_new)
    l = alpha * l + jnp.sum(p, axis=0, keepdims=True)
    return m_new, l, alpha * acc + _dot(vt, p.astype(BF16))


def _attend(first, rest):
    items = [(si, qt, (lambda k=k: k), (lambda vt=vt: vt)) for si, (qt, k, vt) in enumerate(first)]
    if rest is not None:
        for c in range(rest[0][2].shape[0]):
            for si, (qt, k_ref, vt_ref) in enumerate(rest):
                items.append((si, qt, (lambda r=k_ref, c=c: r[c * KEY_CHUNK:(c + 1) * KEY_CHUNK, :]),
                              (lambda r=vt_ref, c=c: r[c])))
    ns = len(first)
    groups = [items[i:i + ns] for i in range(0, len(items), ns)]
    scores = lambda grp: [_dot(k_fn(), qt) for (_, qt, k_fn, _) in grp]
    state = [None] * ns
    s_next = scores(groups[0])
    for g, grp in enumerate(groups):
        s_cur = s_next
        if g + 1 < len(groups):
            s_next = scores(groups[g + 1])
        for s, (si, _, _, vt_fn) in zip(s_cur, grp):
            state[si] = _softmax_pv(s, vt_fn(), state[si])
    return [acc / l for (_, l, acc) in state]


def _split_maps(qt):
    row = lax.broadcasted_iota(jnp.int32, qt.shape, 0)
    zero = jnp.zeros_like(qt)
    return jnp.where(row < DH_DIFF, qt, zero), jnp.where(row >= DH_DIFF, qt, zero)


def _lambda(lam_ref, lam_init):
    lv = lam_ref[...]
    a = jnp.sum(lv[0:1] * lv[1:2], axis=-1, keepdims=True)
    b = jnp.sum(lv[2:3] * lv[3:4], axis=-1, keepdims=True)
    return jnp.exp(a) - jnp.exp(b) + lam_init


def _diff_finish(o1_t, o2_t, lam_ref, g_ref, lam_init):
    o = (o1_t - _lambda(lam_ref, lam_init) * o2_t).T
    return (_rms_norm(o, g_ref[...]) * (1.0 - lam_init)).astype(BF16)


def _mla_finish(oa_t, ob_t):
    row = lax.broadcasted_iota(jnp.int32, oa_t.shape, 0)
    return jnp.where(row < V_MLA, oa_t, ob_t).T.astype(BF16)


def _diff_ctx_kernel(qt_ref, k_ref, vt_ref, lam_ref, g_ref, out_ref, *, lam_init):
    first = []
    for h in range(H_DIFF):
        sl = slice(h * LANES, (h + 1) * LANES)
        k, vt = k_ref[:, sl], vt_ref[0, sl, :]
        first += [(q, k, vt) for q in _split_maps(qt_ref[sl, :])]
    o = _attend(first, None)
    out_ref[...] = jnp.concatenate(
        [_diff_finish(o[2 * h], o[2 * h + 1], lam_ref, g_ref, lam_init) for h in range(H_DIFF)], axis=1)


def _diff_lat_kernel(qut_ref, qrt_ref, kc_ref, vc_ref, k_ref, vt_ref, lam_ref, g_ref, out_ref, *, lam_init):
    kc, vct = kc_ref[...].astype(BF16), vc_ref[...].T.astype(BF16)
    first = [(q, kc, vct) for q in _split_maps(qut_ref[...])]
    rest = [(q, k_ref, vt_ref) for q in _split_maps(qrt_ref[...])]
    o1, o2 = _attend(first, rest)
    out_ref[...] = _diff_finish(o1, o2, lam_ref, g_ref, lam_init)


def _mla_ctx_kernel(qt_ref, k_ref, vt_ref, out_ref):
    first = []
    for h in range(H_MLA):
        sl = slice(h * MLA_SLAB, (h + 1) * MLA_SLAB)
        pair = slice((h // 2) * LANES, (h // 2 + 1) * LANES)
        first.append((qt_ref[sl, :], k_ref[:, sl], vt_ref[0, pair, :]))
    o = _attend(first, None)
    out_ref[...] = jnp.concatenate(
        [_mla_finish(o[2 * j], o[2 * j + 1]) for j in range(H_MLA // 2)], axis=1)


def _mla_lat_kernel(qua_ref, qub_ref, qra_ref, qrb_ref, kca_ref, kcb_ref, vct_ref, ka_ref, kb_ref, vt_ref,
                    out_ref):
    vct = vct_ref[0]
    first = [(qua_ref[...], kca_ref[...], vct), (qub_ref[...], kcb_ref[...], vct)]
    rest = [(qra_ref[...], ka_ref, vt_ref), (qrb_ref[...], kb_ref, vt_ref)]
    oa, ob = _attend(first, rest)
    out_ref[...] = _mla_finish(oa, ob)


def _attention_ctx(streams, lam4, diff_g, lam_init, b, t):
    qd_t, kd, vd_t, qm_t, km, vm_t = streams
    per_chunk = vd_t.shape[2] // t
    sem = _params("arbitrary")
    qt = lambda w: pl.BlockSpec((w, t), lambda i: (0, i))
    keys = lambda w: pl.BlockSpec((t, w), lambda i: (i, 0))
    vals = lambda w: pl.BlockSpec((1, w, t), lambda i: (i // per_chunk, 0, i % per_chunk))
    o_d = pl.pallas_call(
        functools.partial(_diff_ctx_kernel, lam_init=lam_init),
        grid=(b,),
        in_specs=[qt(DIFF_QK), keys(DIFF_QK), vals(DIFF_V),
                  pl.BlockSpec(lam4.shape, lambda i: (0, 0)),
                  pl.BlockSpec(diff_g.shape, lambda i: (0, 0))],
        out_specs=keys(DIFF_V),
        out_shape=jax.ShapeDtypeStruct((b * t, DIFF_V), BF16),
        compiler_params=sem, name="diff_attn_ctx",
    )(qd_t, kd, vd_t, lam4, diff_g)
    o_m = pl.pallas_call(
        _mla_ctx_kernel,
        grid=(b,),
        in_specs=[qt(MLA_WIDE), keys(MLA_WIDE), vals(H_MLA * V_MLA)],
        out_specs=keys(H_MLA * V_MLA),
        out_shape=jax.ShapeDtypeStruct((b * t, H_MLA * V_MLA), BF16),
        compiler_params=sem, name="mla_attn_ctx",
    )(qm_t, km, vm_t)
    return o_d, o_m


def _attention_lat(streams, cache, lam4, diff_g, lam_init, b, t):
    qdu_t, qdr_t, kdr, vd_t, qmu_t, qmr_t, kmr, vm_t = streams
    kd_c, vd_c, km_c, vmc_t = cache
    past = kd_c.shape[0] // b
    nq = t // Q_TILE
    n_chunks = t // KEY_CHUNK
    sem = _params("arbitrary", "arbitrary", "arbitrary")
    qt = lambda f: pl.BlockSpec((LANES, Q_TILE), lambda i, h, j: (f(h), i * nq + j))
    keys = lambda n, f: pl.BlockSpec((n, LANES), lambda i, h, j: (i, f(h)))
    vals = pl.BlockSpec((n_chunks, LANES, KEY_CHUNK), lambda i, h, j: (i, h, 0))
    out = pl.BlockSpec((Q_TILE, LANES), lambda i, h, j: (i * nq + j, h))
    same, even, odd = (lambda h: h), (lambda h: 2 * h), (lambda h: 2 * h + 1)
    o_d = pl.pallas_call(
        functools.partial(_diff_lat_kernel, lam_init=lam_init),
        grid=(b, H_DIFF, nq),
        in_specs=[qt(same), qt(same), keys(past, same), keys(past, same), keys(t, same), vals,
                  pl.BlockSpec(lam4.shape, lambda i, h, j: (0, 0)),
                  pl.BlockSpec(diff_g.shape, lambda i, h, j: (0, 0))],
        out_specs=out,
        out_shape=jax.ShapeDtypeStruct((b * t, DIFF_V), BF16),
        compiler_params=sem, name="diff_attn_lat",
    )(qdu_t, qdr_t, kd_c, vd_c, kdr, vd_t, lam4, diff_g)
    o_m = pl.pallas_call(
        _mla_lat_kernel,
        grid=(b, H_MLA // 2, nq),
        in_specs=[qt(even), qt(odd), qt(even), qt(odd), keys(past, even), keys(past, odd),
                  pl.BlockSpec((1, LANES, past), lambda i, h, j: (i, h, 0)),
                  keys(t, even), keys(t, odd), vals],
        out_specs=out,
        out_shape=jax.ShapeDtypeStruct((b * t, H_MLA * V_MLA), BF16),
        compiler_params=sem, name="mla_attn_lat",
    )(qmu_t, qmu_t, qmr_t, qmr_t, km_c, km_c, vmc_t, kmr, kmr, vm_t)
    return o_d, o_m


def _post_kernel(x_ref, ma_ref, mb_ref, mods_ref, wo_ref, w1_ref, w2_ref, ln_ref, out_ref):
    mods = mods_ref[0, 0]
    g_m, sh_f, sc_f, g_f = mods[2:3], mods[3:4], mods[4:5], mods[5:6]
    ln = ln_ref[...]
    mix = jnp.concatenate([ma_ref[...], mb_ref[...]], axis=1)
    y = _dot(mix, wo_ref[...])
    x1 = _layer_norm(DEEPNORM_ALPHA * x_ref[...] + g_m * y, ln[0:1], ln[1:2])
    h = (x1 * (1.0 + sc_f) + sh_f).astype(BF16)
    f = jnp.zeros(x1.shape, F32)
    for c in range(D_FF // FF_CHUNK):
        a = jnp.maximum(_dot(h, w1_ref[:, c * FF_CHUNK:(c + 1) * FF_CHUNK]), 0.0)
        f = f + _dot((a * a).astype(BF16), w2_ref[c * FF_CHUNK:(c + 1) * FF_CHUNK, :])
    out_ref[...] = _layer_norm(DEEPNORM_ALPHA * x1 + g_f * f, ln[2:3], ln[3:4])


def _post(x2d, mix_a, mix_b, mods, layer, mod_row_fn, wo, w1, w2, ln4, tm):
    t = x2d.shape[0]
    tok = lambda n: pl.BlockSpec((tm, n), lambda i: (i, 0))
    return pl.pallas_call(
        _post_kernel, grid=(t // tm,),
        in_specs=[tok(D_MODEL), tok(mix_a.shape[1]), tok(mix_b.shape[1]),
                  pl.BlockSpec((1, 1, 6, D_MODEL), lambda i: (layer, mod_row_fn(i), 0, 0)),
                  _const_spec(wo.shape), _const_spec(w1.shape), _const_spec(w2.shape),
                  _const_spec(ln4.shape)],
        out_specs=tok(D_MODEL),
        out_shape=jax.ShapeDtypeStruct((t, D_MODEL), F32),
        compiler_params=_params("arbitrary"), name=f"post_mlp_l{layer}",
    )(x2d, mix_a, mix_b, mods, wo, w1, w2, ln4)


def _conv_in_kernel(x_ref, mods_ref, w_ref, u_ref, pz_ref):
    mods = mods_ref[0, 0]
    sh, sc = mods[0:1], mods[1:2]
    h = (x_ref[...] * (1.0 + sc) + sh).astype(BF16)
    proj = _dot(h, w_ref[...])
    a, gate = proj[:, 0:CONV_CH], proj[:, CONV_CH:2 * CONV_CH]
    u_ref[...] = a * jax.nn.sigmoid(gate)
    pz_ref[...] = proj[:, 2 * CONV_CH:]


def _conv_in(x2d, mods, layer, mod_row_fn, w, tm):
    t = x2d.shape[0]
    tok = lambda n: pl.BlockSpec((tm, n), lambda i: (i, 0))
    return pl.pallas_call(
        _conv_in_kernel, grid=(t // tm,),
        in_specs=[tok(D_MODEL),
                  pl.BlockSpec((1, 1, 6, D_MODEL), lambda i: (layer, mod_row_fn(i), 0, 0)),
                  _const_spec(w.shape)],
        out_specs=[tok(CONV_CH), tok(POOL_CH)],
        out_shape=[jax.ShapeDtypeStruct((t, CONV_CH), F32), jax.ShapeDtypeStruct((t, POOL_CH), F32)],
        compiler_params=_params("arbitrary"), name="conv_in",
    )(x2d, mods, w)


def _conv_pool_kernel(*refs, n_tiles, seq_len):
    if n_tiles > 1:
        (u_ref, ul_ref, ur_ref, pz_ref, pl_ref, pr_ref, cw_ref, cp_ref, wp_ref, ps_ref,
         uo_ref, do_ref, ubuf, pbuf, ushift, cbuf) = refs
    else:
        u_ref, pz_ref, cw_ref, cp_ref, wp_ref, ps_ref, uo_ref, do_ref, ubuf, pbuf, ushift, cbuf = refs
    tm = u_ref.shape[1]
    j = pl.program_id(1)
    halo_zero = jnp.zeros((CONV_HALO, CONV_CH), F32)
    for buf, mid, sides in ((ubuf, u_ref, (ul_ref, ur_ref) if n_tiles > 1 else None),
                            (pbuf, pz_ref, (pl_ref, pr_ref) if n_tiles > 1 else None)):
        buf[CONV_HALO:CONV_HALO + tm, :] = mid[0]
        if sides is None:
            buf[0:CONV_HALO, :] = halo_zero
            buf[CONV_HALO + tm:, :] = halo_zero
        else:
            buf[0:CONV_HALO, :] = jnp.where(j > 0, sides[0][0], halo_zero)
            buf[CONV_HALO + tm:, :] = jnp.where(j < n_tiles - 1, sides[1][0], halo_zero)

    rows_sh = ushift.shape[1]
    for b in range(SUBLANES):
        ushift[b] = ubuf[b:b + rows_sh, :]
    base = CONV_HALO - CONV_WIDTH // 2
    cp = cp_ref[...]

    def strip(i, carry):
        r0 = pl.multiple_of(i * CONV_STRIP, CONV_STRIP)
        acc = jnp.zeros((CONV_STRIP, CONV_CH), F32)
        for k in range(CONV_WIDTH):
            a, b = divmod(base + k, SUBLANES)
            w = jnp.concatenate([cw_ref[k]] * (CONV_STRIP // SUBLANES), axis=0)
            acc = acc + ushift[b, pl.ds(r0 + a * SUBLANES, CONV_STRIP), :] * w
        cbuf[pl.ds(r0, CONV_STRIP), :] = acc
        return carry

    lax.fori_loop(0, tm // CONV_STRIP, strip, 0)
    z = _layer_norm(cbuf[...] + cp[0:1], cp[1:2], cp[2:3])
    uo_ref[0] = (z * jax.nn.sigmoid(z)).astype(uo_ref.dtype)

    pos = j * tm + lax.broadcasted_iota(jnp.int32, (tm, 1), 0)
    outs = []
    for g, w in enumerate(POOL_WINDOWS):
        cols = slice(g * POOL_GC, (g + 1) * POOL_GC)
        s = jnp.zeros((tm, POOL_GC), F32)
        for d in range(-(w // 2), w // 2):
            s = s + pbuf[CONV_HALO + d:CONV_HALO + d + tm, cols]
        lo = jnp.maximum(pos - w // 2, 0)
        hi = jnp.minimum(pos + w // 2 - 1, seq_len - 1)
        cnt = (hi - lo + 1).astype(F32)
        dgrp = s / cnt - pz_ref[0][:, cols]
        outs.append(_dot(dgrp.astype(BF16), wp_ref[g]))
    do_ref[0] = (jnp.concatenate(outs, axis=1) * ps_ref[...]).astype(do_ref.dtype)


def _conv_pool(u2d, pz2d, b, t, tm, cw, cp, wp, ps):
    n_tiles = t // tm
    u3, p3 = u2d.reshape(b, t, CONV_CH), pz2d.reshape(b, t, POOL_CH)
    mid = pl.BlockSpec((1, tm, CONV_CH), lambda i, j: (i, j, 0))
    r = tm // CONV_HALO
    left = pl.BlockSpec((1, CONV_HALO, CONV_CH), lambda i, j: (i, jnp.maximum(j * r - 1, 0), 0))
    right = pl.BlockSpec((1, CONV_HALO, CONV_CH),
                         lambda i, j: (i, jnp.minimum((j + 1) * r, t // CONV_HALO - 1), 0))
    const = lambda a: pl.BlockSpec(a.shape, lambda i, j: (0,) * a.ndim)
    if n_tiles > 1:
        in_specs = [mid, left, right, mid, left, right]
        args = (u3, u3, u3, p3, p3, p3)
    else:
        in_specs = [mid, mid]
        args = (u3, p3)
    in_specs += [const(cw), const(cp), const(wp), const(ps)]
    uo, do = pl.pallas_call(
        functools.partial(_conv_pool_kernel, n_tiles=n_tiles, seq_len=t),
        grid=(b, n_tiles), in_specs=in_specs,
        out_specs=[mid, mid],
        out_shape=[jax.ShapeDtypeStruct((b, t, CONV_CH), BF16), jax.ShapeDtypeStruct((b, t, POOL_CH), BF16)],
        scratch_shapes=[pltpu.VMEM((tm + 2 * CONV_HALO, CONV_CH), F32),
                        pltpu.VMEM((tm + 2 * CONV_HALO, POOL_CH), F32),
                        pltpu.VMEM((SUBLANES, tm + 2 * CONV_HALO - SUBLANES, CONV_CH), F32),
                        pltpu.VMEM((tm, CONV_CH), F32)],
        compiler_params=_params("arbitrary", "arbitrary"),
        name=f"conv_pool_{'lat' if n_tiles > 1 else 'ctx'}",
    )(*args, cw, cp, wp, ps)
    return uo.reshape(b * t, CONV_CH), do.reshape(b * t, POOL_CH)


def _att_in_weights(w_att_in, with_v_tok):
    o = 2 * DIFF_QK + DIFF_V
    q_d, k_d, v_d = w_att_in[:, :DIFF_QK], w_att_in[:, DIFF_QK:2 * DIFF_QK], w_att_in[:, 2 * DIFF_QK:o]
    cq, ckv = w_att_in[:, o:o + Q_LORA], w_att_in[:, o + Q_LORA:o + Q_LORA + KV_LORA]
    kpe = jnp.pad(w_att_in[:, o + Q_LORA + KV_LORA:], ((0, 0), (0, LANES - QK_ROPE)))
    w_t = jnp.concatenate([q_d, v_d, cq, ckv], axis=1).T.astype(BF16)
    w_r = jnp.concatenate([k_d, ckv, kpe] + ([v_d] if with_v_tok else []), axis=1).astype(BF16)
    return w_t, w_r


def _mla_weights(w_uq, w_ukv):
    wq = w_uq.reshape(Q_LORA, H_MLA, QK_NOPE + QK_ROPE)
    wq = jnp.pad(wq, ((0, 0), (0, 0), (0, MLA_SLAB - QK_NOPE - QK_ROPE))).reshape(Q_LORA, MLA_WIDE)
    wkv = w_ukv.reshape(KV_LORA, H_MLA, QK_NOPE + V_MLA)
    wk = jnp.pad(wkv[:, :, :QK_NOPE], ((0, 0), (0, 0), (0, MLA_SLAB - QK_NOPE))).reshape(KV_LORA, MLA_WIDE)
    wv = wkv[:, :, QK_NOPE:].reshape(KV_LORA, H_MLA * V_MLA)
    place = np.zeros((LANES, H_MLA, MLA_SLAB), np.float32)
    for h in range(H_MLA):
        for r in range(QK_ROPE):
            place[r, h, QK_NOPE + r] = 1.0
    ek = jnp.asarray(place.reshape(LANES, MLA_WIDE))
    return wq.T.astype(BF16), wk.astype(BF16), ek.astype(BF16), wv.T.astype(BF16)


def kernel(x_prompt, x_sample, cache_diff_k, cache_diff_v, cache_mla_ckv, cache_mla_krope, c, c_ctx,
           w_ada, b_ada, ln_mix_g, ln_mix_b, ln_mlp_g, ln_mlp_b, w_mlp_in, w_mlp_out,
           w_att_in, w_uq, w_ukv, q_norm_g, kv_norm_g, lam_q1, lam_k1, lam_q2, lam_k2, diff_norm_g, w_att_out,
           w_conv_in, conv_w, conv_b, conv_norm_g, conv_norm_b, w_pool, pool_scale, w_conv_out):
    bc, tc, d = x_prompt.shape
    bl, tl, _ = x_sample.shape
    past = cache_diff_k.shape[2]

    cond8 = jnp.zeros((8, d), F32).at[0].set(c_ctx).at[1:1 + bl].set(c)
    mods = _ada_mods(cond8, w_ada.astype(BF16), b_ada).reshape(DEPTH, 8, 6, d)
    ctx_row = lambda i: 0
    lat_row = lambda i: 1 + i // (tl // TOKEN_TILE)

    wqt, wk, ek, wvt = _mla_weights(w_uq[0], w_ukv[0])
    norm_w = (q_norm_g[0].reshape(Q_LORA, 1), kv_norm_g[0].reshape(1, KV_LORA), kv_norm_g[0].reshape(KV_LORA, 1))
    att_w = lambda with_v: _att_in_weights(w_att_in[0], with_v) + norm_w + (wqt, wk, ek, wvt)
    lam4 = jnp.stack([lam_q1[0], lam_k1[0], lam_q2[0], lam_k2[0]])
    diff_g = diff_norm_g[0].reshape(1, 2 * DH_DIFF)
    lam_init = 0.8 - 0.6 * math.exp(-0.3 * 0)
    ln4 = [jnp.stack([ln_mix_g[l], ln_mix_b[l], ln_mlp_g[l], ln_mlp_b[l]]) for l in range(DEPTH)]
    w1 = [w_mlp_in[l].astype(BF16) for l in range(DEPTH)]
    w2 = [w_mlp_out[l].astype(BF16) for l in range(DEPTH)]
    w_ao = w_att_out[0].astype(BF16)
    w_ci = w_conv_in[0].astype(BF16)
    w_co = w_conv_out[0].astype(BF16)
    conv_p = jnp.stack([conv_b[0], conv_norm_g[0], conv_norm_b[0]])
    conv_w8 = jnp.broadcast_to(conv_w[0][:, None, :], (CONV_WIDTH, SUBLANES, CONV_CH))
    wp = w_pool[0].astype(BF16)
    ps = pool_scale[0].reshape(1, POOL_CH)

    xp = x_prompt.reshape(bc * tc, d)
    outs = _att_in(xp, mods, ctx_row, att_w(True), None, TOKEN_TILE)
    kd_f, vd_f, ckv_f, kpe_f = outs[6:]
    o_d, o_m = _attention_ctx(outs[:6], lam4, diff_g, lam_init, bc, tc)
    xp = _post(xp, o_d, o_m, mods, 0, ctx_row, w_ao, w1[0], w2[0], ln4[0], TOKEN_TILE)
    u, pz = _conv_in(xp, mods, 1, ctx_row, w_ci, TOKEN_TILE)
    uo, do = _conv_pool(u, pz, bc, tc, tc, conv_w8, conv_p, wp, ps)
    xp = _post(xp, uo, do, mods, 1, ctx_row, w_co, w1[1], w2[1], ln4[1], TOKEN_TILE)

    xs = x_sample.reshape(bl * tl, d)
    outs = _att_in(xs, mods, lat_row, att_w(False), _rope_tables(tl), TOKEN_TILE)
    km_c, vmc_t = _cache_kv(cache_mla_ckv[:, 0].reshape(bl * past, KV_LORA),
                            jnp.pad(cache_mla_krope[:, 0].reshape(bl * past, QK_ROPE),
                                    ((0, 0), (0, LANES - QK_ROPE))), wk, ek, wvt, bl)
    cache = (cache_diff_k[:, 0].reshape(bl * past, DIFF_QK), cache_diff_v[:, 0].reshape(bl * past, DIFF_V),
             km_c, vmc_t)
    o_d, o_m = _attention_lat(outs, cache, lam4, diff_g, lam_init, bl, tl)
    xs = _post(xs, o_d, o_m, mods, 0, lat_row, w_ao, w1[0], w2[0], ln4[0], TOKEN_TILE)
    u, pz = _conv_in(xs, mods, 1, lat_row, w_ci, TOKEN_TILE)
    uo, do = _conv_pool(u, pz, bl, tl, CONV_TILE, conv_w8, conv_p, wp, ps)
    xs = _post(xs, uo, do, mods, 1, lat_row, w_co, w1[1], w2[1], ln4[1], TOKEN_TILE)

    return (xp.reshape(bc, tc, d), xs.reshape(bl, tl, d),
            kd_f.reshape(bc, 1, tc, H_DIFF, 2 * DH_DIFF), vd_f.reshape(bc, 1, tc, H_DIFF, 2 * DH_DIFF),
            ckv_f.reshape(bc, 1, tc, KV_LORA), kpe_f.reshape(bc, 1, tc, QK_ROPE))
```

```python
import functools
import math

import jax
import jax.numpy as jnp
import numpy as np
from jax import lax
from jax.experimental import pallas as pl
from jax.experimental.pallas import tpu as pltpu

D_MODEL = 1024
DEPTH = 2
GRID_W = 64
H_DIFF = 4
DH_DIFF = 64
DIFF_QK = H_DIFF * 2 * DH_DIFF
DIFF_V = H_DIFF * 2 * DH_DIFF
H_MLA = 8
Q_LORA = 256
KV_LORA = 128
QK_NOPE = 64
QK_ROPE = 32
V_MLA = 64
ATT_IN = 2 * DIFF_QK + DIFF_V + Q_LORA + KV_LORA + QK_ROPE
CONV_CH = 512
CONV_WIDTH = 31
POOL_CH = 512
POOL_WINDOWS = (2, 4, 8, 16)
POOL_GC = POOL_CH // len(POOL_WINDOWS)
CONV_IN = 2 * CONV_CH + POOL_CH
D_FF = 4 * D_MODEL
ROPE_BASE = 10000.0
NORM_EPS = 1e-5
DEEPNORM_ALPHA = (2 * DEPTH) ** 0.25
LOG2E = math.log2(math.e)

LANES = 128
SUBLANES = 8
MLA_SLAB = LANES
MLA_WIDE = H_MLA * MLA_SLAB
_ROPE_LANE_PAD = ((0, 0), (QK_NOPE, MLA_SLAB - QK_NOPE - QK_ROPE))
VMEM_LIMIT = 56 * 1024 * 1024

TOKEN_TILE = 512
KEY_CHUNK = TOKEN_TILE
Q_TILE = 512
CONV_TILE = 512
CONV_HALO = 16
CONV_STRIP = 32
FF_CHUNK = 1024
POST_GROUPS = 2

F32 = jnp.float32
BF16 = jnp.bfloat16


def _dot(a, b):
    return jnp.dot(a, b, preferred_element_type=F32)


def _dot_nt(a, b):
    return lax.dot_general(a, b, (((1,), (1,)), ((), ())), preferred_element_type=F32)


def _layer_norm(x, g, b):
    mu = jnp.mean(x, axis=-1, keepdims=True)
    xc = x - mu
    var = jnp.mean(xc * xc, axis=-1, keepdims=True)
    return xc * lax.rsqrt(var + NORM_EPS) * g + b


def _rms_norm(x, g, axis=-1):
    ms = jnp.mean(x * x, axis=axis, keepdims=True)
    return x * lax.rsqrt(ms + NORM_EPS) * g


def _const_spec(shape):
    nd = len(shape)
    return pl.BlockSpec(shape, lambda *_: (0,) * nd, pipeline_mode=pl.Buffered(1))


def _params(*sem):
    return pltpu.CompilerParams(dimension_semantics=sem, vmem_limit_bytes=VMEM_LIMIT)


ADA_TILE = 1536


def _ada_kernel(cond_ref, w_ref, b_ref, out_ref):
    cond = cond_ref[...]
    act = (cond * jax.nn.sigmoid(cond)).astype(BF16)
    out_ref[0] = _dot(act, w_ref[0].astype(BF16)) + b_ref[0]


def _ada_mods(cond8, w_ada, b_ada):
    n = 6 * D_MODEL
    return pl.pallas_call(
        _ada_kernel,
        grid=(DEPTH, n // ADA_TILE),
        in_specs=[
            pl.BlockSpec((8, D_MODEL), lambda l, j: (0, 0)),
            pl.BlockSpec((1, D_MODEL, ADA_TILE), lambda l, j: (l, 0, j)),
            pl.BlockSpec((1, 1, ADA_TILE), lambda l, j: (l, 0, j)),
        ],
        out_specs=pl.BlockSpec((1, 8, ADA_TILE), lambda l, j: (l, 0, j)),
        out_shape=jax.ShapeDtypeStruct((DEPTH, 8, n), F32),
        compiler_params=_params("arbitrary", "arbitrary"),
        name="ada_mods",
    )(cond8, w_ada, b_ada.reshape(DEPTH, 1, n))


def _rope_tables(t_lat):
    pos = np.arange(t_lat)
    row = (pos // GRID_W).astype(np.float64)
    col = (pos % GRID_W).astype(np.float64)

    def tables(kinds):
        cos = np.ones((t_lat, LANES))
        sa = np.zeros((t_lat, LANES))
        sb = np.zeros((t_lat, LANES))
        for lane, kind in enumerate(kinds):
            if kind is None:
                continue
            axis, half, j, upper = kind
            ang = (row if axis == 0 else col) * ROPE_BASE ** (-float(j) / half)
            cos[:, lane] = np.cos(ang)
            if upper:
                sb[:, lane] = np.sin(ang)
            else:
                sa[:, lane] = -np.sin(ang)
        return [cos, sa, sb]

    def rot_kinds(n):
        half = n // 4
        kinds = []
        for i in range(n):
            axis, r = divmod(i, n // 2)
            kinds.append((axis, half, r % half, r >= half))
        return kinds

    diff = rot_kinds(DH_DIFF) * 2
    mla_q = [None] * QK_NOPE + rot_kinds(QK_ROPE) + [None] * (LANES - QK_NOPE - QK_ROPE)
    chan = np.stack([t.T for t in tables(diff) + tables(mla_q)]).astype(np.float32)
    tok = np.stack(tables(diff) + tables(mla_q)).astype(np.float32)
    return jnp.asarray(chan), jnp.asarray(tok)


def _rope_tok(x, cos, sa, sb, shift):
    return x * cos + pltpu.roll(x, LANES - shift, 1) * sa + pltpu.roll(x, shift, 1) * sb


def _rope_tok_wide(x, cos, sa, sb, shift):
    n = x.shape[1] // LANES
    return jnp.concatenate(
        [_rope_tok(x[:, i * LANES:(i + 1) * LANES], cos, sa, sb, shift) for i in range(n)], axis=1)


def _rope_chan(x, cos, sa, sb, shift):
    up = jnp.concatenate([x[shift:], x[:shift]], axis=0)
    down = jnp.concatenate([x[-shift:], x[:-shift]], axis=0)
    return x * cos + up * sa + down * sb


def _rope_chan_wide(x, cos, sa, sb, shift):
    n = x.shape[0] // LANES
    return jnp.concatenate(
        [_rope_chan(x[i * LANES:(i + 1) * LANES], cos, sa, sb, shift) for i in range(n)], axis=0)


_T_QD, _T_VD, _T_CQ, _T_CKV = 0, DIFF_QK, DIFF_QK + DIFF_V, DIFF_QK + DIFF_V + Q_LORA
_T_ROWS = _T_CKV + KV_LORA
_R_KD, _R_CKV, _R_KPE, _R_VD = 0, DIFF_QK, DIFF_QK + KV_LORA, DIFF_QK + KV_LORA + LANES

DIFF_QSCALE = DH_DIFF ** -0.5 * LOG2E
MLA_QSCALE = (QK_NOPE + QK_ROPE) ** -0.5 * LOG2E


def _att_in_body(x_ref, mods_ref, wt_ref, wr_ref, qg_ref, kvg_row_ref, kvg_col_ref, wqt_ref, wk_ref, wvt_ref):
    mods = mods_ref[0, 0]
    sh, sc = mods[0:1], mods[1:2]
    h = (x_ref[...] * (1.0 + sc) + sh).astype(BF16)
    pt = _dot_nt(wt_ref[...], h)
    pr = _dot(h, wr_ref[...])
    qd_t = pt[_T_QD:_T_QD + DIFF_QK] * DIFF_QSCALE
    vd_t = pt[_T_VD:_T_VD + DIFF_V]
    cqn_t = _rms_norm(pt[_T_CQ:_T_CQ + Q_LORA], qg_ref[...], axis=0)
    qm_t = _dot(wqt_ref[...], cqn_t.astype(BF16)) * MLA_QSCALE
    ckvn_t = _rms_norm(pt[_T_CKV:_T_CKV + KV_LORA], kvg_col_ref[...], axis=0)
    vm_t = _dot(wvt_ref[...], ckvn_t.astype(BF16))
    k_d = pr[:, _R_KD:_R_KD + DIFF_QK]
    ckv_n = _rms_norm(pr[:, _R_CKV:_R_CKV + KV_LORA], kvg_row_ref[...])
    kpe = pr[:, _R_KPE:_R_KPE + LANES]
    k_nope = _dot(ckv_n.astype(BF16), wk_ref[...])
    return pr, qd_t, vd_t, qm_t, vm_t, k_d, ckv_n, kpe, k_nope


def _mla_keys(k_nope, kpe_slab):
    return (k_nope + jnp.concatenate([kpe_slab] * H_MLA, axis=1)).astype(BF16)


def _att_in_ctx_kernel(x_ref, mods_ref, wt_ref, wr_ref, qg_ref, kvg_row_ref, kvg_col_ref, wqt_ref, wk_ref,
                       wvt_ref,
                       qdt_ref, kd_ref, vdt_ref, qmt_ref, km_ref, vmt_ref,
                       kdf_ref, vdf_ref, ckvf_ref, kpef_ref):
    pr, qd_t, vd_t, qm_t, vm_t, k_d, ckv_n, kpe, k_nope = _att_in_body(
        x_ref, mods_ref, wt_ref, wr_ref, qg_ref, kvg_row_ref, kvg_col_ref, wqt_ref, wk_ref, wvt_ref)
    qdt_ref[...] = qd_t.astype(BF16)
    kd_ref[...] = k_d.astype(BF16)
    vdt_ref[0] = vd_t.astype(BF16)
    qmt_ref[...] = qm_t.astype(BF16)
    km_ref[...] = _mla_keys(k_nope, kpe)
    vmt_ref[0] = vm_t.astype(BF16)
    kdf_ref[...] = k_d
    vdf_ref[...] = pr[:, _R_VD:_R_VD + DIFF_V]
    ckvf_ref[...] = ckv_n
    kpef_ref[...] = kpe[:, QK_NOPE:QK_NOPE + QK_ROPE]


def _att_in_lat_kernel(x_ref, mods_ref, wt_ref, wr_ref, qg_ref, kvg_row_ref, kvg_col_ref, wqt_ref, wk_ref,
                       wvt_ref, tabc_ref, tabt_ref,
                       qdut_ref, qdrt_ref, kdr_ref, vdt_ref, qmut_ref, qmrt_ref, kmr_ref, vmt_ref):
    _, qd_t, vd_t, qm_t, vm_t, k_d, ckv_n, kpe, k_nope = _att_in_body(
        x_ref, mods_ref, wt_ref, wr_ref, qg_ref, kvg_row_ref, kvg_col_ref, wqt_ref, wk_ref, wvt_ref)
    qdut_ref[...] = qd_t.astype(BF16)
    qdrt_ref[...] = _rope_chan_wide(qd_t, tabc_ref[0], tabc_ref[1], tabc_ref[2], DH_DIFF // 4).astype(BF16)
    kdr_ref[...] = _rope_tok_wide(k_d, tabt_ref[0], tabt_ref[1], tabt_ref[2], DH_DIFF // 4).astype(BF16)
    vdt_ref[0] = vd_t.astype(BF16)
    qmut_ref[...] = qm_t.astype(BF16)
    qmrt_ref[...] = _rope_chan_wide(qm_t, tabc_ref[3], tabc_ref[4], tabc_ref[5], QK_ROPE // 4).astype(BF16)
    kpe_r = _rope_tok(kpe, tabt_ref[3], tabt_ref[4], tabt_ref[5], QK_ROPE // 4)
    kmr_ref[...] = _mla_keys(k_nope, kpe_r)
    vmt_ref[0] = vm_t.astype(BF16)


def _att_in(x2d, mods, mod_row_fn, wts, tables, tm):
    t = x2d.shape[0]
    nt = t // tm
    tok = lambda n: pl.BlockSpec((tm, n), lambda i: (i, 0))
    chan = lambda n: pl.BlockSpec((n, tm), lambda i: (0, i))
    chunk = lambda n: pl.BlockSpec((1, n, tm), lambda i: (i, 0, 0))
    in_specs = [tok(D_MODEL), pl.BlockSpec((1, 1, 6, D_MODEL), lambda i: (0, mod_row_fn(i), 0, 0))]
    in_specs += [_const_spec(w.shape) for w in wts]
    s_tok = lambda n, dt=BF16: jax.ShapeDtypeStruct((t, n), dt)
    s_chan = lambda n: jax.ShapeDtypeStruct((n, t), BF16)
    s_chunk = lambda n: jax.ShapeDtypeStruct((nt, n, tm), BF16)
    if tables is None:
        out_shape = [s_chan(DIFF_QK), s_tok(DIFF_QK), s_chunk(DIFF_V), s_chan(MLA_WIDE), s_tok(MLA_WIDE),
                     s_chunk(H_MLA * V_MLA),
                     s_tok(DIFF_QK, F32), s_tok(DIFF_V, F32), s_tok(KV_LORA, F32), s_tok(QK_ROPE, F32)]
        out_specs = [chan(DIFF_QK), tok(DIFF_QK), chunk(DIFF_V), chan(MLA_WIDE), tok(MLA_WIDE),
                     chunk(H_MLA * V_MLA), tok(DIFF_QK), tok(DIFF_V), tok(KV_LORA), tok(QK_ROPE)]
        kern, args, name = _att_in_ctx_kernel, (), "att_in_ctx"
    else:
        tab_c, tab_t = tables
        t_lat = tab_t.shape[1]
        in_specs += [pl.BlockSpec((6, LANES, tm), lambda i: (0, 0, i % (t_lat // tm))),
                     pl.BlockSpec((6, tm, LANES), lambda i: (0, i % (t_lat // tm), 0))]
        out_shape = [s_chan(DIFF_QK), s_chan(DIFF_QK), s_tok(DIFF_QK), s_chunk(DIFF_V),
                     s_chan(MLA_WIDE), s_chan(MLA_WIDE), s_tok(MLA_WIDE), s_chunk(H_MLA * V_MLA)]
        out_specs = [chan(DIFF_QK), chan(DIFF_QK), tok(DIFF_QK), chunk(DIFF_V),
                     chan(MLA_WIDE), chan(MLA_WIDE), tok(MLA_WIDE), chunk(H_MLA * V_MLA)]
        kern, args, name = _att_in_lat_kernel, (tab_c, tab_t), "att_in_lat"
    return pl.pallas_call(
        kern, grid=(nt,), in_specs=in_specs, out_specs=out_specs, out_shape=out_shape,
        compiler_params=_params("arbitrary"), name=name,
    )(x2d, mods, *wts, *args)


def _cache_kv_kernel(ckv_ref, kpe_ref, wk_ref, wvt_ref, km_ref, vmt_ref):
    ckv_b = ckv_ref[...].astype(BF16)
    km_ref[...] = _mla_keys(_dot(ckv_b, wk_ref[...]), kpe_ref[...])
    past = vmt_ref.shape[2]
    vt = _dot_nt(wvt_ref[...], ckv_b)
    for b in range(vmt_ref.shape[0]):
        vmt_ref[b] = vt[:, b * past:(b + 1) * past].astype(BF16)


def _cache_kv(ckv2d, kpe_slab2d, wk, wvt, bl):
    t = ckv2d.shape[0]
    full = lambda a: pl.BlockSpec(a.shape, lambda i: (0,) * a.ndim)
    return pl.pallas_call(
        _cache_kv_kernel, grid=(1,),
        in_specs=[full(ckv2d), full(kpe_slab2d), full(wk), full(wvt)],
        out_specs=[pl.BlockSpec((t, MLA_WIDE), lambda i: (0, 0)),
                   pl.BlockSpec((bl, H_MLA * V_MLA, t // bl), lambda i: (0, 0, 0))],
        out_shape=[jax.ShapeDtypeStruct((t, MLA_WIDE), BF16),
                   jax.ShapeDtypeStruct((bl, H_MLA * V_MLA, t // bl), BF16)],
        compiler_params=_params("arbitrary"), name="cache_kv",
    )(ckv2d, kpe_slab2d, wk, wvt)


def _softmax_pv(s, vt, state):
    m_c = jnp.max(s, axis=0, keepdims=True)
    if state is None:
        p = jnp.exp2(s - m_c)
        return m_c, jnp.sum(p, axis=0, keepdims=True), _dot(vt, p.astype(BF16))
    m, l, acc = state
    m_new = jnp.maximum(m, m_c)
    alpha = jnp.exp2(m - m_new)
    p = jnp.exp2(s - m_new)
    l = alpha * l + jnp.sum(p, axis=0, keepdims=True)
    return m_new, l, alpha * acc + _dot(vt, p.astype(BF16))


def _attend(first, rest):
    items = [(si, qt, (lambda k=k: k), (lambda vt=vt: vt)) for si, (qt, k, vt) in enumerate(first)]
    if rest is not None:
        for c in range(rest[0][2].shape[0]):
            for si, (qt, k_ref, vt_ref) in enumerate(rest):
                items.append((si, qt, (lambda r=k_ref, c=c: r[c * KEY_CHUNK:(c + 1) * KEY_CHUNK, :]),
                              (lambda r=vt_ref, c=c: r[c])))
    ns = len(first)
    groups = [items[i:i + ns] for i in range(0, len(items), ns)]
    scores = lambda grp: [_dot(k_fn(), qt) for (_, qt, k_fn, _) in grp]
    state = [None] * ns
    s_next = scores(groups[0])
    for g, grp in enumerate(groups):
        s_cur = s_next
        if g + 1 < len(groups):
            s_next = scores(groups[g + 1])
        for s, (si, _, _, vt_fn) in zip(s_cur, grp):
            state[si] = _softmax_pv(s, vt_fn(), state[si])
    return [acc / l for (_, l, acc) in state]


def _split_maps(qt):
    row = lax.broadcasted_iota(jnp.int32, qt.shape, 0)
    zero = jnp.zeros_like(qt)
    return jnp.where(row < DH_DIFF, qt, zero), jnp.where(row >= DH_DIFF, qt, zero)


def _lambda(lam_ref, lam_init):
    lv = lam_ref[...]
    a = jnp.sum(lv[0:1] * lv[1:2], axis=-1, keepdims=True)
    b = jnp.sum(lv[2:3] * lv[3:4], axis=-1, keepdims=True)
    return jnp.exp(a) - jnp.exp(b) + lam_init


def _diff_finish(o1_t, o2_t, lam_ref, g_ref, lam_init):
    o = (o1_t - _lambda(lam_ref, lam_init) * o2_t).T
    return (_rms_norm(o, g_ref[...]) * (1.0 - lam_init)).astype(BF16)


def _mla_finish(oa_t, ob_t):
    return jnp.concatenate([oa_t, ob_t], axis=0).T.astype(BF16)


def _diff_ctx_kernel(qt_ref, k_ref, vt_ref, lam_ref, g_ref, out_ref, *, lam_init):
    first = []
    for h in range(H_DIFF):
        sl = slice(h * LANES, (h + 1) * LANES)
        k, vt = k_ref[:, sl], vt_ref[0, sl, :]
        first += [(q, k, vt) for q in _split_maps(qt_ref[sl, :])]
    o = _attend(first, None)
    out_ref[...] = jnp.concatenate(
        [_diff_finish(o[2 * h], o[2 * h + 1], lam_ref, g_ref, lam_init) for h in range(H_DIFF)], axis=1)


def _diff_lat_kernel(qut_ref, qrt_ref, kc_ref, vc_ref, k_ref, vt_ref, lam_ref, g_ref, out_ref, *, lam_init):
    kc, vct = kc_ref[...].astype(BF16), vc_ref[...].T.astype(BF16)
    first = [(q, kc, vct) for q in _split_maps(qut_ref[...])]
    rest = [(q, k_ref, vt_ref) for q in _split_maps(qrt_ref[...])]
    o1, o2 = _attend(first, rest)
    out_ref[...] = _diff_finish(o1, o2, lam_ref, g_ref, lam_init)


def _mla_ctx_kernel(qt_ref, k_ref, vt_ref, out_ref):
    first = []
    for h in range(H_MLA):
        sl = slice(h * MLA_SLAB, (h + 1) * MLA_SLAB)
        first.append((qt_ref[sl, :], k_ref[:, sl], vt_ref[0, h * V_MLA:(h + 1) * V_MLA, :]))
    o = _attend(first, None)
    out_ref[...] = jnp.concatenate(
        [_mla_finish(o[2 * j], o[2 * j + 1]) for j in range(H_MLA // 2)], axis=1)


def _mla_lat_kernel(qua_ref, qub_ref, qra_ref, qrb_ref, kca_ref, kcb_ref, vcta_ref, vctb_ref, ka_ref, kb_ref,
                    vta_ref, vtb_ref, out_ref):
    first = [(qua_ref[...], kca_ref[...], vcta_ref[0]), (qub_ref[...], kcb_ref[...], vctb_ref[0])]
    rest = [(qra_ref[...], ka_ref, vta_ref), (qrb_ref[...], kb_ref, vtb_ref)]
    oa, ob = _attend(first, rest)
    out_ref[...] = _mla_finish(oa, ob)


def _attention_ctx(streams, lam4, diff_g, lam_init, b, t):
    qd_t, kd, vd_t, qm_t, km, vm_t = streams
    per_chunk = vd_t.shape[2] // t
    sem = _params("arbitrary")
    qt = lambda w: pl.BlockSpec((w, t), lambda i: (0, i))
    keys = lambda w: pl.BlockSpec((t, w), lambda i: (i, 0))
    vals = lambda w: pl.BlockSpec((1, w, t), lambda i: (i // per_chunk, 0, i % per_chunk))
    o_d = pl.pallas_call(
        functools.partial(_diff_ctx_kernel, lam_init=lam_init),
        grid=(b,),
        in_specs=[qt(DIFF_QK), keys(DIFF_QK), vals(DIFF_V),
                  pl.BlockSpec(lam4.shape, lambda i: (0, 0)),
                  pl.BlockSpec(diff_g.shape, lambda i: (0, 0))],
        out_specs=keys(DIFF_V),
        out_shape=jax.ShapeDtypeStruct((b * t, DIFF_V), BF16),
        compiler_params=sem, name="diff_attn_ctx",
    )(qd_t, kd, vd_t, lam4, diff_g)
    o_m = pl.pallas_call(
        _mla_ctx_kernel,
        grid=(b,),
        in_specs=[qt(MLA_WIDE), keys(MLA_WIDE), vals(H_MLA * V_MLA)],
        out_specs=keys(H_MLA * V_MLA),
        out_shape=jax.ShapeDtypeStruct((b * t, H_MLA * V_MLA), BF16),
        compiler_params=sem, name="mla_attn_ctx",
    )(qm_t, km, vm_t)
    return o_d, o_m


def _attention_lat(streams, cache, lam4, diff_g, lam_init, b, t):
    qdu_t, qdr_t, kdr, vd_t, qmu_t, qmr_t, kmr, vm_t = streams
    kd_c, vd_c, km_c, vmc_t = cache
    past = kd_c.shape[0] // b
    nq = t // Q_TILE
    n_chunks = t // KEY_CHUNK
    sem = _params("arbitrary", "arbitrary", "arbitrary")
    qt = lambda f: pl.BlockSpec((LANES, Q_TILE), lambda i, h, j: (f(h), i * nq + j))
    keys = lambda n, f: pl.BlockSpec((n, LANES), lambda i, h, j: (i, f(h)))
    vals = pl.BlockSpec((n_chunks, LANES, KEY_CHUNK), lambda i, h, j: (i, h, 0))
    out = pl.BlockSpec((Q_TILE, LANES), lambda i, h, j: (i * nq + j, h))
    same, even, odd = (lambda h: h), (lambda h: 2 * h), (lambda h: 2 * h + 1)
    o_d = pl.pallas_call(
        functools.partial(_diff_lat_kernel, lam_init=lam_init),
        grid=(b, H_DIFF, nq),
        in_specs=[qt(same), qt(same), keys(past, same), keys(past, same), keys(t, same), vals,
                  pl.BlockSpec(lam4.shape, lambda i, h, j: (0, 0)),
                  pl.BlockSpec(diff_g.shape, lambda i, h, j: (0, 0))],
        out_specs=out,
        out_shape=jax.ShapeDtypeStruct((b * t, DIFF_V), BF16),
        compiler_params=sem, name="diff_attn_lat",
    )(qdu_t, qdr_t, kd_c, vd_c, kdr, vd_t, lam4, diff_g)
    o_m = pl.pallas_call(
        _mla_lat_kernel,
        grid=(b, H_MLA // 2, nq),
        in_specs=[qt(even), qt(odd), qt(even), qt(odd), keys(past, even), keys(past, odd),
                  pl.BlockSpec((1, V_MLA, past), lambda i, h, j: (i, 2 * h, 0)),
                  pl.BlockSpec((1, V_MLA, past), lambda i, h, j: (i, 2 * h + 1, 0)),
                  keys(t, even), keys(t, odd),
                  pl.BlockSpec((n_chunks, V_MLA, KEY_CHUNK), lambda i, h, j: (i, 2 * h, 0)),
                  pl.BlockSpec((n_chunks, V_MLA, KEY_CHUNK), lambda i, h, j: (i, 2 * h + 1, 0))],
        out_specs=out,
        out_shape=jax.ShapeDtypeStruct((b * t, H_MLA * V_MLA), BF16),
        compiler_params=sem, name="mla_attn_lat",
    )(qmu_t, qmu_t, qmr_t, qmr_t, km_c, km_c, vmc_t, vmc_t, kmr, kmr, vm_t, vm_t)
    return o_d, o_m


def _post_kernel(x_ref, ma_ref, mb_ref, mods_ref, wo_ref, w1_ref, w2_ref, ln_ref, out_ref):
    mods = mods_ref[0, 0]
    g_m, sh_f, sc_f, g_f = mods[2:3], mods[3:4], mods[4:5], mods[5:6]
    ln = ln_ref[...]
    rows = x_ref.shape[0] // POST_GROUPS
    grp = [slice(r * rows, (r + 1) * rows) for r in range(POST_GROUPS)]
    ys = [_dot(jnp.concatenate([ma_ref[g, :], mb_ref[g, :]], axis=1), wo_ref[...]) for g in grp]
    x1s = [_layer_norm(DEEPNORM_ALPHA * x_ref[g, :] + g_m * y, ln[0:1], ln[1:2]) for g, y in zip(grp, ys)]
    fs = []
    for x1 in x1s:
        h = (x1 * (1.0 + sc_f) + sh_f).astype(BF16)
        f = jnp.zeros(x1.shape, F32)
        for c in range(D_FF // FF_CHUNK):
            a = jnp.maximum(_dot(h, w1_ref[:, c * FF_CHUNK:(c + 1) * FF_CHUNK]), 0.0)
            f = f + _dot((a * a).astype(BF16), w2_ref[c * FF_CHUNK:(c + 1) * FF_CHUNK, :])
        fs.append(f)
    for g, x1, f in zip(grp, x1s, fs):
        out_ref[g, :] = _layer_norm(DEEPNORM_ALPHA * x1 + g_f * f, ln[2:3], ln[3:4])


def _post(x2d, mix_a, mix_b, mods, layer, mod_row_fn, wo, w1, w2, ln4, tm):
    t = x2d.shape[0]
    tok = lambda n: pl.BlockSpec((tm, n), lambda i: (i, 0))
    return pl.pallas_call(
        _post_kernel, grid=(t // tm,),
        in_specs=[tok(D_MODEL), tok(mix_a.shape[1]), tok(mix_b.shape[1]),
                  pl.BlockSpec((1, 1, 6, D_MODEL), lambda i: (layer, mod_row_fn(i), 0, 0)),
                  _const_spec(wo.shape), _const_spec(w1.shape), _const_spec(w2.shape),
                  _const_spec(ln4.shape)],
        out_specs=tok(D_MODEL),
        out_shape=jax.ShapeDtypeStruct((t, D_MODEL), F32),
        compiler_params=_params("arbitrary"), name=f"post_mlp_l{layer}",
    )(x2d, mix_a, mix_b, mods, wo, w1, w2, ln4)


def _conv_in_kernel(x_ref, mods_ref, w_ref, u_ref, pz_ref):
    mods = mods_ref[0, 0]
    sh, sc = mods[0:1], mods[1:2]
    h = (x_ref[...] * (1.0 + sc) + sh).astype(BF16)
    proj = _dot(h, w_ref[...])
    a, gate = proj[:, 0:CONV_CH], proj[:, CONV_CH:2 * CONV_CH]
    u_ref[...] = a * jax.nn.sigmoid(gate)
    pz_ref[...] = proj[:, 2 * CONV_CH:]


def _conv_in(x2d, mods, layer, mod_row_fn, w, tm):
    t = x2d.shape[0]
    tok = lambda n: pl.BlockSpec((tm, n), lambda i: (i, 0))
    return pl.pallas_call(
        _conv_in_kernel, grid=(t // tm,),
        in_specs=[tok(D_MODEL),
                  pl.BlockSpec((1, 1, 6, D_MODEL), lambda i: (layer, mod_row_fn(i), 0, 0)),
                  _const_spec(w.shape)],
        out_specs=[tok(CONV_CH), tok(POOL_CH)],
        out_shape=[jax.ShapeDtypeStruct((t, CONV_CH), F32), jax.ShapeDtypeStruct((t, POOL_CH), F32)],
        compiler_params=_params("arbitrary"), name="conv_in",
    )(x2d, mods, w)


def _conv_pool_kernel(*refs, n_tiles, seq_len):
    if n_tiles > 1:
        (u_ref, ul_ref, ur_ref, pz_ref, pl_ref, pr_ref, cw_ref, cp_ref, wp_ref, ps_ref,
         uo_ref, do_ref, ubuf, pbuf, ushift, cbuf) = refs
    else:
        u_ref, pz_ref, cw_ref, cp_ref, wp_ref, ps_ref, uo_ref, do_ref, ubuf, pbuf, ushift, cbuf = refs
    tm = u_ref.shape[1]
    j = pl.program_id(1)
    halo_zero = jnp.zeros((CONV_HALO, CONV_CH), F32)
    for buf, mid, sides in ((ubuf, u_ref, (ul_ref, ur_ref) if n_tiles > 1 else None),
                            (pbuf, pz_ref, (pl_ref, pr_ref) if n_tiles > 1 else None)):
        buf[CONV_HALO:CONV_HALO + tm, :] = mid[0]
        if sides is None:
            buf[0:CONV_HALO, :] = halo_zero
            buf[CONV_HALO + tm:, :] = halo_zero
        else:
            buf[0:CONV_HALO, :] = jnp.where(j > 0, sides[0][0], halo_zero)
            buf[CONV_HALO + tm:, :] = jnp.where(j < n_tiles - 1, sides[1][0], halo_zero)

    rows_sh = ushift.shape[1]
    for b in range(SUBLANES):
        ushift[b] = ubuf[b:b + rows_sh, :]
    base = CONV_HALO - CONV_WIDTH // 2
    cp = cp_ref[...]

    def strip(i, carry):
        r0 = pl.multiple_of(i * CONV_STRIP, CONV_STRIP)
        acc = jnp.zeros((CONV_STRIP, CONV_CH), F32)
        for k in range(CONV_WIDTH):
            a, b = divmod(base + k, SUBLANES)
            w = jnp.concatenate([cw_ref[k]] * (CONV_STRIP // SUBLANES), axis=0)
            acc = acc + ushift[b, pl.ds(r0 + a * SUBLANES, CONV_STRIP), :] * w
        cbuf[pl.ds(r0, CONV_STRIP), :] = acc
        return carry

    lax.fori_loop(0, tm // CONV_STRIP, strip, 0)
    z = _layer_norm(cbuf[...] + cp[0:1], cp[1:2], cp[2:3])
    uo_ref[0] = (z * jax.nn.sigmoid(z)).astype(uo_ref.dtype)

    pos = j * tm + lax.broadcasted_iota(jnp.int32, (tm, 1), 0)
    outs = []
    for g, w in enumerate(POOL_WINDOWS):
        cols = slice(g * POOL_GC, (g + 1) * POOL_GC)
        s = jnp.zeros((tm, POOL_GC), F32)
        for d in range(-(w // 2), w // 2):
            s = s + pbuf[CONV_HALO + d:CONV_HALO + d + tm, cols]
        lo = jnp.maximum(pos - w // 2, 0)
        hi = jnp.minimum(pos + w // 2 - 1, seq_len - 1)
        cnt = (hi - lo + 1).astype(F32)
        dgrp = s / cnt - pz_ref[0][:, cols]
        outs.append(_dot(dgrp.astype(BF16), wp_ref[g]))
    do_ref[0] = (jnp.concatenate(outs, axis=1) * ps_ref[...]).astype(do_ref.dtype)


def _conv_pool(u2d, pz2d, b, t, tm, cw, cp, wp, ps):
    n_tiles = t // tm
    u3, p3 = u2d.reshape(b, t, CONV_CH), pz2d.reshape(b, t, POOL_CH)
    mid = pl.BlockSpec((1, tm, CONV_CH), lambda i, j: (i, j, 0))
    r = tm // CONV_HALO
    left = pl.BlockSpec((1, CONV_HALO, CONV_CH), lambda i, j: (i, jnp.maximum(j * r - 1, 0), 0))
    right = pl.BlockSpec((1, CONV_HALO, CONV_CH),
                         lambda i, j: (i, jnp.minimum((j + 1) * r, t // CONV_HALO - 1), 0))
    const = lambda a: pl.BlockSpec(a.shape, lambda i, j: (0,) * a.ndim)
    if n_tiles > 1:
        in_specs = [mid, left, right, mid, left, right]
        args = (u3, u3, u3, p3, p3, p3)
    else:
        in_specs = [mid, mid]
        args = (u3, p3)
    in_specs += [const(cw), const(cp), const(wp), const(ps)]
    uo, do = pl.pallas_call(
        functools.partial(_conv_pool_kernel, n_tiles=n_tiles, seq_len=t),
        grid=(b, n_tiles), in_specs=in_specs,
        out_specs=[mid, mid],
        out_shape=[jax.ShapeDtypeStruct((b, t, CONV_CH), BF16), jax.ShapeDtypeStruct((b, t, POOL_CH), BF16)],
        scratch_shapes=[pltpu.VMEM((tm + 2 * CONV_HALO, CONV_CH), F32),
                        pltpu.VMEM((tm + 2 * CONV_HALO, POOL_CH), F32),
                        pltpu.VMEM((SUBLANES, tm + 2 * CONV_HALO - SUBLANES, CONV_CH), F32),
                        pltpu.VMEM((tm, CONV_CH), F32)],
        compiler_params=_params("arbitrary", "arbitrary"),
        name=f"conv_pool_{'lat' if n_tiles > 1 else 'ctx'}",
    )(*args, cw, cp, wp, ps)
    return uo.reshape(b * t, CONV_CH), do.reshape(b * t, POOL_CH)


def _att_in_weights(w_att_in, with_v_tok):
    o = 2 * DIFF_QK + DIFF_V
    q_d, k_d, v_d = w_att_in[:, :DIFF_QK], w_att_in[:, DIFF_QK:2 * DIFF_QK], w_att_in[:, 2 * DIFF_QK:o]
    cq, ckv = w_att_in[:, o:o + Q_LORA], w_att_in[:, o + Q_LORA:o + Q_LORA + KV_LORA]
    kpe = jnp.pad(w_att_in[:, o + Q_LORA + KV_LORA:], _ROPE_LANE_PAD)
    w_t = jnp.concatenate([q_d, v_d, cq, ckv], axis=1).T.astype(BF16)
    w_r = jnp.concatenate([k_d, ckv, kpe] + ([v_d] if with_v_tok else []), axis=1).astype(BF16)
    return w_t, w_r


def _mla_weights(w_uq, w_ukv):
    wq = w_uq.reshape(Q_LORA, H_MLA, QK_NOPE + QK_ROPE)
    wq = jnp.pad(wq, ((0, 0), (0, 0), (0, MLA_SLAB - QK_NOPE - QK_ROPE))).reshape(Q_LORA, MLA_WIDE)
    wkv = w_ukv.reshape(KV_LORA, H_MLA, QK_NOPE + V_MLA)
    wk = jnp.pad(wkv[:, :, :QK_NOPE], ((0, 0), (0, 0), (0, MLA_SLAB - QK_NOPE))).reshape(KV_LORA, MLA_WIDE)
    wv = wkv[:, :, QK_NOPE:].reshape(KV_LORA, H_MLA * V_MLA)
    return wq.T.astype(BF16), wk.astype(BF16), wv.T.astype(BF16)


def kernel(x_prompt, x_sample, cache_diff_k, cache_diff_v, cache_mla_ckv, cache_mla_krope, c, c_ctx,
           w_ada, b_ada, ln_mix_g, ln_mix_b, ln_mlp_g, ln_mlp_b, w_mlp_in, w_mlp_out,
           w_att_in, w_uq, w_ukv, q_norm_g, kv_norm_g, lam_q1, lam_k1, lam_q2, lam_k2, diff_norm_g, w_att_out,
           w_conv_in, conv_w, conv_b, conv_norm_g, conv_norm_b, w_pool, pool_scale, w_conv_out):
    bc, tc, d = x_prompt.shape
    bl, tl, _ = x_sample.shape
    past = cache_diff_k.shape[2]

    cond8 = jnp.zeros((8, d), F32).at[0].set(c_ctx).at[1:1 + bl].set(c)
    mods = _ada_mods(cond8, w_ada, b_ada).reshape(DEPTH, 8, 6, d)
    ctx_row = lambda i: 0
    lat_row = lambda i: 1 + i // (tl // TOKEN_TILE)

    wqt, wk, wvt = _mla_weights(w_uq[0], w_ukv[0])
    norm_w = (q_norm_g[0].reshape(Q_LORA, 1), kv_norm_g[0].reshape(1, KV_LORA), kv_norm_g[0].reshape(KV_LORA, 1))
    att_w = lambda with_v: _att_in_weights(w_att_in[0], with_v) + norm_w + (wqt, wk, wvt)
    lam4 = jnp.stack([lam_q1[0], lam_k1[0], lam_q2[0], lam_k2[0]])
    diff_g = diff_norm_g[0].reshape(1, 2 * DH_DIFF)
    lam_init = 0.8 - 0.6 * math.exp(-0.3 * 0)
    ln4 = [jnp.stack([ln_mix_g[l], ln_mix_b[l], ln_mlp_g[l], ln_mlp_b[l]]) for l in range(DEPTH)]
    w1 = [w_mlp_in[l].astype(BF16) for l in range(DEPTH)]
    w2 = [w_mlp_out[l].astype(BF16) for l in range(DEPTH)]
    w_ao = w_att_out[0].astype(BF16)
    w_ci = w_conv_in[0].astype(BF16)
    w_co = w_conv_out[0].astype(BF16)
    conv_p = jnp.stack([conv_b[0], conv_norm_g[0], conv_norm_b[0]])
    conv_w8 = jnp.broadcast_to(conv_w[0][:, None, :], (CONV_WIDTH, SUBLANES, CONV_CH))
    wp = w_pool[0].astype(BF16)
    ps = pool_scale[0].reshape(1, POOL_CH)

    xp = x_prompt.reshape(bc * tc, d)
    outs = _att_in(xp, mods, ctx_row, att_w(True), None, TOKEN_TILE)
    kd_f, vd_f, ckv_f, kpe_f = outs[6:]
    o_d, o_m = _attention_ctx(outs[:6], lam4, diff_g, lam_init, bc, tc)
    xp = _post(xp, o_d, o_m, mods, 0, ctx_row, w_ao, w1[0], w2[0], ln4[0], TOKEN_TILE)
    u, pz = _conv_in(xp, mods, 1, ctx_row, w_ci, TOKEN_TILE)
    uo, do = _conv_pool(u, pz, bc, tc, tc, conv_w8, conv_p, wp, ps)
    xp = _post(xp, uo, do, mods, 1, ctx_row, w_co, w1[1], w2[1], ln4[1], TOKEN_TILE)

    xs = x_sample.reshape(bl * tl, d)
    outs = _att_in(xs, mods, lat_row, att_w(False), _rope_tables(tl), TOKEN_TILE)
    km_c, vmc_t = _cache_kv(cache_mla_ckv[:, 0].reshape(bl * past, KV_LORA),
                            jnp.pad(cache_mla_krope[:, 0].reshape(bl * past, QK_ROPE), _ROPE_LANE_PAD),
                            wk, wvt, bl)
    cache = (cache_diff_k[:, 0].reshape(bl * past, DIFF_QK), cache_diff_v[:, 0].reshape(bl * past, DIFF_V),
             km_c, vmc_t)
    o_d, o_m = _attention_lat(outs, cache, lam4, diff_g, lam_init, bl, tl)
    xs = _post(xs, o_d, o_m, mods, 0, lat_row, w_ao, w1[0], w2[0], ln4[0], TOKEN_TILE)
    u, pz = _conv_in(xs, mods, 1, lat_row, w_ci, TOKEN_TILE)
    uo, do = _conv_pool(u, pz, bl, tl, CONV_TILE, conv_w8, conv_p, wp, ps)
    xs = _post(xs, uo, do, mods, 1, lat_row, w_co, w1[1], w2[1], ln4[1], TOKEN_TILE)

    return (xp.reshape(bc, tc, d), xs.reshape(bl, tl, d),
            kd_f.reshape(bc, 1, tc, H_DIFF, 2 * DH_DIFF), vd_f.reshape(bc, 1, tc, H_DIFF, 2 * DH_DIFF),
            ckv_f.reshape(bc, 1, tc, KV_LORA), kpe_f.reshape(bc, 1, tc, QK_ROPE))
```

```python
import functools
import math

import jax
import jax.numpy as jnp
import numpy as np
from jax import lax
from jax.experimental import pallas as pl
from jax.experimental.pallas import tpu as pltpu

D_MODEL = 1024
DEPTH = 2
GRID_W = 64
H_DIFF = 4
DH_DIFF = 64
DIFF_QK = H_DIFF * 2 * DH_DIFF
DIFF_V = H_DIFF * 2 * DH_DIFF
H_MLA = 8
Q_LORA = 256
KV_LORA = 128
QK_NOPE = 64
QK_ROPE = 32
V_MLA = 64
ATT_IN = 2 * DIFF_QK + DIFF_V + Q_LORA + KV_LORA + QK_ROPE
CONV_CH = 512
CONV_WIDTH = 31
POOL_CH = 512
POOL_WINDOWS = (2, 4, 8, 16)
POOL_GC = POOL_CH // len(POOL_WINDOWS)
CONV_IN = 2 * CONV_CH + POOL_CH
D_FF = 4 * D_MODEL
ROPE_BASE = 10000.0
NORM_EPS = 1e-5
DEEPNORM_ALPHA = (2 * DEPTH) ** 0.25
LOG2E = math.log2(math.e)

LANES = 128
SUBLANES = 8
MLA_SLAB = LANES
MLA_WIDE = H_MLA * MLA_SLAB
_ROPE_LANE_PAD = ((0, 0), (QK_NOPE, MLA_SLAB - QK_NOPE - QK_ROPE))
VMEM_LIMIT = 56 * 1024 * 1024

TOKEN_TILE = 512
KEY_CHUNK = TOKEN_TILE
Q_TILE = 512
MLA_CHUNKS_PER_ROUND = 2
CONV_TILE = 512
CONV_HALO = 16
CONV_STRIP = 32
FF_CHUNK = 1024
POST_GROUPS = 2

F32 = jnp.float32
BF16 = jnp.bfloat16


def _dot(a, b):
    return jnp.dot(a, b, preferred_element_type=F32)


def _dot_nt(a, b):
    return lax.dot_general(a, b, (((1,), (1,)), ((), ())), preferred_element_type=F32)


def _layer_norm(x, g, b):
    mu = jnp.mean(x, axis=-1, keepdims=True)
    xc = x - mu
    var = jnp.mean(xc * xc, axis=-1, keepdims=True)
    return xc * lax.rsqrt(var + NORM_EPS) * g + b


def _rms_norm(x, g, axis=-1):
    ms = jnp.mean(x * x, axis=axis, keepdims=True)
    return x * lax.rsqrt(ms + NORM_EPS) * g


def _const_spec(shape):
    nd = len(shape)
    return pl.BlockSpec(shape, lambda *_: (0,) * nd, pipeline_mode=pl.Buffered(1))


def _params(*sem):
    return pltpu.CompilerParams(dimension_semantics=sem, vmem_limit_bytes=VMEM_LIMIT)


ADA_TILE = 1536


def _ada_kernel(cond_ref, w_ref, b_ref, out_ref):
    cond = cond_ref[...]
    act = (cond * jax.nn.sigmoid(cond)).astype(BF16)
    out_ref[0] = _dot(act, w_ref[0].astype(BF16)) + b_ref[0]


def _ada_mods(cond8, w_ada, b_ada):
    n = 6 * D_MODEL
    return pl.pallas_call(
        _ada_kernel,
        grid=(DEPTH, n // ADA_TILE),
        in_specs=[
            pl.BlockSpec((8, D_MODEL), lambda l, j: (0, 0)),
            pl.BlockSpec((1, D_MODEL, ADA_TILE), lambda l, j: (l, 0, j)),
            pl.BlockSpec((1, 1, ADA_TILE), lambda l, j: (l, 0, j)),
        ],
        out_specs=pl.BlockSpec((1, 8, ADA_TILE), lambda l, j: (l, 0, j)),
        out_shape=jax.ShapeDtypeStruct((DEPTH, 8, n), F32),
        compiler_params=_params("arbitrary", "arbitrary"),
        name="ada_mods",
    )(cond8, w_ada, b_ada.reshape(DEPTH, 1, n))


def _rope_tables(t_lat):
    pos = np.arange(t_lat)
    row = (pos // GRID_W).astype(np.float64)
    col = (pos % GRID_W).astype(np.float64)

    def tables(kinds):
        cos = np.ones((t_lat, LANES))
        sa = np.zeros((t_lat, LANES))
        sb = np.zeros((t_lat, LANES))
        for lane, kind in enumerate(kinds):
            if kind is None:
                continue
            axis, half, j, upper = kind
            ang = (row if axis == 0 else col) * ROPE_BASE ** (-float(j) / half)
            cos[:, lane] = np.cos(ang)
            if upper:
                sb[:, lane] = np.sin(ang)
            else:
                sa[:, lane] = -np.sin(ang)
        return [cos, sa, sb]

    def rot_kinds(n):
        half = n // 4
        kinds = []
        for i in range(n):
            axis, r = divmod(i, n // 2)
            kinds.append((axis, half, r % half, r >= half))
        return kinds

    diff = rot_kinds(DH_DIFF) * 2
    mla_q = [None] * QK_NOPE + rot_kinds(QK_ROPE) + [None] * (LANES - QK_NOPE - QK_ROPE)
    chan = np.stack([t.T for t in tables(diff) + tables(mla_q)]).astype(np.float32)
    tok = np.stack(tables(diff) + tables(mla_q)).astype(np.float32)
    return jnp.asarray(chan), jnp.asarray(tok)


def _rope_tok(x, cos, sa, sb, shift):
    return x * cos + pltpu.roll(x, LANES - shift, 1) * sa + pltpu.roll(x, shift, 1) * sb


def _rope_tok_wide(x, cos, sa, sb, shift):
    n = x.shape[1] // LANES
    return jnp.concatenate(
        [_rope_tok(x[:, i * LANES:(i + 1) * LANES], cos, sa, sb, shift) for i in range(n)], axis=1)


def _rope_chan(x, cos, sa, sb, shift):
    up = jnp.concatenate([x[shift:], x[:shift]], axis=0)
    down = jnp.concatenate([x[-shift:], x[:-shift]], axis=0)
    return x * cos + up * sa + down * sb


def _rope_chan_wide(x, cos, sa, sb, shift):
    n = x.shape[0] // LANES
    return jnp.concatenate(
        [_rope_chan(x[i * LANES:(i + 1) * LANES], cos, sa, sb, shift) for i in range(n)], axis=0)


_T_QD, _T_VD, _T_CQ, _T_CKV = 0, DIFF_QK, DIFF_QK + DIFF_V, DIFF_QK + DIFF_V + Q_LORA
_T_ROWS = _T_CKV + KV_LORA
_R_KD, _R_CKV, _R_KPE, _R_VD = 0, DIFF_QK, DIFF_QK + KV_LORA, DIFF_QK + KV_LORA + LANES

DIFF_QSCALE = DH_DIFF ** -0.5 * LOG2E
MLA_QSCALE = (QK_NOPE + QK_ROPE) ** -0.5 * LOG2E


def _att_in_body(x_ref, mods_ref, wt_ref, wr_ref, qg_ref, kvg_row_ref, kvg_col_ref, wqt_ref, wk_ref, wvt_ref):
    mods = mods_ref[0, 0]
    sh, sc = mods[0:1], mods[1:2]
    h = (x_ref[...] * (1.0 + sc) + sh).astype(BF16)
    pt = _dot_nt(wt_ref[...], h)
    pr = _dot(h, wr_ref[...])
    qd_t = pt[_T_QD:_T_QD + DIFF_QK] * DIFF_QSCALE
    vd_t = pt[_T_VD:_T_VD + DIFF_V]
    cqn_t = _rms_norm(pt[_T_CQ:_T_CQ + Q_LORA], qg_ref[...], axis=0)
    qm_t = _dot(wqt_ref[...], cqn_t.astype(BF16)) * MLA_QSCALE
    ckvn_t = _rms_norm(pt[_T_CKV:_T_CKV + KV_LORA], kvg_col_ref[...], axis=0)
    vm_t = _dot(wvt_ref[...], ckvn_t.astype(BF16))
    k_d = pr[:, _R_KD:_R_KD + DIFF_QK]
    ckv_n = _rms_norm(pr[:, _R_CKV:_R_CKV + KV_LORA], kvg_row_ref[...])
    kpe = pr[:, _R_KPE:_R_KPE + LANES]
    k_nope = _dot(ckv_n.astype(BF16), wk_ref[...])
    return pr, qd_t, vd_t, qm_t, vm_t, k_d, ckv_n, kpe, k_nope


def _mla_keys(k_nope, kpe_slab):
    return (k_nope + jnp.concatenate([kpe_slab] * H_MLA, axis=1)).astype(BF16)


def _att_in_ctx_kernel(x_ref, mods_ref, wt_ref, wr_ref, qg_ref, kvg_row_ref, kvg_col_ref, wqt_ref, wk_ref,
                       wvt_ref,
                       qdt_ref, kd_ref, vdt_ref, qmt_ref, km_ref, vmt_ref,
                       kdf_ref, vdf_ref, ckvf_ref, kpef_ref):
    pr, qd_t, vd_t, qm_t, vm_t, k_d, ckv_n, kpe, k_nope = _att_in_body(
        x_ref, mods_ref, wt_ref, wr_ref, qg_ref, kvg_row_ref, kvg_col_ref, wqt_ref, wk_ref, wvt_ref)
    qdt_ref[...] = qd_t.astype(BF16)
    kd_ref[...] = k_d.astype(BF16)
    vdt_ref[0] = vd_t.astype(BF16)
    qmt_ref[...] = qm_t.astype(BF16)
    km_ref[...] = _mla_keys(k_nope, kpe)
    vmt_ref[0] = vm_t.astype(BF16)
    v_d = pr[:, _R_VD:_R_VD + DIFF_V]
    for h in range(H_DIFF):
        kdf_ref[:, h, :] = k_d[:, h * LANES:(h + 1) * LANES]
        vdf_ref[:, h, :] = v_d[:, h * LANES:(h + 1) * LANES]
    ckvf_ref[...] = ckv_n
    kpef_ref[...] = kpe[:, QK_NOPE:QK_NOPE + QK_ROPE]


def _att_in_lat_kernel(x_ref, mods_ref, wt_ref, wr_ref, qg_ref, kvg_row_ref, kvg_col_ref, wqt_ref, wk_ref,
                       wvt_ref, tabc_ref, tabt_ref,
                       qdut_ref, qdrt_ref, kdr_ref, vdt_ref, qmut_ref, qmrt_ref, kmr_ref, vmt_ref):
    _, qd_t, vd_t, qm_t, vm_t, k_d, ckv_n, kpe, k_nope = _att_in_body(
        x_ref, mods_ref, wt_ref, wr_ref, qg_ref, kvg_row_ref, kvg_col_ref, wqt_ref, wk_ref, wvt_ref)
    qdut_ref[...] = qd_t.astype(BF16)
    qdrt_ref[...] = _rope_chan_wide(qd_t, tabc_ref[0], tabc_ref[1], tabc_ref[2], DH_DIFF // 4).astype(BF16)
    kdr_ref[...] = _rope_tok_wide(k_d, tabt_ref[0], tabt_ref[1], tabt_ref[2], DH_DIFF // 4).astype(BF16)
    vdt_ref[0] = vd_t.astype(BF16)
    qmut_ref[...] = qm_t.astype(BF16)
    qmrt_ref[...] = _rope_chan_wide(qm_t, tabc_ref[3], tabc_ref[4], tabc_ref[5], QK_ROPE // 4).astype(BF16)
    kpe_r = _rope_tok(kpe, tabt_ref[3], tabt_ref[4], tabt_ref[5], QK_ROPE // 4)
    kmr_ref[...] = _mla_keys(k_nope, kpe_r)
    vmt_ref[0] = vm_t.astype(BF16)


def _att_in(x2d, mods, mod_row_fn, wts, tables, tm):
    t = x2d.shape[0]
    nt = t // tm
    tok = lambda n: pl.BlockSpec((tm, n), lambda i: (i, 0))
    chan = lambda n: pl.BlockSpec((n, tm), lambda i: (0, i))
    chunk = lambda n: pl.BlockSpec((1, n, tm), lambda i: (i, 0, 0))
    in_specs = [tok(D_MODEL), pl.BlockSpec((1, 1, 6, D_MODEL), lambda i: (0, mod_row_fn(i), 0, 0))]
    in_specs += [_const_spec(w.shape) for w in wts]
    s_tok = lambda n, dt=BF16: jax.ShapeDtypeStruct((t, n), dt)
    s_chan = lambda n: jax.ShapeDtypeStruct((n, t), BF16)
    s_chunk = lambda n: jax.ShapeDtypeStruct((nt, n, tm), BF16)
    if tables is None:
        heads = pl.BlockSpec((tm, H_DIFF, 2 * DH_DIFF), lambda i: (i, 0, 0))
        s_heads = jax.ShapeDtypeStruct((t, H_DIFF, 2 * DH_DIFF), F32)
        out_shape = [s_chan(DIFF_QK), s_tok(DIFF_QK), s_chunk(DIFF_V), s_chan(MLA_WIDE), s_tok(MLA_WIDE),
                     s_chunk(H_MLA * V_MLA),
                     s_heads, s_heads, s_tok(KV_LORA, F32), s_tok(QK_ROPE, F32)]
        out_specs = [chan(DIFF_QK), tok(DIFF_QK), chunk(DIFF_V), chan(MLA_WIDE), tok(MLA_WIDE),
                     chunk(H_MLA * V_MLA), heads, heads, tok(KV_LORA), tok(QK_ROPE)]
        kern, args, name = _att_in_ctx_kernel, (), "att_in_ctx"
    else:
        tab_c, tab_t = tables
        t_lat = tab_t.shape[1]
        in_specs += [pl.BlockSpec((6, LANES, tm), lambda i: (0, 0, i % (t_lat // tm))),
                     pl.BlockSpec((6, tm, LANES), lambda i: (0, i % (t_lat // tm), 0))]
        out_shape = [s_chan(DIFF_QK), s_chan(DIFF_QK), s_tok(DIFF_QK), s_chunk(DIFF_V),
                     s_chan(MLA_WIDE), s_chan(MLA_WIDE), s_tok(MLA_WIDE), s_chunk(H_MLA * V_MLA)]
        out_specs = [chan(DIFF_QK), chan(DIFF_QK), tok(DIFF_QK), chunk(DIFF_V),
                     chan(MLA_WIDE), chan(MLA_WIDE), tok(MLA_WIDE), chunk(H_MLA * V_MLA)]
        kern, args, name = _att_in_lat_kernel, (tab_c, tab_t), "att_in_lat"
    return pl.pallas_call(
        kern, grid=(nt,), in_specs=in_specs, out_specs=out_specs, out_shape=out_shape,
        compiler_params=_params("arbitrary"), name=name,
    )(x2d, mods, *wts, *args)


def _cache_kv_kernel(ckv_ref, kpe_ref, wk_ref, wvt_ref, km_ref, vmt_ref):
    ckv_b = ckv_ref[...].astype(BF16)
    km_ref[...] = _mla_keys(_dot(ckv_b, wk_ref[...]), kpe_ref[...])
    past = vmt_ref.shape[2]
    vt = _dot_nt(wvt_ref[...], ckv_b)
    for b in range(vmt_ref.shape[0]):
        vmt_ref[b] = vt[:, b * past:(b + 1) * past].astype(BF16)


def _cache_kv(ckv2d, kpe_slab2d, wk, wvt, bl):
    t = ckv2d.shape[0]
    full = lambda a: pl.BlockSpec(a.shape, lambda i: (0,) * a.ndim)
    return pl.pallas_call(
        _cache_kv_kernel, grid=(1,),
        in_specs=[full(ckv2d), full(kpe_slab2d), full(wk), full(wvt)],
        out_specs=[pl.BlockSpec((t, MLA_WIDE), lambda i: (0, 0)),
                   pl.BlockSpec((bl, H_MLA * V_MLA, t // bl), lambda i: (0, 0, 0))],
        out_shape=[jax.ShapeDtypeStruct((t, MLA_WIDE), BF16),
                   jax.ShapeDtypeStruct((bl, H_MLA * V_MLA, t // bl), BF16)],
        compiler_params=_params("arbitrary"), name="cache_kv",
    )(ckv2d, kpe_slab2d, wk, wvt)


def _softmax_pv(s, vt, state):
    m_c = jnp.max(s, axis=0, keepdims=True)
    if state is None:
        p = jnp.exp2(s - m_c)
        return m_c, jnp.sum(p, axis=0, keepdims=True), _dot(vt, p.astype(BF16))
    m, l, acc = state
    m_new = jnp.maximum(m, m_c)
    alpha = jnp.exp2(m - m_new)
    p = jnp.exp2(s - m_new)
    l = alpha * l + jnp.sum(p, axis=0, keepdims=True)
    return m_new, l, alpha * acc + _dot(vt, p.astype(BF16))


def _attend(first, rest, chunks_per_round=1):
    items = [(si, qt, (lambda k=k: k), (lambda vt=vt: vt)) for si, (qt, k, vt) in enumerate(first)]
    if rest is not None:
        keys = KEY_CHUNK * chunks_per_round
        for c in range(rest[0][2].shape[0] // chunks_per_round):
            for si, (qt, k_ref, vt_ref) in enumerate(rest):
                items.append((si, qt, (lambda r=k_ref, c=c: r[c * keys:(c + 1) * keys, :]),
                              (lambda r=vt_ref, c=c: jnp.concatenate(
                                  [r[c * chunks_per_round + i] for i in range(chunks_per_round)], axis=1))))
    ns = len(first)
    groups = [items[i:i + ns] for i in range(0, len(items), ns)]
    scores = lambda grp: [_dot(k_fn(), qt) for (_, qt, k_fn, _) in grp]
    state = [None] * ns
    s_next = scores(groups[0])
    for g, grp in enumerate(groups):
        s_cur = s_next
        if g + 1 < len(groups):
            s_next = scores(groups[g + 1])
        for s, (si, _, _, vt_fn) in zip(s_cur, grp):
            state[si] = _softmax_pv(s, vt_fn(), state[si])
    return [acc / l for (_, l, acc) in state]


def _split_maps(qt):
    row = lax.broadcasted_iota(jnp.int32, qt.shape, 0)
    zero = jnp.zeros_like(qt)
    return jnp.where(row < DH_DIFF, qt, zero), jnp.where(row >= DH_DIFF, qt, zero)


def _lambda(lam_ref, lam_init):
    lv = lam_ref[...]
    a = jnp.sum(lv[0:1] * lv[1:2], axis=-1, keepdims=True)
    b = jnp.sum(lv[2:3] * lv[3:4], axis=-1, keepdims=True)
    return jnp.exp(a) - jnp.exp(b) + lam_init


def _diff_finish(o1_t, o2_t, lam_ref, g_ref, lam_init):
    o = (o1_t - _lambda(lam_ref, lam_init) * o2_t).T
    return (_rms_norm(o, g_ref[...]) * (1.0 - lam_init)).astype(BF16)


def _mla_finish(oa_t, ob_t):
    return jnp.concatenate([oa_t, ob_t], axis=0).T.astype(BF16)


def _diff_ctx_kernel(qt_ref, k_ref, vt_ref, lam_ref, g_ref, out_ref, *, lam_init):
    first = []
    for h in range(H_DIFF):
        sl = slice(h * LANES, (h + 1) * LANES)
        k, vt = k_ref[:, sl], vt_ref[0, sl, :]
        first += [(q, k, vt) for q in _split_maps(qt_ref[sl, :])]
    o = _attend(first, None)
    out_ref[...] = jnp.concatenate(
        [_diff_finish(o[2 * h], o[2 * h + 1], lam_ref, g_ref, lam_init) for h in range(H_DIFF)], axis=1)


def _diff_lat_kernel(qut_ref, qrt_ref, kc_ref, vc_ref, k_ref, vt_ref, lam_ref, g_ref, out_ref, *, lam_init):
    kc, vct = kc_ref[...].astype(BF16), vc_ref[...].T.astype(BF16)
    first = [(q, kc, vct) for q in _split_maps(qut_ref[...])]
    rest = [(q, k_ref, vt_ref) for q in _split_maps(qrt_ref[...])]
    o1, o2 = _attend(first, rest)
    out_ref[...] = _diff_finish(o1, o2, lam_ref, g_ref, lam_init)


def _mla_ctx_kernel(qt_ref, k_ref, vt_ref, out_ref):
    first = []
    for h in range(H_MLA):
        sl = slice(h * MLA_SLAB, (h + 1) * MLA_SLAB)
        first.append((qt_ref[sl, :], k_ref[:, sl], vt_ref[0, h * V_MLA:(h + 1) * V_MLA, :]))
    o = _attend(first, None)
    out_ref[...] = jnp.concatenate(
        [_mla_finish(o[2 * j], o[2 * j + 1]) for j in range(H_MLA // 2)], axis=1)


def _mla_lat_kernel(qua_ref, qub_ref, qra_ref, qrb_ref, kca_ref, kcb_ref, vcta_ref, vctb_ref, ka_ref, kb_ref,
                    vta_ref, vtb_ref, out_ref):
    first = [(qua_ref[...], kca_ref[...], vcta_ref[0]), (qub_ref[...], kcb_ref[...], vctb_ref[0])]
    rest = [(qra_ref[...], ka_ref, vta_ref), (qrb_ref[...], kb_ref, vtb_ref)]
    oa, ob = _attend(first, rest, MLA_CHUNKS_PER_ROUND)
    out_ref[...] = _mla_finish(oa, ob)


def _attention_ctx(streams, lam4, diff_g, lam_init, b, t):
    qd_t, kd, vd_t, qm_t, km, vm_t = streams
    per_chunk = vd_t.shape[2] // t
    sem = _params("arbitrary")
    qt = lambda w: pl.BlockSpec((w, t), lambda i: (0, i))
    keys = lambda w: pl.BlockSpec((t, w), lambda i: (i, 0))
    vals = lambda w: pl.BlockSpec((1, w, t), lambda i: (i // per_chunk, 0, i % per_chunk))
    o_d = pl.pallas_call(
        functools.partial(_diff_ctx_kernel, lam_init=lam_init),
        grid=(b,),
        in_specs=[qt(DIFF_QK), keys(DIFF_QK), vals(DIFF_V),
                  pl.BlockSpec(lam4.shape, lambda i: (0, 0)),
                  pl.BlockSpec(diff_g.shape, lambda i: (0, 0))],
        out_specs=keys(DIFF_V),
        out_shape=jax.ShapeDtypeStruct((b * t, DIFF_V), BF16),
        compiler_params=sem, name="diff_attn_ctx",
    )(qd_t, kd, vd_t, lam4, diff_g)
    o_m = pl.pallas_call(
        _mla_ctx_kernel,
        grid=(b,),
        in_specs=[qt(MLA_WIDE), keys(MLA_WIDE), vals(H_MLA * V_MLA)],
        out_specs=keys(H_MLA * V_MLA),
        out_shape=jax.ShapeDtypeStruct((b * t, H_MLA * V_MLA), BF16),
        compiler_params=sem, name="mla_attn_ctx",
    )(qm_t, km, vm_t)
    return o_d, o_m


def _attention_lat(streams, cache, lam4, diff_g, lam_init, b, t):
    qdu_t, qdr_t, kdr, vd_t, qmu_t, qmr_t, kmr, vm_t = streams
    kd_c, vd_c, km_c, vmc_t = cache
    past = kd_c.shape[0] // b
    nq = t // Q_TILE
    n_chunks = t // KEY_CHUNK
    sem = _params("arbitrary", "arbitrary", "arbitrary")
    qt = lambda f: pl.BlockSpec((LANES, Q_TILE), lambda i, h, j: (f(h), i * nq + j))
    keys = lambda n, f: pl.BlockSpec((n, LANES), lambda i, h, j: (i, f(h)))
    vals = pl.BlockSpec((n_chunks, LANES, KEY_CHUNK), lambda i, h, j: (i, h, 0))
    out = pl.BlockSpec((Q_TILE, LANES), lambda i, h, j: (i * nq + j, h))
    same, even, odd = (lambda h: h), (lambda h: 2 * h), (lambda h: 2 * h + 1)
    o_d = pl.pallas_call(
        functools.partial(_diff_lat_kernel, lam_init=lam_init),
        grid=(b, H_DIFF, nq),
        in_specs=[qt(same), qt(same), keys(past, same), keys(past, same), keys(t, same), vals,
                  pl.BlockSpec(lam4.shape, lambda i, h, j: (0, 0)),
                  pl.BlockSpec(diff_g.shape, lambda i, h, j: (0, 0))],
        out_specs=out,
        out_shape=jax.ShapeDtypeStruct((b * t, DIFF_V), BF16),
        compiler_params=sem, name="diff_attn_lat",
    )(qdu_t, qdr_t, kd_c, vd_c, kdr, vd_t, lam4, diff_g)
    o_m = pl.pallas_call(
        _mla_lat_kernel,
        grid=(b, H_MLA // 2, nq),
        in_specs=[qt(even), qt(odd), qt(even), qt(odd), keys(past, even), keys(past, odd),
                  pl.BlockSpec((1, V_MLA, past), lambda i, h, j: (i, 2 * h, 0)),
                  pl.BlockSpec((1, V_MLA, past), lambda i, h, j: (i, 2 * h + 1, 0)),
                  keys(t, even), keys(t, odd),
                  pl.BlockSpec((n_chunks, V_MLA, KEY_CHUNK), lambda i, h, j: (i, 2 * h, 0)),
                  pl.BlockSpec((n_chunks, V_MLA, KEY_CHUNK), lambda i, h, j: (i, 2 * h + 1, 0))],
        out_specs=out,
        out_shape=jax.ShapeDtypeStruct((b * t, H_MLA * V_MLA), BF16),
        compiler_params=sem, name="mla_attn_lat",
    )(qmu_t, qmu_t, qmr_t, qmr_t, km_c, km_c, vmc_t, vmc_t, kmr, kmr, vm_t, vm_t)
    return o_d, o_m


def _post_kernel(x_ref, ma_ref, mb_ref, mods_ref, wo_ref, w1_ref, w2_ref, ln_ref, out_ref):
    mods = mods_ref[0, 0]
    g_m, sh_f, sc_f, g_f = mods[2:3], mods[3:4], mods[4:5], mods[5:6]
    ln = ln_ref[...]
    rows = x_ref.shape[0] // POST_GROUPS
    grp = [slice(r * rows, (r + 1) * rows) for r in range(POST_GROUPS)]
    ys = [_dot(jnp.concatenate([ma_ref[g, :], mb_ref[g, :]], axis=1), wo_ref[...]) for g in grp]
    x1s = [_layer_norm(DEEPNORM_ALPHA * x_ref[g, :] + g_m * y, ln[0:1], ln[1:2]) for g, y in zip(grp, ys)]
    fs = []
    for x1 in x1s:
        h = (x1 * (1.0 + sc_f) + sh_f).astype(BF16)
        f = jnp.zeros(x1.shape, F32)
        for c in range(D_FF // FF_CHUNK):
            a = jnp.maximum(_dot(h, w1_ref[:, c * FF_CHUNK:(c + 1) * FF_CHUNK]), 0.0)
            f = f + _dot((a * a).astype(BF16), w2_ref[c * FF_CHUNK:(c + 1) * FF_CHUNK, :])
        fs.append(f)
    for g, x1, f in zip(grp, x1s, fs):
        out_ref[g, :] = _layer_norm(DEEPNORM_ALPHA * x1 + g_f * f, ln[2:3], ln[3:4])


def _post(x2d, mix_a, mix_b, mods, layer, mod_row_fn, wo, w1, w2, ln4, tm):
    t = x2d.shape[0]
    tok = lambda n: pl.BlockSpec((tm, n), lambda i: (i, 0))
    return pl.pallas_call(
        _post_kernel, grid=(t // tm,),
        in_specs=[tok(D_MODEL), tok(mix_a.shape[1]), tok(mix_b.shape[1]),
                  pl.BlockSpec((1, 1, 6, D_MODEL), lambda i: (layer, mod_row_fn(i), 0, 0)),
                  _const_spec(wo.shape), _const_spec(w1.shape), _const_spec(w2.shape),
                  _const_spec(ln4.shape)],
        out_specs=tok(D_MODEL),
        out_shape=jax.ShapeDtypeStruct((t, D_MODEL), F32),
        compiler_params=_params("arbitrary"), name=f"post_mlp_l{layer}",
    )(x2d, mix_a, mix_b, mods, wo, w1, w2, ln4)


def _conv_in_kernel(x_ref, mods_ref, w_ref, u_ref, pz_ref):
    mods = mods_ref[0, 0]
    sh, sc = mods[0:1], mods[1:2]
    h = (x_ref[...] * (1.0 + sc) + sh).astype(BF16)
    proj = _dot(h, w_ref[...])
    a, gate = proj[:, 0:CONV_CH], proj[:, CONV_CH:2 * CONV_CH]
    u_ref[...] = a * jax.nn.sigmoid(gate)
    pz_ref[...] = proj[:, 2 * CONV_CH:]


def _conv_in(x2d, mods, layer, mod_row_fn, w, tm):
    t = x2d.shape[0]
    tok = lambda n: pl.BlockSpec((tm, n), lambda i: (i, 0))
    return pl.pallas_call(
        _conv_in_kernel, grid=(t // tm,),
        in_specs=[tok(D_MODEL),
                  pl.BlockSpec((1, 1, 6, D_MODEL), lambda i: (layer, mod_row_fn(i), 0, 0)),
                  _const_spec(w.shape)],
        out_specs=[tok(CONV_CH), tok(POOL_CH)],
        out_shape=[jax.ShapeDtypeStruct((t, CONV_CH), F32), jax.ShapeDtypeStruct((t, POOL_CH), F32)],
        compiler_params=_params("arbitrary"), name="conv_in",
    )(x2d, mods, w)


def _conv_pool_kernel(*refs, n_tiles, seq_len):
    if n_tiles > 1:
        (u_ref, ul_ref, ur_ref, pz_ref, pl_ref, pr_ref, cw_ref, cp_ref, wp_ref, ps_ref,
         uo_ref, do_ref, ubuf, pbuf, ushift, cbuf) = refs
    else:
        u_ref, pz_ref, cw_ref, cp_ref, wp_ref, ps_ref, uo_ref, do_ref, ubuf, pbuf, ushift, cbuf = refs
    tm = u_ref.shape[1]
    j = pl.program_id(1)
    halo_zero = jnp.zeros((CONV_HALO, CONV_CH), F32)
    for buf, mid, sides in ((ubuf, u_ref, (ul_ref, ur_ref) if n_tiles > 1 else None),
                            (pbuf, pz_ref, (pl_ref, pr_ref) if n_tiles > 1 else None)):
        buf[CONV_HALO:CONV_HALO + tm, :] = mid[0]
        if sides is None:
            buf[0:CONV_HALO, :] = halo_zero
            buf[CONV_HALO + tm:, :] = halo_zero
        else:
            buf[0:CONV_HALO, :] = jnp.where(j > 0, sides[0][0], halo_zero)
            buf[CONV_HALO + tm:, :] = jnp.where(j < n_tiles - 1, sides[1][0], halo_zero)

    rows_sh = ushift.shape[1]
    for b in range(SUBLANES):
        ushift[b] = ubuf[b:b + rows_sh, :]
    base = CONV_HALO - CONV_WIDTH // 2
    cp = cp_ref[...]

    def strip(i, carry):
        r0 = pl.multiple_of(i * CONV_STRIP, CONV_STRIP)
        acc = jnp.zeros((CONV_STRIP, CONV_CH), F32)
        for k in range(CONV_WIDTH):
            a, b = divmod(base + k, SUBLANES)
            w = jnp.concatenate([cw_ref[k]] * (CONV_STRIP // SUBLANES), axis=0)
            acc = acc + ushift[b, pl.ds(r0 + a * SUBLANES, CONV_STRIP), :] * w
        cbuf[pl.ds(r0, CONV_STRIP), :] = acc
        return carry

    lax.fori_loop(0, tm // CONV_STRIP, strip, 0)
    z = _layer_norm(cbuf[...] + cp[0:1], cp[1:2], cp[2:3])
    uo_ref[0] = (z * jax.nn.sigmoid(z)).astype(uo_ref.dtype)

    pos = j * tm + lax.broadcasted_iota(jnp.int32, (tm, 1), 0)
    outs = []
    for g, w in enumerate(POOL_WINDOWS):
        cols = slice(g * POOL_GC, (g + 1) * POOL_GC)
        s = jnp.zeros((tm, POOL_GC), F32)
        for d in range(-(w // 2), w // 2):
            s = s + pbuf[CONV_HALO + d:CONV_HALO + d + tm, cols]
        lo = jnp.maximum(pos - w // 2, 0)
        hi = jnp.minimum(pos + w // 2 - 1, seq_len - 1)
        cnt = (hi - lo + 1).astype(F32)
        dgrp = s / cnt - pz_ref[0][:, cols]
        outs.append(_dot(dgrp.astype(BF16), wp_ref[g]))
    do_ref[0] = (jnp.concatenate(outs, axis=1) * ps_ref[...]).astype(do_ref.dtype)


def _conv_pool(u2d, pz2d, b, t, tm, cw, cp, wp, ps):
    n_tiles = t // tm
    u3, p3 = u2d.reshape(b, t, CONV_CH), pz2d.reshape(b, t, POOL_CH)
    mid = pl.BlockSpec((1, tm, CONV_CH), lambda i, j: (i, j, 0))
    r = tm // CONV_HALO
    left = pl.BlockSpec((1, CONV_HALO, CONV_CH), lambda i, j: (i, jnp.maximum(j * r - 1, 0), 0))
    right = pl.BlockSpec((1, CONV_HALO, CONV_CH),
                         lambda i, j: (i, jnp.minimum((j + 1) * r, t // CONV_HALO - 1), 0))
    const = lambda a: pl.BlockSpec(a.shape, lambda i, j: (0,) * a.ndim)
    if n_tiles > 1:
        in_specs = [mid, left, right, mid, left, right]
        args = (u3, u3, u3, p3, p3, p3)
    else:
        in_specs = [mid, mid]
        args = (u3, p3)
    in_specs += [const(cw), const(cp), const(wp), const(ps)]
    uo, do = pl.pallas_call(
        functools.partial(_conv_pool_kernel, n_tiles=n_tiles, seq_len=t),
        grid=(b, n_tiles), in_specs=in_specs,
        out_specs=[mid, mid],
        out_shape=[jax.ShapeDtypeStruct((b, t, CONV_CH), BF16), jax.ShapeDtypeStruct((b, t, POOL_CH), BF16)],
        scratch_shapes=[pltpu.VMEM((tm + 2 * CONV_HALO, CONV_CH), F32),
                        pltpu.VMEM((tm + 2 * CONV_HALO, POOL_CH), F32),
                        pltpu.VMEM((SUBLANES, tm + 2 * CONV_HALO - SUBLANES, CONV_CH), F32),
                        pltpu.VMEM((tm, CONV_CH), F32)],
        compiler_params=_params("arbitrary", "arbitrary"),
        name=f"conv_pool_{'lat' if n_tiles > 1 else 'ctx'}",
    )(*args, cw, cp, wp, ps)
    return uo.reshape(b * t, CONV_CH), do.reshape(b * t, POOL_CH)


def _att_in_weights(w_att_in, with_v_tok):
    o = 2 * DIFF_QK + DIFF_V
    q_d, k_d, v_d = w_att_in[:, :DIFF_QK], w_att_in[:, DIFF_QK:2 * DIFF_QK], w_att_in[:, 2 * DIFF_QK:o]
    cq, ckv = w_att_in[:, o:o + Q_LORA], w_att_in[:, o + Q_LORA:o + Q_LORA + KV_LORA]
    kpe = jnp.pad(w_att_in[:, o + Q_LORA + KV_LORA:], _ROPE_LANE_PAD)
    w_t = jnp.concatenate([q_d, v_d, cq, ckv], axis=1).T.astype(BF16)
    w_r = jnp.concatenate([k_d, ckv, kpe] + ([v_d] if with_v_tok else []), axis=1).astype(BF16)
    return w_t, w_r


def _mla_weights(w_uq, w_ukv):
    wq = w_uq.reshape(Q_LORA, H_MLA, QK_NOPE + QK_ROPE)
    wq = jnp.pad(wq, ((0, 0), (0, 0), (0, MLA_SLAB - QK_NOPE - QK_ROPE))).reshape(Q_LORA, MLA_WIDE)
    wkv = w_ukv.reshape(KV_LORA, H_MLA, QK_NOPE + V_MLA)
    wk = jnp.pad(wkv[:, :, :QK_NOPE], ((0, 0), (0, 0), (0, MLA_SLAB - QK_NOPE))).reshape(KV_LORA, MLA_WIDE)
    wv = wkv[:, :, QK_NOPE:].reshape(KV_LORA, H_MLA * V_MLA)
    return wq.T.astype(BF16), wk.astype(BF16), wv.T.astype(BF16)


def kernel(x_prompt, x_sample, cache_diff_k, cache_diff_v, cache_mla_ckv, cache_mla_krope, c, c_ctx,
           w_ada, b_ada, ln_mix_g, ln_mix_b, ln_mlp_g, ln_mlp_b, w_mlp_in, w_mlp_out,
           w_att_in, w_uq, w_ukv, q_norm_g, kv_norm_g, lam_q1, lam_k1, lam_q2, lam_k2, diff_norm_g, w_att_out,
           w_conv_in, conv_w, conv_b, conv_norm_g, conv_norm_b, w_pool, pool_scale, w_conv_out):
    bc, tc, d = x_prompt.shape
    bl, tl, _ = x_sample.shape
    past = cache_diff_k.shape[2]

    cond8 = jnp.zeros((8, d), F32).at[0].set(c_ctx).at[1:1 + bl].set(c)
    mods = _ada_mods(cond8, w_ada, b_ada).reshape(DEPTH, 8, 6, d)
    ctx_row = lambda i: 0
    lat_row = lambda i: 1 + i // (tl // TOKEN_TILE)

    wqt, wk, wvt = _mla_weights(w_uq[0], w_ukv[0])
    norm_w = (q_norm_g[0].reshape(Q_LORA, 1), kv_norm_g[0].reshape(1, KV_LORA), kv_norm_g[0].reshape(KV_LORA, 1))
    att_w = lambda with_v: _att_in_weights(w_att_in[0], with_v) + norm_w + (wqt, wk, wvt)
    lam4 = jnp.stack([lam_q1[0], lam_k1[0], lam_q2[0], lam_k2[0]])
    diff_g = diff_norm_g[0].reshape(1, 2 * DH_DIFF)
    lam_init = 0.8 - 0.6 * math.exp(-0.3 * 0)
    ln4 = [jnp.stack([ln_mix_g[l], ln_mix_b[l], ln_mlp_g[l], ln_mlp_b[l]]) for l in range(DEPTH)]
    w1 = [w_mlp_in[l].astype(BF16) for l in range(DEPTH)]
    w2 = [w_mlp_out[l].astype(BF16) for l in range(DEPTH)]
    w_ao = w_att_out[0].astype(BF16)
    w_ci = w_conv_in[0].astype(BF16)
    w_co = w_conv_out[0].astype(BF16)
    conv_p = jnp.stack([conv_b[0], conv_norm_g[0], conv_norm_b[0]])
    conv_w8 = jnp.broadcast_to(conv_w[0][:, None, :], (CONV_WIDTH, SUBLANES, CONV_CH))
    wp = w_pool[0].astype(BF16)
    ps = pool_scale[0].reshape(1, POOL_CH)

    xp = x_prompt.reshape(bc * tc, d)
    outs = _att_in(xp, mods, ctx_row, att_w(True), None, TOKEN_TILE)
    kd_f, vd_f, ckv_f, kpe_f = outs[6:]
    o_d, o_m = _attention_ctx(outs[:6], lam4, diff_g, lam_init, bc, tc)
    xp = _post(xp, o_d, o_m, mods, 0, ctx_row, w_ao, w1[0], w2[0], ln4[0], TOKEN_TILE)
    u, pz = _conv_in(xp, mods, 1, ctx_row, w_ci, TOKEN_TILE)
    uo, do = _conv_pool(u, pz, bc, tc, tc, conv_w8, conv_p, wp, ps)
    xp = _post(xp, uo, do, mods, 1, ctx_row, w_co, w1[1], w2[1], ln4[1], TOKEN_TILE)

    xs = x_sample.reshape(bl * tl, d)
    outs = _att_in(xs, mods, lat_row, att_w(False), _rope_tables(tl), TOKEN_TILE)
    km_c, vmc_t = _cache_kv(cache_mla_ckv[:, 0].reshape(bl * past, KV_LORA),
                            jnp.pad(cache_mla_krope[:, 0].reshape(bl * past, QK_ROPE), _ROPE_LANE_PAD),
                            wk, wvt, bl)
    cache = (cache_diff_k[:, 0].reshape(bl * past, DIFF_QK), cache_diff_v[:, 0].reshape(bl * past, DIFF_V),
             km_c, vmc_t)
    o_d, o_m = _attention_lat(outs, cache, lam4, diff_g, lam_init, bl, tl)
    xs = _post(xs, o_d, o_m, mods, 0, lat_row, w_ao, w1[0], w2[0], ln4[0], TOKEN_TILE)
    u, pz = _conv_in(xs, mods, 1, lat_row, w_ci, TOKEN_TILE)
    uo, do = _conv_pool(u, pz, bl, tl, CONV_TILE, conv_w8, conv_p, wp, ps)
    xs = _post(xs, uo, do, mods, 1, lat_row, w_co, w1[1], w2[1], ln4[1], TOKEN_TILE)

    return (xp.reshape(bc, tc, d), xs.reshape(bl, tl, d),
            kd_f.reshape(bc, 1, tc, H_DIFF, 2 * DH_DIFF), vd_f.reshape(bc, 1, tc, H_DIFF, 2 * DH_DIFF),
            ckv_f.reshape(bc, 1, tc, KV_LORA), kpe_f.reshape(bc, 1, tc, QK_ROPE))
```

```python
import functools
import math

import jax
import jax.numpy as jnp
import numpy as np
from jax import lax
from jax.experimental import pallas as pl
from jax.experimental.pallas import tpu as pltpu

D_MODEL = 1024
DEPTH = 2
GRID_W = 64
H_DIFF = 4
DH_DIFF = 64
DIFF_QK = H_DIFF * 2 * DH_DIFF
DIFF_V = H_DIFF * 2 * DH_DIFF
H_MLA = 8
Q_LORA = 256
KV_LORA = 128
QK_NOPE = 64
QK_ROPE = 32
V_MLA = 64
ATT_IN = 2 * DIFF_QK + DIFF_V + Q_LORA + KV_LORA + QK_ROPE
CONV_CH = 512
CONV_WIDTH = 31
POOL_CH = 512
POOL_WINDOWS = (2, 4, 8, 16)
POOL_GC = POOL_CH // len(POOL_WINDOWS)
CONV_IN = 2 * CONV_CH + POOL_CH
D_FF = 4 * D_MODEL
ROPE_BASE = 10000.0
NORM_EPS = 1e-5
DEEPNORM_ALPHA = (2 * DEPTH) ** 0.25
LOG2E = math.log2(math.e)

LANES = 128
SUBLANES = 8
MLA_SLAB = LANES
MLA_WIDE = H_MLA * MLA_SLAB
_ROPE_LANE_PAD = ((0, 0), (QK_NOPE, MLA_SLAB - QK_NOPE - QK_ROPE))
VMEM_LIMIT = 56 * 1024 * 1024

TOKEN_TILE = 512
KEY_CHUNK = TOKEN_TILE
Q_TILE = 512
MLA_CHUNKS_PER_ROUND = 2
CONV_TILE = 512
CONV_HALO = 16
CONV_STRIP = 32
FF_CHUNK = 1024
POST_GROUPS = 2

F32 = jnp.float32
BF16 = jnp.bfloat16


def _dot(a, b):
    return jnp.dot(a, b, preferred_element_type=F32)


def _dot_nt(a, b):
    return lax.dot_general(a, b, (((1,), (1,)), ((), ())), preferred_element_type=F32)


def _layer_norm(x, g, b):
    mu = jnp.mean(x, axis=-1, keepdims=True)
    xc = x - mu
    var = jnp.mean(xc * xc, axis=-1, keepdims=True)
    return xc * lax.rsqrt(var + NORM_EPS) * g + b


def _rms_norm(x, g, axis=-1):
    ms = jnp.mean(x * x, axis=axis, keepdims=True)
    return x * lax.rsqrt(ms + NORM_EPS) * g


def _const_spec(shape):
    nd = len(shape)
    return pl.BlockSpec(shape, lambda *_: (0,) * nd, pipeline_mode=pl.Buffered(1))


def _params(*sem):
    return pltpu.CompilerParams(dimension_semantics=sem, vmem_limit_bytes=VMEM_LIMIT)


ADA_TILE = 1536


def _ada_kernel(cond_ref, w_ref, b_ref, out_ref):
    cond = cond_ref[...]
    act = (cond * jax.nn.sigmoid(cond)).astype(BF16)
    out_ref[0] = _dot(act, w_ref[0].astype(BF16)) + b_ref[0]


def _ada_mods(cond8, w_ada, b_ada):
    n = 6 * D_MODEL
    return pl.pallas_call(
        _ada_kernel,
        grid=(DEPTH, n // ADA_TILE),
        in_specs=[
            pl.BlockSpec((8, D_MODEL), lambda l, j: (0, 0)),
            pl.BlockSpec((1, D_MODEL, ADA_TILE), lambda l, j: (l, 0, j)),
            pl.BlockSpec((1, 1, ADA_TILE), lambda l, j: (l, 0, j)),
        ],
        out_specs=pl.BlockSpec((1, 8, ADA_TILE), lambda l, j: (l, 0, j)),
        out_shape=jax.ShapeDtypeStruct((DEPTH, 8, n), F32),
        compiler_params=_params("arbitrary", "arbitrary"),
        name="ada_mods",
    )(cond8, w_ada, b_ada.reshape(DEPTH, 1, n))


def _rope_tables(t_lat):
    pos = np.arange(t_lat)
    row = (pos // GRID_W).astype(np.float64)
    col = (pos % GRID_W).astype(np.float64)

    def tables(kinds):
        cos = np.ones((t_lat, LANES))
        sa = np.zeros((t_lat, LANES))
        sb = np.zeros((t_lat, LANES))
        for lane, kind in enumerate(kinds):
            if kind is None:
                continue
            axis, half, j, upper = kind
            ang = (row if axis == 0 else col) * ROPE_BASE ** (-float(j) / half)
            cos[:, lane] = np.cos(ang)
            if upper:
                sb[:, lane] = np.sin(ang)
            else:
                sa[:, lane] = -np.sin(ang)
        return [cos, sa, sb]

    def rot_kinds(n):
        half = n // 4
        kinds = []
        for i in range(n):
            axis, r = divmod(i, n // 2)
            kinds.append((axis, half, r % half, r >= half))
        return kinds

    diff = rot_kinds(DH_DIFF) * 2
    mla_q = [None] * QK_NOPE + rot_kinds(QK_ROPE) + [None] * (LANES - QK_NOPE - QK_ROPE)
    chan = np.stack([t.T for t in tables(diff) + tables(mla_q)]).astype(np.float32)
    tok = np.stack(tables(diff) + tables(mla_q)).astype(np.float32)
    return jnp.asarray(chan), jnp.asarray(tok)


def _rope_tok(x, cos, sa, sb, shift):
    return x * cos + pltpu.roll(x, LANES - shift, 1) * sa + pltpu.roll(x, shift, 1) * sb


def _rope_tok_wide(x, cos, sa, sb, shift):
    n = x.shape[1] // LANES
    return jnp.concatenate(
        [_rope_tok(x[:, i * LANES:(i + 1) * LANES], cos, sa, sb, shift) for i in range(n)], axis=1)


def _rope_chan(x, cos, sa, sb, shift):
    up = jnp.concatenate([x[shift:], x[:shift]], axis=0)
    down = jnp.concatenate([x[-shift:], x[:-shift]], axis=0)
    return x * cos + up * sa + down * sb


def _rope_chan_wide(x, cos, sa, sb, shift):
    n = x.shape[0] // LANES
    return jnp.concatenate(
        [_rope_chan(x[i * LANES:(i + 1) * LANES], cos, sa, sb, shift) for i in range(n)], axis=0)


_T_QD, _T_VD, _T_CQ, _T_CKV = 0, DIFF_QK, DIFF_QK + DIFF_V, DIFF_QK + DIFF_V + Q_LORA
_T_ROWS = _T_CKV + KV_LORA
_R_KD, _R_CKV, _R_KPE, _R_VD = 0, DIFF_QK, DIFF_QK + KV_LORA, DIFF_QK + KV_LORA + LANES

DIFF_QSCALE = DH_DIFF ** -0.5 * LOG2E
MLA_QSCALE = (QK_NOPE + QK_ROPE) ** -0.5 * LOG2E


def _att_in_body(x_ref, mods_ref, wt_ref, wr_ref, qg_ref, kvg_row_ref, kvg_col_ref, wqt_ref, wk_ref, wvt_ref):
    mods = mods_ref[0, 0]
    sh, sc = mods[0:1], mods[1:2]
    h = (x_ref[...] * (1.0 + sc) + sh).astype(BF16)
    pt = _dot_nt(wt_ref[...], h)
    pr = _dot(h, wr_ref[...])
    qd_t = pt[_T_QD:_T_QD + DIFF_QK] * DIFF_QSCALE
    vd_t = pt[_T_VD:_T_VD + DIFF_V]
    cqn_t = _rms_norm(pt[_T_CQ:_T_CQ + Q_LORA], qg_ref[...], axis=0)
    qm_t = _dot(wqt_ref[...], cqn_t.astype(BF16)) * MLA_QSCALE
    ckvn_t = _rms_norm(pt[_T_CKV:_T_CKV + KV_LORA], kvg_col_ref[...], axis=0)
    vm_t = _dot(wvt_ref[...], ckvn_t.astype(BF16))
    k_d = pr[:, _R_KD:_R_KD + DIFF_QK]
    ckv_n = _rms_norm(pr[:, _R_CKV:_R_CKV + KV_LORA], kvg_row_ref[...])
    kpe = pr[:, _R_KPE:_R_KPE + LANES]
    k_nope = _dot(ckv_n.astype(BF16), wk_ref[...])
    return pr, qd_t, vd_t, qm_t, vm_t, k_d, ckv_n, kpe, k_nope


def _mla_keys(k_nope, kpe_slab):
    return (k_nope + jnp.concatenate([kpe_slab] * H_MLA, axis=1)).astype(BF16)


def _att_in_ctx_kernel(x_ref, mods_ref, wt_ref, wr_ref, qg_ref, kvg_row_ref, kvg_col_ref, wqt_ref, wk_ref,
                       wvt_ref,
                       qdt_ref, kd_ref, vdt_ref, qmt_ref, km_ref, vmt_ref,
                       kdf_ref, vdf_ref, ckvf_ref, kpef_ref):
    pr, qd_t, vd_t, qm_t, vm_t, k_d, ckv_n, kpe, k_nope = _att_in_body(
        x_ref, mods_ref, wt_ref, wr_ref, qg_ref, kvg_row_ref, kvg_col_ref, wqt_ref, wk_ref, wvt_ref)
    qdt_ref[...] = qd_t.astype(BF16)
    kd_ref[...] = k_d.astype(BF16)
    vdt_ref[0] = vd_t.astype(BF16)
    qmt_ref[...] = qm_t.astype(BF16)
    km_ref[...] = _mla_keys(k_nope, kpe)
    vmt_ref[0] = vm_t.astype(BF16)
    v_d = pr[:, _R_VD:_R_VD + DIFF_V]
    for h in range(H_DIFF):
        kdf_ref[:, h, :] = k_d[:, h * LANES:(h + 1) * LANES]
        vdf_ref[:, h, :] = v_d[:, h * LANES:(h + 1) * LANES]
    ckvf_ref[...] = ckv_n
    kpef_ref[...] = kpe[:, QK_NOPE:QK_NOPE + QK_ROPE]


def _att_in_lat_kernel(x_ref, mods_ref, wt_ref, wr_ref, qg_ref, kvg_row_ref, kvg_col_ref, wqt_ref, wk_ref,
                       wvt_ref, tabc_ref, tabt_ref,
                       qdut_ref, qdrt_ref, kdr_ref, vdt_ref, qmut_ref, qmrt_ref, kmr_ref, vmt_ref):
    _, qd_t, vd_t, qm_t, vm_t, k_d, ckv_n, kpe, k_nope = _att_in_body(
        x_ref, mods_ref, wt_ref, wr_ref, qg_ref, kvg_row_ref, kvg_col_ref, wqt_ref, wk_ref, wvt_ref)
    qdut_ref[...] = qd_t.astype(BF16)
    qdrt_ref[...] = _rope_chan_wide(qd_t, tabc_ref[0], tabc_ref[1], tabc_ref[2], DH_DIFF // 4).astype(BF16)
    kdr_ref[...] = _rope_tok_wide(k_d, tabt_ref[0], tabt_ref[1], tabt_ref[2], DH_DIFF // 4).astype(BF16)
    vdt_ref[0] = vd_t.astype(BF16)
    qmut_ref[...] = qm_t.astype(BF16)
    qmrt_ref[...] = _rope_chan_wide(qm_t, tabc_ref[3], tabc_ref[4], tabc_ref[5], QK_ROPE // 4).astype(BF16)
    kpe_r = _rope_tok(kpe, tabt_ref[3], tabt_ref[4], tabt_ref[5], QK_ROPE // 4)
    kmr_ref[...] = _mla_keys(k_nope, kpe_r)
    vmt_ref[0] = vm_t.astype(BF16)


def _att_in(x2d, mods, mod_row_fn, wts, tables, tm):
    t = x2d.shape[0]
    nt = t // tm
    tok = lambda n: pl.BlockSpec((tm, n), lambda i: (i, 0))
    chan = lambda n: pl.BlockSpec((n, tm), lambda i: (0, i))
    chunk = lambda n: pl.BlockSpec((1, n, tm), lambda i: (i, 0, 0))
    in_specs = [tok(D_MODEL), pl.BlockSpec((1, 1, 6, D_MODEL), lambda i: (0, mod_row_fn(i), 0, 0))]
    in_specs += [_const_spec(w.shape) for w in wts]
    s_tok = lambda n, dt=BF16: jax.ShapeDtypeStruct((t, n), dt)
    s_chan = lambda n: jax.ShapeDtypeStruct((n, t), BF16)
    s_chunk = lambda n: jax.ShapeDtypeStruct((nt, n, tm), BF16)
    if tables is None:
        heads = pl.BlockSpec((tm, H_DIFF, 2 * DH_DIFF), lambda i: (i, 0, 0))
        s_heads = jax.ShapeDtypeStruct((t, H_DIFF, 2 * DH_DIFF), F32)
        out_shape = [s_chan(DIFF_QK), s_tok(DIFF_QK), s_chunk(DIFF_V), s_chan(MLA_WIDE), s_tok(MLA_WIDE),
                     s_chunk(H_MLA * V_MLA),
                     s_heads, s_heads, s_tok(KV_LORA, F32), s_tok(QK_ROPE, F32)]
        out_specs = [chan(DIFF_QK), tok(DIFF_QK), chunk(DIFF_V), chan(MLA_WIDE), tok(MLA_WIDE),
                     chunk(H_MLA * V_MLA), heads, heads, tok(KV_LORA), tok(QK_ROPE)]
        kern, args, name = _att_in_ctx_kernel, (), "att_in_ctx"
    else:
        tab_c, tab_t = tables
        t_lat = tab_t.shape[1]
        in_specs += [pl.BlockSpec((6, LANES, tm), lambda i: (0, 0, i % (t_lat // tm))),
                     pl.BlockSpec((6, tm, LANES), lambda i: (0, i % (t_lat // tm), 0))]
        out_shape = [s_chan(DIFF_QK), s_chan(DIFF_QK), s_tok(DIFF_QK), s_chunk(DIFF_V),
                     s_chan(MLA_WIDE), s_chan(MLA_WIDE), s_tok(MLA_WIDE), s_chunk(H_MLA * V_MLA)]
        out_specs = [chan(DIFF_QK), chan(DIFF_QK), tok(DIFF_QK), chunk(DIFF_V),
                     chan(MLA_WIDE), chan(MLA_WIDE), tok(MLA_WIDE), chunk(H_MLA * V_MLA)]
        kern, args, name = _att_in_lat_kernel, (tab_c, tab_t), "att_in_lat"
    return pl.pallas_call(
        kern, grid=(nt,), in_specs=in_specs, out_specs=out_specs, out_shape=out_shape,
        compiler_params=_params("arbitrary"), name=name,
    )(x2d, mods, *wts, *args)


def _cache_kv_kernel(ckv_ref, kpe_ref, wk_ref, wvt_ref, km_ref, vmt_ref):
    ckv_b = ckv_ref[...].astype(BF16)
    km_ref[...] = _mla_keys(_dot(ckv_b, wk_ref[...]), kpe_ref[...])
    past = vmt_ref.shape[2]
    vt = _dot_nt(wvt_ref[...], ckv_b)
    for b in range(vmt_ref.shape[0]):
        vmt_ref[b] = vt[:, b * past:(b + 1) * past].astype(BF16)


def _cache_kv(ckv2d, kpe_slab2d, wk, wvt, bl):
    t = ckv2d.shape[0]
    full = lambda a: pl.BlockSpec(a.shape, lambda i: (0,) * a.ndim)
    return pl.pallas_call(
        _cache_kv_kernel, grid=(1,),
        in_specs=[full(ckv2d), full(kpe_slab2d), full(wk), full(wvt)],
        out_specs=[pl.BlockSpec((t, MLA_WIDE), lambda i: (0, 0)),
                   pl.BlockSpec((bl, H_MLA * V_MLA, t // bl), lambda i: (0, 0, 0))],
        out_shape=[jax.ShapeDtypeStruct((t, MLA_WIDE), BF16),
                   jax.ShapeDtypeStruct((bl, H_MLA * V_MLA, t // bl), BF16)],
        compiler_params=_params("arbitrary"), name="cache_kv",
    )(ckv2d, kpe_slab2d, wk, wvt)


def _softmax_pv(s, vt, state):
    m_c = jnp.max(s, axis=0, keepdims=True)
    if state is None:
        p = jnp.exp2(s - m_c)
        return m_c, jnp.sum(p, axis=0, keepdims=True), _dot(vt, p.astype(BF16))
    m, l, acc = state
    m_new = jnp.maximum(m, m_c)
    alpha = jnp.exp2(m - m_new)
    p = jnp.exp2(s - m_new)
    l = alpha * l + jnp.sum(p, axis=0, keepdims=True)
    return m_new, l, alpha * acc + _dot(vt, p.astype(BF16))


def _attend(first, rest, chunks_per_round=1):
    items = [(si, qt, (lambda k=k: k), (lambda vt=vt: vt)) for si, (qt, k, vt) in enumerate(first)]
    if rest is not None:
        keys = KEY_CHUNK * chunks_per_round
        for c in range(rest[0][2].shape[0] // chunks_per_round):
            for si, (qt, k_ref, vt_ref) in enumerate(rest):
                items.append((si, qt, (lambda r=k_ref, c=c: r[c * keys:(c + 1) * keys, :]),
                              (lambda r=vt_ref, c=c: jnp.concatenate(
                                  [r[c * chunks_per_round + i] for i in range(chunks_per_round)], axis=1))))
    ns = len(first)
    groups = [items[i:i + ns] for i in range(0, len(items), ns)]
    scores = lambda grp: [_dot(k_fn(), qt) for (_, qt, k_fn, _) in grp]
    state = [None] * ns
    s_next = scores(groups[0])
    for g, grp in enumerate(groups):
        s_cur = s_next
        if g + 1 < len(groups):
            s_next = scores(groups[g + 1])
        for s, (si, _, _, vt_fn) in zip(s_cur, grp):
            state[si] = _softmax_pv(s, vt_fn(), state[si])
    return [acc / l for (_, l, acc) in state]


def _split_maps(qt):
    row = lax.broadcasted_iota(jnp.int32, qt.shape, 0)
    zero = jnp.zeros_like(qt)
    return jnp.where(row < DH_DIFF, qt, zero), jnp.where(row >= DH_DIFF, qt, zero)


def _lambda(lam_ref, lam_init):
    lv = lam_ref[...]
    a = jnp.sum(lv[0:1] * lv[1:2], axis=-1, keepdims=True)
    b = jnp.sum(lv[2:3] * lv[3:4], axis=-1, keepdims=True)
    return jnp.exp(a) - jnp.exp(b) + lam_init


def _diff_finish(o1_t, o2_t, lam_ref, g_ref, lam_init):
    o = (o1_t - _lambda(lam_ref, lam_init) * o2_t).T
    return (_rms_norm(o, g_ref[...]) * (1.0 - lam_init)).astype(BF16)


def _mla_finish(oa_t, ob_t):
    return jnp.concatenate([oa_t, ob_t], axis=0).T.astype(BF16)


def _diff_ctx_kernel(qt_ref, k_ref, vt_ref, lam_ref, g_ref, out_ref, *, lam_init):
    first = []
    for h in range(H_DIFF):
        sl = slice(h * LANES, (h + 1) * LANES)
        k, vt = k_ref[:, sl], vt_ref[0, sl, :]
        first += [(q, k, vt) for q in _split_maps(qt_ref[sl, :])]
    o = _attend(first, None)
    out_ref[...] = jnp.concatenate(
        [_diff_finish(o[2 * h], o[2 * h + 1], lam_ref, g_ref, lam_init) for h in range(H_DIFF)], axis=1)


def _diff_lat_kernel(qut_ref, qrt_ref, kc_ref, vc_ref, k_ref, vt_ref, lam_ref, g_ref, out_ref, *, lam_init):
    kc, vct = kc_ref[...].astype(BF16), vc_ref[...].T.astype(BF16)
    first = [(q, kc, vct) for q in _split_maps(qut_ref[...])]
    rest = [(q, k_ref, vt_ref) for q in _split_maps(qrt_ref[...])]
    o1, o2 = _attend(first, rest)
    out_ref[...] = _diff_finish(o1, o2, lam_ref, g_ref, lam_init)


def _mla_ctx_kernel(qt_ref, k_ref, vt_ref, out_ref):
    first = []
    for h in range(H_MLA):
        sl = slice(h * MLA_SLAB, (h + 1) * MLA_SLAB)
        first.append((qt_ref[sl, :], k_ref[:, sl], vt_ref[0, h * V_MLA:(h + 1) * V_MLA, :]))
    o = _attend(first, None)
    out_ref[...] = jnp.concatenate(
        [_mla_finish(o[2 * j], o[2 * j + 1]) for j in range(H_MLA // 2)], axis=1)


def _mla_lat_kernel(qua_ref, qub_ref, qra_ref, qrb_ref, kca_ref, kcb_ref, vcta_ref, vctb_ref, ka_ref, kb_ref,
                    vta_ref, vtb_ref, out_ref):
    first = [(qua_ref[...], kca_ref[...], vcta_ref[0]), (qub_ref[...], kcb_ref[...], vctb_ref[0])]
    rest = [(qra_ref[...], ka_ref, vta_ref), (qrb_ref[...], kb_ref, vtb_ref)]
    oa, ob = _attend(first, rest, MLA_CHUNKS_PER_ROUND)
    out_ref[...] = _mla_finish(oa, ob)


def _attention_ctx(streams, lam4, diff_g, lam_init, b, t):
    qd_t, kd, vd_t, qm_t, km, vm_t = streams
    per_chunk = vd_t.shape[2] // t
    sem = _params("arbitrary")
    qt = lambda w: pl.BlockSpec((w, t), lambda i: (0, i))
    keys = lambda w: pl.BlockSpec((t, w), lambda i: (i, 0))
    vals = lambda w: pl.BlockSpec((1, w, t), lambda i: (i // per_chunk, 0, i % per_chunk))
    o_d = pl.pallas_call(
        functools.partial(_diff_ctx_kernel, lam_init=lam_init),
        grid=(b,),
        in_specs=[qt(DIFF_QK), keys(DIFF_QK), vals(DIFF_V),
                  pl.BlockSpec(lam4.shape, lambda i: (0, 0)),
                  pl.BlockSpec(diff_g.shape, lambda i: (0, 0))],
        out_specs=keys(DIFF_V),
        out_shape=jax.ShapeDtypeStruct((b * t, DIFF_V), BF16),
        compiler_params=sem, name="diff_attn_ctx",
    )(qd_t, kd, vd_t, lam4, diff_g)
    o_m = pl.pallas_call(
        _mla_ctx_kernel,
        grid=(b,),
        in_specs=[qt(MLA_WIDE), keys(MLA_WIDE), vals(H_MLA * V_MLA)],
        out_specs=keys(H_MLA * V_MLA),
        out_shape=jax.ShapeDtypeStruct((b * t, H_MLA * V_MLA), BF16),
        compiler_params=sem, name="mla_attn_ctx",
    )(qm_t, km, vm_t)
    return o_d, o_m


def _attention_lat(streams, cache, lam4, diff_g, lam_init, b, t):
    qdu_t, qdr_t, kdr, vd_t, qmu_t, qmr_t, kmr, vm_t = streams
    kd_c, vd_c, km_c, vmc_t = cache
    past = kd_c.shape[0] // b
    nq = t // Q_TILE
    n_chunks = t // KEY_CHUNK
    sem = _params("arbitrary", "arbitrary", "arbitrary")
    qt = lambda f: pl.BlockSpec((LANES, Q_TILE), lambda i, h, j: (f(h), i * nq + j))
    keys = lambda n, f: pl.BlockSpec((n, LANES), lambda i, h, j: (i, f(h)))
    vals = pl.BlockSpec((n_chunks, LANES, KEY_CHUNK), lambda i, h, j: (i, h, 0))
    out = pl.BlockSpec((Q_TILE, LANES), lambda i, h, j: (i * nq + j, h))
    same, even, odd = (lambda h: h), (lambda h: 2 * h), (lambda h: 2 * h + 1)
    o_d = pl.pallas_call(
        functools.partial(_diff_lat_kernel, lam_init=lam_init),
        grid=(b, H_DIFF, nq),
        in_specs=[qt(same), qt(same), keys(past, same), keys(past, same), keys(t, same), vals,
                  pl.BlockSpec(lam4.shape, lambda i, h, j: (0, 0)),
                  pl.BlockSpec(diff_g.shape, lambda i, h, j: (0, 0))],
        out_specs=out,
        out_shape=jax.ShapeDtypeStruct((b * t, DIFF_V), BF16),
        compiler_params=sem, name="diff_attn_lat",
    )(qdu_t, qdr_t, kd_c, vd_c, kdr, vd_t, lam4, diff_g)
    o_m = pl.pallas_call(
        _mla_lat_kernel,
        grid=(b, H_MLA // 2, nq),
        in_specs=[qt(even), qt(odd), qt(even), qt(odd), keys(past, even), keys(past, odd),
                  pl.BlockSpec((1, V_MLA, past), lambda i, h, j: (i, 2 * h, 0)),
                  pl.BlockSpec((1, V_MLA, past), lambda i, h, j: (i, 2 * h + 1, 0)),
                  keys(t, even), keys(t, odd),
                  pl.BlockSpec((n_chunks, V_MLA, KEY_CHUNK), lambda i, h, j: (i, 2 * h, 0)),
                  pl.BlockSpec((n_chunks, V_MLA, KEY_CHUNK), lambda i, h, j: (i, 2 * h + 1, 0))],
        out_specs=out,
        out_shape=jax.ShapeDtypeStruct((b * t, H_MLA * V_MLA), BF16),
        compiler_params=sem, name="mla_attn_lat",
    )(qmu_t, qmu_t, qmr_t, qmr_t, km_c, km_c, vmc_t, vmc_t, kmr, kmr, vm_t, vm_t)
    return o_d, o_m


def _post_kernel(x_ref, ma_ref, mb_ref, mods_ref, wo_ref, w1_ref, w2_ref, ln_ref, out_ref):
    mods = mods_ref[0, 0]
    g_m, sh_f, sc_f, g_f = mods[2:3], mods[3:4], mods[4:5], mods[5:6]
    ln = ln_ref[...]
    rows = x_ref.shape[0] // POST_GROUPS
    grp = [slice(r * rows, (r + 1) * rows) for r in range(POST_GROUPS)]
    ys = [_dot(jnp.concatenate([ma_ref[g, :], mb_ref[g, :]], axis=1), wo_ref[...]) for g in grp]
    x1s = [_layer_norm(DEEPNORM_ALPHA * x_ref[g, :] + g_m * y, ln[0:1], ln[1:2]) for g, y in zip(grp, ys)]
    fs = []
    for x1 in x1s:
        h = (x1 * (1.0 + sc_f) + sh_f).astype(BF16)
        f = jnp.zeros(x1.shape, F32)
        for c in range(D_FF // FF_CHUNK):
            a = jnp.maximum(_dot(h, w1_ref[0, :, c * FF_CHUNK:(c + 1) * FF_CHUNK]), 0.0)
            f = f + _dot((a * a).astype(BF16), w2_ref[0, c * FF_CHUNK:(c + 1) * FF_CHUNK, :])
        fs.append(f)
    for g, x1, f in zip(grp, x1s, fs):
        out_ref[g, :] = _layer_norm(DEEPNORM_ALPHA * x1 + g_f * f, ln[2:3], ln[3:4])


def _post(x2d, mix_a, mix_b, mods, layer, mod_row_fn, wo, w1, w2, ln4, tm):
    t = x2d.shape[0]
    tok = lambda n: pl.BlockSpec((tm, n), lambda i: (i, 0))
    layer_slab = lambda w: pl.BlockSpec((1,) + w.shape[1:], lambda i: (layer, 0, 0), pipeline_mode=pl.Buffered(1))
    return pl.pallas_call(
        _post_kernel, grid=(t // tm,),
        in_specs=[tok(D_MODEL), tok(mix_a.shape[1]), tok(mix_b.shape[1]),
                  pl.BlockSpec((1, 1, 6, D_MODEL), lambda i: (layer, mod_row_fn(i), 0, 0)),
                  _const_spec(wo.shape), layer_slab(w1), layer_slab(w2),
                  _const_spec(ln4.shape)],
        out_specs=tok(D_MODEL),
        out_shape=jax.ShapeDtypeStruct((t, D_MODEL), F32),
        compiler_params=_params("arbitrary"), name=f"post_mlp_l{layer}",
    )(x2d, mix_a, mix_b, mods, wo, w1, w2, ln4)


def _conv_in_kernel(x_ref, mods_ref, w_ref, u_ref, pz_ref):
    mods = mods_ref[0, 0]
    sh, sc = mods[0:1], mods[1:2]
    h = (x_ref[...] * (1.0 + sc) + sh).astype(BF16)
    proj = _dot(h, w_ref[...])
    a, gate = proj[:, 0:CONV_CH], proj[:, CONV_CH:2 * CONV_CH]
    u_ref[...] = a * jax.nn.sigmoid(gate)
    pz_ref[...] = proj[:, 2 * CONV_CH:]


def _conv_in(x2d, mods, layer, mod_row_fn, w, tm):
    t = x2d.shape[0]
    tok = lambda n: pl.BlockSpec((tm, n), lambda i: (i, 0))
    return pl.pallas_call(
        _conv_in_kernel, grid=(t // tm,),
        in_specs=[tok(D_MODEL),
                  pl.BlockSpec((1, 1, 6, D_MODEL), lambda i: (layer, mod_row_fn(i), 0, 0)),
                  _const_spec(w.shape)],
        out_specs=[tok(CONV_CH), tok(POOL_CH)],
        out_shape=[jax.ShapeDtypeStruct((t, CONV_CH), F32), jax.ShapeDtypeStruct((t, POOL_CH), F32)],
        compiler_params=_params("arbitrary"), name="conv_in",
    )(x2d, mods, w)


def _conv_pool_kernel(*refs, n_tiles):
    if n_tiles > 1:
        (u_ref, ul_ref, ur_ref, pz_ref, pl_ref, pr_ref, cw_ref, cp_ref, wp_ref, ps_ref, inv_ref,
         uo_ref, do_ref, ubuf, pbuf, ushift, cbuf) = refs
    else:
        u_ref, pz_ref, cw_ref, cp_ref, wp_ref, ps_ref, inv_ref, uo_ref, do_ref, ubuf, pbuf, ushift, cbuf = refs
    tm = u_ref.shape[1]
    j = pl.program_id(1)
    halo_zero = jnp.zeros((CONV_HALO, CONV_CH), F32)
    for buf, mid, sides in ((ubuf, u_ref, (ul_ref, ur_ref) if n_tiles > 1 else None),
                            (pbuf, pz_ref, (pl_ref, pr_ref) if n_tiles > 1 else None)):
        buf[CONV_HALO:CONV_HALO + tm, :] = mid[0]
        if sides is None:
            buf[0:CONV_HALO, :] = halo_zero
            buf[CONV_HALO + tm:, :] = halo_zero
        else:
            buf[0:CONV_HALO, :] = jnp.where(j > 0, sides[0][0], halo_zero)
            buf[CONV_HALO + tm:, :] = jnp.where(j < n_tiles - 1, sides[1][0], halo_zero)

    rows_sh = ushift.shape[1]
    for b in range(SUBLANES):
        ushift[b] = ubuf[b:b + rows_sh, :]
    base = CONV_HALO - CONV_WIDTH // 2
    cp = cp_ref[...]

    def strip(i, carry):
        r0 = pl.multiple_of(i * CONV_STRIP, CONV_STRIP)
        acc = jnp.zeros((CONV_STRIP, CONV_CH), F32)
        for k in range(CONV_WIDTH):
            a, b = divmod(base + k, SUBLANES)
            w = jnp.concatenate([cw_ref[k]] * (CONV_STRIP // SUBLANES), axis=0)
            acc = acc + ushift[b, pl.ds(r0 + a * SUBLANES, CONV_STRIP), :] * w
        cbuf[pl.ds(r0, CONV_STRIP), :] = acc
        return carry

    lax.fori_loop(0, tm // CONV_STRIP, strip, 0)
    z = _layer_norm(cbuf[...] + cp[0:1], cp[1:2], cp[2:3])
    uo_ref[0] = (z * jax.nn.sigmoid(z)).astype(uo_ref.dtype)

    outs = []
    for g, w in enumerate(POOL_WINDOWS):
        cols = slice(g * POOL_GC, (g + 1) * POOL_GC)
        s = jnp.zeros((tm, POOL_GC), F32)
        for d in range(-(w // 2), w // 2):
            s = s + pbuf[CONV_HALO + d:CONV_HALO + d + tm, cols]
        dgrp = s * inv_ref[:, cols] - pz_ref[0][:, cols]
        outs.append(_dot(dgrp.astype(BF16), wp_ref[g]))
    do_ref[0] = (jnp.concatenate(outs, axis=1) * ps_ref[...]).astype(do_ref.dtype)


def _pool_inv_counts(t):
    pos = np.arange(t)
    cols = []
    for w in POOL_WINDOWS:
        cnt = np.minimum(pos + w // 2 - 1, t - 1) - np.maximum(pos - w // 2, 0) + 1
        cols.append(np.repeat((1.0 / cnt)[:, None], POOL_GC, axis=1))
    return jnp.asarray(np.concatenate(cols, axis=1).astype(np.float32))


def _conv_pool(u2d, pz2d, b, t, tm, cw, cp, wp, ps):
    n_tiles = t // tm
    u3, p3 = u2d.reshape(b, t, CONV_CH), pz2d.reshape(b, t, POOL_CH)
    mid = pl.BlockSpec((1, tm, CONV_CH), lambda i, j: (i, j, 0))
    r = tm // CONV_HALO
    left = pl.BlockSpec((1, CONV_HALO, CONV_CH), lambda i, j: (i, jnp.maximum(j * r - 1, 0), 0))
    right = pl.BlockSpec((1, CONV_HALO, CONV_CH),
                         lambda i, j: (i, jnp.minimum((j + 1) * r, t // CONV_HALO - 1), 0))
    const = lambda a: pl.BlockSpec(a.shape, lambda i, j: (0,) * a.ndim)
    if n_tiles > 1:
        in_specs = [mid, left, right, mid, left, right]
        args = (u3, u3, u3, p3, p3, p3)
    else:
        in_specs = [mid, mid]
        args = (u3, p3)
    inv = _pool_inv_counts(t)
    in_specs += [const(cw), const(cp), const(wp), const(ps), pl.BlockSpec((tm, POOL_CH), lambda i, j: (j, 0))]
    uo, do = pl.pallas_call(
        functools.partial(_conv_pool_kernel, n_tiles=n_tiles),
        grid=(b, n_tiles), in_specs=in_specs,
        out_specs=[mid, mid],
        out_shape=[jax.ShapeDtypeStruct((b, t, CONV_CH), BF16), jax.ShapeDtypeStruct((b, t, POOL_CH), BF16)],
        scratch_shapes=[pltpu.VMEM((tm + 2 * CONV_HALO, CONV_CH), F32),
                        pltpu.VMEM((tm + 2 * CONV_HALO, POOL_CH), F32),
                        pltpu.VMEM((SUBLANES, tm + 2 * CONV_HALO - SUBLANES, CONV_CH), F32),
                        pltpu.VMEM((tm, CONV_CH), F32)],
        compiler_params=_params("arbitrary", "arbitrary"),
        name=f"conv_pool_{'lat' if n_tiles > 1 else 'ctx'}",
    )(*args, cw, cp, wp, ps, inv)
    return uo.reshape(b * t, CONV_CH), do.reshape(b * t, POOL_CH)


def _att_in_weights(w_att_in, with_v_tok):
    o = 2 * DIFF_QK + DIFF_V
    q_d, k_d, v_d = w_att_in[:, :DIFF_QK], w_att_in[:, DIFF_QK:2 * DIFF_QK], w_att_in[:, 2 * DIFF_QK:o]
    cq, ckv = w_att_in[:, o:o + Q_LORA], w_att_in[:, o + Q_LORA:o + Q_LORA + KV_LORA]
    kpe = jnp.pad(w_att_in[:, o + Q_LORA + KV_LORA:], _ROPE_LANE_PAD)
    w_t = jnp.concatenate([q_d, v_d, cq, ckv], axis=1).T.astype(BF16)
    w_r = jnp.concatenate([k_d, ckv, kpe] + ([v_d] if with_v_tok else []), axis=1).astype(BF16)
    return w_t, w_r


def _mla_weights(w_uq, w_ukv):
    wq = w_uq.reshape(Q_LORA, H_MLA, QK_NOPE + QK_ROPE)
    wq = jnp.pad(wq, ((0, 0), (0, 0), (0, MLA_SLAB - QK_NOPE - QK_ROPE))).reshape(Q_LORA, MLA_WIDE)
    wkv = w_ukv.reshape(KV_LORA, H_MLA, QK_NOPE + V_MLA)
    wk = jnp.pad(wkv[:, :, :QK_NOPE], ((0, 0), (0, 0), (0, MLA_SLAB - QK_NOPE))).reshape(KV_LORA, MLA_WIDE)
    wv = wkv[:, :, QK_NOPE:].reshape(KV_LORA, H_MLA * V_MLA)
    return wq.T.astype(BF16), wk.astype(BF16), wv.T.astype(BF16)


def kernel(x_prompt, x_sample, cache_diff_k, cache_diff_v, cache_mla_ckv, cache_mla_krope, c, c_ctx,
           w_ada, b_ada, ln_mix_g, ln_mix_b, ln_mlp_g, ln_mlp_b, w_mlp_in, w_mlp_out,
           w_att_in, w_uq, w_ukv, q_norm_g, kv_norm_g, lam_q1, lam_k1, lam_q2, lam_k2, diff_norm_g, w_att_out,
           w_conv_in, conv_w, conv_b, conv_norm_g, conv_norm_b, w_pool, pool_scale, w_conv_out):
    bc, tc, d = x_prompt.shape
    bl, tl, _ = x_sample.shape
    past = cache_diff_k.shape[2]

    cond8 = jnp.zeros((8, d), F32).at[0].set(c_ctx).at[1:1 + bl].set(c)
    mods = _ada_mods(cond8, w_ada, b_ada).reshape(DEPTH, 8, 6, d)
    ctx_row = lambda i: 0
    lat_row = lambda i: 1 + i // (tl // TOKEN_TILE)

    wqt, wk, wvt = _mla_weights(w_uq[0], w_ukv[0])
    norm_w = (q_norm_g[0].reshape(Q_LORA, 1), kv_norm_g[0].reshape(1, KV_LORA), kv_norm_g[0].reshape(KV_LORA, 1))
    att_w = lambda with_v: _att_in_weights(w_att_in[0], with_v) + norm_w + (wqt, wk, wvt)
    lam4 = jnp.stack([lam_q1[0], lam_k1[0], lam_q2[0], lam_k2[0]])
    diff_g = diff_norm_g[0].reshape(1, 2 * DH_DIFF)
    lam_init = 0.8 - 0.6 * math.exp(-0.3 * 0)
    ln4 = [jnp.stack([ln_mix_g[l], ln_mix_b[l], ln_mlp_g[l], ln_mlp_b[l]]) for l in range(DEPTH)]
    w1, w2 = w_mlp_in.astype(BF16), w_mlp_out.astype(BF16)
    w_ao = w_att_out[0].astype(BF16)
    w_ci = w_conv_in[0].astype(BF16)
    w_co = w_conv_out[0].astype(BF16)
    conv_p = jnp.stack([conv_b[0], conv_norm_g[0], conv_norm_b[0]])
    conv_w8 = jnp.broadcast_to(conv_w[0][:, None, :], (CONV_WIDTH, SUBLANES, CONV_CH))
    wp = w_pool[0].astype(BF16)
    ps = pool_scale[0].reshape(1, POOL_CH)

    xp = x_prompt.reshape(bc * tc, d)
    outs = _att_in(xp, mods, ctx_row, att_w(True), None, TOKEN_TILE)
    kd_f, vd_f, ckv_f, kpe_f = outs[6:]
    o_d, o_m = _attention_ctx(outs[:6], lam4, diff_g, lam_init, bc, tc)
    xp = _post(xp, o_d, o_m, mods, 0, ctx_row, w_ao, w1, w2, ln4[0], TOKEN_TILE)
    u, pz = _conv_in(xp, mods, 1, ctx_row, w_ci, TOKEN_TILE)
    uo, do = _conv_pool(u, pz, bc, tc, tc, conv_w8, conv_p, wp, ps)
    xp = _post(xp, uo, do, mods, 1, ctx_row, w_co, w1, w2, ln4[1], TOKEN_TILE)

    xs = x_sample.reshape(bl * tl, d)
    outs = _att_in(xs, mods, lat_row, att_w(False), _rope_tables(tl), TOKEN_TILE)
    km_c, vmc_t = _cache_kv(cache_mla_ckv[:, 0].reshape(bl * past, KV_LORA),
                            jnp.pad(cache_mla_krope[:, 0].reshape(bl * past, QK_ROPE), _ROPE_LANE_PAD),
                            wk, wvt, bl)
    cache = (cache_diff_k[:, 0].reshape(bl * past, DIFF_QK), cache_diff_v[:, 0].reshape(bl * past, DIFF_V),
             km_c, vmc_t)
    o_d, o_m = _attention_lat(outs, cache, lam4, diff_g, lam_init, bl, tl)
    xs = _post(xs, o_d, o_m, mods, 0, lat_row, w_ao, w1, w2, ln4[0], TOKEN_TILE)
    u, pz = _conv_in(xs, mods, 1, lat_row, w_ci, TOKEN_TILE)
    uo, do = _conv_pool(u, pz, bl, tl, CONV_TILE, conv_w8, conv_p, wp, ps)
    xs = _post(xs, uo, do, mods, 1, lat_row, w_co, w1, w2, ln4[1], TOKEN_TILE)

    return (xp.reshape(bc, tc, d), xs.reshape(bl, tl, d),
            kd_f.reshape(bc, 1, tc, H_DIFF, 2 * DH_DIFF), vd_f.reshape(bc, 1, tc, H_DIFF, 2 * DH_DIFF),
            ckv_f.reshape(bc, 1, tc, KV_LORA), kpe_f.reshape(bc, 1, tc, QK_ROPE))
```

```python
import functools
import math

import jax
import jax.numpy as jnp
import numpy as np
from jax import lax
from jax.experimental import pallas as pl
from jax.experimental.pallas import tpu as pltpu

D_MODEL = 1024
DEPTH = 2
GRID_W = 64
H_DIFF = 4
DH_DIFF = 64
DIFF_QK = H_DIFF * 2 * DH_DIFF
DIFF_V = H_DIFF * 2 * DH_DIFF
H_MLA = 8
Q_LORA = 256
KV_LORA = 128
QK_NOPE = 64
QK_ROPE = 32
V_MLA = 64
ATT_IN = 2 * DIFF_QK + DIFF_V + Q_LORA + KV_LORA + QK_ROPE
CONV_CH = 512
CONV_WIDTH = 31
POOL_CH = 512
POOL_WINDOWS = (2, 4, 8, 16)
POOL_GC = POOL_CH // len(POOL_WINDOWS)
CONV_IN = 2 * CONV_CH + POOL_CH
D_FF = 4 * D_MODEL
ROPE_BASE = 10000.0
NORM_EPS = 1e-5
DEEPNORM_ALPHA = (2 * DEPTH) ** 0.25
LOG2E = math.log2(math.e)

LANES = 128
SUBLANES = 8
MLA_SLAB = LANES
MLA_WIDE = H_MLA * MLA_SLAB
_ROPE_LANE_PAD = ((0, 0), (QK_NOPE, MLA_SLAB - QK_NOPE - QK_ROPE))
VMEM_LIMIT = 56 * 1024 * 1024

TOKEN_TILE = 512
KEY_CHUNK = TOKEN_TILE
Q_TILE = 512
MLA_CHUNKS_PER_ROUND = 2
STALE_MAX_LIMIT = 64.0
CONV_TILE = 512
CONV_HALO = 16
CONV_STRIP = 32
FF_CHUNK = 1024
POST_GROUPS = 2

F32 = jnp.float32
BF16 = jnp.bfloat16


def _dot(a, b):
    return jnp.dot(a, b, preferred_element_type=F32)


def _dot_nt(a, b):
    return lax.dot_general(a, b, (((1,), (1,)), ((), ())), preferred_element_type=F32)


def _layer_norm(x, g, b):
    mu = jnp.mean(x, axis=-1, keepdims=True)
    xc = x - mu
    var = jnp.mean(xc * xc, axis=-1, keepdims=True)
    return xc * lax.rsqrt(var + NORM_EPS) * g + b


def _rms_norm(x, g, axis=-1):
    ms = jnp.mean(x * x, axis=axis, keepdims=True)
    return x * lax.rsqrt(ms + NORM_EPS) * g


def _const_spec(shape):
    nd = len(shape)
    return pl.BlockSpec(shape, lambda *_: (0,) * nd, pipeline_mode=pl.Buffered(1))


def _params(*sem):
    return pltpu.CompilerParams(dimension_semantics=sem, vmem_limit_bytes=VMEM_LIMIT)


ADA_TILE = 1536


def _ada_kernel(cond_ref, w_ref, b_ref, out_ref):
    cond = cond_ref[...]
    act = (cond * jax.nn.sigmoid(cond)).astype(BF16)
    out_ref[0] = _dot(act, w_ref[0].astype(BF16)) + b_ref[0]


def _ada_mods(cond8, w_ada, b_ada):
    n = 6 * D_MODEL
    return pl.pallas_call(
        _ada_kernel,
        grid=(DEPTH, n // ADA_TILE),
        in_specs=[
            pl.BlockSpec((8, D_MODEL), lambda l, j: (0, 0)),
            pl.BlockSpec((1, D_MODEL, ADA_TILE), lambda l, j: (l, 0, j)),
            pl.BlockSpec((1, 1, ADA_TILE), lambda l, j: (l, 0, j)),
        ],
        out_specs=pl.BlockSpec((1, 8, ADA_TILE), lambda l, j: (l, 0, j)),
        out_shape=jax.ShapeDtypeStruct((DEPTH, 8, n), F32),
        compiler_params=_params("arbitrary", "arbitrary"),
        name="ada_mods",
    )(cond8, w_ada, b_ada.reshape(DEPTH, 1, n))


def _rope_tables(t_lat):
    pos = np.arange(t_lat)
    row = (pos // GRID_W).astype(np.float64)
    col = (pos % GRID_W).astype(np.float64)

    def tables(kinds):
        cos = np.ones((t_lat, LANES))
        sa = np.zeros((t_lat, LANES))
        sb = np.zeros((t_lat, LANES))
        for lane, kind in enumerate(kinds):
            if kind is None:
                continue
            axis, half, j, upper = kind
            ang = (row if axis == 0 else col) * ROPE_BASE ** (-float(j) / half)
            cos[:, lane] = np.cos(ang)
            if upper:
                sb[:, lane] = np.sin(ang)
            else:
                sa[:, lane] = -np.sin(ang)
        return [cos, sa, sb]

    def rot_kinds(n):
        half = n // 4
        kinds = []
        for i in range(n):
            axis, r = divmod(i, n // 2)
            kinds.append((axis, half, r % half, r >= half))
        return kinds

    diff = rot_kinds(DH_DIFF) * 2
    mla_q = [None] * QK_NOPE + rot_kinds(QK_ROPE) + [None] * (LANES - QK_NOPE - QK_ROPE)
    chan = np.stack([t.T for t in tables(diff) + tables(mla_q)]).astype(np.float32)
    tok = np.stack(tables(diff) + tables(mla_q)).astype(np.float32)
    return jnp.asarray(chan), jnp.asarray(tok)


def _rope_tok(x, cos, sa, sb, shift):
    return x * cos + pltpu.roll(x, LANES - shift, 1) * sa + pltpu.roll(x, shift, 1) * sb


def _rope_tok_wide(x, cos, sa, sb, shift):
    n = x.shape[1] // LANES
    return jnp.concatenate(
        [_rope_tok(x[:, i * LANES:(i + 1) * LANES], cos, sa, sb, shift) for i in range(n)], axis=1)


def _rope_chan(x, cos, sa, sb, shift):
    up = jnp.concatenate([x[shift:], x[:shift]], axis=0)
    down = jnp.concatenate([x[-shift:], x[:-shift]], axis=0)
    return x * cos + up * sa + down * sb


def _rope_chan_wide(x, cos, sa, sb, shift):
    n = x.shape[0] // LANES
    return jnp.concatenate(
        [_rope_chan(x[i * LANES:(i + 1) * LANES], cos, sa, sb, shift) for i in range(n)], axis=0)


_T_QD, _T_VD, _T_CQ, _T_CKV = 0, DIFF_QK, DIFF_QK + DIFF_V, DIFF_QK + DIFF_V + Q_LORA
_T_ROWS = _T_CKV + KV_LORA
_R_KD, _R_CKV, _R_KPE, _R_VD = 0, DIFF_QK, DIFF_QK + KV_LORA, DIFF_QK + KV_LORA + LANES

DIFF_QSCALE = DH_DIFF ** -0.5 * LOG2E
MLA_QSCALE = (QK_NOPE + QK_ROPE) ** -0.5 * LOG2E


def _att_in_body(x_ref, mods_ref, wt_ref, wr_ref, qg_ref, kvg_row_ref, kvg_col_ref, wqt_ref, wk_ref, wvt_ref):
    mods = mods_ref[0, 0]
    sh, sc = mods[0:1], mods[1:2]
    h = (x_ref[...] * (1.0 + sc) + sh).astype(BF16)
    pt = _dot_nt(wt_ref[...], h)
    pr = _dot(h, wr_ref[...])
    qd_t = pt[_T_QD:_T_QD + DIFF_QK] * DIFF_QSCALE
    vd_t = pt[_T_VD:_T_VD + DIFF_V]
    cqn_t = _rms_norm(pt[_T_CQ:_T_CQ + Q_LORA], qg_ref[...], axis=0)
    qm_t = _dot(wqt_ref[...], cqn_t.astype(BF16)) * MLA_QSCALE
    ckvn_t = _rms_norm(pt[_T_CKV:_T_CKV + KV_LORA], kvg_col_ref[...], axis=0)
    vm_t = _dot(wvt_ref[...], ckvn_t.astype(BF16))
    k_d = pr[:, _R_KD:_R_KD + DIFF_QK]
    ckv_n = _rms_norm(pr[:, _R_CKV:_R_CKV + KV_LORA], kvg_row_ref[...])
    kpe = pr[:, _R_KPE:_R_KPE + LANES]
    k_nope = _dot(ckv_n.astype(BF16), wk_ref[...])
    return pr, qd_t, vd_t, qm_t, vm_t, k_d, ckv_n, kpe, k_nope


def _mla_keys(k_nope, kpe_slab):
    return (k_nope + jnp.concatenate([kpe_slab] * H_MLA, axis=1)).astype(BF16)


def _att_in_ctx_kernel(x_ref, mods_ref, wt_ref, wr_ref, qg_ref, kvg_row_ref, kvg_col_ref, wqt_ref, wk_ref,
                       wvt_ref,
                       qdt_ref, kd_ref, vdt_ref, qmt_ref, km_ref, vmt_ref,
                       kdf_ref, vdf_ref, ckvf_ref, kpef_ref):
    pr, qd_t, vd_t, qm_t, vm_t, k_d, ckv_n, kpe, k_nope = _att_in_body(
        x_ref, mods_ref, wt_ref, wr_ref, qg_ref, kvg_row_ref, kvg_col_ref, wqt_ref, wk_ref, wvt_ref)
    qdt_ref[...] = qd_t.astype(BF16)
    kd_ref[...] = k_d.astype(BF16)
    vdt_ref[0] = vd_t.astype(BF16)
    qmt_ref[...] = qm_t.astype(BF16)
    km_ref[...] = _mla_keys(k_nope, kpe)
    vmt_ref[0] = vm_t.astype(BF16)
    v_d = pr[:, _R_VD:_R_VD + DIFF_V]
    for h in range(H_DIFF):
        kdf_ref[:, h, :] = k_d[:, h * LANES:(h + 1) * LANES]
        vdf_ref[:, h, :] = v_d[:, h * LANES:(h + 1) * LANES]
    ckvf_ref[...] = ckv_n
    kpef_ref[...] = kpe[:, QK_NOPE:QK_NOPE + QK_ROPE]


def _att_in_lat_kernel(x_ref, mods_ref, wt_ref, wr_ref, qg_ref, kvg_row_ref, kvg_col_ref, wqt_ref, wk_ref,
                       wvt_ref, tabc_ref, tabt_ref,
                       qdut_ref, qdrt_ref, kdr_ref, vdt_ref, qmut_ref, qmrt_ref, kmr_ref, vmt_ref):
    _, qd_t, vd_t, qm_t, vm_t, k_d, ckv_n, kpe, k_nope = _att_in_body(
        x_ref, mods_ref, wt_ref, wr_ref, qg_ref, kvg_row_ref, kvg_col_ref, wqt_ref, wk_ref, wvt_ref)
    qdut_ref[...] = qd_t.astype(BF16)
    qdrt_ref[...] = _rope_chan_wide(qd_t, tabc_ref[0], tabc_ref[1], tabc_ref[2], DH_DIFF // 4).astype(BF16)
    kdr_ref[...] = _rope_tok_wide(k_d, tabt_ref[0], tabt_ref[1], tabt_ref[2], DH_DIFF // 4).astype(BF16)
    vdt_ref[0] = vd_t.astype(BF16)
    qmut_ref[...] = qm_t.astype(BF16)
    qmrt_ref[...] = _rope_chan_wide(qm_t, tabc_ref[3], tabc_ref[4], tabc_ref[5], QK_ROPE // 4).astype(BF16)
    kpe_r = _rope_tok(kpe, tabt_ref[3], tabt_ref[4], tabt_ref[5], QK_ROPE // 4)
    kmr_ref[...] = _mla_keys(k_nope, kpe_r)
    vmt_ref[0] = vm_t.astype(BF16)


def _att_in(x2d, mods, mod_row_fn, wts, tables, tm):
    t = x2d.shape[0]
    nt = t // tm
    tok = lambda n: pl.BlockSpec((tm, n), lambda i: (i, 0))
    chan = lambda n: pl.BlockSpec((n, tm), lambda i: (0, i))
    chunk = lambda n: pl.BlockSpec((1, n, tm), lambda i: (i, 0, 0))
    in_specs = [tok(D_MODEL), pl.BlockSpec((1, 1, 6, D_MODEL), lambda i: (0, mod_row_fn(i), 0, 0))]
    in_specs += [_const_spec(w.shape) for w in wts]
    s_tok = lambda n, dt=BF16: jax.ShapeDtypeStruct((t, n), dt)
    s_chan = lambda n: jax.ShapeDtypeStruct((n, t), BF16)
    s_chunk = lambda n: jax.ShapeDtypeStruct((nt, n, tm), BF16)
    if tables is None:
        heads = pl.BlockSpec((tm, H_DIFF, 2 * DH_DIFF), lambda i: (i, 0, 0))
        s_heads = jax.ShapeDtypeStruct((t, H_DIFF, 2 * DH_DIFF), F32)
        out_shape = [s_chan(DIFF_QK), s_tok(DIFF_QK), s_chunk(DIFF_V), s_chan(MLA_WIDE), s_tok(MLA_WIDE),
                     s_chunk(H_MLA * V_MLA),
                     s_heads, s_heads, s_tok(KV_LORA, F32), s_tok(QK_ROPE, F32)]
        out_specs = [chan(DIFF_QK), tok(DIFF_QK), chunk(DIFF_V), chan(MLA_WIDE), tok(MLA_WIDE),
                     chunk(H_MLA * V_MLA), heads, heads, tok(KV_LORA), tok(QK_ROPE)]
        kern, args, name = _att_in_ctx_kernel, (), "att_in_ctx"
    else:
        tab_c, tab_t = tables
        t_lat = tab_t.shape[1]
        in_specs += [pl.BlockSpec((6, LANES, tm), lambda i: (0, 0, i % (t_lat // tm))),
                     pl.BlockSpec((6, tm, LANES), lambda i: (0, i % (t_lat // tm), 0))]
        out_shape = [s_chan(DIFF_QK), s_chan(DIFF_QK), s_tok(DIFF_QK), s_chunk(DIFF_V),
                     s_chan(MLA_WIDE), s_chan(MLA_WIDE), s_tok(MLA_WIDE), s_chunk(H_MLA * V_MLA)]
        out_specs = [chan(DIFF_QK), chan(DIFF_QK), tok(DIFF_QK), chunk(DIFF_V),
                     chan(MLA_WIDE), chan(MLA_WIDE), tok(MLA_WIDE), chunk(H_MLA * V_MLA)]
        kern, args, name = _att_in_lat_kernel, (tab_c, tab_t), "att_in_lat"
    return pl.pallas_call(
        kern, grid=(nt,), in_specs=in_specs, out_specs=out_specs, out_shape=out_shape,
        compiler_params=_params("arbitrary"), name=name,
    )(x2d, mods, *wts, *args)


def _cache_kv_kernel(ckv_ref, kpe_ref, wk_ref, wvt_ref, km_ref, vmt_ref):
    ckv_b = ckv_ref[...].astype(BF16)
    km_ref[...] = _mla_keys(_dot(ckv_b, wk_ref[...]), kpe_ref[...])
    past = vmt_ref.shape[2]
    vt = _dot_nt(wvt_ref[...], ckv_b)
    for b in range(vmt_ref.shape[0]):
        vmt_ref[b] = vt[:, b * past:(b + 1) * past].astype(BF16)


def _cache_kv(ckv2d, kpe_slab2d, wk, wvt, bl):
    t = ckv2d.shape[0]
    full = lambda a: pl.BlockSpec(a.shape, lambda i: (0,) * a.ndim)
    return pl.pallas_call(
        _cache_kv_kernel, grid=(1,),
        in_specs=[full(ckv2d), full(kpe_slab2d), full(wk), full(wvt)],
        out_specs=[pl.BlockSpec((t, MLA_WIDE), lambda i: (0, 0)),
                   pl.BlockSpec((bl, H_MLA * V_MLA, t // bl), lambda i: (0, 0, 0))],
        out_shape=[jax.ShapeDtypeStruct((t, MLA_WIDE), BF16),
                   jax.ShapeDtypeStruct((bl, H_MLA * V_MLA, t // bl), BF16)],
        compiler_params=_params("arbitrary"), name="cache_kv",
    )(ckv2d, kpe_slab2d, wk, wvt)


def _softmax_pv(s, vt, state, stale_max):
    m_c = jnp.max(s, axis=0, keepdims=True)
    if state is None:
        p = jnp.exp2(s - m_c)
        return m_c, jnp.sum(p, axis=0, keepdims=True), _dot(vt, p.astype(BF16)), jnp.zeros_like(m_c)
    m, l, acc, rise = state
    m_new = jnp.maximum(m, m_c)
    alpha = jnp.exp2(m - m_new)
    if stale_max:
        p = jnp.exp2(s - m)
        l = alpha * (l + jnp.sum(p, axis=0, keepdims=True))
        return m_new, l, alpha * (acc + _dot(vt, p.astype(BF16))), jnp.maximum(rise, m_c - m)
    p = jnp.exp2(s - m_new)
    l = alpha * l + jnp.sum(p, axis=0, keepdims=True)
    return m_new, l, alpha * acc + _dot(vt, p.astype(BF16)), rise


def _attend(first, rest, chunks_per_round=1, stale_max=False):
    items = [(si, qt, (lambda k=k: k), (lambda vt=vt: vt)) for si, (qt, k, vt) in enumerate(first)]
    if rest is not None:
        keys = KEY_CHUNK * chunks_per_round
        for c in range(rest[0][2].shape[0] // chunks_per_round):
            for si, (qt, k_ref, vt_ref) in enumerate(rest):
                items.append((si, qt, (lambda r=k_ref, c=c: r[c * keys:(c + 1) * keys, :]),
                              (lambda r=vt_ref, c=c: jnp.concatenate(
                                  [r[c * chunks_per_round + i] for i in range(chunks_per_round)], axis=1))))
    ns = len(first)
    groups = [items[i:i + ns] for i in range(0, len(items), ns)]
    scores = lambda grp: [_dot(k_fn(), qt) for (_, qt, k_fn, _) in grp]
    state = [None] * ns
    s_next = scores(groups[0])
    for g, grp in enumerate(groups):
        s_cur = s_next
        if g + 1 < len(groups):
            s_next = scores(groups[g + 1])
        for s, (si, _, _, vt_fn) in zip(s_cur, grp):
            state[si] = _softmax_pv(s, vt_fn(), state[si], stale_max)
    rise = functools.reduce(jnp.maximum, [st[3] for st in state])
    return [acc / l for (_, l, acc, _) in state], rise


def _attend_guarded(first, rest, chunks_per_round, finish):
    def run(stale_max):
        o, rise = _attend(first, rest, chunks_per_round, stale_max)
        finish(o)
        return rise

    rise = run(True)

    @pl.when(jnp.max(rise) > STALE_MAX_LIMIT)
    def _():
        run(False)


def _split_maps(qt):
    row = lax.broadcasted_iota(jnp.int32, qt.shape, 0)
    zero = jnp.zeros_like(qt)
    return jnp.where(row < DH_DIFF, qt, zero), jnp.where(row >= DH_DIFF, qt, zero)


def _lambda(lam_ref, lam_init):
    lv = lam_ref[...]
    a = jnp.sum(lv[0:1] * lv[1:2], axis=-1, keepdims=True)
    b = jnp.sum(lv[2:3] * lv[3:4], axis=-1, keepdims=True)
    return jnp.exp(a) - jnp.exp(b) + lam_init


def _diff_finish(o1_t, o2_t, lam_ref, g_ref, lam_init):
    o = (o1_t - _lambda(lam_ref, lam_init) * o2_t).T
    return (_rms_norm(o, g_ref[...]) * (1.0 - lam_init)).astype(BF16)


def _mla_finish(oa_t, ob_t):
    return jnp.concatenate([oa_t, ob_t], axis=0).T.astype(BF16)


def _diff_ctx_kernel(qt_ref, k_ref, vt_ref, lam_ref, g_ref, out_ref, *, lam_init):
    first = []
    for h in range(H_DIFF):
        sl = slice(h * LANES, (h + 1) * LANES)
        k, vt = k_ref[:, sl], vt_ref[0, sl, :]
        first += [(q, k, vt) for q in _split_maps(qt_ref[sl, :])]
    o, _ = _attend(first, None)
    out_ref[...] = jnp.concatenate(
        [_diff_finish(o[2 * h], o[2 * h + 1], lam_ref, g_ref, lam_init) for h in range(H_DIFF)], axis=1)


def _diff_lat_kernel(qut_ref, qrt_ref, kc_ref, vc_ref, k_ref, vt_ref, lam_ref, g_ref, out_ref, *, lam_init):
    kc, vct = kc_ref[...].astype(BF16), vc_ref[...].T.astype(BF16)
    first = [(q, kc, vct) for q in _split_maps(qut_ref[...])]
    rest = [(q, k_ref, vt_ref) for q in _split_maps(qrt_ref[...])]

    def finish(o):
        out_ref[...] = _diff_finish(o[0], o[1], lam_ref, g_ref, lam_init)

    _attend_guarded(first, rest, 1, finish)


def _mla_ctx_kernel(qt_ref, k_ref, vt_ref, out_ref):
    first = []
    for h in range(H_MLA):
        sl = slice(h * MLA_SLAB, (h + 1) * MLA_SLAB)
        first.append((qt_ref[sl, :], k_ref[:, sl], vt_ref[0, h * V_MLA:(h + 1) * V_MLA, :]))
    o, _ = _attend(first, None)
    out_ref[...] = jnp.concatenate(
        [_mla_finish(o[2 * j], o[2 * j + 1]) for j in range(H_MLA // 2)], axis=1)


def _mla_lat_kernel(qua_ref, qub_ref, qra_ref, qrb_ref, kca_ref, kcb_ref, vcta_ref, vctb_ref, ka_ref, kb_ref,
                    vta_ref, vtb_ref, out_ref):
    first = [(qua_ref[...], kca_ref[...], vcta_ref[0]), (qub_ref[...], kcb_ref[...], vctb_ref[0])]
    rest = [(qra_ref[...], ka_ref, vta_ref), (qrb_ref[...], kb_ref, vtb_ref)]

    def finish(o):
        out_ref[...] = _mla_finish(o[0], o[1])

    _attend_guarded(first, rest, MLA_CHUNKS_PER_ROUND, finish)


def _attention_ctx(streams, lam4, diff_g, lam_init, b, t):
    qd_t, kd, vd_t, qm_t, km, vm_t = streams
    per_chunk = vd_t.shape[2] // t
    sem = _params("arbitrary")
    qt = lambda w: pl.BlockSpec((w, t), lambda i: (0, i))
    keys = lambda w: pl.BlockSpec((t, w), lambda i: (i, 0))
    vals = lambda w: pl.BlockSpec((1, w, t), lambda i: (i // per_chunk, 0, i % per_chunk))
    o_d = pl.pallas_call(
        functools.partial(_diff_ctx_kernel, lam_init=lam_init),
        grid=(b,),
        in_specs=[qt(DIFF_QK), keys(DIFF_QK), vals(DIFF_V),
                  pl.BlockSpec(lam4.shape, lambda i: (0, 0)),
                  pl.BlockSpec(diff_g.shape, lambda i: (0, 0))],
        out_specs=keys(DIFF_V),
        out_shape=jax.ShapeDtypeStruct((b * t, DIFF_V), BF16),
        compiler_params=sem, name="diff_attn_ctx",
    )(qd_t, kd, vd_t, lam4, diff_g)
    o_m = pl.pallas_call(
        _mla_ctx_kernel,
        grid=(b,),
        in_specs=[qt(MLA_WIDE), keys(MLA_WIDE), vals(H_MLA * V_MLA)],
        out_specs=keys(H_MLA * V_MLA),
        out_shape=jax.ShapeDtypeStruct((b * t, H_MLA * V_MLA), BF16),
        compiler_params=sem, name="mla_attn_ctx",
    )(qm_t, km, vm_t)
    return o_d, o_m


def _attention_lat(streams, cache, lam4, diff_g, lam_init, b, t):
    qdu_t, qdr_t, kdr, vd_t, qmu_t, qmr_t, kmr, vm_t = streams
    kd_c, vd_c, km_c, vmc_t = cache
    past = kd_c.shape[0] // b
    nq = t // Q_TILE
    n_chunks = t // KEY_CHUNK
    sem = _params("arbitrary", "arbitrary", "arbitrary")
    qt = lambda f: pl.BlockSpec((LANES, Q_TILE), lambda i, h, j: (f(h), i * nq + j))
    keys = lambda n, f: pl.BlockSpec((n, LANES), lambda i, h, j: (i, f(h)))
    vals = pl.BlockSpec((n_chunks, LANES, KEY_CHUNK), lambda i, h, j: (i, h, 0))
    out = pl.BlockSpec((Q_TILE, LANES), lambda i, h, j: (i * nq + j, h))
    same, even, odd = (lambda h: h), (lambda h: 2 * h), (lambda h: 2 * h + 1)
    o_d = pl.pallas_call(
        functools.partial(_diff_lat_kernel, lam_init=lam_init),
        grid=(b, H_DIFF, nq),
        in_specs=[qt(same), qt(same), keys(past, same), keys(past, same), keys(t, same), vals,
                  pl.BlockSpec(lam4.shape, lambda i, h, j: (0, 0)),
                  pl.BlockSpec(diff_g.shape, lambda i, h, j: (0, 0))],
        out_specs=out,
        out_shape=jax.ShapeDtypeStruct((b * t, DIFF_V), BF16),
        compiler_params=sem, name="diff_attn_lat",
    )(qdu_t, qdr_t, kd_c, vd_c, kdr, vd_t, lam4, diff_g)
    o_m = pl.pallas_call(
        _mla_lat_kernel,
        grid=(b, H_MLA // 2, nq),
        in_specs=[qt(even), qt(odd), qt(even), qt(odd), keys(past, even), keys(past, odd),
                  pl.BlockSpec((1, V_MLA, past), lambda i, h, j: (i, 2 * h, 0)),
                  pl.BlockSpec((1, V_MLA, past), lambda i, h, j: (i, 2 * h + 1, 0)),
                  keys(t, even), keys(t, odd),
                  pl.BlockSpec((n_chunks, V_MLA, KEY_CHUNK), lambda i, h, j: (i, 2 * h, 0)),
                  pl.BlockSpec((n_chunks, V_MLA, KEY_CHUNK), lambda i, h, j: (i, 2 * h + 1, 0))],
        out_specs=out,
        out_shape=jax.ShapeDtypeStruct((b * t, H_MLA * V_MLA), BF16),
        compiler_params=sem, name="mla_attn_lat",
    )(qmu_t, qmu_t, qmr_t, qmr_t, km_c, km_c, vmc_t, vmc_t, kmr, kmr, vm_t, vm_t)
    return o_d, o_m


def _post_kernel(x_ref, ma_ref, mb_ref, mods_ref, wo_ref, w1_ref, w2_ref, ln_ref, out_ref):
    mods = mods_ref[0, 0]
    g_m, sh_f, sc_f, g_f = mods[2:3], mods[3:4], mods[4:5], mods[5:6]
    ln = ln_ref[...]
    rows = x_ref.shape[0] // POST_GROUPS
    grp = [slice(r * rows, (r + 1) * rows) for r in range(POST_GROUPS)]
    ys = [_dot(jnp.concatenate([ma_ref[g, :], mb_ref[g, :]], axis=1), wo_ref[...]) for g in grp]
    x1s = [_layer_norm(DEEPNORM_ALPHA * x_ref[g, :] + g_m * y, ln[0:1], ln[1:2]) for g, y in zip(grp, ys)]
    fs = []
    for x1 in x1s:
        h = (x1 * (1.0 + sc_f) + sh_f).astype(BF16)
        f = jnp.zeros(x1.shape, F32)
        for c in range(D_FF // FF_CHUNK):
            a = jnp.maximum(_dot(h, w1_ref[0, :, c * FF_CHUNK:(c + 1) * FF_CHUNK]), 0.0)
            f = f + _dot((a * a).astype(BF16), w2_ref[0, c * FF_CHUNK:(c + 1) * FF_CHUNK, :])
        fs.append(f)
    for g, x1, f in zip(grp, x1s, fs):
        out_ref[g, :] = _layer_norm(DEEPNORM_ALPHA * x1 + g_f * f, ln[2:3], ln[3:4])


def _post(x2d, mix_a, mix_b, mods, layer, mod_row_fn, wo, w1, w2, ln4, tm):
    t = x2d.shape[0]
    tok = lambda n: pl.BlockSpec((tm, n), lambda i: (i, 0))
    layer_slab = lambda w: pl.BlockSpec((1,) + w.shape[1:], lambda i: (layer, 0, 0), pipeline_mode=pl.Buffered(1))
    return pl.pallas_call(
        _post_kernel, grid=(t // tm,),
        in_specs=[tok(D_MODEL), tok(mix_a.shape[1]), tok(mix_b.shape[1]),
                  pl.BlockSpec((1, 1, 6, D_MODEL), lambda i: (layer, mod_row_fn(i), 0, 0)),
                  _const_spec(wo.shape), layer_slab(w1), layer_slab(w2),
                  _const_spec(ln4.shape)],
        out_specs=tok(D_MODEL),
        out_shape=jax.ShapeDtypeStruct((t, D_MODEL), F32),
        compiler_params=_params("arbitrary"), name=f"post_mlp_l{layer}",
    )(x2d, mix_a, mix_b, mods, wo, w1, w2, ln4)


def _conv_in_kernel(x_ref, mods_ref, w_ref, u_ref, pz_ref):
    mods = mods_ref[0, 0]
    sh, sc = mods[0:1], mods[1:2]
    h = (x_ref[...] * (1.0 + sc) + sh).astype(BF16)
    proj = _dot(h, w_ref[...])
    a, gate = proj[:, 0:CONV_CH], proj[:, CONV_CH:2 * CONV_CH]
    u_ref[...] = a * jax.nn.sigmoid(gate)
    pz_ref[...] = proj[:, 2 * CONV_CH:]


def _conv_in(x2d, mods, layer, mod_row_fn, w, tm):
    t = x2d.shape[0]
    tok = lambda n: pl.BlockSpec((tm, n), lambda i: (i, 0))
    return pl.pallas_call(
        _conv_in_kernel, grid=(t // tm,),
        in_specs=[tok(D_MODEL),
                  pl.BlockSpec((1, 1, 6, D_MODEL), lambda i: (layer, mod_row_fn(i), 0, 0)),
                  _const_spec(w.shape)],
        out_specs=[tok(CONV_CH), tok(POOL_CH)],
        out_shape=[jax.ShapeDtypeStruct((t, CONV_CH), F32), jax.ShapeDtypeStruct((t, POOL_CH), F32)],
        compiler_params=_params("arbitrary"), name="conv_in",
    )(x2d, mods, w)


def _conv_pool_kernel(*refs, n_tiles):
    if n_tiles > 1:
        (u_ref, ul_ref, ur_ref, pz_ref, pl_ref, pr_ref, cw_ref, cp_ref, wp_ref, ps_ref, inv_ref,
         uo_ref, do_ref, ubuf, pbuf, ushift, cbuf) = refs
    else:
        u_ref, pz_ref, cw_ref, cp_ref, wp_ref, ps_ref, inv_ref, uo_ref, do_ref, ubuf, pbuf, ushift, cbuf = refs
    tm = u_ref.shape[1]
    j = pl.program_id(1)
    halo_zero = jnp.zeros((CONV_HALO, CONV_CH), F32)
    for buf, mid, sides in ((ubuf, u_ref, (ul_ref, ur_ref) if n_tiles > 1 else None),
                            (pbuf, pz_ref, (pl_ref, pr_ref) if n_tiles > 1 else None)):
        buf[CONV_HALO:CONV_HALO + tm, :] = mid[0]
        if sides is None:
            buf[0:CONV_HALO, :] = halo_zero
            buf[CONV_HALO + tm:, :] = halo_zero
        else:
            buf[0:CONV_HALO, :] = jnp.where(j > 0, sides[0][0], halo_zero)
            buf[CONV_HALO + tm:, :] = jnp.where(j < n_tiles - 1, sides[1][0], halo_zero)

    rows_sh = ushift.shape[1]
    for b in range(SUBLANES):
        ushift[b] = ubuf[b:b + rows_sh, :]
    base = CONV_HALO - CONV_WIDTH // 2
    cp = cp_ref[...]

    def strip(i, carry):
        r0 = pl.multiple_of(i * CONV_STRIP, CONV_STRIP)
        acc = jnp.zeros((CONV_STRIP, CONV_CH), F32)
        for k in range(CONV_WIDTH):
            a, b = divmod(base + k, SUBLANES)
            w = jnp.concatenate([cw_ref[k]] * (CONV_STRIP // SUBLANES), axis=0)
            acc = acc + ushift[b, pl.ds(r0 + a * SUBLANES, CONV_STRIP), :] * w
        cbuf[pl.ds(r0, CONV_STRIP), :] = acc
        return carry

    lax.fori_loop(0, tm // CONV_STRIP, strip, 0)
    z = _layer_norm(cbuf[...] + cp[0:1], cp[1:2], cp[2:3])
    uo_ref[0] = (z * jax.nn.sigmoid(z)).astype(uo_ref.dtype)

    outs = []
    for g, w in enumerate(POOL_WINDOWS):
        cols = slice(g * POOL_GC, (g + 1) * POOL_GC)
        s = jnp.zeros((tm, POOL_GC), F32)
        for d in range(-(w // 2), w // 2):
            s = s + pbuf[CONV_HALO + d:CONV_HALO + d + tm, cols]
        dgrp = s * inv_ref[:, cols] - pz_ref[0][:, cols]
        outs.append(_dot(dgrp.astype(BF16), wp_ref[g]))
    do_ref[0] = (jnp.concatenate(outs, axis=1) * ps_ref[...]).astype(do_ref.dtype)


def _pool_inv_counts(t):
    pos = np.arange(t)
    cols = []
    for w in POOL_WINDOWS:
        cnt = np.minimum(pos + w // 2 - 1, t - 1) - np.maximum(pos - w // 2, 0) + 1
        cols.append(np.repeat((1.0 / cnt)[:, None], POOL_GC, axis=1))
    return jnp.asarray(np.concatenate(cols, axis=1).astype(np.float32))


def _conv_pool(u2d, pz2d, b, t, tm, cw, cp, wp, ps):
    n_tiles = t // tm
    u3, p3 = u2d.reshape(b, t, CONV_CH), pz2d.reshape(b, t, POOL_CH)
    mid = pl.BlockSpec((1, tm, CONV_CH), lambda i, j: (i, j, 0))
    r = tm // CONV_HALO
    left = pl.BlockSpec((1, CONV_HALO, CONV_CH), lambda i, j: (i, jnp.maximum(j * r - 1, 0), 0))
    right = pl.BlockSpec((1, CONV_HALO, CONV_CH),
                         lambda i, j: (i, jnp.minimum((j + 1) * r, t // CONV_HALO - 1), 0))
    const = lambda a: pl.BlockSpec(a.shape, lambda i, j: (0,) * a.ndim)
    if n_tiles > 1:
        in_specs = [mid, left, right, mid, left, right]
        args = (u3, u3, u3, p3, p3, p3)
    else:
        in_specs = [mid, mid]
        args = (u3, p3)
    inv = _pool_inv_counts(t)
    in_specs += [const(cw), const(cp), const(wp), const(ps), pl.BlockSpec((tm, POOL_CH), lambda i, j: (j, 0))]
    uo, do = pl.pallas_call(
        functools.partial(_conv_pool_kernel, n_tiles=n_tiles),
        grid=(b, n_tiles), in_specs=in_specs,
        out_specs=[mid, mid],
        out_shape=[jax.ShapeDtypeStruct((b, t, CONV_CH), BF16), jax.ShapeDtypeStruct((b, t, POOL_CH), BF16)],
        scratch_shapes=[pltpu.VMEM((tm + 2 * CONV_HALO, CONV_CH), F32),
                        pltpu.VMEM((tm + 2 * CONV_HALO, POOL_CH), F32),
                        pltpu.VMEM((SUBLANES, tm + 2 * CONV_HALO - SUBLANES, CONV_CH), F32),
                        pltpu.VMEM((tm, CONV_CH), F32)],
        compiler_params=_params("arbitrary", "arbitrary"),
        name=f"conv_pool_{'lat' if n_tiles > 1 else 'ctx'}",
    )(*args, cw, cp, wp, ps, inv)
    return uo.reshape(b * t, CONV_CH), do.reshape(b * t, POOL_CH)


def _att_in_weights(w_att_in, with_v_tok):
    o = 2 * DIFF_QK + DIFF_V
    q_d, k_d, v_d = w_att_in[:, :DIFF_QK], w_att_in[:, DIFF_QK:2 * DIFF_QK], w_att_in[:, 2 * DIFF_QK:o]
    cq, ckv = w_att_in[:, o:o + Q_LORA], w_att_in[:, o + Q_LORA:o + Q_LORA + KV_LORA]
    kpe = jnp.pad(w_att_in[:, o + Q_LORA + KV_LORA:], _ROPE_LANE_PAD)
    w_t = jnp.concatenate([q_d, v_d, cq, ckv], axis=1).T.astype(BF16)
    w_r = jnp.concatenate([k_d, ckv, kpe] + ([v_d] if with_v_tok else []), axis=1).astype(BF16)
    return w_t, w_r


def _mla_weights(w_uq, w_ukv):
    wq = w_uq.reshape(Q_LORA, H_MLA, QK_NOPE + QK_ROPE)
    wq = jnp.pad(wq, ((0, 0), (0, 0), (0, MLA_SLAB - QK_NOPE - QK_ROPE))).reshape(Q_LORA, MLA_WIDE)
    wkv = w_ukv.reshape(KV_LORA, H_MLA, QK_NOPE + V_MLA)
    wk = jnp.pad(wkv[:, :, :QK_NOPE], ((0, 0), (0, 0), (0, MLA_SLAB - QK_NOPE))).reshape(KV_LORA, MLA_WIDE)
    wv = wkv[:, :, QK_NOPE:].reshape(KV_LORA, H_MLA * V_MLA)
    return wq.T.astype(BF16), wk.astype(BF16), wv.T.astype(BF16)


def kernel(x_prompt, x_sample, cache_diff_k, cache_diff_v, cache_mla_ckv, cache_mla_krope, c, c_ctx,
           w_ada, b_ada, ln_mix_g, ln_mix_b, ln_mlp_g, ln_mlp_b, w_mlp_in, w_mlp_out,
           w_att_in, w_uq, w_ukv, q_norm_g, kv_norm_g, lam_q1, lam_k1, lam_q2, lam_k2, diff_norm_g, w_att_out,
           w_conv_in, conv_w, conv_b, conv_norm_g, conv_norm_b, w_pool, pool_scale, w_conv_out):
    bc, tc, d = x_prompt.shape
    bl, tl, _ = x_sample.shape
    past = cache_diff_k.shape[2]

    cond8 = jnp.zeros((8, d), F32).at[0].set(c_ctx).at[1:1 + bl].set(c)
    mods = _ada_mods(cond8, w_ada, b_ada).reshape(DEPTH, 8, 6, d)
    ctx_row = lambda i: 0
    lat_row = lambda i: 1 + i // (tl // TOKEN_TILE)

    wqt, wk, wvt = _mla_weights(w_uq[0], w_ukv[0])
    norm_w = (q_norm_g[0].reshape(Q_LORA, 1), kv_norm_g[0].reshape(1, KV_LORA), kv_norm_g[0].reshape(KV_LORA, 1))
    att_w = lambda with_v: _att_in_weights(w_att_in[0], with_v) + norm_w + (wqt, wk, wvt)
    lam4 = jnp.stack([lam_q1[0], lam_k1[0], lam_q2[0], lam_k2[0]])
    diff_g = diff_norm_g[0].reshape(1, 2 * DH_DIFF)
    lam_init = 0.8 - 0.6 * math.exp(-0.3 * 0)
    ln4 = [jnp.stack([ln_mix_g[l], ln_mix_b[l], ln_mlp_g[l], ln_mlp_b[l]]) for l in range(DEPTH)]
    w1, w2 = w_mlp_in.astype(BF16), w_mlp_out.astype(BF16)
    w_ao = w_att_out[0].astype(BF16)
    w_ci = w_conv_in[0].astype(BF16)
    w_co = w_conv_out[0].astype(BF16)
    conv_p = jnp.stack([conv_b[0], conv_norm_g[0], conv_norm_b[0]])
    conv_w8 = jnp.broadcast_to(conv_w[0][:, None, :], (CONV_WIDTH, SUBLANES, CONV_CH))
    wp = w_pool[0].astype(BF16)
    ps = pool_scale[0].reshape(1, POOL_CH)

    xp = x_prompt.reshape(bc * tc, d)
    outs = _att_in(xp, mods, ctx_row, att_w(True), None, TOKEN_TILE)
    kd_f, vd_f, ckv_f, kpe_f = outs[6:]
    o_d, o_m = _attention_ctx(outs[:6], lam4, diff_g, lam_init, bc, tc)
    xp = _post(xp, o_d, o_m, mods, 0, ctx_row, w_ao, w1, w2, ln4[0], TOKEN_TILE)
    u, pz = _conv_in(xp, mods, 1, ctx_row, w_ci, TOKEN_TILE)
    uo, do = _conv_pool(u, pz, bc, tc, tc, conv_w8, conv_p, wp, ps)
    xp = _post(xp, uo, do, mods, 1, ctx_row, w_co, w1, w2, ln4[1], TOKEN_TILE)

    xs = x_sample.reshape(bl * tl, d)
    outs = _att_in(xs, mods, lat_row, att_w(False), _rope_tables(tl), TOKEN_TILE)
    km_c, vmc_t = _cache_kv(cache_mla_ckv[:, 0].reshape(bl * past, KV_LORA),
                            jnp.pad(cache_mla_krope[:, 0].reshape(bl * past, QK_ROPE), _ROPE_LANE_PAD),
                            wk, wvt, bl)
    cache = (cache_diff_k[:, 0].reshape(bl * past, DIFF_QK), cache_diff_v[:, 0].reshape(bl * past, DIFF_V),
             km_c, vmc_t)
    o_d, o_m = _attention_lat(outs, cache, lam4, diff_g, lam_init, bl, tl)
    xs = _post(xs, o_d, o_m, mods, 0, lat_row, w_ao, w1, w2, ln4[0], TOKEN_TILE)
    u, pz = _conv_in(xs, mods, 1, lat_row, w_ci, TOKEN_TILE)
    uo, do = _conv_pool(u, pz, bl, tl, CONV_TILE, conv_w8, conv_p, wp, ps)
    xs = _post(xs, uo, do, mods, 1, lat_row, w_co, w1, w2, ln4[1], TOKEN_TILE)

    return (xp.reshape(bc, tc, d), xs.reshape(bl, tl, d),
            kd_f.reshape(bc, 1, tc, H_DIFF, 2 * DH_DIFF), vd_f.reshape(bc, 1, tc, H_DIFF, 2 * DH_DIFF),
            ckv_f.reshape(bc, 1, tc, KV_LORA), kpe_f.reshape(bc, 1, tc, QK_ROPE))
```

```python
import functools
import math

import jax
import jax.numpy as jnp
import numpy as np
from jax import lax
from jax.experimental import pallas as pl
from jax.experimental.pallas import tpu as pltpu

D_MODEL = 1024
DEPTH = 2
GRID_W = 64
H_DIFF = 4
DH_DIFF = 64
DIFF_QK = H_DIFF * 2 * DH_DIFF
DIFF_V = H_DIFF * 2 * DH_DIFF
H_MLA = 8
Q_LORA = 256
KV_LORA = 128
QK_NOPE = 64
QK_ROPE = 32
V_MLA = 64
ATT_IN = 2 * DIFF_QK + DIFF_V + Q_LORA + KV_LORA + QK_ROPE
CONV_CH = 512
CONV_WIDTH = 31
POOL_CH = 512
POOL_WINDOWS = (2, 4, 8, 16)
POOL_GC = POOL_CH // len(POOL_WINDOWS)
CONV_IN = 2 * CONV_CH + POOL_CH
D_FF = 4 * D_MODEL
ROPE_BASE = 10000.0
NORM_EPS = 1e-5
DEEPNORM_ALPHA = (2 * DEPTH) ** 0.25
LOG2E = math.log2(math.e)

LANES = 128
SUBLANES = 8
MLA_SLAB = LANES
MLA_WIDE = H_MLA * MLA_SLAB
_ROPE_LANE_PAD = ((0, 0), (QK_NOPE, MLA_SLAB - QK_NOPE - QK_ROPE))
VMEM_LIMIT = 56 * 1024 * 1024

TOKEN_TILE = 512
KEY_CHUNK = TOKEN_TILE
Q_TILE = 1024
CTX_SEQS = 2
MLA_CHUNKS_PER_ROUND = 2
STALE_MAX_LIMIT = 64.0
CONV_TILE = 512
CONV_HALO = 16
CONV_STRIP = 32
FF_CHUNK = 1024
POST_GROUPS = 2

F32 = jnp.float32
BF16 = jnp.bfloat16


def _dot(a, b):
    return jnp.dot(a, b, preferred_element_type=F32)


def _dot_nt(a, b):
    return lax.dot_general(a, b, (((1,), (1,)), ((), ())), preferred_element_type=F32)


def _layer_norm(x, g, b):
    mu = jnp.mean(x, axis=-1, keepdims=True)
    xc = x - mu
    var = jnp.mean(xc * xc, axis=-1, keepdims=True)
    return xc * lax.rsqrt(var + NORM_EPS) * g + b


def _rms_norm(x, g, axis=-1):
    ms = jnp.mean(x * x, axis=axis, keepdims=True)
    return x * lax.rsqrt(ms + NORM_EPS) * g


def _const_spec(shape):
    nd = len(shape)
    return pl.BlockSpec(shape, lambda *_: (0,) * nd, pipeline_mode=pl.Buffered(1))


def _params(*sem):
    return pltpu.CompilerParams(dimension_semantics=sem, vmem_limit_bytes=VMEM_LIMIT)


ADA_TILE = 1536


def _ada_kernel(cond_ref, w_ref, b_ref, out_ref):
    cond = cond_ref[...]
    act = (cond * jax.nn.sigmoid(cond)).astype(BF16)
    out_ref[0] = _dot(act, w_ref[0].astype(BF16)) + b_ref[0]


def _ada_mods(cond8, w_ada, b_ada):
    n = 6 * D_MODEL
    return pl.pallas_call(
        _ada_kernel,
        grid=(DEPTH, n // ADA_TILE),
        in_specs=[
            pl.BlockSpec((8, D_MODEL), lambda l, j: (0, 0)),
            pl.BlockSpec((1, D_MODEL, ADA_TILE), lambda l, j: (l, 0, j)),
            pl.BlockSpec((1, 1, ADA_TILE), lambda l, j: (l, 0, j)),
        ],
        out_specs=pl.BlockSpec((1, 8, ADA_TILE), lambda l, j: (l, 0, j)),
        out_shape=jax.ShapeDtypeStruct((DEPTH, 8, n), F32),
        compiler_params=_params("arbitrary", "arbitrary"),
        name="ada_mods",
    )(cond8, w_ada, b_ada.reshape(DEPTH, 1, n))


def _rope_tables(t_lat):
    pos = np.arange(t_lat)
    row = (pos // GRID_W).astype(np.float64)
    col = (pos % GRID_W).astype(np.float64)

    def tables(kinds):
        cos = np.ones((t_lat, LANES))
        sa = np.zeros((t_lat, LANES))
        sb = np.zeros((t_lat, LANES))
        for lane, kind in enumerate(kinds):
            if kind is None:
                continue
            axis, half, j, upper = kind
            ang = (row if axis == 0 else col) * ROPE_BASE ** (-float(j) / half)
            cos[:, lane] = np.cos(ang)
            if upper:
                sb[:, lane] = np.sin(ang)
            else:
                sa[:, lane] = -np.sin(ang)
        return [cos, sa, sb]

    def rot_kinds(n):
        half = n // 4
        kinds = []
        for i in range(n):
            axis, r = divmod(i, n // 2)
            kinds.append((axis, half, r % half, r >= half))
        return kinds

    diff = rot_kinds(DH_DIFF) * 2
    mla_q = [None] * QK_NOPE + rot_kinds(QK_ROPE) + [None] * (LANES - QK_NOPE - QK_ROPE)
    chan = np.stack([t.T for t in tables(diff) + tables(mla_q)]).astype(np.float32)
    tok = np.stack(tables(diff) + tables(mla_q)).astype(np.float32)
    return jnp.asarray(chan), jnp.asarray(tok)


def _rope_tok(x, cos, sa, sb, shift):
    return x * cos + pltpu.roll(x, LANES - shift, 1) * sa + pltpu.roll(x, shift, 1) * sb


def _rope_tok_wide(x, cos, sa, sb, shift):
    n = x.shape[1] // LANES
    return jnp.concatenate(
        [_rope_tok(x[:, i * LANES:(i + 1) * LANES], cos, sa, sb, shift) for i in range(n)], axis=1)


def _rope_chan(x, cos, sa, sb, shift):
    up = jnp.concatenate([x[shift:], x[:shift]], axis=0)
    down = jnp.concatenate([x[-shift:], x[:-shift]], axis=0)
    return x * cos + up * sa + down * sb


def _rope_chan_wide(x, cos, sa, sb, shift):
    n = x.shape[0] // LANES
    return jnp.concatenate(
        [_rope_chan(x[i * LANES:(i + 1) * LANES], cos, sa, sb, shift) for i in range(n)], axis=0)


_T_QD, _T_VD, _T_CQ, _T_CKV = 0, DIFF_QK, DIFF_QK + DIFF_V, DIFF_QK + DIFF_V + Q_LORA
_T_ROWS = _T_CKV + KV_LORA
_R_KD, _R_CKV, _R_KPE, _R_VD = 0, DIFF_QK, DIFF_QK + KV_LORA, DIFF_QK + KV_LORA + LANES

DIFF_QSCALE = DH_DIFF ** -0.5 * LOG2E
MLA_QSCALE = (QK_NOPE + QK_ROPE) ** -0.5 * LOG2E


def _att_in_body(x_ref, mods_ref, wt_ref, wr_ref, qg_ref, kvg_row_ref, kvg_col_ref, wqt_ref, wk_ref, wvt_ref):
    mods = mods_ref[0, 0]
    sh, sc = mods[0:1], mods[1:2]
    h = (x_ref[...] * (1.0 + sc) + sh).astype(BF16)
    pt = _dot_nt(wt_ref[...], h)
    pr = _dot(h, wr_ref[...])
    qd_t = pt[_T_QD:_T_QD + DIFF_QK] * DIFF_QSCALE
    vd_t = pt[_T_VD:_T_VD + DIFF_V]
    cqn_t = _rms_norm(pt[_T_CQ:_T_CQ + Q_LORA], qg_ref[...], axis=0)
    qm_t = _dot(wqt_ref[...], cqn_t.astype(BF16)) * MLA_QSCALE
    ckvn_t = _rms_norm(pt[_T_CKV:_T_CKV + KV_LORA], kvg_col_ref[...], axis=0)
    vm_t = _dot(wvt_ref[...], ckvn_t.astype(BF16))
    k_d = pr[:, _R_KD:_R_KD + DIFF_QK]
    ckv_n = _rms_norm(pr[:, _R_CKV:_R_CKV + KV_LORA], kvg_row_ref[...])
    kpe = pr[:, _R_KPE:_R_KPE + LANES]
    k_nope = _dot(ckv_n.astype(BF16), wk_ref[...])
    return pr, qd_t, vd_t, qm_t, vm_t, k_d, ckv_n, kpe, k_nope


def _mla_keys(k_nope, kpe_slab):
    return (k_nope + jnp.concatenate([kpe_slab] * H_MLA, axis=1)).astype(BF16)


def _att_in_ctx_kernel(x_ref, mods_ref, wt_ref, wr_ref, qg_ref, kvg_row_ref, kvg_col_ref, wqt_ref, wk_ref,
                       wvt_ref,
                       qdt_ref, kd_ref, vdt_ref, qmt_ref, km_ref, vmt_ref,
                       kdf_ref, vdf_ref, ckvf_ref, kpef_ref):
    pr, qd_t, vd_t, qm_t, vm_t, k_d, ckv_n, kpe, k_nope = _att_in_body(
        x_ref, mods_ref, wt_ref, wr_ref, qg_ref, kvg_row_ref, kvg_col_ref, wqt_ref, wk_ref, wvt_ref)
    qdt_ref[...] = qd_t.astype(BF16)
    kd_ref[...] = k_d.astype(BF16)
    vdt_ref[0] = vd_t.astype(BF16)
    qmt_ref[...] = qm_t.astype(BF16)
    km_ref[...] = _mla_keys(k_nope, kpe)
    vmt_ref[0] = vm_t.astype(BF16)
    v_d = pr[:, _R_VD:_R_VD + DIFF_V]
    for h in range(H_DIFF):
        kdf_ref[:, h, :] = k_d[:, h * LANES:(h + 1) * LANES]
        vdf_ref[:, h, :] = v_d[:, h * LANES:(h + 1) * LANES]
    ckvf_ref[...] = ckv_n
    kpef_ref[...] = kpe[:, QK_NOPE:QK_NOPE + QK_ROPE]


def _att_in_lat_kernel(x_ref, mods_ref, wt_ref, wr_ref, qg_ref, kvg_row_ref, kvg_col_ref, wqt_ref, wk_ref,
                       wvt_ref, tabc_ref, tabt_ref,
                       qdut_ref, qdrt_ref, kdr_ref, vdt_ref, qmut_ref, qmrt_ref, kmr_ref, vmt_ref):
    _, qd_t, vd_t, qm_t, vm_t, k_d, ckv_n, kpe, k_nope = _att_in_body(
        x_ref, mods_ref, wt_ref, wr_ref, qg_ref, kvg_row_ref, kvg_col_ref, wqt_ref, wk_ref, wvt_ref)
    qdut_ref[...] = qd_t.astype(BF16)
    qdrt_ref[...] = _rope_chan_wide(qd_t, tabc_ref[0], tabc_ref[1], tabc_ref[2], DH_DIFF // 4).astype(BF16)
    kdr_ref[...] = _rope_tok_wide(k_d, tabt_ref[0], tabt_ref[1], tabt_ref[2], DH_DIFF // 4).astype(BF16)
    vdt_ref[0] = vd_t.astype(BF16)
    qmut_ref[...] = qm_t.astype(BF16)
    qmrt_ref[...] = _rope_chan_wide(qm_t, tabc_ref[3], tabc_ref[4], tabc_ref[5], QK_ROPE // 4).astype(BF16)
    kpe_r = _rope_tok(kpe, tabt_ref[3], tabt_ref[4], tabt_ref[5], QK_ROPE // 4)
    kmr_ref[...] = _mla_keys(k_nope, kpe_r)
    vmt_ref[0] = vm_t.astype(BF16)


def _att_in(x2d, mods, mod_row_fn, wts, tables, tm):
    t = x2d.shape[0]
    nt = t // tm
    tok = lambda n: pl.BlockSpec((tm, n), lambda i: (i, 0))
    chan = lambda n: pl.BlockSpec((n, tm), lambda i: (0, i))
    chunk = lambda n: pl.BlockSpec((1, n, tm), lambda i: (i, 0, 0))
    in_specs = [tok(D_MODEL), pl.BlockSpec((1, 1, 6, D_MODEL), lambda i: (0, mod_row_fn(i), 0, 0))]
    in_specs += [_const_spec(w.shape) for w in wts]
    s_tok = lambda n, dt=BF16: jax.ShapeDtypeStruct((t, n), dt)
    s_chan = lambda n: jax.ShapeDtypeStruct((n, t), BF16)
    s_chunk = lambda n: jax.ShapeDtypeStruct((nt, n, tm), BF16)
    if tables is None:
        heads = pl.BlockSpec((tm, H_DIFF, 2 * DH_DIFF), lambda i: (i, 0, 0))
        s_heads = jax.ShapeDtypeStruct((t, H_DIFF, 2 * DH_DIFF), F32)
        out_shape = [s_chan(DIFF_QK), s_tok(DIFF_QK), s_chunk(DIFF_V), s_chan(MLA_WIDE), s_tok(MLA_WIDE),
                     s_chunk(H_MLA * V_MLA),
                     s_heads, s_heads, s_tok(KV_LORA, F32), s_tok(QK_ROPE, F32)]
        out_specs = [chan(DIFF_QK), tok(DIFF_QK), chunk(DIFF_V), chan(MLA_WIDE), tok(MLA_WIDE),
                     chunk(H_MLA * V_MLA), heads, heads, tok(KV_LORA), tok(QK_ROPE)]
        kern, args, name = _att_in_ctx_kernel, (), "att_in_ctx"
    else:
        tab_c, tab_t = tables
        t_lat = tab_t.shape[1]
        in_specs += [pl.BlockSpec((6, LANES, tm), lambda i: (0, 0, i % (t_lat // tm))),
                     pl.BlockSpec((6, tm, LANES), lambda i: (0, i % (t_lat // tm), 0))]
        out_shape = [s_chan(DIFF_QK), s_chan(DIFF_QK), s_tok(DIFF_QK), s_chunk(DIFF_V),
                     s_chan(MLA_WIDE), s_chan(MLA_WIDE), s_tok(MLA_WIDE), s_chunk(H_MLA * V_MLA)]
        out_specs = [chan(DIFF_QK), chan(DIFF_QK), tok(DIFF_QK), chunk(DIFF_V),
                     chan(MLA_WIDE), chan(MLA_WIDE), tok(MLA_WIDE), chunk(H_MLA * V_MLA)]
        kern, args, name = _att_in_lat_kernel, (tab_c, tab_t), "att_in_lat"
    return pl.pallas_call(
        kern, grid=(nt,), in_specs=in_specs, out_specs=out_specs, out_shape=out_shape,
        compiler_params=_params("arbitrary"), name=name,
    )(x2d, mods, *wts, *args)


def _cache_kv_kernel(ckv_ref, kpe_ref, wk_ref, wvt_ref, km_ref, vmt_ref):
    ckv_b = ckv_ref[...].astype(BF16)
    km_ref[...] = _mla_keys(_dot(ckv_b, wk_ref[...]), kpe_ref[...])
    past = vmt_ref.shape[2]
    vt = _dot_nt(wvt_ref[...], ckv_b)
    for b in range(vmt_ref.shape[0]):
        vmt_ref[b] = vt[:, b * past:(b + 1) * past].astype(BF16)


def _cache_kv(ckv2d, kpe_slab2d, wk, wvt, bl):
    t = ckv2d.shape[0]
    full = lambda a: pl.BlockSpec(a.shape, lambda i: (0,) * a.ndim)
    return pl.pallas_call(
        _cache_kv_kernel, grid=(1,),
        in_specs=[full(ckv2d), full(kpe_slab2d), full(wk), full(wvt)],
        out_specs=[pl.BlockSpec((t, MLA_WIDE), lambda i: (0, 0)),
                   pl.BlockSpec((bl, H_MLA * V_MLA, t // bl), lambda i: (0, 0, 0))],
        out_shape=[jax.ShapeDtypeStruct((t, MLA_WIDE), BF16),
                   jax.ShapeDtypeStruct((bl, H_MLA * V_MLA, t // bl), BF16)],
        compiler_params=_params("arbitrary"), name="cache_kv",
    )(ckv2d, kpe_slab2d, wk, wvt)


def _softmax_pv(s, vt, state, stale_max):
    m_c = jnp.max(s, axis=0, keepdims=True)
    if state is None:
        p = jnp.exp2(s - m_c)
        return m_c, jnp.sum(p, axis=0, keepdims=True), _dot(vt, p.astype(BF16)), jnp.zeros_like(m_c)
    m, l, acc, rise = state
    m_new = jnp.maximum(m, m_c)
    alpha = jnp.exp2(m - m_new)
    if stale_max:
        p = jnp.exp2(s - m)
        l = alpha * (l + jnp.sum(p, axis=0, keepdims=True))
        return m_new, l, alpha * (acc + _dot(vt, p.astype(BF16))), jnp.maximum(rise, m_c - m)
    p = jnp.exp2(s - m_new)
    l = alpha * l + jnp.sum(p, axis=0, keepdims=True)
    return m_new, l, alpha * acc + _dot(vt, p.astype(BF16)), rise


def _attend(first, rest, chunks_per_round=1, stale_max=False):
    items = [(si, qt, (lambda k=k: k), (lambda vt=vt: vt)) for si, (qt, k, vt) in enumerate(first)]
    if rest is not None:
        keys = KEY_CHUNK * chunks_per_round
        for c in range(rest[0][2].shape[0] // chunks_per_round):
            for si, (qt, k_ref, vt_ref) in enumerate(rest):
                items.append((si, qt, (lambda r=k_ref, c=c: r[c * keys:(c + 1) * keys, :]),
                              (lambda r=vt_ref, c=c: jnp.concatenate(
                                  [r[c * chunks_per_round + i] for i in range(chunks_per_round)], axis=1))))
    ns = len(first)
    groups = [items[i:i + ns] for i in range(0, len(items), ns)]
    scores = lambda grp: [_dot(k_fn(), qt) for (_, qt, k_fn, _) in grp]
    state = [None] * ns
    s_next = scores(groups[0])
    for g, grp in enumerate(groups):
        s_cur = s_next
        if g + 1 < len(groups):
            s_next = scores(groups[g + 1])
        for s, (si, _, _, vt_fn) in zip(s_cur, grp):
            state[si] = _softmax_pv(s, vt_fn(), state[si], stale_max)
    rise = functools.reduce(jnp.maximum, [st[3] for st in state])
    return [acc / l for (_, l, acc, _) in state], rise


def _attend_guarded(first, rest, chunks_per_round, finish):
    def run(stale_max):
        o, rise = _attend(first, rest, chunks_per_round, stale_max)
        finish(o)
        return rise

    rise = run(True)

    @pl.when(jnp.max(rise) > STALE_MAX_LIMIT)
    def _():
        run(False)


def _split_maps(qt):
    row = lax.broadcasted_iota(jnp.int32, qt.shape, 0)
    zero = jnp.zeros_like(qt)
    return jnp.where(row < DH_DIFF, qt, zero), jnp.where(row >= DH_DIFF, qt, zero)


def _lambda(lam_ref, lam_init):
    lv = lam_ref[...]
    a = jnp.sum(lv[0:1] * lv[1:2], axis=-1, keepdims=True)
    b = jnp.sum(lv[2:3] * lv[3:4], axis=-1, keepdims=True)
    return jnp.exp(a) - jnp.exp(b) + lam_init


def _diff_finish(o1_t, o2_t, lam_ref, g_ref, lam_init):
    o = (o1_t - _lambda(lam_ref, lam_init) * o2_t).T
    return (_rms_norm(o, g_ref[...]) * (1.0 - lam_init)).astype(BF16)


def _mla_finish(oa_t, ob_t):
    return jnp.concatenate([oa_t, ob_t], axis=0).T.astype(BF16)


def _diff_ctx_kernel(qt_ref, k_ref, vt_ref, lam_ref, g_ref, out_ref, *, lam_init):
    t = qt_ref.shape[1] // CTX_SEQS
    first = []
    for q in range(CTX_SEQS):
        ts = slice(q * t, (q + 1) * t)
        for h in range(H_DIFF):
            sl = slice(h * LANES, (h + 1) * LANES)
            k, vt = k_ref[ts, sl], vt_ref[0, sl, ts]
            first += [(qm, k, vt) for qm in _split_maps(qt_ref[sl, ts])]
    o, _ = _attend(first, None)
    n = 2 * H_DIFF
    for q in range(CTX_SEQS):
        out_ref[q * t:(q + 1) * t, :] = jnp.concatenate(
            [_diff_finish(o[q * n + 2 * h], o[q * n + 2 * h + 1], lam_ref, g_ref, lam_init) for h in range(H_DIFF)],
            axis=1)


def _diff_lat_kernel(qut_ref, qrt_ref, kc_ref, vc_ref, k_ref, vt_ref, lam_ref, g_ref, out_ref, *, lam_init):
    kc, vct = kc_ref[...].astype(BF16), vc_ref[...].T.astype(BF16)
    first = [(q, kc, vct) for q in _split_maps(qut_ref[...])]
    rest = [(q, k_ref, vt_ref) for q in _split_maps(qrt_ref[...])]

    def finish(o):
        out_ref[...] = _diff_finish(o[0], o[1], lam_ref, g_ref, lam_init)

    _attend_guarded(first, rest, 1, finish)


def _mla_ctx_kernel(qt_ref, k_ref, vt_ref, out_ref):
    t = qt_ref.shape[1] // CTX_SEQS
    first = []
    for q in range(CTX_SEQS):
        ts = slice(q * t, (q + 1) * t)
        for h in range(H_MLA):
            sl = slice(h * MLA_SLAB, (h + 1) * MLA_SLAB)
            first.append((qt_ref[sl, ts], k_ref[ts, sl], vt_ref[0, h * V_MLA:(h + 1) * V_MLA, ts]))
    o, _ = _attend(first, None)
    for q in range(CTX_SEQS):
        out_ref[q * t:(q + 1) * t, :] = jnp.concatenate(
            [_mla_finish(o[q * H_MLA + 2 * j], o[q * H_MLA + 2 * j + 1]) for j in range(H_MLA // 2)], axis=1)


def _mla_lat_kernel(qua_ref, qub_ref, qra_ref, qrb_ref, kca_ref, kcb_ref, vcta_ref, vctb_ref, ka_ref, kb_ref,
                    vta_ref, vtb_ref, out_ref):
    first = [(qua_ref[...], kca_ref[...], vcta_ref[0]), (qub_ref[...], kcb_ref[...], vctb_ref[0])]
    rest = [(qra_ref[...], ka_ref, vta_ref), (qrb_ref[...], kb_ref, vtb_ref)]

    def finish(o):
        out_ref[...] = _mla_finish(o[0], o[1])

    _attend_guarded(first, rest, MLA_CHUNKS_PER_ROUND, finish)


def _attention_ctx(streams, lam4, diff_g, lam_init, b, t):
    qd_t, kd, vd_t, qm_t, km, vm_t = streams
    per_chunk = vd_t.shape[2] // t // CTX_SEQS
    t, b = t * CTX_SEQS, b // CTX_SEQS
    sem = _params("arbitrary")
    qt = lambda w: pl.BlockSpec((w, t), lambda i: (0, i))
    keys = lambda w: pl.BlockSpec((t, w), lambda i: (i, 0))
    vals = lambda w: pl.BlockSpec((1, w, t), lambda i: (i // per_chunk, 0, i % per_chunk))
    o_d = pl.pallas_call(
        functools.partial(_diff_ctx_kernel, lam_init=lam_init),
        grid=(b,),
        in_specs=[qt(DIFF_QK), keys(DIFF_QK), vals(DIFF_V),
                  pl.BlockSpec(lam4.shape, lambda i: (0, 0)),
                  pl.BlockSpec(diff_g.shape, lambda i: (0, 0))],
        out_specs=keys(DIFF_V),
        out_shape=jax.ShapeDtypeStruct((b * t, DIFF_V), BF16),
        compiler_params=sem, name="diff_attn_ctx",
    )(qd_t, kd, vd_t, lam4, diff_g)
    o_m = pl.pallas_call(
        _mla_ctx_kernel,
        grid=(b,),
        in_specs=[qt(MLA_WIDE), keys(MLA_WIDE), vals(H_MLA * V_MLA)],
        out_specs=keys(H_MLA * V_MLA),
        out_shape=jax.ShapeDtypeStruct((b * t, H_MLA * V_MLA), BF16),
        compiler_params=sem, name="mla_attn_ctx",
    )(qm_t, km, vm_t)
    return o_d, o_m


def _attention_lat(streams, cache, lam4, diff_g, lam_init, b, t):
    qdu_t, qdr_t, kdr, vd_t, qmu_t, qmr_t, kmr, vm_t = streams
    kd_c, vd_c, km_c, vmc_t = cache
    past = kd_c.shape[0] // b
    nq = t // Q_TILE
    n_chunks = t // KEY_CHUNK
    sem = _params("arbitrary", "arbitrary", "arbitrary")
    qt = lambda f: pl.BlockSpec((LANES, Q_TILE), lambda i, h, j: (f(h), i * nq + j))
    keys = lambda n, f: pl.BlockSpec((n, LANES), lambda i, h, j: (i, f(h)))
    vals = pl.BlockSpec((n_chunks, LANES, KEY_CHUNK), lambda i, h, j: (i, h, 0))
    out = pl.BlockSpec((Q_TILE, LANES), lambda i, h, j: (i * nq + j, h))
    same, even, odd = (lambda h: h), (lambda h: 2 * h), (lambda h: 2 * h + 1)
    o_d = pl.pallas_call(
        functools.partial(_diff_lat_kernel, lam_init=lam_init),
        grid=(b, H_DIFF, nq),
        in_specs=[qt(same), qt(same), keys(past, same), keys(past, same), keys(t, same), vals,
                  pl.BlockSpec(lam4.shape, lambda i, h, j: (0, 0)),
                  pl.BlockSpec(diff_g.shape, lambda i, h, j: (0, 0))],
        out_specs=out,
        out_shape=jax.ShapeDtypeStruct((b * t, DIFF_V), BF16),
        compiler_params=sem, name="diff_attn_lat",
    )(qdu_t, qdr_t, kd_c, vd_c, kdr, vd_t, lam4, diff_g)
    o_m = pl.pallas_call(
        _mla_lat_kernel,
        grid=(b, H_MLA // 2, nq),
        in_specs=[qt(even), qt(odd), qt(even), qt(odd), keys(past, even), keys(past, odd),
                  pl.BlockSpec((1, V_MLA, past), lambda i, h, j: (i, 2 * h, 0)),
                  pl.BlockSpec((1, V_MLA, past), lambda i, h, j: (i, 2 * h + 1, 0)),
                  keys(t, even), keys(t, odd),
                  pl.BlockSpec((n_chunks, V_MLA, KEY_CHUNK), lambda i, h, j: (i, 2 * h, 0)),
                  pl.BlockSpec((n_chunks, V_MLA, KEY_CHUNK), lambda i, h, j: (i, 2 * h + 1, 0))],
        out_specs=out,
        out_shape=jax.ShapeDtypeStruct((b * t, H_MLA * V_MLA), BF16),
        compiler_params=sem, name="mla_attn_lat",
    )(qmu_t, qmu_t, qmr_t, qmr_t, km_c, km_c, vmc_t, vmc_t, kmr, kmr, vm_t, vm_t)
    return o_d, o_m


def _post_kernel(x_ref, ma_ref, mb_ref, mods_ref, wo_ref, w1_ref, w2_ref, ln_ref, out_ref):
    mods = mods_ref[0, 0]
    g_m, sh_f, sc_f, g_f = mods[2:3], mods[3:4], mods[4:5], mods[5:6]
    ln = ln_ref[...]
    rows = x_ref.shape[0] // POST_GROUPS
    grp = [slice(r * rows, (r + 1) * rows) for r in range(POST_GROUPS)]
    ys = [_dot(jnp.concatenate([ma_ref[g, :], mb_ref[g, :]], axis=1), wo_ref[...]) for g in grp]
    x1s = [_layer_norm(DEEPNORM_ALPHA * x_ref[g, :] + g_m * y, ln[0:1], ln[1:2]) for g, y in zip(grp, ys)]
    fs = []
    for x1 in x1s:
        h = (x1 * (1.0 + sc_f) + sh_f).astype(BF16)
        f = jnp.zeros(x1.shape, F32)
        for c in range(D_FF // FF_CHUNK):
            a = jnp.maximum(_dot(h, w1_ref[0, :, c * FF_CHUNK:(c + 1) * FF_CHUNK]), 0.0)
            f = f + _dot((a * a).astype(BF16), w2_ref[0, c * FF_CHUNK:(c + 1) * FF_CHUNK, :])
        fs.append(f)
    for g, x1, f in zip(grp, x1s, fs):
        out_ref[g, :] = _layer_norm(DEEPNORM_ALPHA * x1 + g_f * f, ln[2:3], ln[3:4])


def _post(x2d, mix_a, mix_b, mods, layer, mod_row_fn, wo, w1, w2, ln4, tm):
    t = x2d.shape[0]
    tok = lambda n: pl.BlockSpec((tm, n), lambda i: (i, 0))
    layer_slab = lambda w: pl.BlockSpec((1,) + w.shape[1:], lambda i: (layer, 0, 0), pipeline_mode=pl.Buffered(1))
    return pl.pallas_call(
        _post_kernel, grid=(t // tm,),
        in_specs=[tok(D_MODEL), tok(mix_a.shape[1]), tok(mix_b.shape[1]),
                  pl.BlockSpec((1, 1, 6, D_MODEL), lambda i: (layer, mod_row_fn(i), 0, 0)),
                  _const_spec(wo.shape), layer_slab(w1), layer_slab(w2),
                  _const_spec(ln4.shape)],
        out_specs=tok(D_MODEL),
        out_shape=jax.ShapeDtypeStruct((t, D_MODEL), F32),
        compiler_params=_params("arbitrary"), name=f"post_mlp_l{layer}",
    )(x2d, mix_a, mix_b, mods, wo, w1, w2, ln4)


def _conv_in_kernel(x_ref, mods_ref, w_ref, u_ref, pz_ref):
    mods = mods_ref[0, 0]
    sh, sc = mods[0:1], mods[1:2]
    h = (x_ref[...] * (1.0 + sc) + sh).astype(BF16)
    proj = _dot(h, w_ref[...])
    a, gate = proj[:, 0:CONV_CH], proj[:, CONV_CH:2 * CONV_CH]
    u_ref[...] = a * jax.nn.sigmoid(gate)
    pz_ref[...] = proj[:, 2 * CONV_CH:]


def _conv_in(x2d, mods, layer, mod_row_fn, w, tm):
    t = x2d.shape[0]
    tok = lambda n: pl.BlockSpec((tm, n), lambda i: (i, 0))
    return pl.pallas_call(
        _conv_in_kernel, grid=(t // tm,),
        in_specs=[tok(D_MODEL),
                  pl.BlockSpec((1, 1, 6, D_MODEL), lambda i: (layer, mod_row_fn(i), 0, 0)),
                  _const_spec(w.shape)],
        out_specs=[tok(CONV_CH), tok(POOL_CH)],
        out_shape=[jax.ShapeDtypeStruct((t, CONV_CH), F32), jax.ShapeDtypeStruct((t, POOL_CH), F32)],
        compiler_params=_params("arbitrary"), name="conv_in",
    )(x2d, mods, w)


def _conv_pool_kernel(*refs, n_tiles):
    if n_tiles > 1:
        (u_ref, ul_ref, ur_ref, pz_ref, pl_ref, pr_ref, cw_ref, cp_ref, wp_ref, ps_ref, inv_ref,
         uo_ref, do_ref, ubuf, pbuf, ushift, cbuf) = refs
    else:
        u_ref, pz_ref, cw_ref, cp_ref, wp_ref, ps_ref, inv_ref, uo_ref, do_ref, ubuf, pbuf, ushift, cbuf = refs
    tm = u_ref.shape[1]
    j = pl.program_id(1)
    halo_zero = jnp.zeros((CONV_HALO, CONV_CH), F32)
    for buf, mid, sides in ((ubuf, u_ref, (ul_ref, ur_ref) if n_tiles > 1 else None),
                            (pbuf, pz_ref, (pl_ref, pr_ref) if n_tiles > 1 else None)):
        buf[CONV_HALO:CONV_HALO + tm, :] = mid[0]
        if sides is None:
            buf[0:CONV_HALO, :] = halo_zero
            buf[CONV_HALO + tm:, :] = halo_zero
        else:
            buf[0:CONV_HALO, :] = jnp.where(j > 0, sides[0][0], halo_zero)
            buf[CONV_HALO + tm:, :] = jnp.where(j < n_tiles - 1, sides[1][0], halo_zero)

    rows_sh = ushift.shape[1]
    for b in range(SUBLANES):
        ushift[b] = ubuf[b:b + rows_sh, :]
    base = CONV_HALO - CONV_WIDTH // 2
    cp = cp_ref[...]

    def strip(i, carry):
        r0 = pl.multiple_of(i * CONV_STRIP, CONV_STRIP)
        acc = jnp.zeros((CONV_STRIP, CONV_CH), F32)
        for k in range(CONV_WIDTH):
            a, b = divmod(base + k, SUBLANES)
            w = jnp.concatenate([cw_ref[k]] * (CONV_STRIP // SUBLANES), axis=0)
            acc = acc + ushift[b, pl.ds(r0 + a * SUBLANES, CONV_STRIP), :] * w
        cbuf[pl.ds(r0, CONV_STRIP), :] = acc
        return carry

    lax.fori_loop(0, tm // CONV_STRIP, strip, 0)
    z = _layer_norm(cbuf[...] + cp[0:1], cp[1:2], cp[2:3])
    uo_ref[0] = (z * jax.nn.sigmoid(z)).astype(uo_ref.dtype)

    outs = []
    for g, w in enumerate(POOL_WINDOWS):
        cols = slice(g * POOL_GC, (g + 1) * POOL_GC)
        s = jnp.zeros((tm, POOL_GC), F32)
        for d in range(-(w // 2), w // 2):
            s = s + pbuf[CONV_HALO + d:CONV_HALO + d + tm, cols]
        dgrp = s * inv_ref[:, cols] - pz_ref[0][:, cols]
        outs.append(_dot(dgrp.astype(BF16), wp_ref[g]))
    do_ref[0] = (jnp.concatenate(outs, axis=1) * ps_ref[...]).astype(do_ref.dtype)


def _pool_inv_counts(t):
    pos = np.arange(t)
    cols = []
    for w in POOL_WINDOWS:
        cnt = np.minimum(pos + w // 2 - 1, t - 1) - np.maximum(pos - w // 2, 0) + 1
        cols.append(np.repeat((1.0 / cnt)[:, None], POOL_GC, axis=1))
    return jnp.asarray(np.concatenate(cols, axis=1).astype(np.float32))


def _conv_pool(u2d, pz2d, b, t, tm, cw, cp, wp, ps):
    n_tiles = t // tm
    u3, p3 = u2d.reshape(b, t, CONV_CH), pz2d.reshape(b, t, POOL_CH)
    mid = pl.BlockSpec((1, tm, CONV_CH), lambda i, j: (i, j, 0))
    r = tm // CONV_HALO
    left = pl.BlockSpec((1, CONV_HALO, CONV_CH), lambda i, j: (i, jnp.maximum(j * r - 1, 0), 0))
    right = pl.BlockSpec((1, CONV_HALO, CONV_CH),
                         lambda i, j: (i, jnp.minimum((j + 1) * r, t // CONV_HALO - 1), 0))
    const = lambda a: pl.BlockSpec(a.shape, lambda i, j: (0,) * a.ndim)
    if n_tiles > 1:
        in_specs = [mid, left, right, mid, left, right]
        args = (u3, u3, u3, p3, p3, p3)
    else:
        in_specs = [mid, mid]
        args = (u3, p3)
    inv = _pool_inv_counts(t)
    in_specs += [const(cw), const(cp), const(wp), const(ps), pl.BlockSpec((tm, POOL_CH), lambda i, j: (j, 0))]
    uo, do = pl.pallas_call(
        functools.partial(_conv_pool_kernel, n_tiles=n_tiles),
        grid=(b, n_tiles), in_specs=in_specs,
        out_specs=[mid, mid],
        out_shape=[jax.ShapeDtypeStruct((b, t, CONV_CH), BF16), jax.ShapeDtypeStruct((b, t, POOL_CH), BF16)],
        scratch_shapes=[pltpu.VMEM((tm + 2 * CONV_HALO, CONV_CH), F32),
                        pltpu.VMEM((tm + 2 * CONV_HALO, POOL_CH), F32),
                        pltpu.VMEM((SUBLANES, tm + 2 * CONV_HALO - SUBLANES, CONV_CH), F32),
                        pltpu.VMEM((tm, CONV_CH), F32)],
        compiler_params=_params("arbitrary", "arbitrary"),
        name=f"conv_pool_{'lat' if n_tiles > 1 else 'ctx'}",
    )(*args, cw, cp, wp, ps, inv)
    return uo.reshape(b * t, CONV_CH), do.reshape(b * t, POOL_CH)


def _att_in_weights(w_att_in, with_v_tok):
    o = 2 * DIFF_QK + DIFF_V
    q_d, k_d, v_d = w_att_in[:, :DIFF_QK], w_att_in[:, DIFF_QK:2 * DIFF_QK], w_att_in[:, 2 * DIFF_QK:o]
    cq, ckv = w_att_in[:, o:o + Q_LORA], w_att_in[:, o + Q_LORA:o + Q_LORA + KV_LORA]
    kpe = jnp.pad(w_att_in[:, o + Q_LORA + KV_LORA:], _ROPE_LANE_PAD)
    w_t = jnp.concatenate([q_d, v_d, cq, ckv], axis=1).T.astype(BF16)
    w_r = jnp.concatenate([k_d, ckv, kpe] + ([v_d] if with_v_tok else []), axis=1).astype(BF16)
    return w_t, w_r


def _mla_weights(w_uq, w_ukv):
    wq = w_uq.reshape(Q_LORA, H_MLA, QK_NOPE + QK_ROPE)
    wq = jnp.pad(wq, ((0, 0), (0, 0), (0, MLA_SLAB - QK_NOPE - QK_ROPE))).reshape(Q_LORA, MLA_WIDE)
    wkv = w_ukv.reshape(KV_LORA, H_MLA, QK_NOPE + V_MLA)
    wk = jnp.pad(wkv[:, :, :QK_NOPE], ((0, 0), (0, 0), (0, MLA_SLAB - QK_NOPE))).reshape(KV_LORA, MLA_WIDE)
    wv = wkv[:, :, QK_NOPE:].reshape(KV_LORA, H_MLA * V_MLA)
    return wq.T.astype(BF16), wk.astype(BF16), wv.T.astype(BF16)


def kernel(x_prompt, x_sample, cache_diff_k, cache_diff_v, cache_mla_ckv, cache_mla_krope, c, c_ctx,
           w_ada, b_ada, ln_mix_g, ln_mix_b, ln_mlp_g, ln_mlp_b, w_mlp_in, w_mlp_out,
           w_att_in, w_uq, w_ukv, q_norm_g, kv_norm_g, lam_q1, lam_k1, lam_q2, lam_k2, diff_norm_g, w_att_out,
           w_conv_in, conv_w, conv_b, conv_norm_g, conv_norm_b, w_pool, pool_scale, w_conv_out):
    bc, tc, d = x_prompt.shape
    bl, tl, _ = x_sample.shape
    past = cache_diff_k.shape[2]

    cond8 = jnp.zeros((8, d), F32).at[0].set(c_ctx).at[1:1 + bl].set(c)
    mods = _ada_mods(cond8, w_ada, b_ada).reshape(DEPTH, 8, 6, d)
    ctx_row = lambda i: 0
    lat_row = lambda i: 1 + i // (tl // TOKEN_TILE)

    wqt, wk, wvt = _mla_weights(w_uq[0], w_ukv[0])
    norm_w = (q_norm_g[0].reshape(Q_LORA, 1), kv_norm_g[0].reshape(1, KV_LORA), kv_norm_g[0].reshape(KV_LORA, 1))
    att_w = lambda with_v: _att_in_weights(w_att_in[0], with_v) + norm_w + (wqt, wk, wvt)
    lam4 = jnp.stack([lam_q1[0], lam_k1[0], lam_q2[0], lam_k2[0]])
    diff_g = diff_norm_g[0].reshape(1, 2 * DH_DIFF)
    lam_init = 0.8 - 0.6 * math.exp(-0.3 * 0)
    ln4 = [jnp.stack([ln_mix_g[l], ln_mix_b[l], ln_mlp_g[l], ln_mlp_b[l]]) for l in range(DEPTH)]
    w1, w2 = w_mlp_in.astype(BF16), w_mlp_out.astype(BF16)
    w_ao = w_att_out[0].astype(BF16)
    w_ci = w_conv_in[0].astype(BF16)
    w_co = w_conv_out[0].astype(BF16)
    conv_p = jnp.stack([conv_b[0], conv_norm_g[0], conv_norm_b[0]])
    conv_w8 = jnp.broadcast_to(conv_w[0][:, None, :], (CONV_WIDTH, SUBLANES, CONV_CH))
    wp = w_pool[0].astype(BF16)
    ps = pool_scale[0].reshape(1, POOL_CH)

    xp = x_prompt.reshape(bc * tc, d)
    outs = _att_in(xp, mods, ctx_row, att_w(True), None, TOKEN_TILE)
    kd_f, vd_f, ckv_f, kpe_f = outs[6:]
    o_d, o_m = _attention_ctx(outs[:6], lam4, diff_g, lam_init, bc, tc)
    xp = _post(xp, o_d, o_m, mods, 0, ctx_row, w_ao, w1, w2, ln4[0], TOKEN_TILE)
    u, pz = _conv_in(xp, mods, 1, ctx_row, w_ci, TOKEN_TILE)
    uo, do = _conv_pool(u, pz, bc, tc, tc, conv_w8, conv_p, wp, ps)
    xp = _post(xp, uo, do, mods, 1, ctx_row, w_co, w1, w2, ln4[1], TOKEN_TILE)

    xs = x_sample.reshape(bl * tl, d)
    outs = _att_in(xs, mods, lat_row, att_w(False), _rope_tables(tl), TOKEN_TILE)
    km_c, vmc_t = _cache_kv(cache_mla_ckv[:, 0].reshape(bl * past, KV_LORA),
                            jnp.pad(cache_mla_krope[:, 0].reshape(bl * past, QK_ROPE), _ROPE_LANE_PAD),
                            wk, wvt, bl)
    cache = (cache_diff_k[:, 0].reshape(bl * past, DIFF_QK), cache_diff_v[:, 0].reshape(bl * past, DIFF_V),
             km_c, vmc_t)
    o_d, o_m = _attention_lat(outs, cache, lam4, diff_g, lam_init, bl, tl)
    xs = _post(xs, o_d, o_m, mods, 0, lat_row, w_ao, w1, w2, ln4[0], TOKEN_TILE)
    u, pz = _conv_in(xs, mods, 1, lat_row, w_ci, TOKEN_TILE)
    uo, do = _conv_pool(u, pz, bl, tl, CONV_TILE, conv_w8, conv_p, wp, ps)
    xs = _post(xs, uo, do, mods, 1, lat_row, w_co, w1, w2, ln4[1], TOKEN_TILE)

    return (xp.reshape(bc, tc, d), xs.reshape(bl, tl, d),
            kd_f.reshape(bc, 1, tc, H_DIFF, 2 * DH_DIFF), vd_f.reshape(bc, 1, tc, H_DIFF, 2 * DH_DIFF),
            ckv_f.reshape(bc, 1, tc, KV_LORA), kpe_f.reshape(bc, 1, tc, QK_ROPE))
```

```python
import functools
import math

import jax
import jax.numpy as jnp
import numpy as np
from jax import lax
from jax.experimental import pallas as pl
from jax.experimental.pallas import tpu as pltpu

D_MODEL = 1024
DEPTH = 2
GRID_W = 64
H_DIFF = 4
DH_DIFF = 64
DIFF_QK = H_DIFF * 2 * DH_DIFF
DIFF_V = H_DIFF * 2 * DH_DIFF
H_MLA = 8
Q_LORA = 256
KV_LORA = 128
QK_NOPE = 64
QK_ROPE = 32
V_MLA = 64
ATT_IN = 2 * DIFF_QK + DIFF_V + Q_LORA + KV_LORA + QK_ROPE
CONV_CH = 512
CONV_WIDTH = 31
POOL_CH = 512
POOL_WINDOWS = (2, 4, 8, 16)
POOL_GC = POOL_CH // len(POOL_WINDOWS)
CONV_IN = 2 * CONV_CH + POOL_CH
D_FF = 4 * D_MODEL
ROPE_BASE = 10000.0
NORM_EPS = 1e-5
DEEPNORM_ALPHA = (2 * DEPTH) ** 0.25
LOG2E = math.log2(math.e)

LANES = 128
SUBLANES = 8
MLA_SLAB = LANES
MLA_WIDE = H_MLA * MLA_SLAB
_ROPE_LANE_PAD = ((0, 0), (QK_NOPE, MLA_SLAB - QK_NOPE - QK_ROPE))
VMEM_LIMIT = 56 * 1024 * 1024

TOKEN_TILE = 512
KEY_CHUNK = TOKEN_TILE
Q_TILE = 1024
CTX_SEQS = 2
MLA_CHUNKS_PER_ROUND = 2
STALE_MAX_LIMIT = 64.0
CONV_TILE = 512
CONV_HALO = 16
CONV_STRIP = 32
FF_CHUNK = 1024
POST_GROUPS = 2

F32 = jnp.float32
BF16 = jnp.bfloat16


def _dot(a, b):
    return jnp.dot(a, b, preferred_element_type=F32)


def _dot_nt(a, b):
    return lax.dot_general(a, b, (((1,), (1,)), ((), ())), preferred_element_type=F32)


def _layer_norm(x, g, b):
    mu = jnp.mean(x, axis=-1, keepdims=True)
    xc = x - mu
    var = jnp.mean(xc * xc, axis=-1, keepdims=True)
    return xc * lax.rsqrt(var + NORM_EPS) * g + b


def _rms_norm(x, g, axis=-1):
    ms = jnp.mean(x * x, axis=axis, keepdims=True)
    return x * lax.rsqrt(ms + NORM_EPS) * g


def _const_spec(shape):
    nd = len(shape)
    return pl.BlockSpec(shape, lambda *_: (0,) * nd, pipeline_mode=pl.Buffered(1))


def _cast_specs(cast, n_steps):
    if cast is None:
        return [], [], [], []
    layer, ws = cast
    in_specs = [pl.BlockSpec((1, w.shape[1] // n_steps, w.shape[2]), lambda i: (layer, i, 0)) for w in ws]
    out_specs = [pl.BlockSpec((w.shape[1] // n_steps, w.shape[2]), lambda i: (i, 0)) for w in ws]
    out_shape = [jax.ShapeDtypeStruct(w.shape[1:], BF16) for w in ws]
    return in_specs, out_specs, out_shape, list(ws)


def _ride_along_casts(rest, n_out):
    n_cast = (len(rest) - n_out) // 2
    for src, dst in zip(rest[:n_cast], rest[n_cast + n_out:]):
        dst[...] = src[0].astype(BF16)
    return rest[n_cast:n_cast + n_out]


def _params(*sem):
    return pltpu.CompilerParams(dimension_semantics=sem, vmem_limit_bytes=VMEM_LIMIT)


ADA_TILE = 1536


def _ada_kernel(cond_ref, w_ref, b_ref, out_ref):
    cond = cond_ref[...]
    act = (cond * jax.nn.sigmoid(cond)).astype(BF16)
    out_ref[0] = _dot(act, w_ref[0].astype(BF16)) + b_ref[0]


def _ada_mods(cond8, w_ada, b_ada):
    n = 6 * D_MODEL
    return pl.pallas_call(
        _ada_kernel,
        grid=(DEPTH, n // ADA_TILE),
        in_specs=[
            pl.BlockSpec((8, D_MODEL), lambda l, j: (0, 0)),
            pl.BlockSpec((1, D_MODEL, ADA_TILE), lambda l, j: (l, 0, j)),
            pl.BlockSpec((1, 1, ADA_TILE), lambda l, j: (l, 0, j)),
        ],
        out_specs=pl.BlockSpec((1, 8, ADA_TILE), lambda l, j: (l, 0, j)),
        out_shape=jax.ShapeDtypeStruct((DEPTH, 8, n), F32),
        compiler_params=_params("arbitrary", "arbitrary"),
        name="ada_mods",
    )(cond8, w_ada, b_ada.reshape(DEPTH, 1, n))


def _rope_tables(t_lat):
    pos = np.arange(t_lat)
    row = (pos // GRID_W).astype(np.float64)
    col = (pos % GRID_W).astype(np.float64)

    def tables(kinds):
        cos = np.ones((t_lat, LANES))
        sa = np.zeros((t_lat, LANES))
        sb = np.zeros((t_lat, LANES))
        for lane, kind in enumerate(kinds):
            if kind is None:
                continue
            axis, half, j, upper = kind
            ang = (row if axis == 0 else col) * ROPE_BASE ** (-float(j) / half)
            cos[:, lane] = np.cos(ang)
            if upper:
                sb[:, lane] = np.sin(ang)
            else:
                sa[:, lane] = -np.sin(ang)
        return [cos, sa, sb]

    def rot_kinds(n):
        half = n // 4
        kinds = []
        for i in range(n):
            axis, r = divmod(i, n // 2)
            kinds.append((axis, half, r % half, r >= half))
        return kinds

    diff = rot_kinds(DH_DIFF) * 2
    mla_q = [None] * QK_NOPE + rot_kinds(QK_ROPE) + [None] * (LANES - QK_NOPE - QK_ROPE)
    chan = np.stack([t.T for t in tables(diff) + tables(mla_q)]).astype(np.float32)
    tok = np.stack(tables(diff) + tables(mla_q)).astype(np.float32)
    return jnp.asarray(chan), jnp.asarray(tok)


def _rope_tok(x, cos, sa, sb, shift):
    return x * cos + pltpu.roll(x, LANES - shift, 1) * sa + pltpu.roll(x, shift, 1) * sb


def _rope_tok_wide(x, cos, sa, sb, shift):
    n = x.shape[1] // LANES
    return jnp.concatenate(
        [_rope_tok(x[:, i * LANES:(i + 1) * LANES], cos, sa, sb, shift) for i in range(n)], axis=1)


def _rope_chan(x, cos, sa, sb, shift):
    up = jnp.concatenate([x[shift:], x[:shift]], axis=0)
    down = jnp.concatenate([x[-shift:], x[:-shift]], axis=0)
    return x * cos + up * sa + down * sb


def _rope_chan_wide(x, cos, sa, sb, shift):
    n = x.shape[0] // LANES
    return jnp.concatenate(
        [_rope_chan(x[i * LANES:(i + 1) * LANES], cos, sa, sb, shift) for i in range(n)], axis=0)


_T_QD, _T_VD, _T_CQ, _T_CKV = 0, DIFF_QK, DIFF_QK + DIFF_V, DIFF_QK + DIFF_V + Q_LORA
_T_ROWS = _T_CKV + KV_LORA
_R_KD, _R_CKV, _R_KPE, _R_VD = 0, DIFF_QK, DIFF_QK + KV_LORA, DIFF_QK + KV_LORA + LANES

DIFF_QSCALE = DH_DIFF ** -0.5 * LOG2E
MLA_QSCALE = (QK_NOPE + QK_ROPE) ** -0.5 * LOG2E


def _att_in_body(x_ref, mods_ref, wt_ref, wr_ref, qg_ref, kvg_row_ref, kvg_col_ref, wqt_ref, wk_ref, wvt_ref):
    mods = mods_ref[0, 0]
    sh, sc = mods[0:1], mods[1:2]
    h = (x_ref[...] * (1.0 + sc) + sh).astype(BF16)
    pt = _dot_nt(wt_ref[...], h)
    pr = _dot(h, wr_ref[...])
    qd_t = pt[_T_QD:_T_QD + DIFF_QK] * DIFF_QSCALE
    vd_t = pt[_T_VD:_T_VD + DIFF_V]
    cqn_t = _rms_norm(pt[_T_CQ:_T_CQ + Q_LORA], qg_ref[...], axis=0)
    qm_t = _dot(wqt_ref[...], cqn_t.astype(BF16)) * MLA_QSCALE
    ckvn_t = _rms_norm(pt[_T_CKV:_T_CKV + KV_LORA], kvg_col_ref[...], axis=0)
    vm_t = _dot(wvt_ref[...], ckvn_t.astype(BF16))
    k_d = pr[:, _R_KD:_R_KD + DIFF_QK]
    ckv_n = _rms_norm(pr[:, _R_CKV:_R_CKV + KV_LORA], kvg_row_ref[...])
    kpe = pr[:, _R_KPE:_R_KPE + LANES]
    k_nope = _dot(ckv_n.astype(BF16), wk_ref[...])
    return pr, qd_t, vd_t, qm_t, vm_t, k_d, ckv_n, kpe, k_nope


def _mla_keys(k_nope, kpe_slab):
    return (k_nope + jnp.concatenate([kpe_slab] * H_MLA, axis=1)).astype(BF16)


def _att_in_ctx_kernel(x_ref, mods_ref, wt_ref, wr_ref, qg_ref, kvg_row_ref, kvg_col_ref, wqt_ref, wk_ref,
                       wvt_ref, *rest):
    (qdt_ref, kd_ref, vdt_ref, qmt_ref, km_ref, vmt_ref,
     kdf_ref, vdf_ref, ckvf_ref, kpef_ref) = _ride_along_casts(rest, 10)
    pr, qd_t, vd_t, qm_t, vm_t, k_d, ckv_n, kpe, k_nope = _att_in_body(
        x_ref, mods_ref, wt_ref, wr_ref, qg_ref, kvg_row_ref, kvg_col_ref, wqt_ref, wk_ref, wvt_ref)
    qdt_ref[...] = qd_t.astype(BF16)
    kd_ref[...] = k_d.astype(BF16)
    vdt_ref[0] = vd_t.astype(BF16)
    qmt_ref[...] = qm_t.astype(BF16)
    km_ref[...] = _mla_keys(k_nope, kpe)
    vmt_ref[0] = vm_t.astype(BF16)
    v_d = pr[:, _R_VD:_R_VD + DIFF_V]
    for h in range(H_DIFF):
        kdf_ref[:, h, :] = k_d[:, h * LANES:(h + 1) * LANES]
        vdf_ref[:, h, :] = v_d[:, h * LANES:(h + 1) * LANES]
    ckvf_ref[...] = ckv_n
    kpef_ref[...] = kpe[:, QK_NOPE:QK_NOPE + QK_ROPE]


def _att_in_lat_kernel(x_ref, mods_ref, wt_ref, wr_ref, qg_ref, kvg_row_ref, kvg_col_ref, wqt_ref, wk_ref,
                       wvt_ref, tabc_ref, tabt_ref,
                       qdut_ref, qdrt_ref, kdr_ref, vdt_ref, qmut_ref, qmrt_ref, kmr_ref, vmt_ref):
    _, qd_t, vd_t, qm_t, vm_t, k_d, ckv_n, kpe, k_nope = _att_in_body(
        x_ref, mods_ref, wt_ref, wr_ref, qg_ref, kvg_row_ref, kvg_col_ref, wqt_ref, wk_ref, wvt_ref)
    qdut_ref[...] = qd_t.astype(BF16)
    qdrt_ref[...] = _rope_chan_wide(qd_t, tabc_ref[0], tabc_ref[1], tabc_ref[2], DH_DIFF // 4).astype(BF16)
    kdr_ref[...] = _rope_tok_wide(k_d, tabt_ref[0], tabt_ref[1], tabt_ref[2], DH_DIFF // 4).astype(BF16)
    vdt_ref[0] = vd_t.astype(BF16)
    qmut_ref[...] = qm_t.astype(BF16)
    qmrt_ref[...] = _rope_chan_wide(qm_t, tabc_ref[3], tabc_ref[4], tabc_ref[5], QK_ROPE // 4).astype(BF16)
    kpe_r = _rope_tok(kpe, tabt_ref[3], tabt_ref[4], tabt_ref[5], QK_ROPE // 4)
    kmr_ref[...] = _mla_keys(k_nope, kpe_r)
    vmt_ref[0] = vm_t.astype(BF16)


def _att_in(x2d, mods, mod_row_fn, wts, tables, tm, cast=None):
    t = x2d.shape[0]
    nt = t // tm
    tok = lambda n: pl.BlockSpec((tm, n), lambda i: (i, 0))
    chan = lambda n: pl.BlockSpec((n, tm), lambda i: (0, i))
    chunk = lambda n: pl.BlockSpec((1, n, tm), lambda i: (i, 0, 0))
    in_specs = [tok(D_MODEL), pl.BlockSpec((1, 1, 6, D_MODEL), lambda i: (0, mod_row_fn(i), 0, 0))]
    in_specs += [_const_spec(w.shape) for w in wts]
    s_tok = lambda n, dt=BF16: jax.ShapeDtypeStruct((t, n), dt)
    s_chan = lambda n: jax.ShapeDtypeStruct((n, t), BF16)
    s_chunk = lambda n: jax.ShapeDtypeStruct((nt, n, tm), BF16)
    if tables is None:
        heads = pl.BlockSpec((tm, H_DIFF, 2 * DH_DIFF), lambda i: (i, 0, 0))
        s_heads = jax.ShapeDtypeStruct((t, H_DIFF, 2 * DH_DIFF), F32)
        out_shape = [s_chan(DIFF_QK), s_tok(DIFF_QK), s_chunk(DIFF_V), s_chan(MLA_WIDE), s_tok(MLA_WIDE),
                     s_chunk(H_MLA * V_MLA),
                     s_heads, s_heads, s_tok(KV_LORA, F32), s_tok(QK_ROPE, F32)]
        out_specs = [chan(DIFF_QK), tok(DIFF_QK), chunk(DIFF_V), chan(MLA_WIDE), tok(MLA_WIDE),
                     chunk(H_MLA * V_MLA), heads, heads, tok(KV_LORA), tok(QK_ROPE)]
        c_in, c_out, c_shape, c_args = _cast_specs(cast, nt)
        in_specs, out_specs, out_shape = in_specs + c_in, out_specs + c_out, out_shape + c_shape
        kern, args, name = _att_in_ctx_kernel, tuple(c_args), "att_in_ctx"
    else:
        tab_c, tab_t = tables
        t_lat = tab_t.shape[1]
        in_specs += [pl.BlockSpec((6, LANES, tm), lambda i: (0, 0, i % (t_lat // tm))),
                     pl.BlockSpec((6, tm, LANES), lambda i: (0, i % (t_lat // tm), 0))]
        out_shape = [s_chan(DIFF_QK), s_chan(DIFF_QK), s_tok(DIFF_QK), s_chunk(DIFF_V),
                     s_chan(MLA_WIDE), s_chan(MLA_WIDE), s_tok(MLA_WIDE), s_chunk(H_MLA * V_MLA)]
        out_specs = [chan(DIFF_QK), chan(DIFF_QK), tok(DIFF_QK), chunk(DIFF_V),
                     chan(MLA_WIDE), chan(MLA_WIDE), tok(MLA_WIDE), chunk(H_MLA * V_MLA)]
        kern, args, name = _att_in_lat_kernel, (tab_c, tab_t), "att_in_lat"
    return pl.pallas_call(
        kern, grid=(nt,), in_specs=in_specs, out_specs=out_specs, out_shape=out_shape,
        compiler_params=_params("arbitrary"), name=name,
    )(x2d, mods, *wts, *args)


def _cache_kv_kernel(ckv_ref, kpe_ref, wk_ref, wvt_ref, km_ref, vmt_ref):
    ckv_b = ckv_ref[...].astype(BF16)
    km_ref[...] = _mla_keys(_dot(ckv_b, wk_ref[...]), kpe_ref[...])
    past = vmt_ref.shape[2]
    vt = _dot_nt(wvt_ref[...], ckv_b)
    for b in range(vmt_ref.shape[0]):
        vmt_ref[b] = vt[:, b * past:(b + 1) * past].astype(BF16)


def _cache_kv(ckv2d, kpe_slab2d, wk, wvt, bl):
    t = ckv2d.shape[0]
    full = lambda a: pl.BlockSpec(a.shape, lambda i: (0,) * a.ndim)
    return pl.pallas_call(
        _cache_kv_kernel, grid=(1,),
        in_specs=[full(ckv2d), full(kpe_slab2d), full(wk), full(wvt)],
        out_specs=[pl.BlockSpec((t, MLA_WIDE), lambda i: (0, 0)),
                   pl.BlockSpec((bl, H_MLA * V_MLA, t // bl), lambda i: (0, 0, 0))],
        out_shape=[jax.ShapeDtypeStruct((t, MLA_WIDE), BF16),
                   jax.ShapeDtypeStruct((bl, H_MLA * V_MLA, t // bl), BF16)],
        compiler_params=_params("arbitrary"), name="cache_kv",
    )(ckv2d, kpe_slab2d, wk, wvt)


def _softmax_pv(s, vt, state, stale_max):
    m_c = jnp.max(s, axis=0, keepdims=True)
    if state is None:
        p = jnp.exp2(s - m_c)
        return m_c, jnp.sum(p, axis=0, keepdims=True), _dot(vt, p.astype(BF16)), jnp.zeros_like(m_c)
    m, l, acc, rise = state
    m_new = jnp.maximum(m, m_c)
    alpha = jnp.exp2(m - m_new)
    if stale_max:
        p = jnp.exp2(s - m)
        l = alpha * (l + jnp.sum(p, axis=0, keepdims=True))
        return m_new, l, alpha * (acc + _dot(vt, p.astype(BF16))), jnp.maximum(rise, m_c - m)
    p = jnp.exp2(s - m_new)
    l = alpha * l + jnp.sum(p, axis=0, keepdims=True)
    return m_new, l, alpha * acc + _dot(vt, p.astype(BF16)), rise


def _attend(first, rest, chunks_per_round=1, stale_max=False):
    items = [(si, qt, (lambda k=k: k), (lambda vt=vt: vt)) for si, (qt, k, vt) in enumerate(first)]
    if rest is not None:
        keys = KEY_CHUNK * chunks_per_round
        for c in range(rest[0][2].shape[0] // chunks_per_round):
            for si, (qt, k_ref, vt_ref) in enumerate(rest):
                items.append((si, qt, (lambda r=k_ref, c=c: r[c * keys:(c + 1) * keys, :]),
                              (lambda r=vt_ref, c=c: jnp.concatenate(
                                  [r[c * chunks_per_round + i] for i in range(chunks_per_round)], axis=1))))
    ns = len(first)
    groups = [items[i:i + ns] for i in range(0, len(items), ns)]
    scores = lambda grp: [_dot(k_fn(), qt) for (_, qt, k_fn, _) in grp]
    state = [None] * ns
    s_next = scores(groups[0])
    for g, grp in enumerate(groups):
        s_cur = s_next
        if g + 1 < len(groups):
            s_next = scores(groups[g + 1])
        for s, (si, _, _, vt_fn) in zip(s_cur, grp):
            state[si] = _softmax_pv(s, vt_fn(), state[si], stale_max)
    rise = functools.reduce(jnp.maximum, [st[3] for st in state])
    return [acc / l for (_, l, acc, _) in state], rise


def _attend_guarded(first, rest, chunks_per_round, finish):
    def run(stale_max):
        o, rise = _attend(first, rest, chunks_per_round, stale_max)
        finish(o)
        return rise

    rise = run(True)

    @pl.when(jnp.max(rise) > STALE_MAX_LIMIT)
    def _():
        run(False)


def _split_maps(qt):
    row = lax.broadcasted_iota(jnp.int32, qt.shape, 0)
    zero = jnp.zeros_like(qt)
    return jnp.where(row < DH_DIFF, qt, zero), jnp.where(row >= DH_DIFF, qt, zero)


def _lambda(lam_ref, lam_init):
    lv = lam_ref[...]
    a = jnp.sum(lv[0:1] * lv[1:2], axis=-1, keepdims=True)
    b = jnp.sum(lv[2:3] * lv[3:4], axis=-1, keepdims=True)
    return jnp.exp(a) - jnp.exp(b) + lam_init


def _diff_finish(o1_t, o2_t, lam_ref, g_ref, lam_init):
    o = (o1_t - _lambda(lam_ref, lam_init) * o2_t).T
    return (_rms_norm(o, g_ref[...]) * (1.0 - lam_init)).astype(BF16)


def _mla_finish(oa_t, ob_t):
    return jnp.concatenate([oa_t, ob_t], axis=0).T.astype(BF16)


def _diff_ctx_kernel(qt_ref, k_ref, vt_ref, lam_ref, g_ref, *rest, lam_init):
    out_ref = _ride_along_casts(rest, 1)[0]
    t = qt_ref.shape[1] // CTX_SEQS
    first = []
    for q in range(CTX_SEQS):
        ts = slice(q * t, (q + 1) * t)
        for h in range(H_DIFF):
            sl = slice(h * LANES, (h + 1) * LANES)
            k, vt = k_ref[ts, sl], vt_ref[0, sl, ts]
            first += [(qm, k, vt) for qm in _split_maps(qt_ref[sl, ts])]
    o, _ = _attend(first, None)
    n = 2 * H_DIFF
    for q in range(CTX_SEQS):
        out_ref[q * t:(q + 1) * t, :] = jnp.concatenate(
            [_diff_finish(o[q * n + 2 * h], o[q * n + 2 * h + 1], lam_ref, g_ref, lam_init) for h in range(H_DIFF)],
            axis=1)


def _diff_lat_kernel(qut_ref, qrt_ref, kc_ref, vc_ref, k_ref, vt_ref, lam_ref, g_ref, out_ref, *, lam_init):
    kc, vct = kc_ref[...].astype(BF16), vc_ref[...].T.astype(BF16)
    first = [(q, kc, vct) for q in _split_maps(qut_ref[...])]
    rest = [(q, k_ref, vt_ref) for q in _split_maps(qrt_ref[...])]

    def finish(o):
        out_ref[...] = _diff_finish(o[0], o[1], lam_ref, g_ref, lam_init)

    _attend_guarded(first, rest, 1, finish)


def _mla_ctx_kernel(qt_ref, k_ref, vt_ref, *rest):
    out_ref = _ride_along_casts(rest, 1)[0]
    t = qt_ref.shape[1] // CTX_SEQS
    first = []
    for q in range(CTX_SEQS):
        ts = slice(q * t, (q + 1) * t)
        for h in range(H_MLA):
            sl = slice(h * MLA_SLAB, (h + 1) * MLA_SLAB)
            first.append((qt_ref[sl, ts], k_ref[ts, sl], vt_ref[0, h * V_MLA:(h + 1) * V_MLA, ts]))
    o, _ = _attend(first, None)
    for q in range(CTX_SEQS):
        out_ref[q * t:(q + 1) * t, :] = jnp.concatenate(
            [_mla_finish(o[q * H_MLA + 2 * j], o[q * H_MLA + 2 * j + 1]) for j in range(H_MLA // 2)], axis=1)


def _mla_lat_kernel(qua_ref, qub_ref, qra_ref, qrb_ref, kca_ref, kcb_ref, vcta_ref, vctb_ref, ka_ref, kb_ref,
                    vta_ref, vtb_ref, out_ref):
    first = [(qua_ref[...], kca_ref[...], vcta_ref[0]), (qub_ref[...], kcb_ref[...], vctb_ref[0])]
    rest = [(qra_ref[...], ka_ref, vta_ref), (qrb_ref[...], kb_ref, vtb_ref)]

    def finish(o):
        out_ref[...] = _mla_finish(o[0], o[1])

    _attend_guarded(first, rest, MLA_CHUNKS_PER_ROUND, finish)


def _attention_ctx(streams, lam4, diff_g, lam_init, b, t, cast):
    qd_t, kd, vd_t, qm_t, km, vm_t = streams
    per_chunk = vd_t.shape[2] // t // CTX_SEQS
    t, b = t * CTX_SEQS, b // CTX_SEQS
    sem = _params("arbitrary")
    qt = lambda w: pl.BlockSpec((w, t), lambda i: (0, i))
    keys = lambda w: pl.BlockSpec((t, w), lambda i: (i, 0))
    vals = lambda w: pl.BlockSpec((1, w, t), lambda i: (i // per_chunk, 0, i % per_chunk))
    casts = [_cast_specs((cast[0], [w]), b) for w in cast[1]]
    o_d, w_a = pl.pallas_call(
        functools.partial(_diff_ctx_kernel, lam_init=lam_init),
        grid=(b,),
        in_specs=[qt(DIFF_QK), keys(DIFF_QK), vals(DIFF_V),
                  pl.BlockSpec(lam4.shape, lambda i: (0, 0)),
                  pl.BlockSpec(diff_g.shape, lambda i: (0, 0))] + casts[0][0],
        out_specs=[keys(DIFF_V)] + casts[0][1],
        out_shape=[jax.ShapeDtypeStruct((b * t, DIFF_V), BF16)] + casts[0][2],
        compiler_params=sem, name="diff_attn_ctx",
    )(qd_t, kd, vd_t, lam4, diff_g, *casts[0][3])
    o_m, w_b = pl.pallas_call(
        _mla_ctx_kernel,
        grid=(b,),
        in_specs=[qt(MLA_WIDE), keys(MLA_WIDE), vals(H_MLA * V_MLA)] + casts[1][0],
        out_specs=[keys(H_MLA * V_MLA)] + casts[1][1],
        out_shape=[jax.ShapeDtypeStruct((b * t, H_MLA * V_MLA), BF16)] + casts[1][2],
        compiler_params=sem, name="mla_attn_ctx",
    )(qm_t, km, vm_t, *casts[1][3])
    return o_d, o_m, w_a, w_b


def _attention_lat(streams, cache, lam4, diff_g, lam_init, b, t):
    qdu_t, qdr_t, kdr, vd_t, qmu_t, qmr_t, kmr, vm_t = streams
    kd_c, vd_c, km_c, vmc_t = cache
    past = kd_c.shape[0] // b
    nq = t // Q_TILE
    n_chunks = t // KEY_CHUNK
    sem = _params("arbitrary", "arbitrary", "arbitrary")
    qt = lambda f: pl.BlockSpec((LANES, Q_TILE), lambda i, h, j: (f(h), i * nq + j))
    keys = lambda n, f: pl.BlockSpec((n, LANES), lambda i, h, j: (i, f(h)))
    vals = pl.BlockSpec((n_chunks, LANES, KEY_CHUNK), lambda i, h, j: (i, h, 0))
    out = pl.BlockSpec((Q_TILE, LANES), lambda i, h, j: (i * nq + j, h))
    same, even, odd = (lambda h: h), (lambda h: 2 * h), (lambda h: 2 * h + 1)
    o_d = pl.pallas_call(
        functools.partial(_diff_lat_kernel, lam_init=lam_init),
        grid=(b, H_DIFF, nq),
        in_specs=[qt(same), qt(same), keys(past, same), keys(past, same), keys(t, same), vals,
                  pl.BlockSpec(lam4.shape, lambda i, h, j: (0, 0)),
                  pl.BlockSpec(diff_g.shape, lambda i, h, j: (0, 0))],
        out_specs=out,
        out_shape=jax.ShapeDtypeStruct((b * t, DIFF_V), BF16),
        compiler_params=sem, name="diff_attn_lat",
    )(qdu_t, qdr_t, kd_c, vd_c, kdr, vd_t, lam4, diff_g)
    o_m = pl.pallas_call(
        _mla_lat_kernel,
        grid=(b, H_MLA // 2, nq),
        in_specs=[qt(even), qt(odd), qt(even), qt(odd), keys(past, even), keys(past, odd),
                  pl.BlockSpec((1, V_MLA, past), lambda i, h, j: (i, 2 * h, 0)),
                  pl.BlockSpec((1, V_MLA, past), lambda i, h, j: (i, 2 * h + 1, 0)),
                  keys(t, even), keys(t, odd),
                  pl.BlockSpec((n_chunks, V_MLA, KEY_CHUNK), lambda i, h, j: (i, 2 * h, 0)),
                  pl.BlockSpec((n_chunks, V_MLA, KEY_CHUNK), lambda i, h, j: (i, 2 * h + 1, 0))],
        out_specs=out,
        out_shape=jax.ShapeDtypeStruct((b * t, H_MLA * V_MLA), BF16),
        compiler_params=sem, name="mla_attn_lat",
    )(qmu_t, qmu_t, qmr_t, qmr_t, km_c, km_c, vmc_t, vmc_t, kmr, kmr, vm_t, vm_t)
    return o_d, o_m


def _post_kernel(x_ref, ma_ref, mb_ref, mods_ref, wo_ref, w1_ref, w2_ref, ln_ref, *rest):
    out_ref = _ride_along_casts(rest, 1)[0]
    mods = mods_ref[0, 0]
    g_m, sh_f, sc_f, g_f = mods[2:3], mods[3:4], mods[4:5], mods[5:6]
    ln = ln_ref[...]
    rows = x_ref.shape[0] // POST_GROUPS
    grp = [slice(r * rows, (r + 1) * rows) for r in range(POST_GROUPS)]
    ys = [_dot(jnp.concatenate([ma_ref[g, :], mb_ref[g, :]], axis=1), wo_ref[...]) for g in grp]
    x1s = [_layer_norm(DEEPNORM_ALPHA * x_ref[g, :] + g_m * y, ln[0:1], ln[1:2]) for g, y in zip(grp, ys)]
    fs = []
    for x1 in x1s:
        h = (x1 * (1.0 + sc_f) + sh_f).astype(BF16)
        f = jnp.zeros(x1.shape, F32)
        for c in range(D_FF // FF_CHUNK):
            a = jnp.maximum(_dot(h, w1_ref[:, c * FF_CHUNK:(c + 1) * FF_CHUNK]), 0.0)
            f = f + _dot((a * a).astype(BF16), w2_ref[c * FF_CHUNK:(c + 1) * FF_CHUNK, :])
        fs.append(f)
    for g, x1, f in zip(grp, x1s, fs):
        out_ref[g, :] = _layer_norm(DEEPNORM_ALPHA * x1 + g_f * f, ln[2:3], ln[3:4])


def _post(x2d, mix_a, mix_b, mods, layer, mod_row_fn, wo, w1, w2, ln4, tm, cast=None):
    t = x2d.shape[0]
    tok = lambda n: pl.BlockSpec((tm, n), lambda i: (i, 0))
    c_in, c_out, c_shape, c_args = _cast_specs(cast, t // tm)
    return pl.pallas_call(
        _post_kernel, grid=(t // tm,),
        in_specs=[tok(D_MODEL), tok(mix_a.shape[1]), tok(mix_b.shape[1]),
                  pl.BlockSpec((1, 1, 6, D_MODEL), lambda i: (layer, mod_row_fn(i), 0, 0)),
                  _const_spec(wo.shape), _const_spec(w1.shape), _const_spec(w2.shape),
                  _const_spec(ln4.shape)] + c_in,
        out_specs=[tok(D_MODEL)] + c_out,
        out_shape=[jax.ShapeDtypeStruct((t, D_MODEL), F32)] + c_shape,
        compiler_params=_params("arbitrary"), name=f"post_mlp_l{layer}",
    )(x2d, mix_a, mix_b, mods, wo, w1, w2, ln4, *c_args)


def _conv_in_kernel(x_ref, mods_ref, w_ref, u_ref, pz_ref):
    mods = mods_ref[0, 0]
    sh, sc = mods[0:1], mods[1:2]
    h = (x_ref[...] * (1.0 + sc) + sh).astype(BF16)
    proj = _dot(h, w_ref[...])
    a, gate = proj[:, 0:CONV_CH], proj[:, CONV_CH:2 * CONV_CH]
    u_ref[...] = a * jax.nn.sigmoid(gate)
    pz_ref[...] = proj[:, 2 * CONV_CH:]


def _conv_in(x2d, mods, layer, mod_row_fn, w, tm):
    t = x2d.shape[0]
    tok = lambda n: pl.BlockSpec((tm, n), lambda i: (i, 0))
    return pl.pallas_call(
        _conv_in_kernel, grid=(t // tm,),
        in_specs=[tok(D_MODEL),
                  pl.BlockSpec((1, 1, 6, D_MODEL), lambda i: (layer, mod_row_fn(i), 0, 0)),
                  _const_spec(w.shape)],
        out_specs=[tok(CONV_CH), tok(POOL_CH)],
        out_shape=[jax.ShapeDtypeStruct((t, CONV_CH), F32), jax.ShapeDtypeStruct((t, POOL_CH), F32)],
        compiler_params=_params("arbitrary"), name="conv_in",
    )(x2d, mods, w)


def _conv_pool_kernel(*refs, n_tiles):
    if n_tiles > 1:
        (u_ref, ul_ref, ur_ref, pz_ref, pl_ref, pr_ref, cw_ref, cp_ref, wp_ref, ps_ref, inv_ref,
         uo_ref, do_ref, ubuf, pbuf, ushift, cbuf) = refs
    else:
        u_ref, pz_ref, cw_ref, cp_ref, wp_ref, ps_ref, inv_ref, uo_ref, do_ref, ubuf, pbuf, ushift, cbuf = refs
    tm = u_ref.shape[1]
    j = pl.program_id(1)
    halo_zero = jnp.zeros((CONV_HALO, CONV_CH), F32)
    for buf, mid, sides in ((ubuf, u_ref, (ul_ref, ur_ref) if n_tiles > 1 else None),
                            (pbuf, pz_ref, (pl_ref, pr_ref) if n_tiles > 1 else None)):
        buf[CONV_HALO:CONV_HALO + tm, :] = mid[0]
        if sides is None:
            buf[0:CONV_HALO, :] = halo_zero
            buf[CONV_HALO + tm:, :] = halo_zero
        else:
            buf[0:CONV_HALO, :] = jnp.where(j > 0, sides[0][0], halo_zero)
            buf[CONV_HALO + tm:, :] = jnp.where(j < n_tiles - 1, sides[1][0], halo_zero)

    rows_sh = ushift.shape[1]
    for b in range(SUBLANES):
        ushift[b] = ubuf[b:b + rows_sh, :]
    base = CONV_HALO - CONV_WIDTH // 2
    cp = cp_ref[...]

    def strip(i, carry):
        r0 = pl.multiple_of(i * CONV_STRIP, CONV_STRIP)
        acc = jnp.zeros((CONV_STRIP, CONV_CH), F32)
        for k in range(CONV_WIDTH):
            a, b = divmod(base + k, SUBLANES)
            w = jnp.concatenate([cw_ref[k]] * (CONV_STRIP // SUBLANES), axis=0)
            acc = acc + ushift[b, pl.ds(r0 + a * SUBLANES, CONV_STRIP), :] * w
        cbuf[pl.ds(r0, CONV_STRIP), :] = acc
        return carry

    lax.fori_loop(0, tm // CONV_STRIP, strip, 0)
    z = _layer_norm(cbuf[...] + cp[0:1], cp[1:2], cp[2:3])
    uo_ref[0] = (z * jax.nn.sigmoid(z)).astype(uo_ref.dtype)

    outs = []
    for g, w in enumerate(POOL_WINDOWS):
        cols = slice(g * POOL_GC, (g + 1) * POOL_GC)
        s = jnp.zeros((tm, POOL_GC), F32)
        for d in range(-(w // 2), w // 2):
            s = s + pbuf[CONV_HALO + d:CONV_HALO + d + tm, cols]
        dgrp = s * inv_ref[:, cols] - pz_ref[0][:, cols]
        outs.append(_dot(dgrp.astype(BF16), wp_ref[g]))
    do_ref[0] = (jnp.concatenate(outs, axis=1) * ps_ref[...]).astype(do_ref.dtype)


def _pool_inv_counts(t):
    pos = np.arange(t)
    cols = []
    for w in POOL_WINDOWS:
        cnt = np.minimum(pos + w // 2 - 1, t - 1) - np.maximum(pos - w // 2, 0) + 1
        cols.append(np.repeat((1.0 / cnt)[:, None], POOL_GC, axis=1))
    return jnp.asarray(np.concatenate(cols, axis=1).astype(np.float32))


def _conv_pool(u2d, pz2d, b, t, tm, cw, cp, wp, ps):
    n_tiles = t // tm
    u3, p3 = u2d.reshape(b, t, CONV_CH), pz2d.reshape(b, t, POOL_CH)
    mid = pl.BlockSpec((1, tm, CONV_CH), lambda i, j: (i, j, 0))
    r = tm // CONV_HALO
    left = pl.BlockSpec((1, CONV_HALO, CONV_CH), lambda i, j: (i, jnp.maximum(j * r - 1, 0), 0))
    right = pl.BlockSpec((1, CONV_HALO, CONV_CH),
                         lambda i, j: (i, jnp.minimum((j + 1) * r, t // CONV_HALO - 1), 0))
    const = lambda a: pl.BlockSpec(a.shape, lambda i, j: (0,) * a.ndim)
    if n_tiles > 1:
        in_specs = [mid, left, right, mid, left, right]
        args = (u3, u3, u3, p3, p3, p3)
    else:
        in_specs = [mid, mid]
        args = (u3, p3)
    inv = _pool_inv_counts(t)
    in_specs += [const(cw), const(cp), const(wp), const(ps), pl.BlockSpec((tm, POOL_CH), lambda i, j: (j, 0))]
    uo, do = pl.pallas_call(
        functools.partial(_conv_pool_kernel, n_tiles=n_tiles),
        grid=(b, n_tiles), in_specs=in_specs,
        out_specs=[mid, mid],
        out_shape=[jax.ShapeDtypeStruct((b, t, CONV_CH), BF16), jax.ShapeDtypeStruct((b, t, POOL_CH), BF16)],
        scratch_shapes=[pltpu.VMEM((tm + 2 * CONV_HALO, CONV_CH), F32),
                        pltpu.VMEM((tm + 2 * CONV_HALO, POOL_CH), F32),
                        pltpu.VMEM((SUBLANES, tm + 2 * CONV_HALO - SUBLANES, CONV_CH), F32),
                        pltpu.VMEM((tm, CONV_CH), F32)],
        compiler_params=_params("arbitrary", "arbitrary"),
        name=f"conv_pool_{'lat' if n_tiles > 1 else 'ctx'}",
    )(*args, cw, cp, wp, ps, inv)
    return uo.reshape(b * t, CONV_CH), do.reshape(b * t, POOL_CH)


def _att_in_weights(w_att_in, with_v_tok):
    o = 2 * DIFF_QK + DIFF_V
    q_d, k_d, v_d = w_att_in[:, :DIFF_QK], w_att_in[:, DIFF_QK:2 * DIFF_QK], w_att_in[:, 2 * DIFF_QK:o]
    cq, ckv = w_att_in[:, o:o + Q_LORA], w_att_in[:, o + Q_LORA:o + Q_LORA + KV_LORA]
    kpe = jnp.pad(w_att_in[:, o + Q_LORA + KV_LORA:], _ROPE_LANE_PAD)
    w_t = jnp.concatenate([q_d, v_d, cq, ckv], axis=1).T.astype(BF16)
    w_r = jnp.concatenate([k_d, ckv, kpe] + ([v_d] if with_v_tok else []), axis=1).astype(BF16)
    return w_t, w_r


def _mla_weights(w_uq, w_ukv):
    wq = w_uq.reshape(Q_LORA, H_MLA, QK_NOPE + QK_ROPE)
    wq = jnp.pad(wq, ((0, 0), (0, 0), (0, MLA_SLAB - QK_NOPE - QK_ROPE))).reshape(Q_LORA, MLA_WIDE)
    wkv = w_ukv.reshape(KV_LORA, H_MLA, QK_NOPE + V_MLA)
    wk = jnp.pad(wkv[:, :, :QK_NOPE], ((0, 0), (0, 0), (0, MLA_SLAB - QK_NOPE))).reshape(KV_LORA, MLA_WIDE)
    wv = wkv[:, :, QK_NOPE:].reshape(KV_LORA, H_MLA * V_MLA)
    return wq.T.astype(BF16), wk.astype(BF16), wv.T.astype(BF16)


def kernel(x_prompt, x_sample, cache_diff_k, cache_diff_v, cache_mla_ckv, cache_mla_krope, c, c_ctx,
           w_ada, b_ada, ln_mix_g, ln_mix_b, ln_mlp_g, ln_mlp_b, w_mlp_in, w_mlp_out,
           w_att_in, w_uq, w_ukv, q_norm_g, kv_norm_g, lam_q1, lam_k1, lam_q2, lam_k2, diff_norm_g, w_att_out,
           w_conv_in, conv_w, conv_b, conv_norm_g, conv_norm_b, w_pool, pool_scale, w_conv_out):
    bc, tc, d = x_prompt.shape
    bl, tl, _ = x_sample.shape
    past = cache_diff_k.shape[2]

    cond8 = jnp.zeros((8, d), F32).at[0].set(c_ctx).at[1:1 + bl].set(c)
    mods = _ada_mods(cond8, w_ada, b_ada).reshape(DEPTH, 8, 6, d)
    ctx_row = lambda i: 0
    lat_row = lambda i: 1 + i // (tl // TOKEN_TILE)

    wqt, wk, wvt = _mla_weights(w_uq[0], w_ukv[0])
    norm_w = (q_norm_g[0].reshape(Q_LORA, 1), kv_norm_g[0].reshape(1, KV_LORA), kv_norm_g[0].reshape(KV_LORA, 1))
    att_w = lambda with_v: _att_in_weights(w_att_in[0], with_v) + norm_w + (wqt, wk, wvt)
    lam4 = jnp.stack([lam_q1[0], lam_k1[0], lam_q2[0], lam_k2[0]])
    diff_g = diff_norm_g[0].reshape(1, 2 * DH_DIFF)
    lam_init = 0.8 - 0.6 * math.exp(-0.3 * 0)
    ln4 = [jnp.stack([ln_mix_g[l], ln_mix_b[l], ln_mlp_g[l], ln_mlp_b[l]]) for l in range(DEPTH)]
    mlp_w = [w_mlp_in, w_mlp_out]
    w_ao = w_att_out[0].astype(BF16)
    w_ci = w_conv_in[0].astype(BF16)
    w_co = w_conv_out[0].astype(BF16)
    conv_p = jnp.stack([conv_b[0], conv_norm_g[0], conv_norm_b[0]])
    conv_w8 = jnp.broadcast_to(conv_w[0][:, None, :], (CONV_WIDTH, SUBLANES, CONV_CH))
    wp = w_pool[0].astype(BF16)
    ps = pool_scale[0].reshape(1, POOL_CH)

    xp = x_prompt.reshape(bc * tc, d)
    outs = _att_in(xp, mods, ctx_row, att_w(True), None, TOKEN_TILE, cast=(0, mlp_w))
    kd_f, vd_f, ckv_f, kpe_f = outs[6:10]
    w1_0, w2_0 = outs[10:]
    o_d, o_m, w1_1, w2_1 = _attention_ctx(outs[:6], lam4, diff_g, lam_init, bc, tc, cast=(1, mlp_w))
    xp, = _post(xp, o_d, o_m, mods, 0, ctx_row, w_ao, w1_0, w2_0, ln4[0], TOKEN_TILE)
    u, pz = _conv_in(xp, mods, 1, ctx_row, w_ci, TOKEN_TILE)
    uo, do = _conv_pool(u, pz, bc, tc, tc, conv_w8, conv_p, wp, ps)
    xp, = _post(xp, uo, do, mods, 1, ctx_row, w_co, w1_1, w2_1, ln4[1], TOKEN_TILE)

    xs = x_sample.reshape(bl * tl, d)
    outs = _att_in(xs, mods, lat_row, att_w(False), _rope_tables(tl), TOKEN_TILE)
    km_c, vmc_t = _cache_kv(cache_mla_ckv[:, 0].reshape(bl * past, KV_LORA),
                            jnp.pad(cache_mla_krope[:, 0].reshape(bl * past, QK_ROPE), _ROPE_LANE_PAD),
                            wk, wvt, bl)
    cache = (cache_diff_k[:, 0].reshape(bl * past, DIFF_QK), cache_diff_v[:, 0].reshape(bl * past, DIFF_V),
             km_c, vmc_t)
    o_d, o_m = _attention_lat(outs, cache, lam4, diff_g, lam_init, bl, tl)
    xs, = _post(xs, o_d, o_m, mods, 0, lat_row, w_ao, w1_0, w2_0, ln4[0], TOKEN_TILE)
    u, pz = _conv_in(xs, mods, 1, lat_row, w_ci, TOKEN_TILE)
    uo, do = _conv_pool(u, pz, bl, tl, CONV_TILE, conv_w8, conv_p, wp, ps)
    xs, = _post(xs, uo, do, mods, 1, lat_row, w_co, w1_1, w2_1, ln4[1], TOKEN_TILE)

    return (xp.reshape(bc, tc, d), xs.reshape(bl, tl, d),
            kd_f.reshape(bc, 1, tc, H_DIFF, 2 * DH_DIFF), vd_f.reshape(bc, 1, tc, H_DIFF, 2 * DH_DIFF),
            ckv_f.reshape(bc, 1, tc, KV_LORA), kpe_f.reshape(bc, 1, tc, QK_ROPE))
```

```python
import functools
import math

import jax
import jax.numpy as jnp
import numpy as np
from jax import lax
from jax.experimental import pallas as pl
from jax.experimental.pallas import tpu as pltpu

D_MODEL = 1024
DEPTH = 2
GRID_W = 64
H_DIFF = 4
DH_DIFF = 64
DIFF_QK = H_DIFF * 2 * DH_DIFF
DIFF_V = H_DIFF * 2 * DH_DIFF
H_MLA = 8
Q_LORA = 256
KV_LORA = 128
QK_NOPE = 64
QK_ROPE = 32
V_MLA = 64
ATT_IN = 2 * DIFF_QK + DIFF_V + Q_LORA + KV_LORA + QK_ROPE
CONV_CH = 512
CONV_WIDTH = 31
POOL_CH = 512
POOL_WINDOWS = (2, 4, 8, 16)
POOL_GC = POOL_CH // len(POOL_WINDOWS)
CONV_IN = 2 * CONV_CH + POOL_CH
D_FF = 4 * D_MODEL
ROPE_BASE = 10000.0
NORM_EPS = 1e-5
DEEPNORM_ALPHA = (2 * DEPTH) ** 0.25
LOG2E = math.log2(math.e)

LANES = 128
SUBLANES = 8
MLA_SLAB = LANES
MLA_WIDE = H_MLA * MLA_SLAB
_ROPE_LANE_PAD = ((0, 0), (QK_NOPE, MLA_SLAB - QK_NOPE - QK_ROPE))
VMEM_LIMIT = 56 * 1024 * 1024

TOKEN_TILE = 512
KEY_CHUNK = TOKEN_TILE
Q_TILE = 1024
CTX_SEQS = 2
DIFF_CHUNKS_PER_ROUND = 2
MLA_CHUNKS_PER_ROUND = 2
STALE_MAX_LIMIT = 64.0
CONV_TILE = 512
CONV_HALO = 16
CONV_STRIP = 32
FF_CHUNK = 1024
POST_GROUPS = 2

F32 = jnp.float32
BF16 = jnp.bfloat16


def _dot(a, b):
    return jnp.dot(a, b, preferred_element_type=F32)


def _dot_nt(a, b):
    return lax.dot_general(a, b, (((1,), (1,)), ((), ())), preferred_element_type=F32)


def _layer_norm(x, g, b):
    mu = jnp.mean(x, axis=-1, keepdims=True)
    xc = x - mu
    var = jnp.mean(xc * xc, axis=-1, keepdims=True)
    return xc * lax.rsqrt(var + NORM_EPS) * g + b


def _rms_norm(x, g, axis=-1):
    ms = jnp.mean(x * x, axis=axis, keepdims=True)
    return x * lax.rsqrt(ms + NORM_EPS) * g


def _const_spec(shape):
    nd = len(shape)
    return pl.BlockSpec(shape, lambda *_: (0,) * nd, pipeline_mode=pl.Buffered(1))


def _cast_specs(cast, n_steps):
    if cast is None:
        return [], [], [], []
    layer, ws = cast
    in_specs = [pl.BlockSpec((1, w.shape[1] // n_steps, w.shape[2]), lambda i: (layer, i, 0)) for w in ws]
    out_specs = [pl.BlockSpec((w.shape[1] // n_steps, w.shape[2]), lambda i: (i, 0)) for w in ws]
    out_shape = [jax.ShapeDtypeStruct(w.shape[1:], BF16) for w in ws]
    return in_specs, out_specs, out_shape, list(ws)


def _ride_along_casts(rest, n_out):
    n_cast = (len(rest) - n_out) // 2
    for src, dst in zip(rest[:n_cast], rest[n_cast + n_out:]):
        dst[...] = src[0].astype(BF16)
    return rest[n_cast:n_cast + n_out]


def _params(*sem):
    return pltpu.CompilerParams(dimension_semantics=sem, vmem_limit_bytes=VMEM_LIMIT)


ADA_TILE = 3072


def _ada_kernel(cond_ref, w_ref, b_ref, out_ref):
    cond = cond_ref[...]
    act = (cond * jax.nn.sigmoid(cond)).astype(BF16)
    out_ref[0] = _dot(act, w_ref[0].astype(BF16)) + b_ref[0]


def _ada_mods(cond8, w_ada, b_ada):
    n = 6 * D_MODEL
    return pl.pallas_call(
        _ada_kernel,
        grid=(DEPTH, n // ADA_TILE),
        in_specs=[
            pl.BlockSpec((8, D_MODEL), lambda l, j: (0, 0)),
            pl.BlockSpec((1, D_MODEL, ADA_TILE), lambda l, j: (l, 0, j)),
            pl.BlockSpec((1, 1, ADA_TILE), lambda l, j: (l, 0, j)),
        ],
        out_specs=pl.BlockSpec((1, 8, ADA_TILE), lambda l, j: (l, 0, j)),
        out_shape=jax.ShapeDtypeStruct((DEPTH, 8, n), F32),
        compiler_params=_params("arbitrary", "arbitrary"),
        name="ada_mods",
    )(cond8, w_ada, b_ada.reshape(DEPTH, 1, n))


def _rope_tables(t_lat):
    pos = np.arange(t_lat)
    row = (pos // GRID_W).astype(np.float64)
    col = (pos % GRID_W).astype(np.float64)

    def tables(kinds):
        cos = np.ones((t_lat, LANES))
        sa = np.zeros((t_lat, LANES))
        sb = np.zeros((t_lat, LANES))
        for lane, kind in enumerate(kinds):
            if kind is None:
                continue
            axis, half, j, upper = kind
            ang = (row if axis == 0 else col) * ROPE_BASE ** (-float(j) / half)
            cos[:, lane] = np.cos(ang)
            if upper:
                sb[:, lane] = np.sin(ang)
            else:
                sa[:, lane] = -np.sin(ang)
        return [cos, sa, sb]

    def rot_kinds(n):
        half = n // 4
        kinds = []
        for i in range(n):
            axis, r = divmod(i, n // 2)
            kinds.append((axis, half, r % half, r >= half))
        return kinds

    diff = rot_kinds(DH_DIFF) * 2
    mla_q = [None] * QK_NOPE + rot_kinds(QK_ROPE) + [None] * (LANES - QK_NOPE - QK_ROPE)
    chan = np.stack([t.T for t in tables(diff) + tables(mla_q)]).astype(np.float32)
    tok = np.stack(tables(diff) + tables(mla_q)).astype(np.float32)
    return jnp.asarray(chan), jnp.asarray(tok)


def _rope_tok(x, cos, sa, sb, shift):
    return x * cos + pltpu.roll(x, LANES - shift, 1) * sa + pltpu.roll(x, shift, 1) * sb


def _rope_tok_wide(x, cos, sa, sb, shift):
    n = x.shape[1] // LANES
    return jnp.concatenate(
        [_rope_tok(x[:, i * LANES:(i + 1) * LANES], cos, sa, sb, shift) for i in range(n)], axis=1)


def _rope_chan(x, cos, sa, sb, shift):
    up = jnp.concatenate([x[shift:], x[:shift]], axis=0)
    down = jnp.concatenate([x[-shift:], x[:-shift]], axis=0)
    return x * cos + up * sa + down * sb


def _rope_chan_wide(x, cos, sa, sb, shift):
    n = x.shape[0] // LANES
    return jnp.concatenate(
        [_rope_chan(x[i * LANES:(i + 1) * LANES], cos, sa, sb, shift) for i in range(n)], axis=0)


_T_QD, _T_VD, _T_CQ, _T_CKV = 0, DIFF_QK, DIFF_QK + DIFF_V, DIFF_QK + DIFF_V + Q_LORA
_T_ROWS = _T_CKV + KV_LORA
_R_KD, _R_CKV, _R_KPE, _R_VD = 0, DIFF_QK, DIFF_QK + KV_LORA, DIFF_QK + KV_LORA + LANES

DIFF_QSCALE = DH_DIFF ** -0.5 * LOG2E
MLA_QSCALE = (QK_NOPE + QK_ROPE) ** -0.5 * LOG2E


def _att_in_body(x_ref, mods_ref, wt_ref, wr_ref, qg_ref, kvg_row_ref, kvg_col_ref, wqt_ref, wk_ref, wvt_ref):
    mods = mods_ref[0, 0]
    sh, sc = mods[0:1], mods[1:2]
    h = (x_ref[...] * (1.0 + sc) + sh).astype(BF16)
    pt = _dot_nt(wt_ref[...], h)
    pr = _dot(h, wr_ref[...])
    qd_t = pt[_T_QD:_T_QD + DIFF_QK] * DIFF_QSCALE
    vd_t = pt[_T_VD:_T_VD + DIFF_V]
    cqn_t = _rms_norm(pt[_T_CQ:_T_CQ + Q_LORA], qg_ref[...], axis=0)
    qm_t = _dot(wqt_ref[...], cqn_t.astype(BF16)) * MLA_QSCALE
    ckvn_t = _rms_norm(pt[_T_CKV:_T_CKV + KV_LORA], kvg_col_ref[...], axis=0)
    vm_t = _dot(wvt_ref[...], ckvn_t.astype(BF16))
    k_d = pr[:, _R_KD:_R_KD + DIFF_QK]
    ckv_n = _rms_norm(pr[:, _R_CKV:_R_CKV + KV_LORA], kvg_row_ref[...])
    kpe = pr[:, _R_KPE:_R_KPE + LANES]
    k_nope = _dot(ckv_n.astype(BF16), wk_ref[...])
    return pr, qd_t, vd_t, qm_t, vm_t, k_d, ckv_n, kpe, k_nope


def _mla_keys(k_nope, kpe_slab):
    return (k_nope + jnp.concatenate([kpe_slab] * H_MLA, axis=1)).astype(BF16)


def _att_in_ctx_kernel(x_ref, mods_ref, wt_ref, wr_ref, qg_ref, kvg_row_ref, kvg_col_ref, wqt_ref, wk_ref,
                       wvt_ref, *rest):
    (qdt_ref, kd_ref, vdt_ref, qmt_ref, km_ref, vmt_ref,
     kdf_ref, vdf_ref, ckvf_ref, kpef_ref) = _ride_along_casts(rest, 10)
    pr, qd_t, vd_t, qm_t, vm_t, k_d, ckv_n, kpe, k_nope = _att_in_body(
        x_ref, mods_ref, wt_ref, wr_ref, qg_ref, kvg_row_ref, kvg_col_ref, wqt_ref, wk_ref, wvt_ref)
    qdt_ref[...] = qd_t.astype(BF16)
    kd_ref[...] = k_d.astype(BF16)
    vdt_ref[0] = vd_t.astype(BF16)
    qmt_ref[...] = qm_t.astype(BF16)
    km_ref[...] = _mla_keys(k_nope, kpe)
    vmt_ref[0] = vm_t.astype(BF16)
    v_d = pr[:, _R_VD:_R_VD + DIFF_V]
    for h in range(H_DIFF):
        kdf_ref[:, h, :] = k_d[:, h * LANES:(h + 1) * LANES]
        vdf_ref[:, h, :] = v_d[:, h * LANES:(h + 1) * LANES]
    ckvf_ref[...] = ckv_n
    kpef_ref[...] = kpe[:, QK_NOPE:QK_NOPE + QK_ROPE]


def _att_in_lat_kernel(x_ref, mods_ref, wt_ref, wr_ref, qg_ref, kvg_row_ref, kvg_col_ref, wqt_ref, wk_ref,
                       wvt_ref, tabc_ref, tabt_ref,
                       qdut_ref, qdrt_ref, kdr_ref, vdt_ref, qmut_ref, qmrt_ref, kmr_ref, vmt_ref):
    _, qd_t, vd_t, qm_t, vm_t, k_d, ckv_n, kpe, k_nope = _att_in_body(
        x_ref, mods_ref, wt_ref, wr_ref, qg_ref, kvg_row_ref, kvg_col_ref, wqt_ref, wk_ref, wvt_ref)
    qdut_ref[...] = qd_t.astype(BF16)
    qdrt_ref[...] = _rope_chan_wide(qd_t, tabc_ref[0], tabc_ref[1], tabc_ref[2], DH_DIFF // 4).astype(BF16)
    kdr_ref[...] = _rope_tok_wide(k_d, tabt_ref[0], tabt_ref[1], tabt_ref[2], DH_DIFF // 4).astype(BF16)
    vdt_ref[0] = vd_t.astype(BF16)
    qmut_ref[...] = qm_t.astype(BF16)
    qmrt_ref[...] = _rope_chan_wide(qm_t, tabc_ref[3], tabc_ref[4], tabc_ref[5], QK_ROPE // 4).astype(BF16)
    kpe_r = _rope_tok(kpe, tabt_ref[3], tabt_ref[4], tabt_ref[5], QK_ROPE // 4)
    kmr_ref[...] = _mla_keys(k_nope, kpe_r)
    vmt_ref[0] = vm_t.astype(BF16)


def _att_in(x2d, mods, mod_row_fn, wts, tables, tm, cast=None):
    t = x2d.shape[0]
    nt = t // tm
    tok = lambda n: pl.BlockSpec((tm, n), lambda i: (i, 0))
    chan = lambda n: pl.BlockSpec((n, tm), lambda i: (0, i))
    chunk = lambda n: pl.BlockSpec((1, n, tm), lambda i: (i, 0, 0))
    in_specs = [tok(D_MODEL), pl.BlockSpec((1, 1, 6, D_MODEL), lambda i: (0, mod_row_fn(i), 0, 0))]
    in_specs += [_const_spec(w.shape) for w in wts]
    s_tok = lambda n, dt=BF16: jax.ShapeDtypeStruct((t, n), dt)
    s_chan = lambda n: jax.ShapeDtypeStruct((n, t), BF16)
    s_chunk = lambda n: jax.ShapeDtypeStruct((nt, n, tm), BF16)
    if tables is None:
        heads = pl.BlockSpec((tm, H_DIFF, 2 * DH_DIFF), lambda i: (i, 0, 0))
        s_heads = jax.ShapeDtypeStruct((t, H_DIFF, 2 * DH_DIFF), F32)
        out_shape = [s_chan(DIFF_QK), s_tok(DIFF_QK), s_chunk(DIFF_V), s_chan(MLA_WIDE), s_tok(MLA_WIDE),
                     s_chunk(H_MLA * V_MLA),
                     s_heads, s_heads, s_tok(KV_LORA, F32), s_tok(QK_ROPE, F32)]
        out_specs = [chan(DIFF_QK), tok(DIFF_QK), chunk(DIFF_V), chan(MLA_WIDE), tok(MLA_WIDE),
                     chunk(H_MLA * V_MLA), heads, heads, tok(KV_LORA), tok(QK_ROPE)]
        c_in, c_out, c_shape, c_args = _cast_specs(cast, nt)
        in_specs, out_specs, out_shape = in_specs + c_in, out_specs + c_out, out_shape + c_shape
        kern, args, name = _att_in_ctx_kernel, tuple(c_args), "att_in_ctx"
    else:
        tab_c, tab_t = tables
        t_lat = tab_t.shape[1]
        in_specs += [pl.BlockSpec((6, LANES, tm), lambda i: (0, 0, i % (t_lat // tm))),
                     pl.BlockSpec((6, tm, LANES), lambda i: (0, i % (t_lat // tm), 0))]
        out_shape = [s_chan(DIFF_QK), s_chan(DIFF_QK), s_tok(DIFF_QK), s_chunk(DIFF_V),
                     s_chan(MLA_WIDE), s_chan(MLA_WIDE), s_tok(MLA_WIDE), s_chunk(H_MLA * V_MLA)]
        out_specs = [chan(DIFF_QK), chan(DIFF_QK), tok(DIFF_QK), chunk(DIFF_V),
                     chan(MLA_WIDE), chan(MLA_WIDE), tok(MLA_WIDE), chunk(H_MLA * V_MLA)]
        kern, args, name = _att_in_lat_kernel, (tab_c, tab_t), "att_in_lat"
    return pl.pallas_call(
        kern, grid=(nt,), in_specs=in_specs, out_specs=out_specs, out_shape=out_shape,
        compiler_params=_params("arbitrary"), name=name,
    )(x2d, mods, *wts, *args)


def _cache_kv_kernel(ckv_ref, kpe_ref, wk_ref, wvt_ref, km_ref, vmt_ref):
    ckv_b = ckv_ref[...].astype(BF16)
    km_ref[...] = _mla_keys(_dot(ckv_b, wk_ref[...]), kpe_ref[...])
    past = vmt_ref.shape[2]
    vt = _dot_nt(wvt_ref[...], ckv_b)
    for b in range(vmt_ref.shape[0]):
        vmt_ref[b] = vt[:, b * past:(b + 1) * past].astype(BF16)


def _cache_kv(ckv2d, kpe_slab2d, wk, wvt, bl):
    t = ckv2d.shape[0]
    full = lambda a: pl.BlockSpec(a.shape, lambda i: (0,) * a.ndim)
    return pl.pallas_call(
        _cache_kv_kernel, grid=(1,),
        in_specs=[full(ckv2d), full(kpe_slab2d), full(wk), full(wvt)],
        out_specs=[pl.BlockSpec((t, MLA_WIDE), lambda i: (0, 0)),
                   pl.BlockSpec((bl, H_MLA * V_MLA, t // bl), lambda i: (0, 0, 0))],
        out_shape=[jax.ShapeDtypeStruct((t, MLA_WIDE), BF16),
                   jax.ShapeDtypeStruct((bl, H_MLA * V_MLA, t // bl), BF16)],
        compiler_params=_params("arbitrary"), name="cache_kv",
    )(ckv2d, kpe_slab2d, wk, wvt)


def _softmax_pv(s, vt, state, stale_max):
    m_c = jnp.max(s, axis=0, keepdims=True)
    if state is None:
        p = jnp.exp2(s - m_c)
        return m_c, jnp.sum(p, axis=0, keepdims=True), _dot(vt, p.astype(BF16)), jnp.zeros_like(m_c)
    m, l, acc, rise = state
    m_new = jnp.maximum(m, m_c)
    alpha = jnp.exp2(m - m_new)
    if stale_max:
        p = jnp.exp2(s - m)
        l = alpha * (l + jnp.sum(p, axis=0, keepdims=True))
        return m_new, l, alpha * (acc + _dot(vt, p.astype(BF16))), jnp.maximum(rise, m_c - m)
    p = jnp.exp2(s - m_new)
    l = alpha * l + jnp.sum(p, axis=0, keepdims=True)
    return m_new, l, alpha * acc + _dot(vt, p.astype(BF16)), rise


def _attend(first, rest, chunks_per_round=1, stale_max=False):
    items = [(si, qt, (lambda k=k: k), (lambda vt=vt: vt)) for si, (qt, k, vt) in enumerate(first)]
    if rest is not None:
        keys = KEY_CHUNK * chunks_per_round
        for c in range(rest[0][2].shape[0] // chunks_per_round):
            for si, (qt, k_ref, vt_ref) in enumerate(rest):
                items.append((si, qt, (lambda r=k_ref, c=c: r[c * keys:(c + 1) * keys, :]),
                              (lambda r=vt_ref, c=c: jnp.concatenate(
                                  [r[c * chunks_per_round + i] for i in range(chunks_per_round)], axis=1))))
    ns = len(first)
    groups = [items[i:i + ns] for i in range(0, len(items), ns)]
    scores = lambda grp: [_dot(k_fn(), qt) for (_, qt, k_fn, _) in grp]
    state = [None] * ns
    s_next = scores(groups[0])
    for g, grp in enumerate(groups):
        s_cur = s_next
        if g + 1 < len(groups):
            s_next = scores(groups[g + 1])
        for s, (si, _, _, vt_fn) in zip(s_cur, grp):
            state[si] = _softmax_pv(s, vt_fn(), state[si], stale_max)
    rise = functools.reduce(jnp.maximum, [st[3] for st in state])
    return [acc / l for (_, l, acc, _) in state], rise


def _attend_guarded(first, rest, chunks_per_round, finish):
    def run(stale_max):
        o, rise = _attend(first, rest, chunks_per_round, stale_max)
        finish(o)
        return rise

    rise = run(True)

    @pl.when(jnp.max(rise) > STALE_MAX_LIMIT)
    def _():
        run(False)


def _split_maps(qt):
    row = lax.broadcasted_iota(jnp.int32, qt.shape, 0)
    zero = jnp.zeros_like(qt)
    return jnp.where(row < DH_DIFF, qt, zero), jnp.where(row >= DH_DIFF, qt, zero)


def _lambda(lam_ref, lam_init):
    lv = lam_ref[...]
    a = jnp.sum(lv[0:1] * lv[1:2], axis=-1, keepdims=True)
    b = jnp.sum(lv[2:3] * lv[3:4], axis=-1, keepdims=True)
    return jnp.exp(a) - jnp.exp(b) + lam_init


def _diff_finish(o1_t, o2_t, lam_ref, g_ref, lam_init):
    o = (o1_t - _lambda(lam_ref, lam_init) * o2_t).T
    return (_rms_norm(o, g_ref[...]) * (1.0 - lam_init)).astype(BF16)


def _mla_finish(oa_t, ob_t):
    return jnp.concatenate([oa_t, ob_t], axis=0).T.astype(BF16)


def _diff_ctx_kernel(qt_ref, k_ref, vt_ref, lam_ref, g_ref, *rest, lam_init):
    out_ref = _ride_along_casts(rest, 1)[0]
    t = qt_ref.shape[1] // CTX_SEQS
    first = []
    for q in range(CTX_SEQS):
        ts = slice(q * t, (q + 1) * t)
        for h in range(H_DIFF):
            sl = slice(h * LANES, (h + 1) * LANES)
            k, vt = k_ref[ts, sl], vt_ref[0, sl, ts]
            first += [(qm, k, vt) for qm in _split_maps(qt_ref[sl, ts])]
    o, _ = _attend(first, None)
    n = 2 * H_DIFF
    for q in range(CTX_SEQS):
        out_ref[q * t:(q + 1) * t, :] = jnp.concatenate(
            [_diff_finish(o[q * n + 2 * h], o[q * n + 2 * h + 1], lam_ref, g_ref, lam_init) for h in range(H_DIFF)],
            axis=1)


def _diff_lat_kernel(qut_ref, qrt_ref, kc_ref, vc_ref, k_ref, vt_ref, lam_ref, g_ref, out_ref, *, lam_init):
    kc, vct = kc_ref[...].astype(BF16), vc_ref[...].T.astype(BF16)
    first = [(q, kc, vct) for q in _split_maps(qut_ref[...])]
    rest = [(q, k_ref, vt_ref) for q in _split_maps(qrt_ref[...])]

    def finish(o):
        out_ref[...] = _diff_finish(o[0], o[1], lam_ref, g_ref, lam_init)

    _attend_guarded(first, rest, DIFF_CHUNKS_PER_ROUND, finish)


def _mla_ctx_kernel(qt_ref, k_ref, vt_ref, *rest):
    out_ref = _ride_along_casts(rest, 1)[0]
    t = qt_ref.shape[1] // CTX_SEQS
    first = []
    for q in range(CTX_SEQS):
        ts = slice(q * t, (q + 1) * t)
        for h in range(H_MLA):
            sl = slice(h * MLA_SLAB, (h + 1) * MLA_SLAB)
            first.append((qt_ref[sl, ts], k_ref[ts, sl], vt_ref[0, h * V_MLA:(h + 1) * V_MLA, ts]))
    o, _ = _attend(first, None)
    for q in range(CTX_SEQS):
        out_ref[q * t:(q + 1) * t, :] = jnp.concatenate(
            [_mla_finish(o[q * H_MLA + 2 * j], o[q * H_MLA + 2 * j + 1]) for j in range(H_MLA // 2)], axis=1)


def _mla_lat_kernel(qua_ref, qub_ref, qra_ref, qrb_ref, kca_ref, kcb_ref, vcta_ref, vctb_ref, ka_ref, kb_ref,
                    vta_ref, vtb_ref, out_ref):
    first = [(qua_ref[...], kca_ref[...], vcta_ref[0]), (qub_ref[...], kcb_ref[...], vctb_ref[0])]
    rest = [(qra_ref[...], ka_ref, vta_ref), (qrb_ref[...], kb_ref, vtb_ref)]

    def finish(o):
        out_ref[...] = _mla_finish(o[0], o[1])

    _attend_guarded(first, rest, MLA_CHUNKS_PER_ROUND, finish)


def _attention_ctx(streams, lam4, diff_g, lam_init, b, t, cast):
    qd_t, kd, vd_t, qm_t, km, vm_t = streams
    per_chunk = vd_t.shape[2] // t // CTX_SEQS
    t, b = t * CTX_SEQS, b // CTX_SEQS
    sem = _params("arbitrary")
    qt = lambda w: pl.BlockSpec((w, t), lambda i: (0, i))
    keys = lambda w: pl.BlockSpec((t, w), lambda i: (i, 0))
    vals = lambda w: pl.BlockSpec((1, w, t), lambda i: (i // per_chunk, 0, i % per_chunk))
    casts = [_cast_specs((cast[0], [w]), b) for w in cast[1]]
    o_d, w_a = pl.pallas_call(
        functools.partial(_diff_ctx_kernel, lam_init=lam_init),
        grid=(b,),
        in_specs=[qt(DIFF_QK), keys(DIFF_QK), vals(DIFF_V),
                  pl.BlockSpec(lam4.shape, lambda i: (0, 0)),
                  pl.BlockSpec(diff_g.shape, lambda i: (0, 0))] + casts[0][0],
        out_specs=[keys(DIFF_V)] + casts[0][1],
        out_shape=[jax.ShapeDtypeStruct((b * t, DIFF_V), BF16)] + casts[0][2],
        compiler_params=sem, name="diff_attn_ctx",
    )(qd_t, kd, vd_t, lam4, diff_g, *casts[0][3])
    o_m, w_b = pl.pallas_call(
        _mla_ctx_kernel,
        grid=(b,),
        in_specs=[qt(MLA_WIDE), keys(MLA_WIDE), vals(H_MLA * V_MLA)] + casts[1][0],
        out_specs=[keys(H_MLA * V_MLA)] + casts[1][1],
        out_shape=[jax.ShapeDtypeStruct((b * t, H_MLA * V_MLA), BF16)] + casts[1][2],
        compiler_params=sem, name="mla_attn_ctx",
    )(qm_t, km, vm_t, *casts[1][3])
    return o_d, o_m, w_a, w_b


def _attention_lat(streams, cache, lam4, diff_g, lam_init, b, t):
    qdu_t, qdr_t, kdr, vd_t, qmu_t, qmr_t, kmr, vm_t = streams
    kd_c, vd_c, km_c, vmc_t = cache
    past = kd_c.shape[0] // b
    nq = t // Q_TILE
    n_chunks = t // KEY_CHUNK
    sem = _params("arbitrary", "arbitrary", "arbitrary")
    qt = lambda f: pl.BlockSpec((LANES, Q_TILE), lambda i, h, j: (f(h), i * nq + j))
    keys = lambda n, f: pl.BlockSpec((n, LANES), lambda i, h, j: (i, f(h)))
    vals = pl.BlockSpec((n_chunks, LANES, KEY_CHUNK), lambda i, h, j: (i, h, 0))
    out = pl.BlockSpec((Q_TILE, LANES), lambda i, h, j: (i * nq + j, h))
    same, even, odd = (lambda h: h), (lambda h: 2 * h), (lambda h: 2 * h + 1)
    o_d = pl.pallas_call(
        functools.partial(_diff_lat_kernel, lam_init=lam_init),
        grid=(b, H_DIFF, nq),
        in_specs=[qt(same), qt(same), keys(past, same), keys(past, same), keys(t, same), vals,
                  pl.BlockSpec(lam4.shape, lambda i, h, j: (0, 0)),
                  pl.BlockSpec(diff_g.shape, lambda i, h, j: (0, 0))],
        out_specs=out,
        out_shape=jax.ShapeDtypeStruct((b * t, DIFF_V), BF16),
        compiler_params=sem, name="diff_attn_lat",
    )(qdu_t, qdr_t, kd_c, vd_c, kdr, vd_t, lam4, diff_g)
    o_m = pl.pallas_call(
        _mla_lat_kernel,
        grid=(b, H_MLA // 2, nq),
        in_specs=[qt(even), qt(odd), qt(even), qt(odd), keys(past, even), keys(past, odd),
                  pl.BlockSpec((1, V_MLA, past), lambda i, h, j: (i, 2 * h, 0)),
                  pl.BlockSpec((1, V_MLA, past), lambda i, h, j: (i, 2 * h + 1, 0)),
                  keys(t, even), keys(t, odd),
                  pl.BlockSpec((n_chunks, V_MLA, KEY_CHUNK), lambda i, h, j: (i, 2 * h, 0)),
                  pl.BlockSpec((n_chunks, V_MLA, KEY_CHUNK), lambda i, h, j: (i, 2 * h + 1, 0))],
        out_specs=out,
        out_shape=jax.ShapeDtypeStruct((b * t, H_MLA * V_MLA), BF16),
        compiler_params=sem, name="mla_attn_lat",
    )(qmu_t, qmu_t, qmr_t, qmr_t, km_c, km_c, vmc_t, vmc_t, kmr, kmr, vm_t, vm_t)
    return o_d, o_m


def _post_kernel(x_ref, ma_ref, mb_ref, mods_ref, wo_ref, w1_ref, w2_ref, ln_ref, *rest):
    out_ref = _ride_along_casts(rest, 1)[0]
    mods = mods_ref[0, 0]
    g_m, sh_f, sc_f, g_f = mods[2:3], mods[3:4], mods[4:5], mods[5:6]
    ln = ln_ref[...]
    rows = x_ref.shape[0] // POST_GROUPS
    grp = [slice(r * rows, (r + 1) * rows) for r in range(POST_GROUPS)]
    ys = [_dot(jnp.concatenate([ma_ref[g, :], mb_ref[g, :]], axis=1), wo_ref[...]) for g in grp]
    x1s = [_layer_norm(DEEPNORM_ALPHA * x_ref[g, :] + g_m * y, ln[0:1], ln[1:2]) for g, y in zip(grp, ys)]
    fs = []
    for x1 in x1s:
        h = (x1 * (1.0 + sc_f) + sh_f).astype(BF16)
        f = jnp.zeros(x1.shape, F32)
        for c in range(D_FF // FF_CHUNK):
            a = jnp.maximum(_dot(h, w1_ref[:, c * FF_CHUNK:(c + 1) * FF_CHUNK]), 0.0)
            f = f + _dot((a * a).astype(BF16), w2_ref[c * FF_CHUNK:(c + 1) * FF_CHUNK, :])
        fs.append(f)
    for g, x1, f in zip(grp, x1s, fs):
        out_ref[g, :] = _layer_norm(DEEPNORM_ALPHA * x1 + g_f * f, ln[2:3], ln[3:4])


def _post(x2d, mix_a, mix_b, mods, layer, mod_row_fn, wo, w1, w2, ln4, tm, cast=None):
    t = x2d.shape[0]
    tok = lambda n: pl.BlockSpec((tm, n), lambda i: (i, 0))
    c_in, c_out, c_shape, c_args = _cast_specs(cast, t // tm)
    return pl.pallas_call(
        _post_kernel, grid=(t // tm,),
        in_specs=[tok(D_MODEL), tok(mix_a.shape[1]), tok(mix_b.shape[1]),
                  pl.BlockSpec((1, 1, 6, D_MODEL), lambda i: (layer, mod_row_fn(i), 0, 0)),
                  _const_spec(wo.shape), _const_spec(w1.shape), _const_spec(w2.shape),
                  _const_spec(ln4.shape)] + c_in,
        out_specs=[tok(D_MODEL)] + c_out,
        out_shape=[jax.ShapeDtypeStruct((t, D_MODEL), F32)] + c_shape,
        compiler_params=_params("arbitrary"), name=f"post_mlp_l{layer}",
    )(x2d, mix_a, mix_b, mods, wo, w1, w2, ln4, *c_args)


def _conv_in_kernel(x_ref, mods_ref, w_ref, u_ref, pz_ref):
    mods = mods_ref[0, 0]
    sh, sc = mods[0:1], mods[1:2]
    h = (x_ref[...] * (1.0 + sc) + sh).astype(BF16)
    proj = _dot(h, w_ref[...])
    a, gate = proj[:, 0:CONV_CH], proj[:, CONV_CH:2 * CONV_CH]
    u_ref[...] = a * jax.nn.sigmoid(gate)
    pz_ref[...] = proj[:, 2 * CONV_CH:]


def _conv_in(x2d, mods, layer, mod_row_fn, w, tm):
    t = x2d.shape[0]
    tok = lambda n: pl.BlockSpec((tm, n), lambda i: (i, 0))
    return pl.pallas_call(
        _conv_in_kernel, grid=(t // tm,),
        in_specs=[tok(D_MODEL),
                  pl.BlockSpec((1, 1, 6, D_MODEL), lambda i: (layer, mod_row_fn(i), 0, 0)),
                  _const_spec(w.shape)],
        out_specs=[tok(CONV_CH), tok(POOL_CH)],
        out_shape=[jax.ShapeDtypeStruct((t, CONV_CH), F32), jax.ShapeDtypeStruct((t, POOL_CH), F32)],
        compiler_params=_params("arbitrary"), name="conv_in",
    )(x2d, mods, w)


def _conv_pool_kernel(*refs, n_tiles):
    if n_tiles > 1:
        (u_ref, ul_ref, ur_ref, pz_ref, pl_ref, pr_ref, cw_ref, cp_ref, wp_ref, ps_ref, inv_ref,
         uo_ref, do_ref, ubuf, pbuf, ushift, cbuf) = refs
    else:
        u_ref, pz_ref, cw_ref, cp_ref, wp_ref, ps_ref, inv_ref, uo_ref, do_ref, ubuf, pbuf, ushift, cbuf = refs
    tm = u_ref.shape[1]
    j = pl.program_id(1)
    halo_zero = jnp.zeros((CONV_HALO, CONV_CH), F32)
    for buf, mid, sides in ((ubuf, u_ref, (ul_ref, ur_ref) if n_tiles > 1 else None),
                            (pbuf, pz_ref, (pl_ref, pr_ref) if n_tiles > 1 else None)):
        buf[CONV_HALO:CONV_HALO + tm, :] = mid[0]
        if sides is None:
            buf[0:CONV_HALO, :] = halo_zero
            buf[CONV_HALO + tm:, :] = halo_zero
        else:
            buf[0:CONV_HALO, :] = jnp.where(j > 0, sides[0][0], halo_zero)
            buf[CONV_HALO + tm:, :] = jnp.where(j < n_tiles - 1, sides[1][0], halo_zero)

    rows_sh = ushift.shape[1]
    for b in range(SUBLANES):
        ushift[b] = ubuf[b:b + rows_sh, :]
    base = CONV_HALO - CONV_WIDTH // 2
    cp = cp_ref[...]

    def strip(i, carry):
        r0 = pl.multiple_of(i * CONV_STRIP, CONV_STRIP)
        acc = jnp.zeros((CONV_STRIP, CONV_CH), F32)
        for k in range(CONV_WIDTH):
            a, b = divmod(base + k, SUBLANES)
            w = jnp.concatenate([cw_ref[k]] * (CONV_STRIP // SUBLANES), axis=0)
            acc = acc + ushift[b, pl.ds(r0 + a * SUBLANES, CONV_STRIP), :] * w
        cbuf[pl.ds(r0, CONV_STRIP), :] = acc
        return carry

    lax.fori_loop(0, tm // CONV_STRIP, strip, 0)
    z = _layer_norm(cbuf[...] + cp[0:1], cp[1:2], cp[2:3])
    uo_ref[0] = (z * jax.nn.sigmoid(z)).astype(uo_ref.dtype)

    outs = []
    for g, w in enumerate(POOL_WINDOWS):
        cols = slice(g * POOL_GC, (g + 1) * POOL_GC)
        s = jnp.zeros((tm, POOL_GC), F32)
        for d in range(-(w // 2), w // 2):
            s = s + pbuf[CONV_HALO + d:CONV_HALO + d + tm, cols]
        dgrp = s * inv_ref[:, cols] - pz_ref[0][:, cols]
        outs.append(_dot(dgrp.astype(BF16), wp_ref[g]))
    do_ref[0] = (jnp.concatenate(outs, axis=1) * ps_ref[...]).astype(do_ref.dtype)


def _pool_inv_counts(t):
    pos = np.arange(t)
    cols = []
    for w in POOL_WINDOWS:
        cnt = np.minimum(pos + w // 2 - 1, t - 1) - np.maximum(pos - w // 2, 0) + 1
        cols.append(np.repeat((1.0 / cnt)[:, None], POOL_GC, axis=1))
    return jnp.asarray(np.concatenate(cols, axis=1).astype(np.float32))


def _conv_pool(u2d, pz2d, b, t, tm, cw, cp, wp, ps):
    n_tiles = t // tm
    u3, p3 = u2d.reshape(b, t, CONV_CH), pz2d.reshape(b, t, POOL_CH)
    mid = pl.BlockSpec((1, tm, CONV_CH), lambda i, j: (i, j, 0))
    r = tm // CONV_HALO
    left = pl.BlockSpec((1, CONV_HALO, CONV_CH), lambda i, j: (i, jnp.maximum(j * r - 1, 0), 0))
    right = pl.BlockSpec((1, CONV_HALO, CONV_CH),
                         lambda i, j: (i, jnp.minimum((j + 1) * r, t // CONV_HALO - 1), 0))
    const = lambda a: pl.BlockSpec(a.shape, lambda i, j: (0,) * a.ndim)
    if n_tiles > 1:
        in_specs = [mid, left, right, mid, left, right]
        args = (u3, u3, u3, p3, p3, p3)
    else:
        in_specs = [mid, mid]
        args = (u3, p3)
    inv = _pool_inv_counts(t)
    in_specs += [const(cw), const(cp), const(wp), const(ps), pl.BlockSpec((tm, POOL_CH), lambda i, j: (j, 0))]
    uo, do = pl.pallas_call(
        functools.partial(_conv_pool_kernel, n_tiles=n_tiles),
        grid=(b, n_tiles), in_specs=in_specs,
        out_specs=[mid, mid],
        out_shape=[jax.ShapeDtypeStruct((b, t, CONV_CH), BF16), jax.ShapeDtypeStruct((b, t, POOL_CH), BF16)],
        scratch_shapes=[pltpu.VMEM((tm + 2 * CONV_HALO, CONV_CH), F32),
                        pltpu.VMEM((tm + 2 * CONV_HALO, POOL_CH), F32),
                        pltpu.VMEM((SUBLANES, tm + 2 * CONV_HALO - SUBLANES, CONV_CH), F32),
                        pltpu.VMEM((tm, CONV_CH), F32)],
        compiler_params=_params("arbitrary", "arbitrary"),
        name=f"conv_pool_{'lat' if n_tiles > 1 else 'ctx'}",
    )(*args, cw, cp, wp, ps, inv)
    return uo.reshape(b * t, CONV_CH), do.reshape(b * t, POOL_CH)


def _att_in_weights(w_att_in, with_v_tok):
    o = 2 * DIFF_QK + DIFF_V
    q_d, k_d, v_d = w_att_in[:, :DIFF_QK], w_att_in[:, DIFF_QK:2 * DIFF_QK], w_att_in[:, 2 * DIFF_QK:o]
    cq, ckv = w_att_in[:, o:o + Q_LORA], w_att_in[:, o + Q_LORA:o + Q_LORA + KV_LORA]
    kpe = jnp.pad(w_att_in[:, o + Q_LORA + KV_LORA:], _ROPE_LANE_PAD)
    w_t = jnp.concatenate([q_d, v_d, cq, ckv], axis=1).T.astype(BF16)
    w_r = jnp.concatenate([k_d, ckv, kpe] + ([v_d] if with_v_tok else []), axis=1).astype(BF16)
    return w_t, w_r


def _mla_weights(w_uq, w_ukv):
    wq = w_uq.reshape(Q_LORA, H_MLA, QK_NOPE + QK_ROPE)
    wq = jnp.pad(wq, ((0, 0), (0, 0), (0, MLA_SLAB - QK_NOPE - QK_ROPE))).reshape(Q_LORA, MLA_WIDE)
    wkv = w_ukv.reshape(KV_LORA, H_MLA, QK_NOPE + V_MLA)
    wk = jnp.pad(wkv[:, :, :QK_NOPE], ((0, 0), (0, 0), (0, MLA_SLAB - QK_NOPE))).reshape(KV_LORA, MLA_WIDE)
    wv = wkv[:, :, QK_NOPE:].reshape(KV_LORA, H_MLA * V_MLA)
    return wq.T.astype(BF16), wk.astype(BF16), wv.T.astype(BF16)


def kernel(x_prompt, x_sample, cache_diff_k, cache_diff_v, cache_mla_ckv, cache_mla_krope, c, c_ctx,
           w_ada, b_ada, ln_mix_g, ln_mix_b, ln_mlp_g, ln_mlp_b, w_mlp_in, w_mlp_out,
           w_att_in, w_uq, w_ukv, q_norm_g, kv_norm_g, lam_q1, lam_k1, lam_q2, lam_k2, diff_norm_g, w_att_out,
           w_conv_in, conv_w, conv_b, conv_norm_g, conv_norm_b, w_pool, pool_scale, w_conv_out):
    bc, tc, d = x_prompt.shape
    bl, tl, _ = x_sample.shape
    past = cache_diff_k.shape[2]

    cond8 = jnp.zeros((8, d), F32).at[0].set(c_ctx).at[1:1 + bl].set(c)
    mods = _ada_mods(cond8, w_ada, b_ada).reshape(DEPTH, 8, 6, d)
    ctx_row = lambda i: 0
    lat_row = lambda i: 1 + i // (tl // TOKEN_TILE)

    wqt, wk, wvt = _mla_weights(w_uq[0], w_ukv[0])
    norm_w = (q_norm_g[0].reshape(Q_LORA, 1), kv_norm_g[0].reshape(1, KV_LORA), kv_norm_g[0].reshape(KV_LORA, 1))
    att_w = lambda with_v: _att_in_weights(w_att_in[0], with_v) + norm_w + (wqt, wk, wvt)
    lam4 = jnp.stack([lam_q1[0], lam_k1[0], lam_q2[0], lam_k2[0]])
    diff_g = diff_norm_g[0].reshape(1, 2 * DH_DIFF)
    lam_init = 0.8 - 0.6 * math.exp(-0.3 * 0)
    ln4 = [jnp.stack([ln_mix_g[l], ln_mix_b[l], ln_mlp_g[l], ln_mlp_b[l]]) for l in range(DEPTH)]
    mlp_w = [w_mlp_in, w_mlp_out]
    conv_p = jnp.stack([conv_b[0], conv_norm_g[0], conv_norm_b[0]])
    conv_w8 = jnp.broadcast_to(conv_w[0][:, None, :], (CONV_WIDTH, SUBLANES, CONV_CH))
    wp = w_pool[0].astype(BF16)
    ps = pool_scale[0].reshape(1, POOL_CH)

    xp = x_prompt.reshape(bc * tc, d)
    outs = _att_in(xp, mods, ctx_row, att_w(True), None, TOKEN_TILE,
                   cast=(0, mlp_w + [w_att_out, w_conv_in, w_conv_out]))
    kd_f, vd_f, ckv_f, kpe_f = outs[6:10]
    w1_0, w2_0, w_ao, w_ci, w_co = outs[10:]
    o_d, o_m, w1_1, w2_1 = _attention_ctx(outs[:6], lam4, diff_g, lam_init, bc, tc, cast=(1, mlp_w))
    xp, = _post(xp, o_d, o_m, mods, 0, ctx_row, w_ao, w1_0, w2_0, ln4[0], TOKEN_TILE)
    u, pz = _conv_in(xp, mods, 1, ctx_row, w_ci, TOKEN_TILE)
    uo, do = _conv_pool(u, pz, bc, tc, tc, conv_w8, conv_p, wp, ps)
    xp, = _post(xp, uo, do, mods, 1, ctx_row, w_co, w1_1, w2_1, ln4[1], TOKEN_TILE)

    xs = x_sample.reshape(bl * tl, d)
    outs = _att_in(xs, mods, lat_row, att_w(False), _rope_tables(tl), TOKEN_TILE)
    km_c, vmc_t = _cache_kv(cache_mla_ckv[:, 0].reshape(bl * past, KV_LORA),
                            jnp.pad(cache_mla_krope[:, 0].reshape(bl * past, QK_ROPE), _ROPE_LANE_PAD),
                            wk, wvt, bl)
    cache = (cache_diff_k[:, 0].reshape(bl * past, DIFF_QK), cache_diff_v[:, 0].reshape(bl * past, DIFF_V),
             km_c, vmc_t)
    o_d, o_m = _attention_lat(outs, cache, lam4, diff_g, lam_init, bl, tl)
    xs, = _post(xs, o_d, o_m, mods, 0, lat_row, w_ao, w1_0, w2_0, ln4[0], TOKEN_TILE)
    u, pz = _conv_in(xs, mods, 1, lat_row, w_ci, TOKEN_TILE)
    uo, do = _conv_pool(u, pz, bl, tl, CONV_TILE, conv_w8, conv_p, wp, ps)
    xs, = _post(xs, uo, do, mods, 1, lat_row, w_co, w1_1, w2_1, ln4[1], TOKEN_TILE)

    return (xp.reshape(bc, tc, d), xs.reshape(bl, tl, d),
            kd_f.reshape(bc, 1, tc, H_DIFF, 2 * DH_DIFF), vd_f.reshape(bc, 1, tc, H_DIFF, 2 * DH_DIFF),
            ckv_f.reshape(bc, 1, tc, KV_LORA), kpe_f.reshape(bc, 1, tc, QK_ROPE))
```

```python
import functools
import math

import jax
import jax.numpy as jnp
import numpy as np
from jax import lax
from jax.experimental import pallas as pl
from jax.experimental.pallas import tpu as pltpu

D_MODEL = 1024
DEPTH = 2
GRID_W = 64
H_DIFF = 4
DH_DIFF = 64
DIFF_QK = H_DIFF * 2 * DH_DIFF
DIFF_V = H_DIFF * 2 * DH_DIFF
H_MLA = 8
Q_LORA = 256
KV_LORA = 128
QK_NOPE = 64
QK_ROPE = 32
V_MLA = 64
ATT_IN = 2 * DIFF_QK + DIFF_V + Q_LORA + KV_LORA + QK_ROPE
CONV_CH = 512
CONV_WIDTH = 31
POOL_CH = 512
POOL_WINDOWS = (2, 4, 8, 16)
POOL_GC = POOL_CH // len(POOL_WINDOWS)
CONV_IN = 2 * CONV_CH + POOL_CH
D_FF = 4 * D_MODEL
ROPE_BASE = 10000.0
NORM_EPS = 1e-5
DEEPNORM_ALPHA = (2 * DEPTH) ** 0.25
LOG2E = math.log2(math.e)

LANES = 128
SUBLANES = 8
MLA_SLAB = LANES
MLA_WIDE = H_MLA * MLA_SLAB
_ROPE_LANE_PAD = ((0, 0), (QK_NOPE, MLA_SLAB - QK_NOPE - QK_ROPE))
VMEM_LIMIT = 56 * 1024 * 1024

TOKEN_TILE = 512
KEY_CHUNK = TOKEN_TILE
Q_TILE = 1024
CTX_SEQS = 2
DIFF_CHUNKS_PER_ROUND = 2
MLA_CHUNKS_PER_ROUND = 2
STALE_MAX_LIMIT = 64.0
CONV_TILE = 512
CONV_HALO = 16
CONV_STRIP = 32
FF_CHUNK = 1024
POST_GROUPS = 2

F32 = jnp.float32
BF16 = jnp.bfloat16


def _dot(a, b):
    return jnp.dot(a, b, preferred_element_type=F32)


def _dot_nt(a, b):
    return lax.dot_general(a, b, (((1,), (1,)), ((), ())), preferred_element_type=F32)


def _layer_norm(x, g, b):
    mu = jnp.mean(x, axis=-1, keepdims=True)
    xc = x - mu
    var = jnp.mean(xc * xc, axis=-1, keepdims=True)
    return xc * lax.rsqrt(var + NORM_EPS) * g + b


def _rms_norm(x, g, axis=-1):
    ms = jnp.mean(x * x, axis=axis, keepdims=True)
    return x * lax.rsqrt(ms + NORM_EPS) * g


def _const_spec(shape):
    nd = len(shape)
    return pl.BlockSpec(shape, lambda *_: (0,) * nd, pipeline_mode=pl.Buffered(1))


def _cast_specs(cast, n_steps):
    if cast is None:
        return [], [], [], []
    layer, ws = cast
    in_specs = [pl.BlockSpec((1, w.shape[1] // n_steps, w.shape[2]), lambda i: (layer, i, 0)) for w in ws]
    out_specs = [pl.BlockSpec((w.shape[1] // n_steps, w.shape[2]), lambda i: (i, 0)) for w in ws]
    out_shape = [jax.ShapeDtypeStruct(w.shape[1:], BF16) for w in ws]
    return in_specs, out_specs, out_shape, list(ws)


def _ride_along_casts(rest, n_out):
    n_cast = (len(rest) - n_out) // 2
    for src, dst in zip(rest[:n_cast], rest[n_cast + n_out:]):
        dst[...] = src[0].astype(BF16)
    return rest[n_cast:n_cast + n_out]


def _params(*sem):
    return pltpu.CompilerParams(dimension_semantics=sem, vmem_limit_bytes=VMEM_LIMIT)


ADA_TILE = 3072


def _ada_kernel(cond_ref, w_ref, b_ref, out_ref):
    cond = cond_ref[...]
    act = (cond * jax.nn.sigmoid(cond)).astype(BF16)
    out_ref[0] = _dot(act, w_ref[0].astype(BF16)) + b_ref[0]


def _ada_mods(cond8, w_ada, b_ada):
    n = 6 * D_MODEL
    return pl.pallas_call(
        _ada_kernel,
        grid=(DEPTH, n // ADA_TILE),
        in_specs=[
            pl.BlockSpec((8, D_MODEL), lambda l, j: (0, 0)),
            pl.BlockSpec((1, D_MODEL, ADA_TILE), lambda l, j: (l, 0, j)),
            pl.BlockSpec((1, 1, ADA_TILE), lambda l, j: (l, 0, j)),
        ],
        out_specs=pl.BlockSpec((1, 8, ADA_TILE), lambda l, j: (l, 0, j)),
        out_shape=jax.ShapeDtypeStruct((DEPTH, 8, n), F32),
        compiler_params=_params("arbitrary", "arbitrary"),
        name="ada_mods",
    )(cond8, w_ada, b_ada.reshape(DEPTH, 1, n))


def _rope_tables(t_lat):
    pos = np.arange(t_lat)
    row = (pos // GRID_W).astype(np.float64)
    col = (pos % GRID_W).astype(np.float64)

    def tables(kinds):
        cos = np.ones((t_lat, LANES))
        sa = np.zeros((t_lat, LANES))
        sb = np.zeros((t_lat, LANES))
        for lane, kind in enumerate(kinds):
            if kind is None:
                continue
            axis, half, j, upper = kind
            ang = (row if axis == 0 else col) * ROPE_BASE ** (-float(j) / half)
            cos[:, lane] = np.cos(ang)
            if upper:
                sb[:, lane] = np.sin(ang)
            else:
                sa[:, lane] = -np.sin(ang)
        return [cos, sa, sb]

    def rot_kinds(n):
        half = n // 4
        kinds = []
        for i in range(n):
            axis, r = divmod(i, n // 2)
            kinds.append((axis, half, r % half, r >= half))
        return kinds

    diff = rot_kinds(DH_DIFF) * 2
    mla_q = [None] * QK_NOPE + rot_kinds(QK_ROPE) + [None] * (LANES - QK_NOPE - QK_ROPE)
    chan = np.stack([t.T for t in tables(diff) + tables(mla_q)]).astype(np.float32)
    tok = np.stack(tables(diff) + tables(mla_q)).astype(np.float32)
    return jnp.asarray(chan), jnp.asarray(tok)


def _rope_tok(x, cos, sa, sb, shift):
    return x * cos + pltpu.roll(x, LANES - shift, 1) * sa + pltpu.roll(x, shift, 1) * sb


def _rope_tok_wide(x, cos, sa, sb, shift):
    n = x.shape[1] // LANES
    return jnp.concatenate(
        [_rope_tok(x[:, i * LANES:(i + 1) * LANES], cos, sa, sb, shift) for i in range(n)], axis=1)


def _rope_chan(x, cos, sa, sb, shift):
    up = jnp.concatenate([x[shift:], x[:shift]], axis=0)
    down = jnp.concatenate([x[-shift:], x[:-shift]], axis=0)
    return x * cos + up * sa + down * sb


def _rope_chan_wide(x, cos, sa, sb, shift):
    n = x.shape[0] // LANES
    return jnp.concatenate(
        [_rope_chan(x[i * LANES:(i + 1) * LANES], cos, sa, sb, shift) for i in range(n)], axis=0)


_T_QD, _T_VD, _T_CQ, _T_CKV = 0, DIFF_QK, DIFF_QK + DIFF_V, DIFF_QK + DIFF_V + Q_LORA
_T_ROWS = _T_CKV + KV_LORA
_R_KD, _R_CKV, _R_KPE, _R_VD = 0, DIFF_QK, DIFF_QK + KV_LORA, DIFF_QK + KV_LORA + LANES

DIFF_QSCALE = DH_DIFF ** -0.5 * LOG2E
MLA_QSCALE = (QK_NOPE + QK_ROPE) ** -0.5 * LOG2E


def _att_in_body(x_ref, mods_ref, wt_ref, wr_ref, qg_ref, kvg_row_ref, kvg_col_ref, wqt_ref, wk_ref, wvt_ref):
    mods = mods_ref[0, 0]
    sh, sc = mods[0:1], mods[1:2]
    h = (x_ref[...] * (1.0 + sc) + sh).astype(BF16)
    pt = _dot_nt(wt_ref[...], h)
    pr = _dot(h, wr_ref[...])
    qd_t = pt[_T_QD:_T_QD + DIFF_QK] * DIFF_QSCALE
    vd_t = pt[_T_VD:_T_VD + DIFF_V]
    cqn_t = _rms_norm(pt[_T_CQ:_T_CQ + Q_LORA], qg_ref[...], axis=0)
    qm_t = _dot(wqt_ref[...], cqn_t.astype(BF16)) * MLA_QSCALE
    ckvn_t = _rms_norm(pt[_T_CKV:_T_CKV + KV_LORA], kvg_col_ref[...], axis=0)
    vm_t = _dot(wvt_ref[...], ckvn_t.astype(BF16))
    k_d = pr[:, _R_KD:_R_KD + DIFF_QK]
    ckv_n = _rms_norm(pr[:, _R_CKV:_R_CKV + KV_LORA], kvg_row_ref[...])
    kpe = pr[:, _R_KPE:_R_KPE + LANES]
    k_nope = _dot(ckv_n.astype(BF16), wk_ref[...])
    return pr, qd_t, vd_t, qm_t, vm_t, k_d, ckv_n, kpe, k_nope


def _mla_keys(k_nope, kpe_slab):
    return (k_nope + jnp.concatenate([kpe_slab] * H_MLA, axis=1)).astype(BF16)


def _att_in_ctx_kernel(x_ref, mods_ref, wt_ref, wr_ref, qg_ref, kvg_row_ref, kvg_col_ref, wqt_ref, wk_ref,
                       wvt_ref, *rest):
    (qdt_ref, kd_ref, vdt_ref, qmt_ref, km_ref, vmt_ref,
     kdf_ref, vdf_ref, ckvf_ref, kpef_ref) = _ride_along_casts(rest, 10)
    pr, qd_t, vd_t, qm_t, vm_t, k_d, ckv_n, kpe, k_nope = _att_in_body(
        x_ref, mods_ref, wt_ref, wr_ref, qg_ref, kvg_row_ref, kvg_col_ref, wqt_ref, wk_ref, wvt_ref)
    qdt_ref[...] = qd_t.astype(BF16)
    kd_ref[...] = k_d.astype(BF16)
    vdt_ref[0] = vd_t.astype(BF16)
    qmt_ref[...] = qm_t.astype(BF16)
    km_ref[...] = _mla_keys(k_nope, kpe)
    vmt_ref[0] = vm_t.astype(BF16)
    v_d = pr[:, _R_VD:_R_VD + DIFF_V]
    for h in range(H_DIFF):
        kdf_ref[:, h, :] = k_d[:, h * LANES:(h + 1) * LANES]
        vdf_ref[:, h, :] = v_d[:, h * LANES:(h + 1) * LANES]
    ckvf_ref[...] = ckv_n
    kpef_ref[...] = kpe[:, QK_NOPE:QK_NOPE + QK_ROPE]


def _att_in_lat_kernel(x_ref, mods_ref, wt_ref, wr_ref, qg_ref, kvg_row_ref, kvg_col_ref, wqt_ref, wk_ref,
                       wvt_ref, tabc_ref, tabt_ref,
                       qdut_ref, qdrt_ref, kdr_ref, vdt_ref, qmut_ref, qmrt_ref, kmr_ref, vmt_ref):
    _, qd_t, vd_t, qm_t, vm_t, k_d, ckv_n, kpe, k_nope = _att_in_body(
        x_ref, mods_ref, wt_ref, wr_ref, qg_ref, kvg_row_ref, kvg_col_ref, wqt_ref, wk_ref, wvt_ref)
    qdut_ref[...] = qd_t.astype(BF16)
    qdrt_ref[...] = _rope_chan_wide(qd_t, tabc_ref[0], tabc_ref[1], tabc_ref[2], DH_DIFF // 4).astype(BF16)
    kdr_ref[...] = _rope_tok_wide(k_d, tabt_ref[0], tabt_ref[1], tabt_ref[2], DH_DIFF // 4).astype(BF16)
    vdt_ref[0] = vd_t.astype(BF16)
    qmut_ref[...] = qm_t.astype(BF16)
    qmrt_ref[...] = _rope_chan_wide(qm_t, tabc_ref[3], tabc_ref[4], tabc_ref[5], QK_ROPE // 4).astype(BF16)
    kpe_r = _rope_tok(kpe, tabt_ref[3], tabt_ref[4], tabt_ref[5], QK_ROPE // 4)
    kmr_ref[...] = _mla_keys(k_nope, kpe_r)
    vmt_ref[0] = vm_t.astype(BF16)


def _att_in(x2d, mods, mod_row_fn, wts, tables, tm, cast=None):
    t = x2d.shape[0]
    nt = t // tm
    tok = lambda n: pl.BlockSpec((tm, n), lambda i: (i, 0))
    chan = lambda n: pl.BlockSpec((n, tm), lambda i: (0, i))
    chunk = lambda n: pl.BlockSpec((1, n, tm), lambda i: (i, 0, 0))
    in_specs = [tok(D_MODEL), pl.BlockSpec((1, 1, 6, D_MODEL), lambda i: (0, mod_row_fn(i), 0, 0))]
    in_specs += [_const_spec(w.shape) for w in wts]
    s_tok = lambda n, dt=BF16: jax.ShapeDtypeStruct((t, n), dt)
    s_chan = lambda n: jax.ShapeDtypeStruct((n, t), BF16)
    s_chunk = lambda n: jax.ShapeDtypeStruct((nt, n, tm), BF16)
    if tables is None:
        heads = pl.BlockSpec((tm, H_DIFF, 2 * DH_DIFF), lambda i: (i, 0, 0))
        s_heads = jax.ShapeDtypeStruct((t, H_DIFF, 2 * DH_DIFF), F32)
        out_shape = [s_chan(DIFF_QK), s_tok(DIFF_QK), s_chunk(DIFF_V), s_chan(MLA_WIDE), s_tok(MLA_WIDE),
                     s_chunk(H_MLA * V_MLA),
                     s_heads, s_heads, s_tok(KV_LORA, F32), s_tok(QK_ROPE, F32)]
        out_specs = [chan(DIFF_QK), tok(DIFF_QK), chunk(DIFF_V), chan(MLA_WIDE), tok(MLA_WIDE),
                     chunk(H_MLA * V_MLA), heads, heads, tok(KV_LORA), tok(QK_ROPE)]
        c_in, c_out, c_shape, c_args = _cast_specs(cast, nt)
        in_specs, out_specs, out_shape = in_specs + c_in, out_specs + c_out, out_shape + c_shape
        kern, args, name = _att_in_ctx_kernel, tuple(c_args), "att_in_ctx"
    else:
        tab_c, tab_t = tables
        t_lat = tab_t.shape[1]
        in_specs += [pl.BlockSpec((6, LANES, tm), lambda i: (0, 0, i % (t_lat // tm))),
                     pl.BlockSpec((6, tm, LANES), lambda i: (0, i % (t_lat // tm), 0))]
        out_shape = [s_chan(DIFF_QK), s_chan(DIFF_QK), s_tok(DIFF_QK), s_chunk(DIFF_V),
                     s_chan(MLA_WIDE), s_chan(MLA_WIDE), s_tok(MLA_WIDE), s_chunk(H_MLA * V_MLA)]
        out_specs = [chan(DIFF_QK), chan(DIFF_QK), tok(DIFF_QK), chunk(DIFF_V),
                     chan(MLA_WIDE), chan(MLA_WIDE), tok(MLA_WIDE), chunk(H_MLA * V_MLA)]
        kern, args, name = _att_in_lat_kernel, (tab_c, tab_t), "att_in_lat"
    return pl.pallas_call(
        kern, grid=(nt,), in_specs=in_specs, out_specs=out_specs, out_shape=out_shape,
        compiler_params=_params("arbitrary"), name=name,
    )(x2d, mods, *wts, *args)


def _cache_kv_kernel(ckv_ref, kpe_ref, wk_ref, wvt_ref, km_ref, vmt_ref):
    ckv_b = ckv_ref[...].astype(BF16)
    km_ref[...] = _mla_keys(_dot(ckv_b, wk_ref[...]), kpe_ref[...])
    past = vmt_ref.shape[2]
    vt = _dot_nt(wvt_ref[...], ckv_b)
    for b in range(vmt_ref.shape[0]):
        vmt_ref[b] = vt[:, b * past:(b + 1) * past].astype(BF16)


def _cache_kv(ckv2d, kpe_slab2d, wk, wvt, bl):
    t = ckv2d.shape[0]
    full = lambda a: pl.BlockSpec(a.shape, lambda i: (0,) * a.ndim)
    return pl.pallas_call(
        _cache_kv_kernel, grid=(1,),
        in_specs=[full(ckv2d), full(kpe_slab2d), full(wk), full(wvt)],
        out_specs=[pl.BlockSpec((t, MLA_WIDE), lambda i: (0, 0)),
                   pl.BlockSpec((bl, H_MLA * V_MLA, t // bl), lambda i: (0, 0, 0))],
        out_shape=[jax.ShapeDtypeStruct((t, MLA_WIDE), BF16),
                   jax.ShapeDtypeStruct((bl, H_MLA * V_MLA, t // bl), BF16)],
        compiler_params=_params("arbitrary"), name="cache_kv",
    )(ckv2d, kpe_slab2d, wk, wvt)


def _softmax_pv(s, vt, state, stale_max):
    m_c = jnp.max(s, axis=0, keepdims=True)
    if state is None:
        p = jnp.exp2(s - m_c)
        return m_c, jnp.sum(p, axis=0, keepdims=True), _dot(vt, p.astype(BF16)), jnp.zeros_like(m_c)
    m, l, acc, rise = state
    m_new = jnp.maximum(m, m_c)
    alpha = jnp.exp2(m - m_new)
    if stale_max:
        p = jnp.exp2(s - m)
        l = alpha * (l + jnp.sum(p, axis=0, keepdims=True))
        return m_new, l, alpha * (acc + _dot(vt, p.astype(BF16))), jnp.maximum(rise, m_c - m)
    p = jnp.exp2(s - m_new)
    l = alpha * l + jnp.sum(p, axis=0, keepdims=True)
    return m_new, l, alpha * acc + _dot(vt, p.astype(BF16)), rise


def _attend(first, rest, chunks_per_round=1, stale_max=False):
    items = [(si, qt, (lambda k=k: k), (lambda vt=vt: vt)) for si, (qt, k, vt) in enumerate(first)]
    if rest is not None:
        keys = KEY_CHUNK * chunks_per_round
        for c in range(rest[0][2].shape[0] // chunks_per_round):
            for si, (qt, k_ref, vt_ref) in enumerate(rest):
                items.append((si, qt, (lambda r=k_ref, c=c: r[c * keys:(c + 1) * keys, :]),
                              (lambda r=vt_ref, c=c: jnp.concatenate(
                                  [r[c * chunks_per_round + i] for i in range(chunks_per_round)], axis=1))))
    ns = len(first)
    groups = [items[i:i + ns] for i in range(0, len(items), ns)]
    scores = lambda grp: [_dot(k_fn(), qt) for (_, qt, k_fn, _) in grp]
    state = [None] * ns
    s_next = scores(groups[0])
    for g, grp in enumerate(groups):
        s_cur = s_next
        if g + 1 < len(groups):
            s_next = scores(groups[g + 1])
        for s, (si, _, _, vt_fn) in zip(s_cur, grp):
            state[si] = _softmax_pv(s, vt_fn(), state[si], stale_max)
    rise = functools.reduce(jnp.maximum, [st[3] for st in state])
    return [acc / l for (_, l, acc, _) in state], rise


def _attend_rolled(first, rest):
    state = tuple(_softmax_pv(_dot(k, qt), vt, None, False) for qt, k, vt in first)

    def body(c, state):
        off = pl.multiple_of(c * KEY_CHUNK, KEY_CHUNK)
        return tuple(_softmax_pv(_dot(k_ref[pl.ds(off, KEY_CHUNK), :], qt), vt_ref[c], st, False)
                     for (qt, k_ref, vt_ref), st in zip(rest, state))

    state = lax.fori_loop(0, rest[0][2].shape[0], body, state)
    return [acc / l for (_, l, acc, _) in state]


def _attend_guarded(first, rest, chunks_per_round, finish):
    o, rise = _attend(first, rest, chunks_per_round, stale_max=True)
    finish(o)

    @pl.when(jnp.max(rise) > STALE_MAX_LIMIT)
    def _():
        finish(_attend_rolled(first, rest))


def _split_maps(qt):
    row = lax.broadcasted_iota(jnp.int32, qt.shape, 0)
    zero = jnp.zeros_like(qt)
    return jnp.where(row < DH_DIFF, qt, zero), jnp.where(row >= DH_DIFF, qt, zero)


def _lambda(lam_ref, lam_init):
    lv = lam_ref[...]
    a = jnp.sum(lv[0:1] * lv[1:2], axis=-1, keepdims=True)
    b = jnp.sum(lv[2:3] * lv[3:4], axis=-1, keepdims=True)
    return jnp.exp(a) - jnp.exp(b) + lam_init


def _diff_finish(o1_t, o2_t, lam_ref, g_ref, lam_init):
    o = (o1_t - _lambda(lam_ref, lam_init) * o2_t).T
    return (_rms_norm(o, g_ref[...]) * (1.0 - lam_init)).astype(BF16)


def _mla_finish(oa_t, ob_t):
    return jnp.concatenate([oa_t, ob_t], axis=0).T.astype(BF16)


def _diff_ctx_kernel(qt_ref, k_ref, vt_ref, lam_ref, g_ref, *rest, lam_init):
    out_ref = _ride_along_casts(rest, 1)[0]
    t = qt_ref.shape[1] // CTX_SEQS
    first = []
    for q in range(CTX_SEQS):
        ts = slice(q * t, (q + 1) * t)
        for h in range(H_DIFF):
            sl = slice(h * LANES, (h + 1) * LANES)
            k, vt = k_ref[ts, sl], vt_ref[0, sl, ts]
            first += [(qm, k, vt) for qm in _split_maps(qt_ref[sl, ts])]
    o, _ = _attend(first, None)
    n = 2 * H_DIFF
    for q in range(CTX_SEQS):
        out_ref[q * t:(q + 1) * t, :] = jnp.concatenate(
            [_diff_finish(o[q * n + 2 * h], o[q * n + 2 * h + 1], lam_ref, g_ref, lam_init) for h in range(H_DIFF)],
            axis=1)


def _diff_lat_kernel(qut_ref, qrt_ref, kc_ref, vc_ref, k_ref, vt_ref, lam_ref, g_ref, out_ref, *, lam_init):
    kc, vct = kc_ref[...].astype(BF16), vc_ref[...].T.astype(BF16)
    first = [(q, kc, vct) for q in _split_maps(qut_ref[...])]
    rest = [(q, k_ref, vt_ref) for q in _split_maps(qrt_ref[...])]

    def finish(o):
        out_ref[...] = _diff_finish(o[0], o[1], lam_ref, g_ref, lam_init)

    _attend_guarded(first, rest, DIFF_CHUNKS_PER_ROUND, finish)


def _mla_ctx_kernel(qt_ref, k_ref, vt_ref, *rest):
    out_ref = _ride_along_casts(rest, 1)[0]
    t = qt_ref.shape[1] // CTX_SEQS
    first = []
    for q in range(CTX_SEQS):
        ts = slice(q * t, (q + 1) * t)
        for h in range(H_MLA):
            sl = slice(h * MLA_SLAB, (h + 1) * MLA_SLAB)
            first.append((qt_ref[sl, ts], k_ref[ts, sl], vt_ref[0, h * V_MLA:(h + 1) * V_MLA, ts]))
    o, _ = _attend(first, None)
    for q in range(CTX_SEQS):
        out_ref[q * t:(q + 1) * t, :] = jnp.concatenate(
            [_mla_finish(o[q * H_MLA + 2 * j], o[q * H_MLA + 2 * j + 1]) for j in range(H_MLA // 2)], axis=1)


def _mla_lat_kernel(qua_ref, qub_ref, qra_ref, qrb_ref, kca_ref, kcb_ref, vcta_ref, vctb_ref, ka_ref, kb_ref,
                    vta_ref, vtb_ref, out_ref):
    first = [(qua_ref[...], kca_ref[...], vcta_ref[0]), (qub_ref[...], kcb_ref[...], vctb_ref[0])]
    rest = [(qra_ref[...], ka_ref, vta_ref), (qrb_ref[...], kb_ref, vtb_ref)]

    def finish(o):
        out_ref[...] = _mla_finish(o[0], o[1])

    _attend_guarded(first, rest, MLA_CHUNKS_PER_ROUND, finish)


def _attention_ctx(streams, lam4, diff_g, lam_init, b, t, cast):
    qd_t, kd, vd_t, qm_t, km, vm_t = streams
    per_chunk = vd_t.shape[2] // t // CTX_SEQS
    t, b = t * CTX_SEQS, b // CTX_SEQS
    sem = _params("arbitrary")
    qt = lambda w: pl.BlockSpec((w, t), lambda i: (0, i))
    keys = lambda w: pl.BlockSpec((t, w), lambda i: (i, 0))
    vals = lambda w: pl.BlockSpec((1, w, t), lambda i: (i // per_chunk, 0, i % per_chunk))
    casts = [_cast_specs((cast[0], [w]), b) for w in cast[1]]
    o_d, w_a = pl.pallas_call(
        functools.partial(_diff_ctx_kernel, lam_init=lam_init),
        grid=(b,),
        in_specs=[qt(DIFF_QK), keys(DIFF_QK), vals(DIFF_V),
                  pl.BlockSpec(lam4.shape, lambda i: (0, 0)),
                  pl.BlockSpec(diff_g.shape, lambda i: (0, 0))] + casts[0][0],
        out_specs=[keys(DIFF_V)] + casts[0][1],
        out_shape=[jax.ShapeDtypeStruct((b * t, DIFF_V), BF16)] + casts[0][2],
        compiler_params=sem, name="diff_attn_ctx",
    )(qd_t, kd, vd_t, lam4, diff_g, *casts[0][3])
    o_m, w_b = pl.pallas_call(
        _mla_ctx_kernel,
        grid=(b,),
        in_specs=[qt(MLA_WIDE), keys(MLA_WIDE), vals(H_MLA * V_MLA)] + casts[1][0],
        out_specs=[keys(H_MLA * V_MLA)] + casts[1][1],
        out_shape=[jax.ShapeDtypeStruct((b * t, H_MLA * V_MLA), BF16)] + casts[1][2],
        compiler_params=sem, name="mla_attn_ctx",
    )(qm_t, km, vm_t, *casts[1][3])
    return o_d, o_m, w_a, w_b


def _attention_lat(streams, cache, lam4, diff_g, lam_init, b, t):
    qdu_t, qdr_t, kdr, vd_t, qmu_t, qmr_t, kmr, vm_t = streams
    kd_c, vd_c, km_c, vmc_t = cache
    past = kd_c.shape[0] // b
    nq = t // Q_TILE
    n_chunks = t // KEY_CHUNK
    sem = _params("arbitrary", "arbitrary", "arbitrary")
    qt = lambda f: pl.BlockSpec((LANES, Q_TILE), lambda i, h, j: (f(h), i * nq + j))
    keys = lambda n, f: pl.BlockSpec((n, LANES), lambda i, h, j: (i, f(h)))
    vals = pl.BlockSpec((n_chunks, LANES, KEY_CHUNK), lambda i, h, j: (i, h, 0))
    out = pl.BlockSpec((Q_TILE, LANES), lambda i, h, j: (i * nq + j, h))
    same, even, odd = (lambda h: h), (lambda h: 2 * h), (lambda h: 2 * h + 1)
    o_d = pl.pallas_call(
        functools.partial(_diff_lat_kernel, lam_init=lam_init),
        grid=(b, H_DIFF, nq),
        in_specs=[qt(same), qt(same), keys(past, same), keys(past, same), keys(t, same), vals,
                  pl.BlockSpec(lam4.shape, lambda i, h, j: (0, 0)),
                  pl.BlockSpec(diff_g.shape, lambda i, h, j: (0, 0))],
        out_specs=out,
        out_shape=jax.ShapeDtypeStruct((b * t, DIFF_V), BF16),
        compiler_params=sem, name="diff_attn_lat",
    )(qdu_t, qdr_t, kd_c, vd_c, kdr, vd_t, lam4, diff_g)
    o_m = pl.pallas_call(
        _mla_lat_kernel,
        grid=(b, H_MLA // 2, nq),
        in_specs=[qt(even), qt(odd), qt(even), qt(odd), keys(past, even), keys(past, odd),
                  pl.BlockSpec((1, V_MLA, past), lambda i, h, j: (i, 2 * h, 0)),
                  pl.BlockSpec((1, V_MLA, past), lambda i, h, j: (i, 2 * h + 1, 0)),
                  keys(t, even), keys(t, odd),
                  pl.BlockSpec((n_chunks, V_MLA, KEY_CHUNK), lambda i, h, j: (i, 2 * h, 0)),
                  pl.BlockSpec((n_chunks, V_MLA, KEY_CHUNK), lambda i, h, j: (i, 2 * h + 1, 0))],
        out_specs=out,
        out_shape=jax.ShapeDtypeStruct((b * t, H_MLA * V_MLA), BF16),
        compiler_params=sem, name="mla_attn_lat",
    )(qmu_t, qmu_t, qmr_t, qmr_t, km_c, km_c, vmc_t, vmc_t, kmr, kmr, vm_t, vm_t)
    return o_d, o_m


def _post_kernel(x_ref, ma_ref, mb_ref, mods_ref, wo_ref, w1_ref, w2_ref, ln_ref, *rest):
    out_ref = _ride_along_casts(rest, 1)[0]
    mods = mods_ref[0, 0]
    g_m, sh_f, sc_f, g_f = mods[2:3], mods[3:4], mods[4:5], mods[5:6]
    ln = ln_ref[...]
    rows = x_ref.shape[0] // POST_GROUPS
    grp = [slice(r * rows, (r + 1) * rows) for r in range(POST_GROUPS)]
    ys = [_dot(jnp.concatenate([ma_ref[g, :], mb_ref[g, :]], axis=1), wo_ref[...]) for g in grp]
    x1s = [_layer_norm(DEEPNORM_ALPHA * x_ref[g, :] + g_m * y, ln[0:1], ln[1:2]) for g, y in zip(grp, ys)]
    fs = []
    for x1 in x1s:
        h = (x1 * (1.0 + sc_f) + sh_f).astype(BF16)
        f = jnp.zeros(x1.shape, F32)
        for c in range(D_FF // FF_CHUNK):
            a = jnp.maximum(_dot(h, w1_ref[:, c * FF_CHUNK:(c + 1) * FF_CHUNK]), 0.0)
            f = f + _dot((a * a).astype(BF16), w2_ref[c * FF_CHUNK:(c + 1) * FF_CHUNK, :])
        fs.append(f)
    for g, x1, f in zip(grp, x1s, fs):
        out_ref[g, :] = _layer_norm(DEEPNORM_ALPHA * x1 + g_f * f, ln[2:3], ln[3:4])


def _post(x2d, mix_a, mix_b, mods, layer, mod_row_fn, wo, w1, w2, ln4, tm, cast=None):
    t = x2d.shape[0]
    tok = lambda n: pl.BlockSpec((tm, n), lambda i: (i, 0))
    c_in, c_out, c_shape, c_args = _cast_specs(cast, t // tm)
    return pl.pallas_call(
        _post_kernel, grid=(t // tm,),
        in_specs=[tok(D_MODEL), tok(mix_a.shape[1]), tok(mix_b.shape[1]),
                  pl.BlockSpec((1, 1, 6, D_MODEL), lambda i: (layer, mod_row_fn(i), 0, 0)),
                  _const_spec(wo.shape), _const_spec(w1.shape), _const_spec(w2.shape),
                  _const_spec(ln4.shape)] + c_in,
        out_specs=[tok(D_MODEL)] + c_out,
        out_shape=[jax.ShapeDtypeStruct((t, D_MODEL), F32)] + c_shape,
        compiler_params=_params("arbitrary"), name=f"post_mlp_l{layer}",
    )(x2d, mix_a, mix_b, mods, wo, w1, w2, ln4, *c_args)


def _conv_in_kernel(x_ref, mods_ref, w_ref, u_ref, pz_ref):
    mods = mods_ref[0, 0]
    sh, sc = mods[0:1], mods[1:2]
    h = (x_ref[...] * (1.0 + sc) + sh).astype(BF16)
    proj = _dot(h, w_ref[...])
    a, gate = proj[:, 0:CONV_CH], proj[:, CONV_CH:2 * CONV_CH]
    u_ref[...] = a * jax.nn.sigmoid(gate)
    pz_ref[...] = proj[:, 2 * CONV_CH:]


def _conv_in(x2d, mods, layer, mod_row_fn, w, tm):
    t = x2d.shape[0]
    tok = lambda n: pl.BlockSpec((tm, n), lambda i: (i, 0))
    return pl.pallas_call(
        _conv_in_kernel, grid=(t // tm,),
        in_specs=[tok(D_MODEL),
                  pl.BlockSpec((1, 1, 6, D_MODEL), lambda i: (layer, mod_row_fn(i), 0, 0)),
                  _const_spec(w.shape)],
        out_specs=[tok(CONV_CH), tok(POOL_CH)],
        out_shape=[jax.ShapeDtypeStruct((t, CONV_CH), F32), jax.ShapeDtypeStruct((t, POOL_CH), F32)],
        compiler_params=_params("arbitrary"), name="conv_in",
    )(x2d, mods, w)


def _conv_pool_kernel(*refs, n_tiles):
    if n_tiles > 1:
        (u_ref, ul_ref, ur_ref, pz_ref, pl_ref, pr_ref, cw_ref, cp_ref, wp_ref, ps_ref, inv_ref,
         uo_ref, do_ref, ubuf, pbuf, ushift, cbuf) = refs
    else:
        u_ref, pz_ref, cw_ref, cp_ref, wp_ref, ps_ref, inv_ref, uo_ref, do_ref, ubuf, pbuf, ushift, cbuf = refs
    tm = u_ref.shape[1]
    j = pl.program_id(1)
    halo_zero = jnp.zeros((CONV_HALO, CONV_CH), F32)
    for buf, mid, sides in ((ubuf, u_ref, (ul_ref, ur_ref) if n_tiles > 1 else None),
                            (pbuf, pz_ref, (pl_ref, pr_ref) if n_tiles > 1 else None)):
        buf[CONV_HALO:CONV_HALO + tm, :] = mid[0]
        if sides is None:
            buf[0:CONV_HALO, :] = halo_zero
            buf[CONV_HALO + tm:, :] = halo_zero
        else:
            buf[0:CONV_HALO, :] = jnp.where(j > 0, sides[0][0], halo_zero)
            buf[CONV_HALO + tm:, :] = jnp.where(j < n_tiles - 1, sides[1][0], halo_zero)

    rows_sh = ushift.shape[1]
    for b in range(SUBLANES):
        ushift[b] = ubuf[b:b + rows_sh, :]
    base = CONV_HALO - CONV_WIDTH // 2
    cp = cp_ref[...]

    def strip(i, carry):
        r0 = pl.multiple_of(i * CONV_STRIP, CONV_STRIP)
        acc = jnp.zeros((CONV_STRIP, CONV_CH), F32)
        for k in range(CONV_WIDTH):
            a, b = divmod(base + k, SUBLANES)
            w = jnp.concatenate([cw_ref[k]] * (CONV_STRIP // SUBLANES), axis=0)
            acc = acc + ushift[b, pl.ds(r0 + a * SUBLANES, CONV_STRIP), :] * w
        cbuf[pl.ds(r0, CONV_STRIP), :] = acc
        return carry

    lax.fori_loop(0, tm // CONV_STRIP, strip, 0)
    z = _layer_norm(cbuf[...] + cp[0:1], cp[1:2], cp[2:3])
    uo_ref[0] = (z * jax.nn.sigmoid(z)).astype(uo_ref.dtype)

    outs = []
    for g, w in enumerate(POOL_WINDOWS):
        cols = slice(g * POOL_GC, (g + 1) * POOL_GC)
        s = jnp.zeros((tm, POOL_GC), F32)
        for d in range(-(w // 2), w // 2):
            s = s + pbuf[CONV_HALO + d:CONV_HALO + d + tm, cols]
        dgrp = s * inv_ref[:, cols] - pz_ref[0][:, cols]
        outs.append(_dot(dgrp.astype(BF16), wp_ref[g]))
    do_ref[0] = (jnp.concatenate(outs, axis=1) * ps_ref[...]).astype(do_ref.dtype)


def _pool_inv_counts(t):
    pos = np.arange(t)
    cols = []
    for w in POOL_WINDOWS:
        cnt = np.minimum(pos + w // 2 - 1, t - 1) - np.maximum(pos - w // 2, 0) + 1
        cols.append(np.repeat((1.0 / cnt)[:, None], POOL_GC, axis=1))
    return jnp.asarray(np.concatenate(cols, axis=1).astype(np.float32))


def _conv_pool(u2d, pz2d, b, t, tm, cw, cp, wp, ps):
    n_tiles = t // tm
    u3, p3 = u2d.reshape(b, t, CONV_CH), pz2d.reshape(b, t, POOL_CH)
    mid = pl.BlockSpec((1, tm, CONV_CH), lambda i, j: (i, j, 0))
    r = tm // CONV_HALO
    left = pl.BlockSpec((1, CONV_HALO, CONV_CH), lambda i, j: (i, jnp.maximum(j * r - 1, 0), 0))
    right = pl.BlockSpec((1, CONV_HALO, CONV_CH),
                         lambda i, j: (i, jnp.minimum((j + 1) * r, t // CONV_HALO - 1), 0))
    const = lambda a: pl.BlockSpec(a.shape, lambda i, j: (0,) * a.ndim)
    if n_tiles > 1:
        in_specs = [mid, left, right, mid, left, right]
        args = (u3, u3, u3, p3, p3, p3)
    else:
        in_specs = [mid, mid]
        args = (u3, p3)
    inv = _pool_inv_counts(t)
    in_specs += [const(cw), const(cp), const(wp), const(ps), pl.BlockSpec((tm, POOL_CH), lambda i, j: (j, 0))]
    uo, do = pl.pallas_call(
        functools.partial(_conv_pool_kernel, n_tiles=n_tiles),
        grid=(b, n_tiles), in_specs=in_specs,
        out_specs=[mid, mid],
        out_shape=[jax.ShapeDtypeStruct((b, t, CONV_CH), BF16), jax.ShapeDtypeStruct((b, t, POOL_CH), BF16)],
        scratch_shapes=[pltpu.VMEM((tm + 2 * CONV_HALO, CONV_CH), F32),
                        pltpu.VMEM((tm + 2 * CONV_HALO, POOL_CH), F32),
                        pltpu.VMEM((SUBLANES, tm + 2 * CONV_HALO - SUBLANES, CONV_CH), F32),
                        pltpu.VMEM((tm, CONV_CH), F32)],
        compiler_params=_params("arbitrary", "arbitrary"),
        name=f"conv_pool_{'lat' if n_tiles > 1 else 'ctx'}",
    )(*args, cw, cp, wp, ps, inv)
    return uo.reshape(b * t, CONV_CH), do.reshape(b * t, POOL_CH)


def _att_in_weights(w_att_in, with_v_tok):
    o = 2 * DIFF_QK + DIFF_V
    q_d, k_d, v_d = w_att_in[:, :DIFF_QK], w_att_in[:, DIFF_QK:2 * DIFF_QK], w_att_in[:, 2 * DIFF_QK:o]
    cq, ckv = w_att_in[:, o:o + Q_LORA], w_att_in[:, o + Q_LORA:o + Q_LORA + KV_LORA]
    kpe = jnp.pad(w_att_in[:, o + Q_LORA + KV_LORA:], _ROPE_LANE_PAD)
    w_t = jnp.concatenate([q_d, v_d, cq, ckv], axis=1).T.astype(BF16)
    w_r = jnp.concatenate([k_d, ckv, kpe] + ([v_d] if with_v_tok else []), axis=1).astype(BF16)
    return w_t, w_r


def _mla_weights(w_uq, w_ukv):
    wq = w_uq.reshape(Q_LORA, H_MLA, QK_NOPE + QK_ROPE)
    wq = jnp.pad(wq, ((0, 0), (0, 0), (0, MLA_SLAB - QK_NOPE - QK_ROPE))).reshape(Q_LORA, MLA_WIDE)
    wkv = w_ukv.reshape(KV_LORA, H_MLA, QK_NOPE + V_MLA)
    wk = jnp.pad(wkv[:, :, :QK_NOPE], ((0, 0), (0, 0), (0, MLA_SLAB - QK_NOPE))).reshape(KV_LORA, MLA_WIDE)
    wv = wkv[:, :, QK_NOPE:].reshape(KV_LORA, H_MLA * V_MLA)
    return wq.T.astype(BF16), wk.astype(BF16), wv.T.astype(BF16)


def kernel(x_prompt, x_sample, cache_diff_k, cache_diff_v, cache_mla_ckv, cache_mla_krope, c, c_ctx,
           w_ada, b_ada, ln_mix_g, ln_mix_b, ln_mlp_g, ln_mlp_b, w_mlp_in, w_mlp_out,
           w_att_in, w_uq, w_ukv, q_norm_g, kv_norm_g, lam_q1, lam_k1, lam_q2, lam_k2, diff_norm_g, w_att_out,
           w_conv_in, conv_w, conv_b, conv_norm_g, conv_norm_b, w_pool, pool_scale, w_conv_out):
    bc, tc, d = x_prompt.shape
    bl, tl, _ = x_sample.shape
    past = cache_diff_k.shape[2]

    cond8 = jnp.zeros((8, d), F32).at[0].set(c_ctx).at[1:1 + bl].set(c)
    mods = _ada_mods(cond8, w_ada, b_ada).reshape(DEPTH, 8, 6, d)
    ctx_row = lambda i: 0
    lat_row = lambda i: 1 + i // (tl // TOKEN_TILE)

    wqt, wk, wvt = _mla_weights(w_uq[0], w_ukv[0])
    norm_w = (q_norm_g[0].reshape(Q_LORA, 1), kv_norm_g[0].reshape(1, KV_LORA), kv_norm_g[0].reshape(KV_LORA, 1))
    att_w = lambda with_v: _att_in_weights(w_att_in[0], with_v) + norm_w + (wqt, wk, wvt)
    lam4 = jnp.stack([lam_q1[0], lam_k1[0], lam_q2[0], lam_k2[0]])
    diff_g = diff_norm_g[0].reshape(1, 2 * DH_DIFF)
    lam_init = 0.8 - 0.6 * math.exp(-0.3 * 0)
    ln4 = [jnp.stack([ln_mix_g[l], ln_mix_b[l], ln_mlp_g[l], ln_mlp_b[l]]) for l in range(DEPTH)]
    mlp_w = [w_mlp_in, w_mlp_out]
    conv_p = jnp.stack([conv_b[0], conv_norm_g[0], conv_norm_b[0]])
    conv_w8 = jnp.broadcast_to(conv_w[0][:, None, :], (CONV_WIDTH, SUBLANES, CONV_CH))
    wp = w_pool[0].astype(BF16)
    ps = pool_scale[0].reshape(1, POOL_CH)

    xp = x_prompt.reshape(bc * tc, d)
    outs = _att_in(xp, mods, ctx_row, att_w(True), None, TOKEN_TILE,
                   cast=(0, mlp_w + [w_att_out, w_conv_in, w_conv_out]))
    kd_f, vd_f, ckv_f, kpe_f = outs[6:10]
    w1_0, w2_0, w_ao, w_ci, w_co = outs[10:]
    o_d, o_m, w1_1, w2_1 = _attention_ctx(outs[:6], lam4, diff_g, lam_init, bc, tc, cast=(1, mlp_w))
    xp, = _post(xp, o_d, o_m, mods, 0, ctx_row, w_ao, w1_0, w2_0, ln4[0], TOKEN_TILE)
    u, pz = _conv_in(xp, mods, 1, ctx_row, w_ci, TOKEN_TILE)
    uo, do = _conv_pool(u, pz, bc, tc, tc, conv_w8, conv_p, wp, ps)
    xp, = _post(xp, uo, do, mods, 1, ctx_row, w_co, w1_1, w2_1, ln4[1], TOKEN_TILE)

    xs = x_sample.reshape(bl * tl, d)
    outs = _att_in(xs, mods, lat_row, att_w(False), _rope_tables(tl), TOKEN_TILE)
    km_c, vmc_t = _cache_kv(cache_mla_ckv[:, 0].reshape(bl * past, KV_LORA),
                            jnp.pad(cache_mla_krope[:, 0].reshape(bl * past, QK_ROPE), _ROPE_LANE_PAD),
                            wk, wvt, bl)
    cache = (cache_diff_k[:, 0].reshape(bl * past, DIFF_QK), cache_diff_v[:, 0].reshape(bl * past, DIFF_V),
             km_c, vmc_t)
    o_d, o_m = _attention_lat(outs, cache, lam4, diff_g, lam_init, bl, tl)
    xs, = _post(xs, o_d, o_m, mods, 0, lat_row, w_ao, w1_0, w2_0, ln4[0], TOKEN_TILE)
    u, pz = _conv_in(xs, mods, 1, lat_row, w_ci, TOKEN_TILE)
    uo, do = _conv_pool(u, pz, bl, tl, CONV_TILE, conv_w8, conv_p, wp, ps)
    xs, = _post(xs, uo, do, mods, 1, lat_row, w_co, w1_1, w2_1, ln4[1], TOKEN_TILE)

    return (xp.reshape(bc, tc, d), xs.reshape(bl, tl, d),
            kd_f.reshape(bc, 1, tc, H_DIFF, 2 * DH_DIFF), vd_f.reshape(bc, 1, tc, H_DIFF, 2 * DH_DIFF),
            ckv_f.reshape(bc, 1, tc, KV_LORA), kpe_f.reshape(bc, 1, tc, QK_ROPE))
```

```python
import functools
import math

import jax
import jax.numpy as jnp
import numpy as np
from jax import lax
from jax.experimental import pallas as pl
from jax.experimental.pallas import tpu as pltpu

D_MODEL = 1024
DEPTH = 2
GRID_W = 64
H_DIFF = 4
DH_DIFF = 64
DIFF_QK = H_DIFF * 2 * DH_DIFF
DIFF_V = H_DIFF * 2 * DH_DIFF
H_MLA = 8
Q_LORA = 256
KV_LORA = 128
QK_NOPE = 64
QK_ROPE = 32
V_MLA = 64
ATT_IN = 2 * DIFF_QK + DIFF_V + Q_LORA + KV_LORA + QK_ROPE
CONV_CH = 512
CONV_WIDTH = 31
POOL_CH = 512
POOL_WINDOWS = (2, 4, 8, 16)
POOL_GC = POOL_CH // len(POOL_WINDOWS)
CONV_IN = 2 * CONV_CH + POOL_CH
D_FF = 4 * D_MODEL
ROPE_BASE = 10000.0
NORM_EPS = 1e-5
DEEPNORM_ALPHA = (2 * DEPTH) ** 0.25
LOG2E = math.log2(math.e)

LANES = 128
SUBLANES = 8
MLA_SLAB = LANES
MLA_WIDE = H_MLA * MLA_SLAB
_ROPE_LANE_PAD = ((0, 0), (QK_NOPE, MLA_SLAB - QK_NOPE - QK_ROPE))
VMEM_LIMIT = 56 * 1024 * 1024

TOKEN_TILE = 512
KEY_CHUNK = TOKEN_TILE
Q_TILE = 1024
CTX_SEQS = 2
DIFF_CHUNKS_PER_ROUND = 4
MLA_CHUNKS_PER_ROUND = 2
STALE_MAX_LIMIT = 64.0
CONV_TILE = 512
CONV_HALO = 16
CONV_STRIP = 32
FF_CHUNK = 1024
POST_GROUPS = 2

F32 = jnp.float32
BF16 = jnp.bfloat16


def _dot(a, b):
    return jnp.dot(a, b, preferred_element_type=F32)


def _dot_nt(a, b):
    return lax.dot_general(a, b, (((1,), (1,)), ((), ())), preferred_element_type=F32)


def _layer_norm(x, g, b):
    mu = jnp.mean(x, axis=-1, keepdims=True)
    xc = x - mu
    var = jnp.mean(xc * xc, axis=-1, keepdims=True)
    return xc * lax.rsqrt(var + NORM_EPS) * g + b


def _rms_norm(x, g, axis=-1):
    ms = jnp.mean(x * x, axis=axis, keepdims=True)
    return x * lax.rsqrt(ms + NORM_EPS) * g


def _const_spec(shape):
    nd = len(shape)
    return pl.BlockSpec(shape, lambda *_: (0,) * nd, pipeline_mode=pl.Buffered(1))


def _cast_specs(cast, n_steps):
    if cast is None:
        return [], [], [], []
    layer, ws = cast
    in_specs = [pl.BlockSpec((1, w.shape[1] // n_steps, w.shape[2]), lambda i: (layer, i, 0)) for w in ws]
    out_specs = [pl.BlockSpec((w.shape[1] // n_steps, w.shape[2]), lambda i: (i, 0)) for w in ws]
    out_shape = [jax.ShapeDtypeStruct(w.shape[1:], BF16) for w in ws]
    return in_specs, out_specs, out_shape, list(ws)


def _ride_along_casts(rest, n_out):
    n_cast = (len(rest) - n_out) // 2
    for src, dst in zip(rest[:n_cast], rest[n_cast + n_out:]):
        dst[...] = src[0].astype(BF16)
    return rest[n_cast:n_cast + n_out]


def _params(*sem):
    return pltpu.CompilerParams(dimension_semantics=sem, vmem_limit_bytes=VMEM_LIMIT)


ADA_TILE = 3072


def _ada_kernel(cond_ref, w_ref, b_ref, out_ref):
    cond = cond_ref[...]
    act = (cond * jax.nn.sigmoid(cond)).astype(BF16)
    out_ref[0] = _dot(act, w_ref[0].astype(BF16)) + b_ref[0]


def _ada_mods(cond8, w_ada, b_ada):
    n = 6 * D_MODEL
    return pl.pallas_call(
        _ada_kernel,
        grid=(DEPTH, n // ADA_TILE),
        in_specs=[
            pl.BlockSpec((8, D_MODEL), lambda l, j: (0, 0)),
            pl.BlockSpec((1, D_MODEL, ADA_TILE), lambda l, j: (l, 0, j)),
            pl.BlockSpec((1, 1, ADA_TILE), lambda l, j: (l, 0, j)),
        ],
        out_specs=pl.BlockSpec((1, 8, ADA_TILE), lambda l, j: (l, 0, j)),
        out_shape=jax.ShapeDtypeStruct((DEPTH, 8, n), F32),
        compiler_params=_params("arbitrary", "arbitrary"),
        name="ada_mods",
    )(cond8, w_ada, b_ada.reshape(DEPTH, 1, n))


def _rope_tables(t_lat):
    pos = np.arange(t_lat)
    row = (pos // GRID_W).astype(np.float64)
    col = (pos % GRID_W).astype(np.float64)

    def tables(kinds):
        cos = np.ones((t_lat, LANES))
        sa = np.zeros((t_lat, LANES))
        sb = np.zeros((t_lat, LANES))
        for lane, kind in enumerate(kinds):
            if kind is None:
                continue
            axis, half, j, upper = kind
            ang = (row if axis == 0 else col) * ROPE_BASE ** (-float(j) / half)
            cos[:, lane] = np.cos(ang)
            if upper:
                sb[:, lane] = np.sin(ang)
            else:
                sa[:, lane] = -np.sin(ang)
        return [cos, sa, sb]

    def rot_kinds(n):
        half = n // 4
        kinds = []
        for i in range(n):
            axis, r = divmod(i, n // 2)
            kinds.append((axis, half, r % half, r >= half))
        return kinds

    diff = rot_kinds(DH_DIFF) * 2
    mla_q = [None] * QK_NOPE + rot_kinds(QK_ROPE) + [None] * (LANES - QK_NOPE - QK_ROPE)
    chan = np.stack([t.T for t in tables(diff) + tables(mla_q)]).astype(np.float32)
    tok = np.stack(tables(diff) + tables(mla_q)).astype(np.float32)
    return jnp.asarray(chan), jnp.asarray(tok)


def _rope_tok(x, cos, sa, sb, shift):
    return x * cos + pltpu.roll(x, LANES - shift, 1) * sa + pltpu.roll(x, shift, 1) * sb


def _rope_tok_wide(x, cos, sa, sb, shift):
    n = x.shape[1] // LANES
    return jnp.concatenate(
        [_rope_tok(x[:, i * LANES:(i + 1) * LANES], cos, sa, sb, shift) for i in range(n)], axis=1)


def _rope_chan(x, cos, sa, sb, shift):
    up = jnp.concatenate([x[shift:], x[:shift]], axis=0)
    down = jnp.concatenate([x[-shift:], x[:-shift]], axis=0)
    return x * cos + up * sa + down * sb


def _rope_chan_wide(x, cos, sa, sb, shift):
    n = x.shape[0] // LANES
    return jnp.concatenate(
        [_rope_chan(x[i * LANES:(i + 1) * LANES], cos, sa, sb, shift) for i in range(n)], axis=0)


_T_QD, _T_VD, _T_CQ, _T_CKV = 0, DIFF_QK, DIFF_QK + DIFF_V, DIFF_QK + DIFF_V + Q_LORA
_T_ROWS = _T_CKV + KV_LORA
_R_KD, _R_CKV, _R_KPE, _R_VD = 0, DIFF_QK, DIFF_QK + KV_LORA, DIFF_QK + KV_LORA + LANES

DIFF_QSCALE = DH_DIFF ** -0.5 * LOG2E
MLA_QSCALE = (QK_NOPE + QK_ROPE) ** -0.5 * LOG2E


def _att_in_body(x_ref, mods_ref, wt_ref, wr_ref, qg_ref, kvg_row_ref, kvg_col_ref, wqt_ref, wk_ref, wvt_ref):
    mods = mods_ref[0, 0]
    sh, sc = mods[0:1], mods[1:2]
    h = (x_ref[...] * (1.0 + sc) + sh).astype(BF16)
    pt = _dot_nt(wt_ref[...], h)
    pr = _dot(h, wr_ref[...])
    qd_t = pt[_T_QD:_T_QD + DIFF_QK] * DIFF_QSCALE
    vd_t = pt[_T_VD:_T_VD + DIFF_V]
    cqn_t = _rms_norm(pt[_T_CQ:_T_CQ + Q_LORA], qg_ref[...], axis=0)
    qm_t = _dot(wqt_ref[...], cqn_t.astype(BF16)) * MLA_QSCALE
    ckvn_t = _rms_norm(pt[_T_CKV:_T_CKV + KV_LORA], kvg_col_ref[...], axis=0)
    vm_t = _dot(wvt_ref[...], ckvn_t.astype(BF16))
    k_d = pr[:, _R_KD:_R_KD + DIFF_QK]
    ckv_n = _rms_norm(pr[:, _R_CKV:_R_CKV + KV_LORA], kvg_row_ref[...])
    kpe = pr[:, _R_KPE:_R_KPE + LANES]
    k_nope = _dot(ckv_n.astype(BF16), wk_ref[...])
    return pr, qd_t, vd_t, qm_t, vm_t, k_d, ckv_n, kpe, k_nope


def _mla_keys(k_nope, kpe_slab):
    return (k_nope + jnp.concatenate([kpe_slab] * H_MLA, axis=1)).astype(BF16)


def _att_in_ctx_kernel(x_ref, mods_ref, wt_ref, wr_ref, qg_ref, kvg_row_ref, kvg_col_ref, wqt_ref, wk_ref,
                       wvt_ref, *rest):
    (qdt_ref, kd_ref, vdt_ref, qmt_ref, km_ref, vmt_ref,
     kdf_ref, vdf_ref, ckvf_ref, kpef_ref) = _ride_along_casts(rest, 10)
    pr, qd_t, vd_t, qm_t, vm_t, k_d, ckv_n, kpe, k_nope = _att_in_body(
        x_ref, mods_ref, wt_ref, wr_ref, qg_ref, kvg_row_ref, kvg_col_ref, wqt_ref, wk_ref, wvt_ref)
    qdt_ref[...] = qd_t.astype(BF16)
    kd_ref[...] = k_d.astype(BF16)
    vdt_ref[0] = vd_t.astype(BF16)
    qmt_ref[...] = qm_t.astype(BF16)
    km_ref[...] = _mla_keys(k_nope, kpe)
    vmt_ref[0] = vm_t.astype(BF16)
    v_d = pr[:, _R_VD:_R_VD + DIFF_V]
    for h in range(H_DIFF):
        kdf_ref[:, h, :] = k_d[:, h * LANES:(h + 1) * LANES]
        vdf_ref[:, h, :] = v_d[:, h * LANES:(h + 1) * LANES]
    ckvf_ref[...] = ckv_n
    kpef_ref[...] = kpe[:, QK_NOPE:QK_NOPE + QK_ROPE]


def _att_in_lat_kernel(x_ref, mods_ref, wt_ref, wr_ref, qg_ref, kvg_row_ref, kvg_col_ref, wqt_ref, wk_ref,
                       wvt_ref, tabc_ref, tabt_ref,
                       qdut_ref, qdrt_ref, kdr_ref, vdt_ref, qmut_ref, qmrt_ref, kmr_ref, vmt_ref):
    _, qd_t, vd_t, qm_t, vm_t, k_d, ckv_n, kpe, k_nope = _att_in_body(
        x_ref, mods_ref, wt_ref, wr_ref, qg_ref, kvg_row_ref, kvg_col_ref, wqt_ref, wk_ref, wvt_ref)
    qdut_ref[...] = qd_t.astype(BF16)
    qdrt_ref[...] = _rope_chan_wide(qd_t, tabc_ref[0], tabc_ref[1], tabc_ref[2], DH_DIFF // 4).astype(BF16)
    kdr_ref[...] = _rope_tok_wide(k_d, tabt_ref[0], tabt_ref[1], tabt_ref[2], DH_DIFF // 4).astype(BF16)
    vdt_ref[0] = vd_t.astype(BF16)
    qmut_ref[...] = qm_t.astype(BF16)
    qmrt_ref[...] = _rope_chan_wide(qm_t, tabc_ref[3], tabc_ref[4], tabc_ref[5], QK_ROPE // 4).astype(BF16)
    kpe_r = _rope_tok(kpe, tabt_ref[3], tabt_ref[4], tabt_ref[5], QK_ROPE // 4)
    kmr_ref[...] = _mla_keys(k_nope, kpe_r)
    vmt_ref[0] = vm_t.astype(BF16)


def _att_in(x2d, mods, mod_row_fn, wts, tables, tm, cast=None):
    t = x2d.shape[0]
    nt = t // tm
    tok = lambda n: pl.BlockSpec((tm, n), lambda i: (i, 0))
    chan = lambda n: pl.BlockSpec((n, tm), lambda i: (0, i))
    chunk = lambda n: pl.BlockSpec((1, n, tm), lambda i: (i, 0, 0))
    in_specs = [tok(D_MODEL), pl.BlockSpec((1, 1, 6, D_MODEL), lambda i: (0, mod_row_fn(i), 0, 0))]
    in_specs += [_const_spec(w.shape) for w in wts]
    s_tok = lambda n, dt=BF16: jax.ShapeDtypeStruct((t, n), dt)
    s_chan = lambda n: jax.ShapeDtypeStruct((n, t), BF16)
    s_chunk = lambda n: jax.ShapeDtypeStruct((nt, n, tm), BF16)
    if tables is None:
        heads = pl.BlockSpec((tm, H_DIFF, 2 * DH_DIFF), lambda i: (i, 0, 0))
        s_heads = jax.ShapeDtypeStruct((t, H_DIFF, 2 * DH_DIFF), F32)
        out_shape = [s_chan(DIFF_QK), s_tok(DIFF_QK), s_chunk(DIFF_V), s_chan(MLA_WIDE), s_tok(MLA_WIDE),
                     s_chunk(H_MLA * V_MLA),
                     s_heads, s_heads, s_tok(KV_LORA, F32), s_tok(QK_ROPE, F32)]
        out_specs = [chan(DIFF_QK), tok(DIFF_QK), chunk(DIFF_V), chan(MLA_WIDE), tok(MLA_WIDE),
                     chunk(H_MLA * V_MLA), heads, heads, tok(KV_LORA), tok(QK_ROPE)]
        c_in, c_out, c_shape, c_args = _cast_specs(cast, nt)
        in_specs, out_specs, out_shape = in_specs + c_in, out_specs + c_out, out_shape + c_shape
        kern, args, name = _att_in_ctx_kernel, tuple(c_args), "att_in_ctx"
    else:
        tab_c, tab_t = tables
        t_lat = tab_t.shape[1]
        in_specs += [pl.BlockSpec((6, LANES, tm), lambda i: (0, 0, i % (t_lat // tm))),
                     pl.BlockSpec((6, tm, LANES), lambda i: (0, i % (t_lat // tm), 0))]
        out_shape = [s_chan(DIFF_QK), s_chan(DIFF_QK), s_tok(DIFF_QK), s_chunk(DIFF_V),
                     s_chan(MLA_WIDE), s_chan(MLA_WIDE), s_tok(MLA_WIDE), s_chunk(H_MLA * V_MLA)]
        out_specs = [chan(DIFF_QK), chan(DIFF_QK), tok(DIFF_QK), chunk(DIFF_V),
                     chan(MLA_WIDE), chan(MLA_WIDE), tok(MLA_WIDE), chunk(H_MLA * V_MLA)]
        kern, args, name = _att_in_lat_kernel, (tab_c, tab_t), "att_in_lat"
    return pl.pallas_call(
        kern, grid=(nt,), in_specs=in_specs, out_specs=out_specs, out_shape=out_shape,
        compiler_params=_params("arbitrary"), name=name,
    )(x2d, mods, *wts, *args)


def _cache_kv_kernel(ckv_ref, kpe_ref, wk_ref, wvt_ref, km_ref, vmt_ref):
    ckv_b = ckv_ref[...].astype(BF16)
    km_ref[...] = _mla_keys(_dot(ckv_b, wk_ref[...]), kpe_ref[...])
    past = vmt_ref.shape[2]
    vt = _dot_nt(wvt_ref[...], ckv_b)
    for b in range(vmt_ref.shape[0]):
        vmt_ref[b] = vt[:, b * past:(b + 1) * past].astype(BF16)


def _cache_kv(ckv2d, kpe_slab2d, wk, wvt, bl):
    t = ckv2d.shape[0]
    full = lambda a: pl.BlockSpec(a.shape, lambda i: (0,) * a.ndim)
    return pl.pallas_call(
        _cache_kv_kernel, grid=(1,),
        in_specs=[full(ckv2d), full(kpe_slab2d), full(wk), full(wvt)],
        out_specs=[pl.BlockSpec((t, MLA_WIDE), lambda i: (0, 0)),
                   pl.BlockSpec((bl, H_MLA * V_MLA, t // bl), lambda i: (0, 0, 0))],
        out_shape=[jax.ShapeDtypeStruct((t, MLA_WIDE), BF16),
                   jax.ShapeDtypeStruct((bl, H_MLA * V_MLA, t // bl), BF16)],
        compiler_params=_params("arbitrary"), name="cache_kv",
    )(ckv2d, kpe_slab2d, wk, wvt)


def _softmax_pv(s, vt, state, stale_max):
    m_c = jnp.max(s, axis=0, keepdims=True)
    if state is None:
        p = jnp.exp2(s - m_c)
        return m_c, jnp.sum(p, axis=0, keepdims=True), _dot(vt, p.astype(BF16)), jnp.zeros_like(m_c)
    m, l, acc, rise = state
    m_new = jnp.maximum(m, m_c)
    alpha = jnp.exp2(m - m_new)
    if stale_max:
        p = jnp.exp2(s - m)
        l = alpha * (l + jnp.sum(p, axis=0, keepdims=True))
        return m_new, l, alpha * (acc + _dot(vt, p.astype(BF16))), jnp.maximum(rise, m_c - m)
    p = jnp.exp2(s - m_new)
    l = alpha * l + jnp.sum(p, axis=0, keepdims=True)
    return m_new, l, alpha * acc + _dot(vt, p.astype(BF16)), rise


def _attend(first, rest, chunks_per_round=1, stale_max=False):
    items = [(si, qt, (lambda k=k: k), (lambda vt=vt: vt)) for si, (qt, k, vt) in enumerate(first)]
    if rest is not None:
        keys = KEY_CHUNK * chunks_per_round
        for c in range(rest[0][2].shape[0] // chunks_per_round):
            for si, (qt, k_ref, vt_ref) in enumerate(rest):
                items.append((si, qt, (lambda r=k_ref, c=c: r[c * keys:(c + 1) * keys, :]),
                              (lambda r=vt_ref, c=c: jnp.concatenate(
                                  [r[c * chunks_per_round + i] for i in range(chunks_per_round)], axis=1))))
    ns = len(first)
    groups = [items[i:i + ns] for i in range(0, len(items), ns)]
    scores = lambda grp: [_dot(k_fn(), qt) for (_, qt, k_fn, _) in grp]
    state = [None] * ns
    s_next = scores(groups[0])
    for g, grp in enumerate(groups):
        s_cur = s_next
        if g + 1 < len(groups):
            s_next = scores(groups[g + 1])
        for s, (si, _, _, vt_fn) in zip(s_cur, grp):
            state[si] = _softmax_pv(s, vt_fn(), state[si], stale_max)
    rise = functools.reduce(jnp.maximum, [st[3] for st in state])
    return [acc / l for (_, l, acc, _) in state], rise


def _attend_rolled(first, rest):
    state = tuple(_softmax_pv(_dot(k, qt), vt, None, False) for qt, k, vt in first)

    def body(c, state):
        off = pl.multiple_of(c * KEY_CHUNK, KEY_CHUNK)
        return tuple(_softmax_pv(_dot(k_ref[pl.ds(off, KEY_CHUNK), :], qt), vt_ref[c], st, False)
                     for (qt, k_ref, vt_ref), st in zip(rest, state))

    state = lax.fori_loop(0, rest[0][2].shape[0], body, state)
    return [acc / l for (_, l, acc, _) in state]


def _attend_guarded(first, rest, chunks_per_round, finish):
    o, rise = _attend(first, rest, chunks_per_round, stale_max=True)
    finish(o)

    @pl.when(jnp.max(rise) > STALE_MAX_LIMIT)
    def _():
        finish(_attend_rolled(first, rest))


def _split_maps(qt):
    row = lax.broadcasted_iota(jnp.int32, qt.shape, 0)
    zero = jnp.zeros_like(qt)
    return jnp.where(row < DH_DIFF, qt, zero), jnp.where(row >= DH_DIFF, qt, zero)


def _lambda(lam_ref, lam_init):
    lv = lam_ref[...]
    a = jnp.sum(lv[0:1] * lv[1:2], axis=-1, keepdims=True)
    b = jnp.sum(lv[2:3] * lv[3:4], axis=-1, keepdims=True)
    return jnp.exp(a) - jnp.exp(b) + lam_init


def _diff_finish(o1_t, o2_t, lam_ref, g_ref, lam_init):
    o = (o1_t - _lambda(lam_ref, lam_init) * o2_t).T
    return (_rms_norm(o, g_ref[...]) * (1.0 - lam_init)).astype(BF16)


def _mla_finish(oa_t, ob_t):
    return jnp.concatenate([oa_t, ob_t], axis=0).T.astype(BF16)


def _diff_ctx_kernel(qt_ref, k_ref, vt_ref, lam_ref, g_ref, *rest, lam_init):
    out_ref = _ride_along_casts(rest, 1)[0]
    t = qt_ref.shape[1] // CTX_SEQS
    first = []
    for q in range(CTX_SEQS):
        ts = slice(q * t, (q + 1) * t)
        for h in range(H_DIFF):
            sl = slice(h * LANES, (h + 1) * LANES)
            k, vt = k_ref[ts, sl], vt_ref[0, sl, ts]
            first += [(qm, k, vt) for qm in _split_maps(qt_ref[sl, ts])]
    o, _ = _attend(first, None)
    n = 2 * H_DIFF
    for q in range(CTX_SEQS):
        out_ref[q * t:(q + 1) * t, :] = jnp.concatenate(
            [_diff_finish(o[q * n + 2 * h], o[q * n + 2 * h + 1], lam_ref, g_ref, lam_init) for h in range(H_DIFF)],
            axis=1)


def _diff_lat_kernel(qut_ref, qrt_ref, kc_ref, vc_ref, k_ref, vt_ref, lam_ref, g_ref, out_ref, *, lam_init):
    kc, vct = kc_ref[...].astype(BF16), vc_ref[...].T.astype(BF16)
    first = [(q, kc, vct) for q in _split_maps(qut_ref[...])]
    rest = [(q, k_ref, vt_ref) for q in _split_maps(qrt_ref[...])]

    def finish(o):
        out_ref[...] = _diff_finish(o[0], o[1], lam_ref, g_ref, lam_init)

    _attend_guarded(first, rest, DIFF_CHUNKS_PER_ROUND, finish)


def _mla_ctx_kernel(qt_ref, k_ref, vt_ref, *rest):
    out_ref = _ride_along_casts(rest, 1)[0]
    t = qt_ref.shape[1] // CTX_SEQS
    first = []
    for q in range(CTX_SEQS):
        ts = slice(q * t, (q + 1) * t)
        for h in range(H_MLA):
            sl = slice(h * MLA_SLAB, (h + 1) * MLA_SLAB)
            first.append((qt_ref[sl, ts], k_ref[ts, sl], vt_ref[0, h * V_MLA:(h + 1) * V_MLA, ts]))
    o, _ = _attend(first, None)
    for q in range(CTX_SEQS):
        out_ref[q * t:(q + 1) * t, :] = jnp.concatenate(
            [_mla_finish(o[q * H_MLA + 2 * j], o[q * H_MLA + 2 * j + 1]) for j in range(H_MLA // 2)], axis=1)


def _mla_lat_kernel(qua_ref, qub_ref, qra_ref, qrb_ref, kca_ref, kcb_ref, vcta_ref, vctb_ref, ka_ref, kb_ref,
                    vta_ref, vtb_ref, out_ref):
    first = [(qua_ref[...], kca_ref[...], vcta_ref[0]), (qub_ref[...], kcb_ref[...], vctb_ref[0])]
    rest = [(qra_ref[...], ka_ref, vta_ref), (qrb_ref[...], kb_ref, vtb_ref)]

    def finish(o):
        out_ref[...] = _mla_finish(o[0], o[1])

    _attend_guarded(first, rest, MLA_CHUNKS_PER_ROUND, finish)


def _attention_ctx(streams, lam4, diff_g, lam_init, b, t, cast):
    qd_t, kd, vd_t, qm_t, km, vm_t = streams
    per_chunk = vd_t.shape[2] // t // CTX_SEQS
    t, b = t * CTX_SEQS, b // CTX_SEQS
    sem = _params("arbitrary")
    qt = lambda w: pl.BlockSpec((w, t), lambda i: (0, i))
    keys = lambda w: pl.BlockSpec((t, w), lambda i: (i, 0))
    vals = lambda w: pl.BlockSpec((1, w, t), lambda i: (i // per_chunk, 0, i % per_chunk))
    casts = [_cast_specs((cast[0], [w]), b) for w in cast[1]]
    o_d, w_a = pl.pallas_call(
        functools.partial(_diff_ctx_kernel, lam_init=lam_init),
        grid=(b,),
        in_specs=[qt(DIFF_QK), keys(DIFF_QK), vals(DIFF_V),
                  pl.BlockSpec(lam4.shape, lambda i: (0, 0)),
                  pl.BlockSpec(diff_g.shape, lambda i: (0, 0))] + casts[0][0],
        out_specs=[keys(DIFF_V)] + casts[0][1],
        out_shape=[jax.ShapeDtypeStruct((b * t, DIFF_V), BF16)] + casts[0][2],
        compiler_params=sem, name="diff_attn_ctx",
    )(qd_t, kd, vd_t, lam4, diff_g, *casts[0][3])
    o_m, w_b = pl.pallas_call(
        _mla_ctx_kernel,
        grid=(b,),
        in_specs=[qt(MLA_WIDE), keys(MLA_WIDE), vals(H_MLA * V_MLA)] + casts[1][0],
        out_specs=[keys(H_MLA * V_MLA)] + casts[1][1],
        out_shape=[jax.ShapeDtypeStruct((b * t, H_MLA * V_MLA), BF16)] + casts[1][2],
        compiler_params=sem, name="mla_attn_ctx",
    )(qm_t, km, vm_t, *casts[1][3])
    return o_d, o_m, w_a, w_b


def _attention_lat(streams, cache, lam4, diff_g, lam_init, b, t):
    qdu_t, qdr_t, kdr, vd_t, qmu_t, qmr_t, kmr, vm_t = streams
    kd_c, vd_c, km_c, vmc_t = cache
    past = kd_c.shape[0] // b
    nq = t // Q_TILE
    n_chunks = t // KEY_CHUNK
    sem = _params("arbitrary", "arbitrary", "arbitrary")
    qt = lambda f: pl.BlockSpec((LANES, Q_TILE), lambda i, h, j: (f(h), i * nq + j))
    keys = lambda n, f: pl.BlockSpec((n, LANES), lambda i, h, j: (i, f(h)))
    vals = pl.BlockSpec((n_chunks, LANES, KEY_CHUNK), lambda i, h, j: (i, h, 0))
    out = pl.BlockSpec((Q_TILE, LANES), lambda i, h, j: (i * nq + j, h))
    same, even, odd = (lambda h: h), (lambda h: 2 * h), (lambda h: 2 * h + 1)
    o_d = pl.pallas_call(
        functools.partial(_diff_lat_kernel, lam_init=lam_init),
        grid=(b, H_DIFF, nq),
        in_specs=[qt(same), qt(same), keys(past, same), keys(past, same), keys(t, same), vals,
                  pl.BlockSpec(lam4.shape, lambda i, h, j: (0, 0)),
                  pl.BlockSpec(diff_g.shape, lambda i, h, j: (0, 0))],
        out_specs=out,
        out_shape=jax.ShapeDtypeStruct((b * t, DIFF_V), BF16),
        compiler_params=sem, name="diff_attn_lat",
    )(qdu_t, qdr_t, kd_c, vd_c, kdr, vd_t, lam4, diff_g)
    o_m = pl.pallas_call(
        _mla_lat_kernel,
        grid=(b, H_MLA // 2, nq),
        in_specs=[qt(even), qt(odd), qt(even), qt(odd), keys(past, even), keys(past, odd),
                  pl.BlockSpec((1, V_MLA, past), lambda i, h, j: (i, 2 * h, 0)),
                  pl.BlockSpec((1, V_MLA, past), lambda i, h, j: (i, 2 * h + 1, 0)),
                  keys(t, even), keys(t, odd),
                  pl.BlockSpec((n_chunks, V_MLA, KEY_CHUNK), lambda i, h, j: (i, 2 * h, 0)),
                  pl.BlockSpec((n_chunks, V_MLA, KEY_CHUNK), lambda i, h, j: (i, 2 * h + 1, 0))],
        out_specs=out,
        out_shape=jax.ShapeDtypeStruct((b * t, H_MLA * V_MLA), BF16),
        compiler_params=sem, name="mla_attn_lat",
    )(qmu_t, qmu_t, qmr_t, qmr_t, km_c, km_c, vmc_t, vmc_t, kmr, kmr, vm_t, vm_t)
    return o_d, o_m


def _post_kernel(x_ref, ma_ref, mb_ref, mods_ref, wo_ref, w1_ref, w2_ref, ln_ref, *rest):
    out_ref = _ride_along_casts(rest, 1)[0]
    mods = mods_ref[0, 0]
    g_m, sh_f, sc_f, g_f = mods[2:3], mods[3:4], mods[4:5], mods[5:6]
    ln = ln_ref[...]
    rows = x_ref.shape[0] // POST_GROUPS
    grp = [slice(r * rows, (r + 1) * rows) for r in range(POST_GROUPS)]
    ys = [_dot(jnp.concatenate([ma_ref[g, :], mb_ref[g, :]], axis=1), wo_ref[...]) for g in grp]
    x1s = [_layer_norm(DEEPNORM_ALPHA * x_ref[g, :] + g_m * y, ln[0:1], ln[1:2]) for g, y in zip(grp, ys)]
    fs = []
    for x1 in x1s:
        h = (x1 * (1.0 + sc_f) + sh_f).astype(BF16)
        f = jnp.zeros(x1.shape, F32)
        for c in range(D_FF // FF_CHUNK):
            a = jnp.maximum(_dot(h, w1_ref[:, c * FF_CHUNK:(c + 1) * FF_CHUNK]), 0.0)
            f = f + _dot((a * a).astype(BF16), w2_ref[c * FF_CHUNK:(c + 1) * FF_CHUNK, :])
        fs.append(f)
    for g, x1, f in zip(grp, x1s, fs):
        out_ref[g, :] = _layer_norm(DEEPNORM_ALPHA * x1 + g_f * f, ln[2:3], ln[3:4])


def _post(x2d, mix_a, mix_b, mods, layer, mod_row_fn, wo, w1, w2, ln4, tm, cast=None):
    t = x2d.shape[0]
    tok = lambda n: pl.BlockSpec((tm, n), lambda i: (i, 0))
    c_in, c_out, c_shape, c_args = _cast_specs(cast, t // tm)
    return pl.pallas_call(
        _post_kernel, grid=(t // tm,),
        in_specs=[tok(D_MODEL), tok(mix_a.shape[1]), tok(mix_b.shape[1]),
                  pl.BlockSpec((1, 1, 6, D_MODEL), lambda i: (layer, mod_row_fn(i), 0, 0)),
                  _const_spec(wo.shape), _const_spec(w1.shape), _const_spec(w2.shape),
                  _const_spec(ln4.shape)] + c_in,
        out_specs=[tok(D_MODEL)] + c_out,
        out_shape=[jax.ShapeDtypeStruct((t, D_MODEL), F32)] + c_shape,
        compiler_params=_params("arbitrary"), name=f"post_mlp_l{layer}",
    )(x2d, mix_a, mix_b, mods, wo, w1, w2, ln4, *c_args)


def _conv_in_kernel(x_ref, mods_ref, w_ref, u_ref, pz_ref):
    mods = mods_ref[0, 0]
    sh, sc = mods[0:1], mods[1:2]
    h = (x_ref[...] * (1.0 + sc) + sh).astype(BF16)
    proj = _dot(h, w_ref[...])
    a, gate = proj[:, 0:CONV_CH], proj[:, CONV_CH:2 * CONV_CH]
    u_ref[...] = a * jax.nn.sigmoid(gate)
    pz_ref[...] = proj[:, 2 * CONV_CH:]


def _conv_in(x2d, mods, layer, mod_row_fn, w, tm):
    t = x2d.shape[0]
    tok = lambda n: pl.BlockSpec((tm, n), lambda i: (i, 0))
    return pl.pallas_call(
        _conv_in_kernel, grid=(t // tm,),
        in_specs=[tok(D_MODEL),
                  pl.BlockSpec((1, 1, 6, D_MODEL), lambda i: (layer, mod_row_fn(i), 0, 0)),
                  _const_spec(w.shape)],
        out_specs=[tok(CONV_CH), tok(POOL_CH)],
        out_shape=[jax.ShapeDtypeStruct((t, CONV_CH), F32), jax.ShapeDtypeStruct((t, POOL_CH), F32)],
        compiler_params=_params("arbitrary"), name="conv_in",
    )(x2d, mods, w)


def _conv_pool_kernel(*refs, n_tiles):
    if n_tiles > 1:
        (u_ref, ul_ref, ur_ref, pz_ref, pl_ref, pr_ref, cw_ref, cp_ref, wp_ref, ps_ref, inv_ref,
         uo_ref, do_ref, ubuf, pbuf, ushift, cbuf) = refs
    else:
        u_ref, pz_ref, cw_ref, cp_ref, wp_ref, ps_ref, inv_ref, uo_ref, do_ref, ubuf, pbuf, ushift, cbuf = refs
    tm = u_ref.shape[1]
    j = pl.program_id(1)
    halo_zero = jnp.zeros((CONV_HALO, CONV_CH), F32)
    for buf, mid, sides in ((ubuf, u_ref, (ul_ref, ur_ref) if n_tiles > 1 else None),
                            (pbuf, pz_ref, (pl_ref, pr_ref) if n_tiles > 1 else None)):
        buf[CONV_HALO:CONV_HALO + tm, :] = mid[0]
        if sides is None:
            buf[0:CONV_HALO, :] = halo_zero
            buf[CONV_HALO + tm:, :] = halo_zero
        else:
            buf[0:CONV_HALO, :] = jnp.where(j > 0, sides[0][0], halo_zero)
            buf[CONV_HALO + tm:, :] = jnp.where(j < n_tiles - 1, sides[1][0], halo_zero)

    rows_sh = ushift.shape[1]
    for b in range(SUBLANES):
        ushift[b] = ubuf[b:b + rows_sh, :]
    base = CONV_HALO - CONV_WIDTH // 2
    cp = cp_ref[...]

    def strip(i, carry):
        r0 = pl.multiple_of(i * CONV_STRIP, CONV_STRIP)
        acc = jnp.zeros((CONV_STRIP, CONV_CH), F32)
        for k in range(CONV_WIDTH):
            a, b = divmod(base + k, SUBLANES)
            w = jnp.concatenate([cw_ref[k]] * (CONV_STRIP // SUBLANES), axis=0)
            acc = acc + ushift[b, pl.ds(r0 + a * SUBLANES, CONV_STRIP), :] * w
        cbuf[pl.ds(r0, CONV_STRIP), :] = acc
        return carry

    lax.fori_loop(0, tm // CONV_STRIP, strip, 0)
    z = _layer_norm(cbuf[...] + cp[0:1], cp[1:2], cp[2:3])
    uo_ref[0] = (z * jax.nn.sigmoid(z)).astype(uo_ref.dtype)

    outs = []
    for g, w in enumerate(POOL_WINDOWS):
        cols = slice(g * POOL_GC, (g + 1) * POOL_GC)
        s = jnp.zeros((tm, POOL_GC), F32)
        for d in range(-(w // 2), w // 2):
            s = s + pbuf[CONV_HALO + d:CONV_HALO + d + tm, cols]
        dgrp = s * inv_ref[:, cols] - pz_ref[0][:, cols]
        outs.append(_dot(dgrp.astype(BF16), wp_ref[g]))
    do_ref[0] = (jnp.concatenate(outs, axis=1) * ps_ref[...]).astype(do_ref.dtype)


def _pool_inv_counts(t):
    pos = np.arange(t)
    cols = []
    for w in POOL_WINDOWS:
        cnt = np.minimum(pos + w // 2 - 1, t - 1) - np.maximum(pos - w // 2, 0) + 1
        cols.append(np.repeat((1.0 / cnt)[:, None], POOL_GC, axis=1))
    return jnp.asarray(np.concatenate(cols, axis=1).astype(np.float32))


def _conv_pool(u2d, pz2d, b, t, tm, cw, cp, wp, ps):
    n_tiles = t // tm
    u3, p3 = u2d.reshape(b, t, CONV_CH), pz2d.reshape(b, t, POOL_CH)
    mid = pl.BlockSpec((1, tm, CONV_CH), lambda i, j: (i, j, 0))
    r = tm // CONV_HALO
    left = pl.BlockSpec((1, CONV_HALO, CONV_CH), lambda i, j: (i, jnp.maximum(j * r - 1, 0), 0))
    right = pl.BlockSpec((1, CONV_HALO, CONV_CH),
                         lambda i, j: (i, jnp.minimum((j + 1) * r, t // CONV_HALO - 1), 0))
    const = lambda a: pl.BlockSpec(a.shape, lambda i, j: (0,) * a.ndim)
    if n_tiles > 1:
        in_specs = [mid, left, right, mid, left, right]
        args = (u3, u3, u3, p3, p3, p3)
    else:
        in_specs = [mid, mid]
        args = (u3, p3)
    inv = _pool_inv_counts(t)
    in_specs += [const(cw), const(cp), const(wp), const(ps), pl.BlockSpec((tm, POOL_CH), lambda i, j: (j, 0))]
    uo, do = pl.pallas_call(
        functools.partial(_conv_pool_kernel, n_tiles=n_tiles),
        grid=(b, n_tiles), in_specs=in_specs,
        out_specs=[mid, mid],
        out_shape=[jax.ShapeDtypeStruct((b, t, CONV_CH), BF16), jax.ShapeDtypeStruct((b, t, POOL_CH), BF16)],
        scratch_shapes=[pltpu.VMEM((tm + 2 * CONV_HALO, CONV_CH), F32),
                        pltpu.VMEM((tm + 2 * CONV_HALO, POOL_CH), F32),
                        pltpu.VMEM((SUBLANES, tm + 2 * CONV_HALO - SUBLANES, CONV_CH), F32),
                        pltpu.VMEM((tm, CONV_CH), F32)],
        compiler_params=_params("arbitrary", "arbitrary"),
        name=f"conv_pool_{'lat' if n_tiles > 1 else 'ctx'}",
    )(*args, cw, cp, wp, ps, inv)
    return uo.reshape(b * t, CONV_CH), do.reshape(b * t, POOL_CH)


def _att_in_weights(w_att_in, with_v_tok):
    o = 2 * DIFF_QK + DIFF_V
    q_d, k_d, v_d = w_att_in[:, :DIFF_QK], w_att_in[:, DIFF_QK:2 * DIFF_QK], w_att_in[:, 2 * DIFF_QK:o]
    cq, ckv = w_att_in[:, o:o + Q_LORA], w_att_in[:, o + Q_LORA:o + Q_LORA + KV_LORA]
    kpe = jnp.pad(w_att_in[:, o + Q_LORA + KV_LORA:], _ROPE_LANE_PAD)
    w_t = jnp.concatenate([q_d, v_d, cq, ckv], axis=1).T.astype(BF16)
    w_r = jnp.concatenate([k_d, ckv, kpe] + ([v_d] if with_v_tok else []), axis=1).astype(BF16)
    return w_t, w_r


def _mla_weights(w_uq, w_ukv):
    wq = w_uq.reshape(Q_LORA, H_MLA, QK_NOPE + QK_ROPE)
    wq = jnp.pad(wq, ((0, 0), (0, 0), (0, MLA_SLAB - QK_NOPE - QK_ROPE))).reshape(Q_LORA, MLA_WIDE)
    wkv = w_ukv.reshape(KV_LORA, H_MLA, QK_NOPE + V_MLA)
    wk = jnp.pad(wkv[:, :, :QK_NOPE], ((0, 0), (0, 0), (0, MLA_SLAB - QK_NOPE))).reshape(KV_LORA, MLA_WIDE)
    wv = wkv[:, :, QK_NOPE:].reshape(KV_LORA, H_MLA * V_MLA)
    return wq.T.astype(BF16), wk.astype(BF16), wv.T.astype(BF16)


def kernel(x_prompt, x_sample, cache_diff_k, cache_diff_v, cache_mla_ckv, cache_mla_krope, c, c_ctx,
           w_ada, b_ada, ln_mix_g, ln_mix_b, ln_mlp_g, ln_mlp_b, w_mlp_in, w_mlp_out,
           w_att_in, w_uq, w_ukv, q_norm_g, kv_norm_g, lam_q1, lam_k1, lam_q2, lam_k2, diff_norm_g, w_att_out,
           w_conv_in, conv_w, conv_b, conv_norm_g, conv_norm_b, w_pool, pool_scale, w_conv_out):
    bc, tc, d = x_prompt.shape
    bl, tl, _ = x_sample.shape
    past = cache_diff_k.shape[2]

    cond8 = jnp.zeros((8, d), F32).at[0].set(c_ctx).at[1:1 + bl].set(c)
    mods = _ada_mods(cond8, w_ada, b_ada).reshape(DEPTH, 8, 6, d)
    ctx_row = lambda i: 0
    lat_row = lambda i: 1 + i // (tl // TOKEN_TILE)

    wqt, wk, wvt = _mla_weights(w_uq[0], w_ukv[0])
    norm_w = (q_norm_g[0].reshape(Q_LORA, 1), kv_norm_g[0].reshape(1, KV_LORA), kv_norm_g[0].reshape(KV_LORA, 1))
    att_w = lambda with_v: _att_in_weights(w_att_in[0], with_v) + norm_w + (wqt, wk, wvt)
    lam4 = jnp.stack([lam_q1[0], lam_k1[0], lam_q2[0], lam_k2[0]])
    diff_g = diff_norm_g[0].reshape(1, 2 * DH_DIFF)
    lam_init = 0.8 - 0.6 * math.exp(-0.3 * 0)
    ln4 = [jnp.stack([ln_mix_g[l], ln_mix_b[l], ln_mlp_g[l], ln_mlp_b[l]]) for l in range(DEPTH)]
    mlp_w = [w_mlp_in, w_mlp_out]
    conv_p = jnp.stack([conv_b[0], conv_norm_g[0], conv_norm_b[0]])
    conv_w8 = jnp.broadcast_to(conv_w[0][:, None, :], (CONV_WIDTH, SUBLANES, CONV_CH))
    wp = w_pool[0].astype(BF16)
    ps = pool_scale[0].reshape(1, POOL_CH)

    xp = x_prompt.reshape(bc * tc, d)
    outs = _att_in(xp, mods, ctx_row, att_w(True), None, TOKEN_TILE,
                   cast=(0, mlp_w + [w_att_out, w_conv_in, w_conv_out]))
    kd_f, vd_f, ckv_f, kpe_f = outs[6:10]
    w1_0, w2_0, w_ao, w_ci, w_co = outs[10:]
    o_d, o_m, w1_1, w2_1 = _attention_ctx(outs[:6], lam4, diff_g, lam_init, bc, tc, cast=(1, mlp_w))
    xp, = _post(xp, o_d, o_m, mods, 0, ctx_row, w_ao, w1_0, w2_0, ln4[0], TOKEN_TILE)
    u, pz = _conv_in(xp, mods, 1, ctx_row, w_ci, TOKEN_TILE)
    uo, do = _conv_pool(u, pz, bc, tc, tc, conv_w8, conv_p, wp, ps)
    xp, = _post(xp, uo, do, mods, 1, ctx_row, w_co, w1_1, w2_1, ln4[1], TOKEN_TILE)

    xs = x_sample.reshape(bl * tl, d)
    outs = _att_in(xs, mods, lat_row, att_w(False), _rope_tables(tl), TOKEN_TILE)
    km_c, vmc_t = _cache_kv(cache_mla_ckv[:, 0].reshape(bl * past, KV_LORA),
                            jnp.pad(cache_mla_krope[:, 0].reshape(bl * past, QK_ROPE), _ROPE_LANE_PAD),
                            wk, wvt, bl)
    cache = (cache_diff_k[:, 0].reshape(bl * past, DIFF_QK), cache_diff_v[:, 0].reshape(bl * past, DIFF_V),
             km_c, vmc_t)
    o_d, o_m = _attention_lat(outs, cache, lam4, diff_g, lam_init, bl, tl)
    xs, = _post(xs, o_d, o_m, mods, 0, lat_row, w_ao, w1_0, w2_0, ln4[0], TOKEN_TILE)
    u, pz = _conv_in(xs, mods, 1, lat_row, w_ci, TOKEN_TILE)
    uo, do = _conv_pool(u, pz, bl, tl, CONV_TILE, conv_w8, conv_p, wp, ps)
    xs, = _post(xs, uo, do, mods, 1, lat_row, w_co, w1_1, w2_1, ln4[1], TOKEN_TILE)

    return (xp.reshape(bc, tc, d), xs.reshape(bl, tl, d),
            kd_f.reshape(bc, 1, tc, H_DIFF, 2 * DH_DIFF), vd_f.reshape(bc, 1, tc, H_DIFF, 2 * DH_DIFF),
            ckv_f.reshape(bc, 1, tc, KV_LORA), kpe_f.reshape(bc, 1, tc, QK_ROPE))
```

```python
import functools
import math

import jax
import jax.numpy as jnp
import numpy as np
from jax import lax
from jax.experimental import pallas as pl
from jax.experimental.pallas import tpu as pltpu

D_MODEL = 1024
DEPTH = 2
GRID_W = 64
H_DIFF = 4
DH_DIFF = 64
DIFF_QK = H_DIFF * 2 * DH_DIFF
DIFF_V = H_DIFF * 2 * DH_DIFF
H_MLA = 8
Q_LORA = 256
KV_LORA = 128
QK_NOPE = 64
QK_ROPE = 32
V_MLA = 64
ATT_IN = 2 * DIFF_QK + DIFF_V + Q_LORA + KV_LORA + QK_ROPE
CONV_CH = 512
CONV_WIDTH = 31
POOL_CH = 512
POOL_WINDOWS = (2, 4, 8, 16)
POOL_GC = POOL_CH // len(POOL_WINDOWS)
CONV_IN = 2 * CONV_CH + POOL_CH
D_FF = 4 * D_MODEL
ROPE_BASE = 10000.0
NORM_EPS = 1e-5
DEEPNORM_ALPHA = (2 * DEPTH) ** 0.25
LOG2E = math.log2(math.e)

LANES = 128
SUBLANES = 8
MLA_SLAB = LANES
MLA_WIDE = H_MLA * MLA_SLAB
_ROPE_LANE_PAD = ((0, 0), (QK_NOPE, MLA_SLAB - QK_NOPE - QK_ROPE))
VMEM_LIMIT = 56 * 1024 * 1024

TOKEN_TILE = 512
KEY_CHUNK = TOKEN_TILE
Q_TILE = 1024
CTX_SEQS = 2
DIFF_CHUNKS_PER_ROUND = 4
MLA_CHUNKS_PER_ROUND = 4
STALE_MAX_LIMIT = 64.0
CONV_TILE = 512
CONV_HALO = 16
CONV_STRIP = 32
FF_CHUNK = 1024
POST_GROUPS = 2

F32 = jnp.float32
BF16 = jnp.bfloat16


def _dot(a, b):
    return jnp.dot(a, b, preferred_element_type=F32)


def _dot_nt(a, b):
    return lax.dot_general(a, b, (((1,), (1,)), ((), ())), preferred_element_type=F32)


def _layer_norm(x, g, b):
    mu = jnp.mean(x, axis=-1, keepdims=True)
    xc = x - mu
    var = jnp.mean(xc * xc, axis=-1, keepdims=True)
    return xc * lax.rsqrt(var + NORM_EPS) * g + b


def _rms_norm(x, g, axis=-1):
    ms = jnp.mean(x * x, axis=axis, keepdims=True)
    return x * lax.rsqrt(ms + NORM_EPS) * g


def _const_spec(shape):
    nd = len(shape)
    return pl.BlockSpec(shape, lambda *_: (0,) * nd, pipeline_mode=pl.Buffered(1))


def _cast_specs(cast, n_steps):
    if cast is None:
        return [], [], [], []
    layer, ws = cast
    in_specs = [pl.BlockSpec((1, w.shape[1] // n_steps, w.shape[2]), lambda i: (layer, i, 0)) for w in ws]
    out_specs = [pl.BlockSpec((w.shape[1] // n_steps, w.shape[2]), lambda i: (i, 0)) for w in ws]
    out_shape = [jax.ShapeDtypeStruct(w.shape[1:], BF16) for w in ws]
    return in_specs, out_specs, out_shape, list(ws)


def _ride_along_casts(rest, n_out):
    n_cast = (len(rest) - n_out) // 2
    for src, dst in zip(rest[:n_cast], rest[n_cast + n_out:]):
        dst[...] = src[0].astype(BF16)
    return rest[n_cast:n_cast + n_out]


def _params(*sem):
    return pltpu.CompilerParams(dimension_semantics=sem, vmem_limit_bytes=VMEM_LIMIT)


ADA_TILE = 3072


def _ada_kernel(cond_ref, w_ref, b_ref, out_ref):
    cond = cond_ref[...]
    act = (cond * jax.nn.sigmoid(cond)).astype(BF16)
    out_ref[0] = _dot(act, w_ref[0].astype(BF16)) + b_ref[0]


def _ada_mods(cond8, w_ada, b_ada):
    n = 6 * D_MODEL
    return pl.pallas_call(
        _ada_kernel,
        grid=(DEPTH, n // ADA_TILE),
        in_specs=[
            pl.BlockSpec((8, D_MODEL), lambda l, j: (0, 0)),
            pl.BlockSpec((1, D_MODEL, ADA_TILE), lambda l, j: (l, 0, j)),
            pl.BlockSpec((1, 1, ADA_TILE), lambda l, j: (l, 0, j)),
        ],
        out_specs=pl.BlockSpec((1, 8, ADA_TILE), lambda l, j: (l, 0, j)),
        out_shape=jax.ShapeDtypeStruct((DEPTH, 8, n), F32),
        compiler_params=_params("arbitrary", "arbitrary"),
        name="ada_mods",
    )(cond8, w_ada, b_ada.reshape(DEPTH, 1, n))


def _rope_tables(t_lat):
    pos = np.arange(t_lat)
    row = (pos // GRID_W).astype(np.float64)
    col = (pos % GRID_W).astype(np.float64)

    def tables(kinds):
        cos = np.ones((t_lat, LANES))
        sa = np.zeros((t_lat, LANES))
        sb = np.zeros((t_lat, LANES))
        for lane, kind in enumerate(kinds):
            if kind is None:
                continue
            axis, half, j, upper = kind
            ang = (row if axis == 0 else col) * ROPE_BASE ** (-float(j) / half)
            cos[:, lane] = np.cos(ang)
            if upper:
                sb[:, lane] = np.sin(ang)
            else:
                sa[:, lane] = -np.sin(ang)
        return [cos, sa, sb]

    def rot_kinds(n):
        half = n // 4
        kinds = []
        for i in range(n):
            axis, r = divmod(i, n // 2)
            kinds.append((axis, half, r % half, r >= half))
        return kinds

    diff = rot_kinds(DH_DIFF) * 2
    mla_q = [None] * QK_NOPE + rot_kinds(QK_ROPE) + [None] * (LANES - QK_NOPE - QK_ROPE)
    chan = np.stack([t.T for t in tables(diff) + tables(mla_q)]).astype(np.float32)
    tok = np.stack(tables(diff) + tables(mla_q)).astype(np.float32)
    return jnp.asarray(chan), jnp.asarray(tok)


def _rope_tok(x, cos, sa, sb, shift):
    return x * cos + pltpu.roll(x, LANES - shift, 1) * sa + pltpu.roll(x, shift, 1) * sb


def _rope_tok_wide(x, cos, sa, sb, shift):
    n = x.shape[1] // LANES
    return jnp.concatenate(
        [_rope_tok(x[:, i * LANES:(i + 1) * LANES], cos, sa, sb, shift) for i in range(n)], axis=1)


def _rope_chan(x, cos, sa, sb, shift):
    up = jnp.concatenate([x[shift:], x[:shift]], axis=0)
    down = jnp.concatenate([x[-shift:], x[:-shift]], axis=0)
    return x * cos + up * sa + down * sb


def _rope_chan_wide(x, cos, sa, sb, shift):
    n = x.shape[0] // LANES
    return jnp.concatenate(
        [_rope_chan(x[i * LANES:(i + 1) * LANES], cos, sa, sb, shift) for i in range(n)], axis=0)


_T_QD, _T_VD, _T_CQ, _T_CKV = 0, DIFF_QK, DIFF_QK + DIFF_V, DIFF_QK + DIFF_V + Q_LORA
_T_ROWS = _T_CKV + KV_LORA
_R_KD, _R_CKV, _R_KPE, _R_VD = 0, DIFF_QK, DIFF_QK + KV_LORA, DIFF_QK + KV_LORA + LANES

DIFF_QSCALE = DH_DIFF ** -0.5 * LOG2E
MLA_QSCALE = (QK_NOPE + QK_ROPE) ** -0.5 * LOG2E


def _att_in_body(x_ref, mods_ref, wt_ref, wr_ref, qg_ref, kvg_row_ref, kvg_col_ref, wqt_ref, wk_ref, wvt_ref):
    mods = mods_ref[0, 0]
    sh, sc = mods[0:1], mods[1:2]
    h = (x_ref[...] * (1.0 + sc) + sh).astype(BF16)
    pt = _dot_nt(wt_ref[...], h)
    pr = _dot(h, wr_ref[...])
    qd_t = pt[_T_QD:_T_QD + DIFF_QK] * DIFF_QSCALE
    vd_t = pt[_T_VD:_T_VD + DIFF_V]
    cqn_t = _rms_norm(pt[_T_CQ:_T_CQ + Q_LORA], qg_ref[...], axis=0)
    qm_t = _dot(wqt_ref[...], cqn_t.astype(BF16)) * MLA_QSCALE
    ckvn_t = _rms_norm(pt[_T_CKV:_T_CKV + KV_LORA], kvg_col_ref[...], axis=0)
    vm_t = _dot(wvt_ref[...], ckvn_t.astype(BF16))
    k_d = pr[:, _R_KD:_R_KD + DIFF_QK]
    ckv_n = _rms_norm(pr[:, _R_CKV:_R_CKV + KV_LORA], kvg_row_ref[...])
    kpe = pr[:, _R_KPE:_R_KPE + LANES]
    k_nope = _dot(ckv_n.astype(BF16), wk_ref[...])
    return pr, qd_t, vd_t, qm_t, vm_t, k_d, ckv_n, kpe, k_nope


def _mla_keys(k_nope, kpe_slab):
    return (k_nope + jnp.concatenate([kpe_slab] * H_MLA, axis=1)).astype(BF16)


def _att_in_ctx_kernel(x_ref, mods_ref, wt_ref, wr_ref, qg_ref, kvg_row_ref, kvg_col_ref, wqt_ref, wk_ref,
                       wvt_ref, *rest):
    (qdt_ref, kd_ref, vdt_ref, qmt_ref, km_ref, vmt_ref,
     kdf_ref, vdf_ref, ckvf_ref, kpef_ref) = _ride_along_casts(rest, 10)
    pr, qd_t, vd_t, qm_t, vm_t, k_d, ckv_n, kpe, k_nope = _att_in_body(
        x_ref, mods_ref, wt_ref, wr_ref, qg_ref, kvg_row_ref, kvg_col_ref, wqt_ref, wk_ref, wvt_ref)
    qdt_ref[...] = qd_t.astype(BF16)
    kd_ref[...] = k_d.astype(BF16)
    vdt_ref[0] = vd_t.astype(BF16)
    qmt_ref[...] = qm_t.astype(BF16)
    km_ref[...] = _mla_keys(k_nope, kpe)
    vmt_ref[0] = vm_t.astype(BF16)
    v_d = pr[:, _R_VD:_R_VD + DIFF_V]
    for h in range(H_DIFF):
        kdf_ref[:, h, :] = k_d[:, h * LANES:(h + 1) * LANES]
        vdf_ref[:, h, :] = v_d[:, h * LANES:(h + 1) * LANES]
    ckvf_ref[...] = ckv_n
    kpef_ref[...] = kpe[:, QK_NOPE:QK_NOPE + QK_ROPE]


def _att_in_lat_kernel(x_ref, mods_ref, wt_ref, wr_ref, qg_ref, kvg_row_ref, kvg_col_ref, wqt_ref, wk_ref,
                       wvt_ref, tabc_ref, tabt_ref,
                       qdut_ref, qdrt_ref, kdr_ref, vdt_ref, qmut_ref, qmrt_ref, kmr_ref, vmt_ref):
    _, qd_t, vd_t, qm_t, vm_t, k_d, ckv_n, kpe, k_nope = _att_in_body(
        x_ref, mods_ref, wt_ref, wr_ref, qg_ref, kvg_row_ref, kvg_col_ref, wqt_ref, wk_ref, wvt_ref)
    qdut_ref[...] = qd_t.astype(BF16)
    qdrt_ref[...] = _rope_chan_wide(qd_t, tabc_ref[0], tabc_ref[1], tabc_ref[2], DH_DIFF // 4).astype(BF16)
    kdr_ref[...] = _rope_tok_wide(k_d, tabt_ref[0], tabt_ref[1], tabt_ref[2], DH_DIFF // 4).astype(BF16)
    vdt_ref[0] = vd_t.astype(BF16)
    qmut_ref[...] = qm_t.astype(BF16)
    qmrt_ref[...] = _rope_chan_wide(qm_t, tabc_ref[3], tabc_ref[4], tabc_ref[5], QK_ROPE // 4).astype(BF16)
    kpe_r = _rope_tok(kpe, tabt_ref[3], tabt_ref[4], tabt_ref[5], QK_ROPE // 4)
    kmr_ref[...] = _mla_keys(k_nope, kpe_r)
    vmt_ref[0] = vm_t.astype(BF16)


def _att_in(x2d, mods, mod_row_fn, wts, tables, tm, cast=None):
    t = x2d.shape[0]
    nt = t // tm
    tok = lambda n: pl.BlockSpec((tm, n), lambda i: (i, 0))
    chan = lambda n: pl.BlockSpec((n, tm), lambda i: (0, i))
    chunk = lambda n: pl.BlockSpec((1, n, tm), lambda i: (i, 0, 0))
    in_specs = [tok(D_MODEL), pl.BlockSpec((1, 1, 6, D_MODEL), lambda i: (0, mod_row_fn(i), 0, 0))]
    in_specs += [_const_spec(w.shape) for w in wts]
    s_tok = lambda n, dt=BF16: jax.ShapeDtypeStruct((t, n), dt)
    s_chan = lambda n: jax.ShapeDtypeStruct((n, t), BF16)
    s_chunk = lambda n: jax.ShapeDtypeStruct((nt, n, tm), BF16)
    if tables is None:
        heads = pl.BlockSpec((tm, H_DIFF, 2 * DH_DIFF), lambda i: (i, 0, 0))
        s_heads = jax.ShapeDtypeStruct((t, H_DIFF, 2 * DH_DIFF), F32)
        out_shape = [s_chan(DIFF_QK), s_tok(DIFF_QK), s_chunk(DIFF_V), s_chan(MLA_WIDE), s_tok(MLA_WIDE),
                     s_chunk(H_MLA * V_MLA),
                     s_heads, s_heads, s_tok(KV_LORA, F32), s_tok(QK_ROPE, F32)]
        out_specs = [chan(DIFF_QK), tok(DIFF_QK), chunk(DIFF_V), chan(MLA_WIDE), tok(MLA_WIDE),
                     chunk(H_MLA * V_MLA), heads, heads, tok(KV_LORA), tok(QK_ROPE)]
        c_in, c_out, c_shape, c_args = _cast_specs(cast, nt)
        in_specs, out_specs, out_shape = in_specs + c_in, out_specs + c_out, out_shape + c_shape
        kern, args, name = _att_in_ctx_kernel, tuple(c_args), "att_in_ctx"
    else:
        tab_c, tab_t = tables
        t_lat = tab_t.shape[1]
        in_specs += [pl.BlockSpec((6, LANES, tm), lambda i: (0, 0, i % (t_lat // tm))),
                     pl.BlockSpec((6, tm, LANES), lambda i: (0, i % (t_lat // tm), 0))]
        out_shape = [s_chan(DIFF_QK), s_chan(DIFF_QK), s_tok(DIFF_QK), s_chunk(DIFF_V),
                     s_chan(MLA_WIDE), s_chan(MLA_WIDE), s_tok(MLA_WIDE), s_chunk(H_MLA * V_MLA)]
        out_specs = [chan(DIFF_QK), chan(DIFF_QK), tok(DIFF_QK), chunk(DIFF_V),
                     chan(MLA_WIDE), chan(MLA_WIDE), tok(MLA_WIDE), chunk(H_MLA * V_MLA)]
        kern, args, name = _att_in_lat_kernel, (tab_c, tab_t), "att_in_lat"
    return pl.pallas_call(
        kern, grid=(nt,), in_specs=in_specs, out_specs=out_specs, out_shape=out_shape,
        compiler_params=_params("arbitrary"), name=name,
    )(x2d, mods, *wts, *args)


def _cache_kv_kernel(ckv_ref, kpe_ref, wk_ref, wvt_ref, km_ref, vmt_ref):
    ckv_b = ckv_ref[...].astype(BF16)
    km_ref[...] = _mla_keys(_dot(ckv_b, wk_ref[...]), kpe_ref[...])
    past = vmt_ref.shape[2]
    vt = _dot_nt(wvt_ref[...], ckv_b)
    for b in range(vmt_ref.shape[0]):
        vmt_ref[b] = vt[:, b * past:(b + 1) * past].astype(BF16)


def _cache_kv(ckv2d, kpe_slab2d, wk, wvt, bl):
    t = ckv2d.shape[0]
    full = lambda a: pl.BlockSpec(a.shape, lambda i: (0,) * a.ndim)
    return pl.pallas_call(
        _cache_kv_kernel, grid=(1,),
        in_specs=[full(ckv2d), full(kpe_slab2d), full(wk), full(wvt)],
        out_specs=[pl.BlockSpec((t, MLA_WIDE), lambda i: (0, 0)),
                   pl.BlockSpec((bl, H_MLA * V_MLA, t // bl), lambda i: (0, 0, 0))],
        out_shape=[jax.ShapeDtypeStruct((t, MLA_WIDE), BF16),
                   jax.ShapeDtypeStruct((bl, H_MLA * V_MLA, t // bl), BF16)],
        compiler_params=_params("arbitrary"), name="cache_kv",
    )(ckv2d, kpe_slab2d, wk, wvt)


def _softmax_pv(s, vt, state, stale_max):
    m_c = jnp.max(s, axis=0, keepdims=True)
    if state is None:
        p = jnp.exp2(s - m_c)
        return m_c, jnp.sum(p, axis=0, keepdims=True), _dot(vt, p.astype(BF16)), jnp.zeros_like(m_c)
    m, l, acc, rise = state
    m_new = jnp.maximum(m, m_c)
    alpha = jnp.exp2(m - m_new)
    if stale_max:
        p = jnp.exp2(s - m)
        l = alpha * (l + jnp.sum(p, axis=0, keepdims=True))
        return m_new, l, alpha * (acc + _dot(vt, p.astype(BF16))), jnp.maximum(rise, m_c - m)
    p = jnp.exp2(s - m_new)
    l = alpha * l + jnp.sum(p, axis=0, keepdims=True)
    return m_new, l, alpha * acc + _dot(vt, p.astype(BF16)), rise


def _attend(first, rest, chunks_per_round=1, stale_max=False):
    items = [(si, qt, (lambda k=k: k), (lambda vt=vt: vt)) for si, (qt, k, vt) in enumerate(first)]
    if rest is not None:
        keys = KEY_CHUNK * chunks_per_round
        for c in range(rest[0][2].shape[0] // chunks_per_round):
            for si, (qt, k_ref, vt_ref) in enumerate(rest):
                items.append((si, qt, (lambda r=k_ref, c=c: r[c * keys:(c + 1) * keys, :]),
                              (lambda r=vt_ref, c=c: jnp.concatenate(
                                  [r[c * chunks_per_round + i] for i in range(chunks_per_round)], axis=1))))
    ns = len(first)
    groups = [items[i:i + ns] for i in range(0, len(items), ns)]
    scores = lambda grp: [_dot(k_fn(), qt) for (_, qt, k_fn, _) in grp]
    state = [None] * ns
    s_next = scores(groups[0])
    for g, grp in enumerate(groups):
        s_cur = s_next
        if g + 1 < len(groups):
            s_next = scores(groups[g + 1])
        for s, (si, _, _, vt_fn) in zip(s_cur, grp):
            state[si] = _softmax_pv(s, vt_fn(), state[si], stale_max)
    rise = functools.reduce(jnp.maximum, [st[3] for st in state])
    return [acc / l for (_, l, acc, _) in state], rise


def _attend_rolled(first, rest):
    state = tuple(_softmax_pv(_dot(k, qt), vt, None, False) for qt, k, vt in first)

    def body(c, state):
        off = pl.multiple_of(c * KEY_CHUNK, KEY_CHUNK)
        return tuple(_softmax_pv(_dot(k_ref[pl.ds(off, KEY_CHUNK), :], qt), vt_ref[c], st, False)
                     for (qt, k_ref, vt_ref), st in zip(rest, state))

    state = lax.fori_loop(0, rest[0][2].shape[0], body, state)
    return [acc / l for (_, l, acc, _) in state]


def _attend_guarded(first, rest, chunks_per_round, finish):
    o, rise = _attend(first, rest, chunks_per_round, stale_max=True)
    finish(o)

    @pl.when(jnp.max(rise) > STALE_MAX_LIMIT)
    def _():
        finish(_attend_rolled(first, rest))


def _split_maps(qt):
    row = lax.broadcasted_iota(jnp.int32, qt.shape, 0)
    zero = jnp.zeros_like(qt)
    return jnp.where(row < DH_DIFF, qt, zero), jnp.where(row >= DH_DIFF, qt, zero)


def _lambda(lam_ref, lam_init):
    lv = lam_ref[...]
    a = jnp.sum(lv[0:1] * lv[1:2], axis=-1, keepdims=True)
    b = jnp.sum(lv[2:3] * lv[3:4], axis=-1, keepdims=True)
    return jnp.exp(a) - jnp.exp(b) + lam_init


def _diff_finish(o1_t, o2_t, lam_ref, g_ref, lam_init):
    o = (o1_t - _lambda(lam_ref, lam_init) * o2_t).T
    return (_rms_norm(o, g_ref[...]) * (1.0 - lam_init)).astype(BF16)


def _mla_finish(oa_t, ob_t):
    return jnp.concatenate([oa_t, ob_t], axis=0).T.astype(BF16)


def _diff_ctx_kernel(qt_ref, k_ref, vt_ref, lam_ref, g_ref, *rest, lam_init):
    out_ref = _ride_along_casts(rest, 1)[0]
    t = qt_ref.shape[1] // CTX_SEQS
    first = []
    for q in range(CTX_SEQS):
        ts = slice(q * t, (q + 1) * t)
        for h in range(H_DIFF):
            sl = slice(h * LANES, (h + 1) * LANES)
            k, vt = k_ref[ts, sl], vt_ref[0, sl, ts]
            first += [(qm, k, vt) for qm in _split_maps(qt_ref[sl, ts])]
    o, _ = _attend(first, None)
    n = 2 * H_DIFF
    for q in range(CTX_SEQS):
        out_ref[q * t:(q + 1) * t, :] = jnp.concatenate(
            [_diff_finish(o[q * n + 2 * h], o[q * n + 2 * h + 1], lam_ref, g_ref, lam_init) for h in range(H_DIFF)],
            axis=1)


def _diff_lat_kernel(qut_ref, qrt_ref, kc_ref, vc_ref, k_ref, vt_ref, lam_ref, g_ref, out_ref, *, lam_init):
    kc, vct = kc_ref[...].astype(BF16), vc_ref[...].T.astype(BF16)
    first = [(q, kc, vct) for q in _split_maps(qut_ref[...])]
    rest = [(q, k_ref, vt_ref) for q in _split_maps(qrt_ref[...])]

    def finish(o):
        out_ref[...] = _diff_finish(o[0], o[1], lam_ref, g_ref, lam_init)

    _attend_guarded(first, rest, DIFF_CHUNKS_PER_ROUND, finish)


def _mla_ctx_kernel(qt_ref, k_ref, vt_ref, *rest):
    out_ref = _ride_along_casts(rest, 1)[0]
    t = qt_ref.shape[1] // CTX_SEQS
    first = []
    for q in range(CTX_SEQS):
        ts = slice(q * t, (q + 1) * t)
        for h in range(H_MLA):
            sl = slice(h * MLA_SLAB, (h + 1) * MLA_SLAB)
            first.append((qt_ref[sl, ts], k_ref[ts, sl], vt_ref[0, h * V_MLA:(h + 1) * V_MLA, ts]))
    o, _ = _attend(first, None)
    for q in range(CTX_SEQS):
        out_ref[q * t:(q + 1) * t, :] = jnp.concatenate(
            [_mla_finish(o[q * H_MLA + 2 * j], o[q * H_MLA + 2 * j + 1]) for j in range(H_MLA // 2)], axis=1)


def _mla_lat_kernel(qua_ref, qub_ref, qra_ref, qrb_ref, kca_ref, kcb_ref, vcta_ref, vctb_ref, ka_ref, kb_ref,
                    vta_ref, vtb_ref, out_ref):
    first = [(qua_ref[...], kca_ref[...], vcta_ref[0]), (qub_ref[...], kcb_ref[...], vctb_ref[0])]
    rest = [(qra_ref[...], ka_ref, vta_ref), (qrb_ref[...], kb_ref, vtb_ref)]

    def finish(o):
        out_ref[...] = _mla_finish(o[0], o[1])

    _attend_guarded(first, rest, MLA_CHUNKS_PER_ROUND, finish)


def _attention_ctx(streams, lam4, diff_g, lam_init, b, t, cast):
    qd_t, kd, vd_t, qm_t, km, vm_t = streams
    per_chunk = vd_t.shape[2] // t // CTX_SEQS
    t, b = t * CTX_SEQS, b // CTX_SEQS
    sem = _params("arbitrary")
    qt = lambda w: pl.BlockSpec((w, t), lambda i: (0, i))
    keys = lambda w: pl.BlockSpec((t, w), lambda i: (i, 0))
    vals = lambda w: pl.BlockSpec((1, w, t), lambda i: (i // per_chunk, 0, i % per_chunk))
    casts = [_cast_specs((cast[0], [w]), b) for w in cast[1]]
    o_d, w_a = pl.pallas_call(
        functools.partial(_diff_ctx_kernel, lam_init=lam_init),
        grid=(b,),
        in_specs=[qt(DIFF_QK), keys(DIFF_QK), vals(DIFF_V),
                  pl.BlockSpec(lam4.shape, lambda i: (0, 0)),
                  pl.BlockSpec(diff_g.shape, lambda i: (0, 0))] + casts[0][0],
        out_specs=[keys(DIFF_V)] + casts[0][1],
        out_shape=[jax.ShapeDtypeStruct((b * t, DIFF_V), BF16)] + casts[0][2],
        compiler_params=sem, name="diff_attn_ctx",
    )(qd_t, kd, vd_t, lam4, diff_g, *casts[0][3])
    o_m, w_b = pl.pallas_call(
        _mla_ctx_kernel,
        grid=(b,),
        in_specs=[qt(MLA_WIDE), keys(MLA_WIDE), vals(H_MLA * V_MLA)] + casts[1][0],
        out_specs=[keys(H_MLA * V_MLA)] + casts[1][1],
        out_shape=[jax.ShapeDtypeStruct((b * t, H_MLA * V_MLA), BF16)] + casts[1][2],
        compiler_params=sem, name="mla_attn_ctx",
    )(qm_t, km, vm_t, *casts[1][3])
    return o_d, o_m, w_a, w_b


def _attention_lat(streams, cache, lam4, diff_g, lam_init, b, t):
    qdu_t, qdr_t, kdr, vd_t, qmu_t, qmr_t, kmr, vm_t = streams
    kd_c, vd_c, km_c, vmc_t = cache
    past = kd_c.shape[0] // b
    nq = t // Q_TILE
    n_chunks = t // KEY_CHUNK
    sem = _params("arbitrary", "arbitrary", "arbitrary")
    qt = lambda f: pl.BlockSpec((LANES, Q_TILE), lambda i, h, j: (f(h), i * nq + j))
    keys = lambda n, f: pl.BlockSpec((n, LANES), lambda i, h, j: (i, f(h)))
    vals = pl.BlockSpec((n_chunks, LANES, KEY_CHUNK), lambda i, h, j: (i, h, 0))
    out = pl.BlockSpec((Q_TILE, LANES), lambda i, h, j: (i * nq + j, h))
    same, even, odd = (lambda h: h), (lambda h: 2 * h), (lambda h: 2 * h + 1)
    o_d = pl.pallas_call(
        functools.partial(_diff_lat_kernel, lam_init=lam_init),
        grid=(b, H_DIFF, nq),
        in_specs=[qt(same), qt(same), keys(past, same), keys(past, same), keys(t, same), vals,
                  pl.BlockSpec(lam4.shape, lambda i, h, j: (0, 0)),
                  pl.BlockSpec(diff_g.shape, lambda i, h, j: (0, 0))],
        out_specs=out,
        out_shape=jax.ShapeDtypeStruct((b * t, DIFF_V), BF16),
        compiler_params=sem, name="diff_attn_lat",
    )(qdu_t, qdr_t, kd_c, vd_c, kdr, vd_t, lam4, diff_g)
    o_m = pl.pallas_call(
        _mla_lat_kernel,
        grid=(b, H_MLA // 2, nq),
        in_specs=[qt(even), qt(odd), qt(even), qt(odd), keys(past, even), keys(past, odd),
                  pl.BlockSpec((1, V_MLA, past), lambda i, h, j: (i, 2 * h, 0)),
                  pl.BlockSpec((1, V_MLA, past), lambda i, h, j: (i, 2 * h + 1, 0)),
                  keys(t, even), keys(t, odd),
                  pl.BlockSpec((n_chunks, V_MLA, KEY_CHUNK), lambda i, h, j: (i, 2 * h, 0)),
                  pl.BlockSpec((n_chunks, V_MLA, KEY_CHUNK), lambda i, h, j: (i, 2 * h + 1, 0))],
        out_specs=out,
        out_shape=jax.ShapeDtypeStruct((b * t, H_MLA * V_MLA), BF16),
        compiler_params=sem, name="mla_attn_lat",
    )(qmu_t, qmu_t, qmr_t, qmr_t, km_c, km_c, vmc_t, vmc_t, kmr, kmr, vm_t, vm_t)
    return o_d, o_m


def _post_kernel(x_ref, ma_ref, mb_ref, mods_ref, wo_ref, w1_ref, w2_ref, ln_ref, *rest):
    out_ref = _ride_along_casts(rest, 1)[0]
    mods = mods_ref[0, 0]
    g_m, sh_f, sc_f, g_f = mods[2:3], mods[3:4], mods[4:5], mods[5:6]
    ln = ln_ref[...]
    rows = x_ref.shape[0] // POST_GROUPS
    grp = [slice(r * rows, (r + 1) * rows) for r in range(POST_GROUPS)]
    ys = [_dot(jnp.concatenate([ma_ref[g, :], mb_ref[g, :]], axis=1), wo_ref[...]) for g in grp]
    x1s = [_layer_norm(DEEPNORM_ALPHA * x_ref[g, :] + g_m * y, ln[0:1], ln[1:2]) for g, y in zip(grp, ys)]
    fs = []
    for x1 in x1s:
        h = (x1 * (1.0 + sc_f) + sh_f).astype(BF16)
        f = jnp.zeros(x1.shape, F32)
        for c in range(D_FF // FF_CHUNK):
            a = jnp.maximum(_dot(h, w1_ref[:, c * FF_CHUNK:(c + 1) * FF_CHUNK]), 0.0)
            f = f + _dot((a * a).astype(BF16), w2_ref[c * FF_CHUNK:(c + 1) * FF_CHUNK, :])
        fs.append(f)
    for g, x1, f in zip(grp, x1s, fs):
        out_ref[g, :] = _layer_norm(DEEPNORM_ALPHA * x1 + g_f * f, ln[2:3], ln[3:4])


def _post(x2d, mix_a, mix_b, mods, layer, mod_row_fn, wo, w1, w2, ln4, tm, cast=None):
    t = x2d.shape[0]
    tok = lambda n: pl.BlockSpec((tm, n), lambda i: (i, 0))
    c_in, c_out, c_shape, c_args = _cast_specs(cast, t // tm)
    return pl.pallas_call(
        _post_kernel, grid=(t // tm,),
        in_specs=[tok(D_MODEL), tok(mix_a.shape[1]), tok(mix_b.shape[1]),
                  pl.BlockSpec((1, 1, 6, D_MODEL), lambda i: (layer, mod_row_fn(i), 0, 0)),
                  _const_spec(wo.shape), _const_spec(w1.shape), _const_spec(w2.shape),
                  _const_spec(ln4.shape)] + c_in,
        out_specs=[tok(D_MODEL)] + c_out,
        out_shape=[jax.ShapeDtypeStruct((t, D_MODEL), F32)] + c_shape,
        compiler_params=_params("arbitrary"), name=f"post_mlp_l{layer}",
    )(x2d, mix_a, mix_b, mods, wo, w1, w2, ln4, *c_args)


def _conv_in_kernel(x_ref, mods_ref, w_ref, u_ref, pz_ref):
    mods = mods_ref[0, 0]
    sh, sc = mods[0:1], mods[1:2]
    h = (x_ref[...] * (1.0 + sc) + sh).astype(BF16)
    proj = _dot(h, w_ref[...])
    a, gate = proj[:, 0:CONV_CH], proj[:, CONV_CH:2 * CONV_CH]
    u_ref[...] = a * jax.nn.sigmoid(gate)
    pz_ref[...] = proj[:, 2 * CONV_CH:]


def _conv_in(x2d, mods, layer, mod_row_fn, w, tm):
    t = x2d.shape[0]
    tok = lambda n: pl.BlockSpec((tm, n), lambda i: (i, 0))
    return pl.pallas_call(
        _conv_in_kernel, grid=(t // tm,),
        in_specs=[tok(D_MODEL),
                  pl.BlockSpec((1, 1, 6, D_MODEL), lambda i: (layer, mod_row_fn(i), 0, 0)),
                  _const_spec(w.shape)],
        out_specs=[tok(CONV_CH), tok(POOL_CH)],
        out_shape=[jax.ShapeDtypeStruct((t, CONV_CH), F32), jax.ShapeDtypeStruct((t, POOL_CH), F32)],
        compiler_params=_params("arbitrary"), name="conv_in",
    )(x2d, mods, w)


def _conv_pool_kernel(*refs, n_tiles):
    if n_tiles > 1:
        (u_ref, ul_ref, ur_ref, pz_ref, pl_ref, pr_ref, cw_ref, cp_ref, wp_ref, ps_ref, inv_ref,
         uo_ref, do_ref, ubuf, pbuf, ushift, cbuf) = refs
    else:
        u_ref, pz_ref, cw_ref, cp_ref, wp_ref, ps_ref, inv_ref, uo_ref, do_ref, ubuf, pbuf, ushift, cbuf = refs
    tm = u_ref.shape[1]
    j = pl.program_id(1)
    halo_zero = jnp.zeros((CONV_HALO, CONV_CH), F32)
    for buf, mid, sides in ((ubuf, u_ref, (ul_ref, ur_ref) if n_tiles > 1 else None),
                            (pbuf, pz_ref, (pl_ref, pr_ref) if n_tiles > 1 else None)):
        buf[CONV_HALO:CONV_HALO + tm, :] = mid[0]
        if sides is None:
            buf[0:CONV_HALO, :] = halo_zero
            buf[CONV_HALO + tm:, :] = halo_zero
        else:
            buf[0:CONV_HALO, :] = jnp.where(j > 0, sides[0][0], halo_zero)
            buf[CONV_HALO + tm:, :] = jnp.where(j < n_tiles - 1, sides[1][0], halo_zero)

    rows_sh = ushift.shape[1]
    for b in range(SUBLANES):
        ushift[b] = ubuf[b:b + rows_sh, :]
    base = CONV_HALO - CONV_WIDTH // 2
    cp = cp_ref[...]

    def strip(i, carry):
        r0 = pl.multiple_of(i * CONV_STRIP, CONV_STRIP)
        acc = jnp.zeros((CONV_STRIP, CONV_CH), F32)
        for k in range(CONV_WIDTH):
            a, b = divmod(base + k, SUBLANES)
            w = jnp.concatenate([cw_ref[k]] * (CONV_STRIP // SUBLANES), axis=0)
            acc = acc + ushift[b, pl.ds(r0 + a * SUBLANES, CONV_STRIP), :] * w
        cbuf[pl.ds(r0, CONV_STRIP), :] = acc
        return carry

    lax.fori_loop(0, tm // CONV_STRIP, strip, 0)
    z = _layer_norm(cbuf[...] + cp[0:1], cp[1:2], cp[2:3])
    uo_ref[0] = (z * jax.nn.sigmoid(z)).astype(uo_ref.dtype)

    outs = []
    for g, w in enumerate(POOL_WINDOWS):
        cols = slice(g * POOL_GC, (g + 1) * POOL_GC)
        s = jnp.zeros((tm, POOL_GC), F32)
        for d in range(-(w // 2), w // 2):
            s = s + pbuf[CONV_HALO + d:CONV_HALO + d + tm, cols]
        dgrp = s * inv_ref[:, cols] - pz_ref[0][:, cols]
        outs.append(_dot(dgrp.astype(BF16), wp_ref[g]))
    do_ref[0] = (jnp.concatenate(outs, axis=1) * ps_ref[...]).astype(do_ref.dtype)


def _pool_inv_counts(t):
    pos = np.arange(t)
    cols = []
    for w in POOL_WINDOWS:
        cnt = np.minimum(pos + w // 2 - 1, t - 1) - np.maximum(pos - w // 2, 0) + 1
        cols.append(np.repeat((1.0 / cnt)[:, None], POOL_GC, axis=1))
    return jnp.asarray(np.concatenate(cols, axis=1).astype(np.float32))


def _conv_pool(u2d, pz2d, b, t, tm, cw, cp, wp, ps):
    n_tiles = t // tm
    u3, p3 = u2d.reshape(b, t, CONV_CH), pz2d.reshape(b, t, POOL_CH)
    mid = pl.BlockSpec((1, tm, CONV_CH), lambda i, j: (i, j, 0))
    r = tm // CONV_HALO
    left = pl.BlockSpec((1, CONV_HALO, CONV_CH), lambda i, j: (i, jnp.maximum(j * r - 1, 0), 0))
    right = pl.BlockSpec((1, CONV_HALO, CONV_CH),
                         lambda i, j: (i, jnp.minimum((j + 1) * r, t // CONV_HALO - 1), 0))
    const = lambda a: pl.BlockSpec(a.shape, lambda i, j: (0,) * a.ndim)
    if n_tiles > 1:
        in_specs = [mid, left, right, mid, left, right]
        args = (u3, u3, u3, p3, p3, p3)
    else:
        in_specs = [mid, mid]
        args = (u3, p3)
    inv = _pool_inv_counts(t)
    in_specs += [const(cw), const(cp), const(wp), const(ps), pl.BlockSpec((tm, POOL_CH), lambda i, j: (j, 0))]
    uo, do = pl.pallas_call(
        functools.partial(_conv_pool_kernel, n_tiles=n_tiles),
        grid=(b, n_tiles), in_specs=in_specs,
        out_specs=[mid, mid],
        out_shape=[jax.ShapeDtypeStruct((b, t, CONV_CH), BF16), jax.ShapeDtypeStruct((b, t, POOL_CH), BF16)],
        scratch_shapes=[pltpu.VMEM((tm + 2 * CONV_HALO, CONV_CH), F32),
                        pltpu.VMEM((tm + 2 * CONV_HALO, POOL_CH), F32),
                        pltpu.VMEM((SUBLANES, tm + 2 * CONV_HALO - SUBLANES, CONV_CH), F32),
                        pltpu.VMEM((tm, CONV_CH), F32)],
        compiler_params=_params("arbitrary", "arbitrary"),
        name=f"conv_pool_{'lat' if n_tiles > 1 else 'ctx'}",
    )(*args, cw, cp, wp, ps, inv)
    return uo.reshape(b * t, CONV_CH), do.reshape(b * t, POOL_CH)


def _att_in_weights(w_att_in, with_v_tok):
    o = 2 * DIFF_QK + DIFF_V
    q_d, k_d, v_d = w_att_in[:, :DIFF_QK], w_att_in[:, DIFF_QK:2 * DIFF_QK], w_att_in[:, 2 * DIFF_QK:o]
    cq, ckv = w_att_in[:, o:o + Q_LORA], w_att_in[:, o + Q_LORA:o + Q_LORA + KV_LORA]
    kpe = jnp.pad(w_att_in[:, o + Q_LORA + KV_LORA:], _ROPE_LANE_PAD)
    w_t = jnp.concatenate([q_d, v_d, cq, ckv], axis=1).T.astype(BF16)
    w_r = jnp.concatenate([k_d, ckv, kpe] + ([v_d] if with_v_tok else []), axis=1).astype(BF16)
    return w_t, w_r


def _mla_weights(w_uq, w_ukv):
    wq = w_uq.reshape(Q_LORA, H_MLA, QK_NOPE + QK_ROPE)
    wq = jnp.pad(wq, ((0, 0), (0, 0), (0, MLA_SLAB - QK_NOPE - QK_ROPE))).reshape(Q_LORA, MLA_WIDE)
    wkv = w_ukv.reshape(KV_LORA, H_MLA, QK_NOPE + V_MLA)
    wk = jnp.pad(wkv[:, :, :QK_NOPE], ((0, 0), (0, 0), (0, MLA_SLAB - QK_NOPE))).reshape(KV_LORA, MLA_WIDE)
    wv = wkv[:, :, QK_NOPE:].reshape(KV_LORA, H_MLA * V_MLA)
    return wq.T.astype(BF16), wk.astype(BF16), wv.T.astype(BF16)


def kernel(x_prompt, x_sample, cache_diff_k, cache_diff_v, cache_mla_ckv, cache_mla_krope, c, c_ctx,
           w_ada, b_ada, ln_mix_g, ln_mix_b, ln_mlp_g, ln_mlp_b, w_mlp_in, w_mlp_out,
           w_att_in, w_uq, w_ukv, q_norm_g, kv_norm_g, lam_q1, lam_k1, lam_q2, lam_k2, diff_norm_g, w_att_out,
           w_conv_in, conv_w, conv_b, conv_norm_g, conv_norm_b, w_pool, pool_scale, w_conv_out):
    bc, tc, d = x_prompt.shape
    bl, tl, _ = x_sample.shape
    past = cache_diff_k.shape[2]

    cond8 = jnp.zeros((8, d), F32).at[0].set(c_ctx).at[1:1 + bl].set(c)
    mods = _ada_mods(cond8, w_ada, b_ada).reshape(DEPTH, 8, 6, d)
    ctx_row = lambda i: 0
    lat_row = lambda i: 1 + i // (tl // TOKEN_TILE)

    wqt, wk, wvt = _mla_weights(w_uq[0], w_ukv[0])
    norm_w = (q_norm_g[0].reshape(Q_LORA, 1), kv_norm_g[0].reshape(1, KV_LORA), kv_norm_g[0].reshape(KV_LORA, 1))
    att_w = lambda with_v: _att_in_weights(w_att_in[0], with_v) + norm_w + (wqt, wk, wvt)
    lam4 = jnp.stack([lam_q1[0], lam_k1[0], lam_q2[0], lam_k2[0]])
    diff_g = diff_norm_g[0].reshape(1, 2 * DH_DIFF)
    lam_init = 0.8 - 0.6 * math.exp(-0.3 * 0)
    ln4 = [jnp.stack([ln_mix_g[l], ln_mix_b[l], ln_mlp_g[l], ln_mlp_b[l]]) for l in range(DEPTH)]
    mlp_w = [w_mlp_in, w_mlp_out]
    conv_p = jnp.stack([conv_b[0], conv_norm_g[0], conv_norm_b[0]])
    conv_w8 = jnp.broadcast_to(conv_w[0][:, None, :], (CONV_WIDTH, SUBLANES, CONV_CH))
    wp = w_pool[0].astype(BF16)
    ps = pool_scale[0].reshape(1, POOL_CH)

    xp = x_prompt.reshape(bc * tc, d)
    outs = _att_in(xp, mods, ctx_row, att_w(True), None, TOKEN_TILE,
                   cast=(0, mlp_w + [w_att_out, w_conv_in, w_conv_out]))
    kd_f, vd_f, ckv_f, kpe_f = outs[6:10]
    w1_0, w2_0, w_ao, w_ci, w_co = outs[10:]
    o_d, o_m, w1_1, w2_1 = _attention_ctx(outs[:6], lam4, diff_g, lam_init, bc, tc, cast=(1, mlp_w))
    xp, = _post(xp, o_d, o_m, mods, 0, ctx_row, w_ao, w1_0, w2_0, ln4[0], TOKEN_TILE)
    u, pz = _conv_in(xp, mods, 1, ctx_row, w_ci, TOKEN_TILE)
    uo, do = _conv_pool(u, pz, bc, tc, tc, conv_w8, conv_p, wp, ps)
    xp, = _post(xp, uo, do, mods, 1, ctx_row, w_co, w1_1, w2_1, ln4[1], TOKEN_TILE)

    xs = x_sample.reshape(bl * tl, d)
    outs = _att_in(xs, mods, lat_row, att_w(False), _rope_tables(tl), TOKEN_TILE)
    km_c, vmc_t = _cache_kv(cache_mla_ckv[:, 0].reshape(bl * past, KV_LORA),
                            jnp.pad(cache_mla_krope[:, 0].reshape(bl * past, QK_ROPE), _ROPE_LANE_PAD),
                            wk, wvt, bl)
    cache = (cache_diff_k[:, 0].reshape(bl * past, DIFF_QK), cache_diff_v[:, 0].reshape(bl * past, DIFF_V),
             km_c, vmc_t)
    o_d, o_m = _attention_lat(outs, cache, lam4, diff_g, lam_init, bl, tl)
    xs, = _post(xs, o_d, o_m, mods, 0, lat_row, w_ao, w1_0, w2_0, ln4[0], TOKEN_TILE)
    u, pz = _conv_in(xs, mods, 1, lat_row, w_ci, TOKEN_TILE)
    uo, do = _conv_pool(u, pz, bl, tl, CONV_TILE, conv_w8, conv_p, wp, ps)
    xs, = _post(xs, uo, do, mods, 1, lat_row, w_co, w1_1, w2_1, ln4[1], TOKEN_TILE)

    return (xp.reshape(bc, tc, d), xs.reshape(bl, tl, d),
            kd_f.reshape(bc, 1, tc, H_DIFF, 2 * DH_DIFF), vd_f.reshape(bc, 1, tc, H_DIFF, 2 * DH_DIFF),
            ckv_f.reshape(bc, 1, tc, KV_LORA), kpe_f.reshape(bc, 1, tc, QK_ROPE))
```

```python
import functools
import math

import jax
import jax.numpy as jnp
import numpy as np
from jax import lax
from jax.experimental import pallas as pl
from jax.experimental.pallas import tpu as pltpu

D_MODEL = 1024
DEPTH = 2
GRID_W = 64
H_DIFF = 4
DH_DIFF = 64
DIFF_QK = H_DIFF * 2 * DH_DIFF
DIFF_V = H_DIFF * 2 * DH_DIFF
H_MLA = 8
Q_LORA = 256
KV_LORA = 128
QK_NOPE = 64
QK_ROPE = 32
V_MLA = 64
ATT_IN = 2 * DIFF_QK + DIFF_V + Q_LORA + KV_LORA + QK_ROPE
CONV_CH = 512
CONV_WIDTH = 31
POOL_CH = 512
POOL_WINDOWS = (2, 4, 8, 16)
POOL_GC = POOL_CH // len(POOL_WINDOWS)
CONV_IN = 2 * CONV_CH + POOL_CH
D_FF = 4 * D_MODEL
ROPE_BASE = 10000.0
NORM_EPS = 1e-5
DEEPNORM_ALPHA = (2 * DEPTH) ** 0.25
LOG2E = math.log2(math.e)

LANES = 128
SUBLANES = 8
MLA_SLAB = LANES
MLA_WIDE = H_MLA * MLA_SLAB
_ROPE_LANE_PAD = ((0, 0), (QK_NOPE, MLA_SLAB - QK_NOPE - QK_ROPE))
VMEM_LIMIT = 56 * 1024 * 1024

TOKEN_TILE = 512
KEY_CHUNK = TOKEN_TILE
Q_TILE = 1024
CTX_SEQS = 2
DIFF_CHUNKS_PER_ROUND = 4
MLA_CHUNKS_PER_ROUND = 2
STALE_MAX_LIMIT = 64.0
CONV_TILE = 1024
CONV_HALO = 16
CONV_STRIP = 64
FF_CHUNK = 1024
POST_GROUPS = 2

F32 = jnp.float32
BF16 = jnp.bfloat16


def _dot(a, b):
    return jnp.dot(a, b, preferred_element_type=F32)


def _dot_nt(a, b):
    return lax.dot_general(a, b, (((1,), (1,)), ((), ())), preferred_element_type=F32)


def _layer_norm(x, g, b):
    mu = jnp.mean(x, axis=-1, keepdims=True)
    xc = x - mu
    var = jnp.mean(xc * xc, axis=-1, keepdims=True)
    return xc * lax.rsqrt(var + NORM_EPS) * g + b


def _rms_norm(x, g, axis=-1):
    ms = jnp.mean(x * x, axis=axis, keepdims=True)
    return x * lax.rsqrt(ms + NORM_EPS) * g


def _const_spec(shape):
    nd = len(shape)
    return pl.BlockSpec(shape, lambda *_: (0,) * nd, pipeline_mode=pl.Buffered(1))


def _cast_specs(cast, n_steps):
    if cast is None:
        return [], [], [], []
    layer, ws = cast
    in_specs = [pl.BlockSpec((1, w.shape[1] // n_steps, w.shape[2]), lambda i: (layer, i, 0)) for w in ws]
    out_specs = [pl.BlockSpec((w.shape[1] // n_steps, w.shape[2]), lambda i: (i, 0)) for w in ws]
    out_shape = [jax.ShapeDtypeStruct(w.shape[1:], BF16) for w in ws]
    return in_specs, out_specs, out_shape, list(ws)


def _ride_along_casts(rest, n_out):
    n_cast = (len(rest) - n_out) // 2
    for src, dst in zip(rest[:n_cast], rest[n_cast + n_out:]):
        dst[...] = src[0].astype(BF16)
    return rest[n_cast:n_cast + n_out]


def _params(*sem):
    return pltpu.CompilerParams(dimension_semantics=sem, vmem_limit_bytes=VMEM_LIMIT)


ADA_TILE = 3072


def _ada_kernel(cond_ref, w_ref, b_ref, out_ref):
    cond = cond_ref[...]
    act = (cond * jax.nn.sigmoid(cond)).astype(BF16)
    out_ref[0] = _dot(act, w_ref[0].astype(BF16)) + b_ref[0]


def _ada_mods(cond8, w_ada, b_ada):
    n = 6 * D_MODEL
    return pl.pallas_call(
        _ada_kernel,
        grid=(DEPTH, n // ADA_TILE),
        in_specs=[
            pl.BlockSpec((8, D_MODEL), lambda l, j: (0, 0)),
            pl.BlockSpec((1, D_MODEL, ADA_TILE), lambda l, j: (l, 0, j)),
            pl.BlockSpec((1, 1, ADA_TILE), lambda l, j: (l, 0, j)),
        ],
        out_specs=pl.BlockSpec((1, 8, ADA_TILE), lambda l, j: (l, 0, j)),
        out_shape=jax.ShapeDtypeStruct((DEPTH, 8, n), F32),
        compiler_params=_params("arbitrary", "arbitrary"),
        name="ada_mods",
    )(cond8, w_ada, b_ada.reshape(DEPTH, 1, n))


def _rope_tables(t_lat):
    pos = np.arange(t_lat)
    row = (pos // GRID_W).astype(np.float64)
    col = (pos % GRID_W).astype(np.float64)

    def tables(kinds):
        cos = np.ones((t_lat, LANES))
        sa = np.zeros((t_lat, LANES))
        sb = np.zeros((t_lat, LANES))
        for lane, kind in enumerate(kinds):
            if kind is None:
                continue
            axis, half, j, upper = kind
            ang = (row if axis == 0 else col) * ROPE_BASE ** (-float(j) / half)
            cos[:, lane] = np.cos(ang)
            if upper:
                sb[:, lane] = np.sin(ang)
            else:
                sa[:, lane] = -np.sin(ang)
        return [cos, sa, sb]

    def rot_kinds(n):
        half = n // 4
        kinds = []
        for i in range(n):
            axis, r = divmod(i, n // 2)
            kinds.append((axis, half, r % half, r >= half))
        return kinds

    diff = rot_kinds(DH_DIFF) * 2
    mla_q = [None] * QK_NOPE + rot_kinds(QK_ROPE) + [None] * (LANES - QK_NOPE - QK_ROPE)
    chan = np.stack([t.T for t in tables(diff) + tables(mla_q)]).astype(np.float32)
    tok = np.stack(tables(diff) + tables(mla_q)).astype(np.float32)
    return jnp.asarray(chan), jnp.asarray(tok)


def _rope_tok(x, cos, sa, sb, shift):
    return x * cos + pltpu.roll(x, LANES - shift, 1) * sa + pltpu.roll(x, shift, 1) * sb


def _rope_tok_wide(x, cos, sa, sb, shift):
    n = x.shape[1] // LANES
    return jnp.concatenate(
        [_rope_tok(x[:, i * LANES:(i + 1) * LANES], cos, sa, sb, shift) for i in range(n)], axis=1)


def _rope_chan(x, cos, sa, sb, shift):
    up = jnp.concatenate([x[shift:], x[:shift]], axis=0)
    down = jnp.concatenate([x[-shift:], x[:-shift]], axis=0)
    return x * cos + up * sa + down * sb


def _rope_chan_wide(x, cos, sa, sb, shift):
    n = x.shape[0] // LANES
    return jnp.concatenate(
        [_rope_chan(x[i * LANES:(i + 1) * LANES], cos, sa, sb, shift) for i in range(n)], axis=0)


_T_QD, _T_VD, _T_CQ, _T_CKV = 0, DIFF_QK, DIFF_QK + DIFF_V, DIFF_QK + DIFF_V + Q_LORA
_T_ROWS = _T_CKV + KV_LORA
_R_KD, _R_CKV, _R_KPE, _R_VD = 0, DIFF_QK, DIFF_QK + KV_LORA, DIFF_QK + KV_LORA + LANES

DIFF_QSCALE = DH_DIFF ** -0.5 * LOG2E
MLA_QSCALE = (QK_NOPE + QK_ROPE) ** -0.5 * LOG2E


def _att_in_body(x_ref, mods_ref, wt_ref, wr_ref, qg_ref, kvg_row_ref, kvg_col_ref, wqt_ref, wk_ref, wvt_ref):
    mods = mods_ref[0, 0]
    sh, sc = mods[0:1], mods[1:2]
    h = (x_ref[...] * (1.0 + sc) + sh).astype(BF16)
    pt = _dot_nt(wt_ref[...], h)
    pr = _dot(h, wr_ref[...])
    qd_t = pt[_T_QD:_T_QD + DIFF_QK] * DIFF_QSCALE
    vd_t = pt[_T_VD:_T_VD + DIFF_V]
    cqn_t = _rms_norm(pt[_T_CQ:_T_CQ + Q_LORA], qg_ref[...], axis=0)
    qm_t = _dot(wqt_ref[...], cqn_t.astype(BF16)) * MLA_QSCALE
    ckvn_t = _rms_norm(pt[_T_CKV:_T_CKV + KV_LORA], kvg_col_ref[...], axis=0)
    vm_t = _dot(wvt_ref[...], ckvn_t.astype(BF16))
    k_d = pr[:, _R_KD:_R_KD + DIFF_QK]
    ckv_n = _rms_norm(pr[:, _R_CKV:_R_CKV + KV_LORA], kvg_row_ref[...])
    kpe = pr[:, _R_KPE:_R_KPE + LANES]
    k_nope = _dot(ckv_n.astype(BF16), wk_ref[...])
    return pr, qd_t, vd_t, qm_t, vm_t, k_d, ckv_n, kpe, k_nope


def _mla_keys(k_nope, kpe_slab):
    return (k_nope + jnp.concatenate([kpe_slab] * H_MLA, axis=1)).astype(BF16)


def _att_in_ctx_kernel(x_ref, mods_ref, wt_ref, wr_ref, qg_ref, kvg_row_ref, kvg_col_ref, wqt_ref, wk_ref,
                       wvt_ref, *rest):
    (qdt_ref, kd_ref, vdt_ref, qmt_ref, km_ref, vmt_ref,
     kdf_ref, vdf_ref, ckvf_ref, kpef_ref) = _ride_along_casts(rest, 10)
    pr, qd_t, vd_t, qm_t, vm_t, k_d, ckv_n, kpe, k_nope = _att_in_body(
        x_ref, mods_ref, wt_ref, wr_ref, qg_ref, kvg_row_ref, kvg_col_ref, wqt_ref, wk_ref, wvt_ref)
    qdt_ref[...] = qd_t.astype(BF16)
    kd_ref[...] = k_d.astype(BF16)
    vdt_ref[0] = vd_t.astype(BF16)
    qmt_ref[...] = qm_t.astype(BF16)
    km_ref[...] = _mla_keys(k_nope, kpe)
    vmt_ref[0] = vm_t.astype(BF16)
    v_d = pr[:, _R_VD:_R_VD + DIFF_V]
    for h in range(H_DIFF):
        kdf_ref[:, h, :] = k_d[:, h * LANES:(h + 1) * LANES]
        vdf_ref[:, h, :] = v_d[:, h * LANES:(h + 1) * LANES]
    ckvf_ref[...] = ckv_n
    kpef_ref[...] = kpe[:, QK_NOPE:QK_NOPE + QK_ROPE]


def _att_in_lat_kernel(x_ref, mods_ref, wt_ref, wr_ref, qg_ref, kvg_row_ref, kvg_col_ref, wqt_ref, wk_ref,
                       wvt_ref, tabc_ref, tabt_ref,
                       qdut_ref, qdrt_ref, kdr_ref, vdt_ref, qmut_ref, qmrt_ref, kmr_ref, vmt_ref):
    _, qd_t, vd_t, qm_t, vm_t, k_d, ckv_n, kpe, k_nope = _att_in_body(
        x_ref, mods_ref, wt_ref, wr_ref, qg_ref, kvg_row_ref, kvg_col_ref, wqt_ref, wk_ref, wvt_ref)
    qdut_ref[...] = qd_t.astype(BF16)
    qdrt_ref[...] = _rope_chan_wide(qd_t, tabc_ref[0], tabc_ref[1], tabc_ref[2], DH_DIFF // 4).astype(BF16)
    kdr_ref[...] = _rope_tok_wide(k_d, tabt_ref[0], tabt_ref[1], tabt_ref[2], DH_DIFF // 4).astype(BF16)
    vdt_ref[0] = vd_t.astype(BF16)
    qmut_ref[...] = qm_t.astype(BF16)
    qmrt_ref[...] = _rope_chan_wide(qm_t, tabc_ref[3], tabc_ref[4], tabc_ref[5], QK_ROPE // 4).astype(BF16)
    kpe_r = _rope_tok(kpe, tabt_ref[3], tabt_ref[4], tabt_ref[5], QK_ROPE // 4)
    kmr_ref[...] = _mla_keys(k_nope, kpe_r)
    vmt_ref[0] = vm_t.astype(BF16)


def _att_in(x2d, mods, mod_row_fn, wts, tables, tm, cast=None):
    t = x2d.shape[0]
    nt = t // tm
    tok = lambda n: pl.BlockSpec((tm, n), lambda i: (i, 0))
    chan = lambda n: pl.BlockSpec((n, tm), lambda i: (0, i))
    chunk = lambda n: pl.BlockSpec((1, n, tm), lambda i: (i, 0, 0))
    in_specs = [tok(D_MODEL), pl.BlockSpec((1, 1, 6, D_MODEL), lambda i: (0, mod_row_fn(i), 0, 0))]
    in_specs += [_const_spec(w.shape) for w in wts]
    s_tok = lambda n, dt=BF16: jax.ShapeDtypeStruct((t, n), dt)
    s_chan = lambda n: jax.ShapeDtypeStruct((n, t), BF16)
    s_chunk = lambda n: jax.ShapeDtypeStruct((nt, n, tm), BF16)
    if tables is None:
        heads = pl.BlockSpec((tm, H_DIFF, 2 * DH_DIFF), lambda i: (i, 0, 0))
        s_heads = jax.ShapeDtypeStruct((t, H_DIFF, 2 * DH_DIFF), F32)
        out_shape = [s_chan(DIFF_QK), s_tok(DIFF_QK), s_chunk(DIFF_V), s_chan(MLA_WIDE), s_tok(MLA_WIDE),
                     s_chunk(H_MLA * V_MLA),
                     s_heads, s_heads, s_tok(KV_LORA, F32), s_tok(QK_ROPE, F32)]
        out_specs = [chan(DIFF_QK), tok(DIFF_QK), chunk(DIFF_V), chan(MLA_WIDE), tok(MLA_WIDE),
                     chunk(H_MLA * V_MLA), heads, heads, tok(KV_LORA), tok(QK_ROPE)]
        c_in, c_out, c_shape, c_args = _cast_specs(cast, nt)
        in_specs, out_specs, out_shape = in_specs + c_in, out_specs + c_out, out_shape + c_shape
        kern, args, name = _att_in_ctx_kernel, tuple(c_args), "att_in_ctx"
    else:
        tab_c, tab_t = tables
        t_lat = tab_t.shape[1]
        in_specs += [pl.BlockSpec((6, LANES, tm), lambda i: (0, 0, i % (t_lat // tm))),
                     pl.BlockSpec((6, tm, LANES), lambda i: (0, i % (t_lat // tm), 0))]
        out_shape = [s_chan(DIFF_QK), s_chan(DIFF_QK), s_tok(DIFF_QK), s_chunk(DIFF_V),
                     s_chan(MLA_WIDE), s_chan(MLA_WIDE), s_tok(MLA_WIDE), s_chunk(H_MLA * V_MLA)]
        out_specs = [chan(DIFF_QK), chan(DIFF_QK), tok(DIFF_QK), chunk(DIFF_V),
                     chan(MLA_WIDE), chan(MLA_WIDE), tok(MLA_WIDE), chunk(H_MLA * V_MLA)]
        kern, args, name = _att_in_lat_kernel, (tab_c, tab_t), "att_in_lat"
    return pl.pallas_call(
        kern, grid=(nt,), in_specs=in_specs, out_specs=out_specs, out_shape=out_shape,
        compiler_params=_params("arbitrary"), name=name,
    )(x2d, mods, *wts, *args)


def _cache_kv_kernel(ckv_ref, kpe_ref, wk_ref, wvt_ref, km_ref, vmt_ref):
    ckv_b = ckv_ref[...].astype(BF16)
    km_ref[...] = _mla_keys(_dot(ckv_b, wk_ref[...]), kpe_ref[...])
    past = vmt_ref.shape[2]
    vt = _dot_nt(wvt_ref[...], ckv_b)
    for b in range(vmt_ref.shape[0]):
        vmt_ref[b] = vt[:, b * past:(b + 1) * past].astype(BF16)


def _cache_kv(ckv2d, kpe_slab2d, wk, wvt, bl):
    t = ckv2d.shape[0]
    full = lambda a: pl.BlockSpec(a.shape, lambda i: (0,) * a.ndim)
    return pl.pallas_call(
        _cache_kv_kernel, grid=(1,),
        in_specs=[full(ckv2d), full(kpe_slab2d), full(wk), full(wvt)],
        out_specs=[pl.BlockSpec((t, MLA_WIDE), lambda i: (0, 0)),
                   pl.BlockSpec((bl, H_MLA * V_MLA, t // bl), lambda i: (0, 0, 0))],
        out_shape=[jax.ShapeDtypeStruct((t, MLA_WIDE), BF16),
                   jax.ShapeDtypeStruct((bl, H_MLA * V_MLA, t // bl), BF16)],
        compiler_params=_params("arbitrary"), name="cache_kv",
    )(ckv2d, kpe_slab2d, wk, wvt)


def _softmax_pv(s, vt, state, stale_max):
    m_c = jnp.max(s, axis=0, keepdims=True)
    if state is None:
        p = jnp.exp2(s - m_c)
        return m_c, jnp.sum(p, axis=0, keepdims=True), _dot(vt, p.astype(BF16)), jnp.zeros_like(m_c)
    m, l, acc, rise = state
    m_new = jnp.maximum(m, m_c)
    alpha = jnp.exp2(m - m_new)
    if stale_max:
        p = jnp.exp2(s - m)
        l = alpha * (l + jnp.sum(p, axis=0, keepdims=True))
        return m_new, l, alpha * (acc + _dot(vt, p.astype(BF16))), jnp.maximum(rise, m_c - m)
    p = jnp.exp2(s - m_new)
    l = alpha * l + jnp.sum(p, axis=0, keepdims=True)
    return m_new, l, alpha * acc + _dot(vt, p.astype(BF16)), rise


def _attend(first, rest, chunks_per_round=1, stale_max=False):
    items = [(si, qt, (lambda k=k: k), (lambda vt=vt: vt)) for si, (qt, k, vt) in enumerate(first)]
    if rest is not None:
        keys = KEY_CHUNK * chunks_per_round
        for c in range(rest[0][2].shape[0] // chunks_per_round):
            for si, (qt, k_ref, vt_ref) in enumerate(rest):
                items.append((si, qt, (lambda r=k_ref, c=c: r[c * keys:(c + 1) * keys, :]),
                              (lambda r=vt_ref, c=c: jnp.concatenate(
                                  [r[c * chunks_per_round + i] for i in range(chunks_per_round)], axis=1))))
    ns = len(first)
    groups = [items[i:i + ns] for i in range(0, len(items), ns)]
    scores = lambda grp: [_dot(k_fn(), qt) for (_, qt, k_fn, _) in grp]
    state = [None] * ns
    s_next = scores(groups[0])
    for g, grp in enumerate(groups):
        s_cur = s_next
        if g + 1 < len(groups):
            s_next = scores(groups[g + 1])
        for s, (si, _, _, vt_fn) in zip(s_cur, grp):
            state[si] = _softmax_pv(s, vt_fn(), state[si], stale_max)
    rise = functools.reduce(jnp.maximum, [st[3] for st in state])
    return [acc / l for (_, l, acc, _) in state], rise


def _attend_rolled(first, rest):
    state = tuple(_softmax_pv(_dot(k, qt), vt, None, False) for qt, k, vt in first)

    def body(c, state):
        off = pl.multiple_of(c * KEY_CHUNK, KEY_CHUNK)
        return tuple(_softmax_pv(_dot(k_ref[pl.ds(off, KEY_CHUNK), :], qt), vt_ref[c], st, False)
                     for (qt, k_ref, vt_ref), st in zip(rest, state))

    state = lax.fori_loop(0, rest[0][2].shape[0], body, state)
    return [acc / l for (_, l, acc, _) in state]


def _attend_guarded(first, rest, chunks_per_round, finish):
    o, rise = _attend(first, rest, chunks_per_round, stale_max=True)
    finish(o)

    @pl.when(jnp.max(rise) > STALE_MAX_LIMIT)
    def _():
        finish(_attend_rolled(first, rest))


def _split_maps(qt):
    row = lax.broadcasted_iota(jnp.int32, qt.shape, 0)
    zero = jnp.zeros_like(qt)
    return jnp.where(row < DH_DIFF, qt, zero), jnp.where(row >= DH_DIFF, qt, zero)


def _lambda(lam_ref, lam_init):
    lv = lam_ref[...]
    a = jnp.sum(lv[0:1] * lv[1:2], axis=-1, keepdims=True)
    b = jnp.sum(lv[2:3] * lv[3:4], axis=-1, keepdims=True)
    return jnp.exp(a) - jnp.exp(b) + lam_init


def _diff_finish(o1_t, o2_t, lam_ref, g_ref, lam_init):
    o = (o1_t - _lambda(lam_ref, lam_init) * o2_t).T
    return (_rms_norm(o, g_ref[...]) * (1.0 - lam_init)).astype(BF16)


def _mla_finish(oa_t, ob_t):
    return jnp.concatenate([oa_t, ob_t], axis=0).T.astype(BF16)


def _diff_ctx_kernel(qt_ref, k_ref, vt_ref, lam_ref, g_ref, *rest, lam_init):
    out_ref = _ride_along_casts(rest, 1)[0]
    t = qt_ref.shape[1] // CTX_SEQS
    first = []
    for q in range(CTX_SEQS):
        ts = slice(q * t, (q + 1) * t)
        for h in range(H_DIFF):
            sl = slice(h * LANES, (h + 1) * LANES)
            k, vt = k_ref[ts, sl], vt_ref[0, sl, ts]
            first += [(qm, k, vt) for qm in _split_maps(qt_ref[sl, ts])]
    o, _ = _attend(first, None)
    n = 2 * H_DIFF
    for q in range(CTX_SEQS):
        out_ref[q * t:(q + 1) * t, :] = jnp.concatenate(
            [_diff_finish(o[q * n + 2 * h], o[q * n + 2 * h + 1], lam_ref, g_ref, lam_init) for h in range(H_DIFF)],
            axis=1)


def _diff_lat_kernel(qut_ref, qrt_ref, kc_ref, vc_ref, k_ref, vt_ref, lam_ref, g_ref, out_ref, *, lam_init):
    kc, vct = kc_ref[...].astype(BF16), vc_ref[...].T.astype(BF16)
    first = [(q, kc, vct) for q in _split_maps(qut_ref[...])]
    rest = [(q, k_ref, vt_ref) for q in _split_maps(qrt_ref[...])]

    def finish(o):
        out_ref[...] = _diff_finish(o[0], o[1], lam_ref, g_ref, lam_init)

    _attend_guarded(first, rest, DIFF_CHUNKS_PER_ROUND, finish)


def _mla_ctx_kernel(qt_ref, k_ref, vt_ref, *rest):
    out_ref = _ride_along_casts(rest, 1)[0]
    t = qt_ref.shape[1] // CTX_SEQS
    first = []
    for q in range(CTX_SEQS):
        ts = slice(q * t, (q + 1) * t)
        for h in range(H_MLA):
            sl = slice(h * MLA_SLAB, (h + 1) * MLA_SLAB)
            first.append((qt_ref[sl, ts], k_ref[ts, sl], vt_ref[0, h * V_MLA:(h + 1) * V_MLA, ts]))
    o, _ = _attend(first, None)
    for q in range(CTX_SEQS):
        out_ref[q * t:(q + 1) * t, :] = jnp.concatenate(
            [_mla_finish(o[q * H_MLA + 2 * j], o[q * H_MLA + 2 * j + 1]) for j in range(H_MLA // 2)], axis=1)


def _mla_lat_kernel(qua_ref, qub_ref, qra_ref, qrb_ref, kca_ref, kcb_ref, vcta_ref, vctb_ref, ka_ref, kb_ref,
                    vta_ref, vtb_ref, out_ref):
    first = [(qua_ref[...], kca_ref[...], vcta_ref[0]), (qub_ref[...], kcb_ref[...], vctb_ref[0])]
    rest = [(qra_ref[...], ka_ref, vta_ref), (qrb_ref[...], kb_ref, vtb_ref)]

    def finish(o):
        out_ref[...] = _mla_finish(o[0], o[1])

    _attend_guarded(first, rest, MLA_CHUNKS_PER_ROUND, finish)


def _attention_ctx(streams, lam4, diff_g, lam_init, b, t, cast):
    qd_t, kd, vd_t, qm_t, km, vm_t = streams
    per_chunk = vd_t.shape[2] // t // CTX_SEQS
    t, b = t * CTX_SEQS, b // CTX_SEQS
    sem = _params("arbitrary")
    qt = lambda w: pl.BlockSpec((w, t), lambda i: (0, i))
    keys = lambda w: pl.BlockSpec((t, w), lambda i: (i, 0))
    vals = lambda w: pl.BlockSpec((1, w, t), lambda i: (i // per_chunk, 0, i % per_chunk))
    casts = [_cast_specs((cast[0], [w]), b) for w in cast[1]]
    o_d, w_a = pl.pallas_call(
        functools.partial(_diff_ctx_kernel, lam_init=lam_init),
        grid=(b,),
        in_specs=[qt(DIFF_QK), keys(DIFF_QK), vals(DIFF_V),
                  pl.BlockSpec(lam4.shape, lambda i: (0, 0)),
                  pl.BlockSpec(diff_g.shape, lambda i: (0, 0))] + casts[0][0],
        out_specs=[keys(DIFF_V)] + casts[0][1],
        out_shape=[jax.ShapeDtypeStruct((b * t, DIFF_V), BF16)] + casts[0][2],
        compiler_params=sem, name="diff_attn_ctx",
    )(qd_t, kd, vd_t, lam4, diff_g, *casts[0][3])
    o_m, w_b = pl.pallas_call(
        _mla_ctx_kernel,
        grid=(b,),
        in_specs=[qt(MLA_WIDE), keys(MLA_WIDE), vals(H_MLA * V_MLA)] + casts[1][0],
        out_specs=[keys(H_MLA * V_MLA)] + casts[1][1],
        out_shape=[jax.ShapeDtypeStruct((b * t, H_MLA * V_MLA), BF16)] + casts[1][2],
        compiler_params=sem, name="mla_attn_ctx",
    )(qm_t, km, vm_t, *casts[1][3])
    return o_d, o_m, w_a, w_b


def _attention_lat(streams, cache, lam4, diff_g, lam_init, b, t):
    qdu_t, qdr_t, kdr, vd_t, qmu_t, qmr_t, kmr, vm_t = streams
    kd_c, vd_c, km_c, vmc_t = cache
    past = kd_c.shape[0] // b
    nq = t // Q_TILE
    n_chunks = t // KEY_CHUNK
    sem = _params("arbitrary", "arbitrary", "arbitrary")
    qt = lambda f: pl.BlockSpec((LANES, Q_TILE), lambda i, h, j: (f(h), i * nq + j))
    keys = lambda n, f: pl.BlockSpec((n, LANES), lambda i, h, j: (i, f(h)))
    vals = pl.BlockSpec((n_chunks, LANES, KEY_CHUNK), lambda i, h, j: (i, h, 0))
    out = pl.BlockSpec((Q_TILE, LANES), lambda i, h, j: (i * nq + j, h))
    same, even, odd = (lambda h: h), (lambda h: 2 * h), (lambda h: 2 * h + 1)
    o_d = pl.pallas_call(
        functools.partial(_diff_lat_kernel, lam_init=lam_init),
        grid=(b, H_DIFF, nq),
        in_specs=[qt(same), qt(same), keys(past, same), keys(past, same), keys(t, same), vals,
                  pl.BlockSpec(lam4.shape, lambda i, h, j: (0, 0)),
                  pl.BlockSpec(diff_g.shape, lambda i, h, j: (0, 0))],
        out_specs=out,
        out_shape=jax.ShapeDtypeStruct((b * t, DIFF_V), BF16),
        compiler_params=sem, name="diff_attn_lat",
    )(qdu_t, qdr_t, kd_c, vd_c, kdr, vd_t, lam4, diff_g)
    o_m = pl.pallas_call(
        _mla_lat_kernel,
        grid=(b, H_MLA // 2, nq),
        in_specs=[qt(even), qt(odd), qt(even), qt(odd), keys(past, even), keys(past, odd),
                  pl.BlockSpec((1, V_MLA, past), lambda i, h, j: (i, 2 * h, 0)),
                  pl.BlockSpec((1, V_MLA, past), lambda i, h, j: (i, 2 * h + 1, 0)),
                  keys(t, even), keys(t, odd),
                  pl.BlockSpec((n_chunks, V_MLA, KEY_CHUNK), lambda i, h, j: (i, 2 * h, 0)),
                  pl.BlockSpec((n_chunks, V_MLA, KEY_CHUNK), lambda i, h, j: (i, 2 * h + 1, 0))],
        out_specs=out,
        out_shape=jax.ShapeDtypeStruct((b * t, H_MLA * V_MLA), BF16),
        compiler_params=sem, name="mla_attn_lat",
    )(qmu_t, qmu_t, qmr_t, qmr_t, km_c, km_c, vmc_t, vmc_t, kmr, kmr, vm_t, vm_t)
    return o_d, o_m


def _post_kernel(x_ref, ma_ref, mb_ref, mods_ref, wo_ref, w1_ref, w2_ref, ln_ref, *rest):
    out_ref = _ride_along_casts(rest, 1)[0]
    mods = mods_ref[0, 0]
    g_m, sh_f, sc_f, g_f = mods[2:3], mods[3:4], mods[4:5], mods[5:6]
    ln = ln_ref[...]
    rows = x_ref.shape[0] // POST_GROUPS
    grp = [slice(r * rows, (r + 1) * rows) for r in range(POST_GROUPS)]
    ys = [_dot(jnp.concatenate([ma_ref[g, :], mb_ref[g, :]], axis=1), wo_ref[...]) for g in grp]
    x1s = [_layer_norm(DEEPNORM_ALPHA * x_ref[g, :] + g_m * y, ln[0:1], ln[1:2]) for g, y in zip(grp, ys)]
    fs = []
    for x1 in x1s:
        h = (x1 * (1.0 + sc_f) + sh_f).astype(BF16)
        f = jnp.zeros(x1.shape, F32)
        for c in range(D_FF // FF_CHUNK):
            a = jnp.maximum(_dot(h, w1_ref[:, c * FF_CHUNK:(c + 1) * FF_CHUNK]), 0.0)
            f = f + _dot((a * a).astype(BF16), w2_ref[c * FF_CHUNK:(c + 1) * FF_CHUNK, :])
        fs.append(f)
    for g, x1, f in zip(grp, x1s, fs):
        out_ref[g, :] = _layer_norm(DEEPNORM_ALPHA * x1 + g_f * f, ln[2:3], ln[3:4])


def _post(x2d, mix_a, mix_b, mods, layer, mod_row_fn, wo, w1, w2, ln4, tm, cast=None):
    t = x2d.shape[0]
    tok = lambda n: pl.BlockSpec((tm, n), lambda i: (i, 0))
    c_in, c_out, c_shape, c_args = _cast_specs(cast, t // tm)
    return pl.pallas_call(
        _post_kernel, grid=(t // tm,),
        in_specs=[tok(D_MODEL), tok(mix_a.shape[1]), tok(mix_b.shape[1]),
                  pl.BlockSpec((1, 1, 6, D_MODEL), lambda i: (layer, mod_row_fn(i), 0, 0)),
                  _const_spec(wo.shape), _const_spec(w1.shape), _const_spec(w2.shape),
                  _const_spec(ln4.shape)] + c_in,
        out_specs=[tok(D_MODEL)] + c_out,
        out_shape=[jax.ShapeDtypeStruct((t, D_MODEL), F32)] + c_shape,
        compiler_params=_params("arbitrary"), name=f"post_mlp_l{layer}",
    )(x2d, mix_a, mix_b, mods, wo, w1, w2, ln4, *c_args)


def _conv_in_kernel(x_ref, mods_ref, w_ref, u_ref, pz_ref):
    mods = mods_ref[0, 0]
    sh, sc = mods[0:1], mods[1:2]
    h = (x_ref[...] * (1.0 + sc) + sh).astype(BF16)
    proj = _dot(h, w_ref[...])
    a, gate = proj[:, 0:CONV_CH], proj[:, CONV_CH:2 * CONV_CH]
    u_ref[...] = a * jax.nn.sigmoid(gate)
    pz_ref[...] = proj[:, 2 * CONV_CH:]


def _conv_in(x2d, mods, layer, mod_row_fn, w, tm):
    t = x2d.shape[0]
    tok = lambda n: pl.BlockSpec((tm, n), lambda i: (i, 0))
    return pl.pallas_call(
        _conv_in_kernel, grid=(t // tm,),
        in_specs=[tok(D_MODEL),
                  pl.BlockSpec((1, 1, 6, D_MODEL), lambda i: (layer, mod_row_fn(i), 0, 0)),
                  _const_spec(w.shape)],
        out_specs=[tok(CONV_CH), tok(POOL_CH)],
        out_shape=[jax.ShapeDtypeStruct((t, CONV_CH), F32), jax.ShapeDtypeStruct((t, POOL_CH), F32)],
        compiler_params=_params("arbitrary"), name="conv_in",
    )(x2d, mods, w)


def _conv_pool_kernel(*refs, n_tiles):
    if n_tiles > 1:
        (u_ref, ul_ref, ur_ref, pz_ref, pl_ref, pr_ref, cw_ref, cp_ref, wp_ref, ps_ref, inv_ref,
         uo_ref, do_ref, ubuf, pbuf, ushift, cbuf) = refs
    else:
        u_ref, pz_ref, cw_ref, cp_ref, wp_ref, ps_ref, inv_ref, uo_ref, do_ref, ubuf, pbuf, ushift, cbuf = refs
    tm = u_ref.shape[1]
    j = pl.program_id(1)
    halo_zero = jnp.zeros((CONV_HALO, CONV_CH), F32)
    for buf, mid, sides in ((ubuf, u_ref, (ul_ref, ur_ref) if n_tiles > 1 else None),
                            (pbuf, pz_ref, (pl_ref, pr_ref) if n_tiles > 1 else None)):
        buf[CONV_HALO:CONV_HALO + tm, :] = mid[0]
        if sides is None:
            buf[0:CONV_HALO, :] = halo_zero
            buf[CONV_HALO + tm:, :] = halo_zero
        else:
            buf[0:CONV_HALO, :] = jnp.where(j > 0, sides[0][0], halo_zero)
            buf[CONV_HALO + tm:, :] = jnp.where(j < n_tiles - 1, sides[1][0], halo_zero)

    rows_sh = ushift.shape[1]
    for b in range(SUBLANES):
        ushift[b] = ubuf[b:b + rows_sh, :]
    base = CONV_HALO - CONV_WIDTH // 2
    cp = cp_ref[...]

    def strip(i, carry):
        r0 = pl.multiple_of(i * CONV_STRIP, CONV_STRIP)
        acc = jnp.zeros((CONV_STRIP, CONV_CH), F32)
        for k in range(CONV_WIDTH):
            a, b = divmod(base + k, SUBLANES)
            w = jnp.concatenate([cw_ref[k]] * (CONV_STRIP // SUBLANES), axis=0)
            acc = acc + ushift[b, pl.ds(r0 + a * SUBLANES, CONV_STRIP), :] * w
        cbuf[pl.ds(r0, CONV_STRIP), :] = acc
        return carry

    lax.fori_loop(0, tm // CONV_STRIP, strip, 0)
    z = _layer_norm(cbuf[...] + cp[0:1], cp[1:2], cp[2:3])
    uo_ref[0] = (z * jax.nn.sigmoid(z)).astype(uo_ref.dtype)

    outs = []
    for g, w in enumerate(POOL_WINDOWS):
        cols = slice(g * POOL_GC, (g + 1) * POOL_GC)
        s = jnp.zeros((tm, POOL_GC), F32)
        for d in range(-(w // 2), w // 2):
            s = s + pbuf[CONV_HALO + d:CONV_HALO + d + tm, cols]
        dgrp = s * inv_ref[:, cols] - pz_ref[0][:, cols]
        outs.append(_dot(dgrp.astype(BF16), wp_ref[g]))
    do_ref[0] = (jnp.concatenate(outs, axis=1) * ps_ref[...]).astype(do_ref.dtype)


def _pool_inv_counts(t):
    pos = np.arange(t)
    cols = []
    for w in POOL_WINDOWS:
        cnt = np.minimum(pos + w // 2 - 1, t - 1) - np.maximum(pos - w // 2, 0) + 1
        cols.append(np.repeat((1.0 / cnt)[:, None], POOL_GC, axis=1))
    return jnp.asarray(np.concatenate(cols, axis=1).astype(np.float32))


def _conv_pool(u2d, pz2d, b, t, tm, cw, cp, wp, ps):
    n_tiles = t // tm
    u3, p3 = u2d.reshape(b, t, CONV_CH), pz2d.reshape(b, t, POOL_CH)
    mid = pl.BlockSpec((1, tm, CONV_CH), lambda i, j: (i, j, 0))
    r = tm // CONV_HALO
    left = pl.BlockSpec((1, CONV_HALO, CONV_CH), lambda i, j: (i, jnp.maximum(j * r - 1, 0), 0))
    right = pl.BlockSpec((1, CONV_HALO, CONV_CH),
                         lambda i, j: (i, jnp.minimum((j + 1) * r, t // CONV_HALO - 1), 0))
    const = lambda a: pl.BlockSpec(a.shape, lambda i, j: (0,) * a.ndim)
    if n_tiles > 1:
        in_specs = [mid, left, right, mid, left, right]
        args = (u3, u3, u3, p3, p3, p3)
    else:
        in_specs = [mid, mid]
        args = (u3, p3)
    inv = _pool_inv_counts(t)
    in_specs += [const(cw), const(cp), const(wp), const(ps), pl.BlockSpec((tm, POOL_CH), lambda i, j: (j, 0))]
    uo, do = pl.pallas_call(
        functools.partial(_conv_pool_kernel, n_tiles=n_tiles),
        grid=(b, n_tiles), in_specs=in_specs,
        out_specs=[mid, mid],
        out_shape=[jax.ShapeDtypeStruct((b, t, CONV_CH), BF16), jax.ShapeDtypeStruct((b, t, POOL_CH), BF16)],
        scratch_shapes=[pltpu.VMEM((tm + 2 * CONV_HALO, CONV_CH), F32),
                        pltpu.VMEM((tm + 2 * CONV_HALO, POOL_CH), F32),
                        pltpu.VMEM((SUBLANES, tm + 2 * CONV_HALO - SUBLANES, CONV_CH), F32),
                        pltpu.VMEM((tm, CONV_CH), F32)],
        compiler_params=_params("arbitrary", "arbitrary"),
        name=f"conv_pool_{'lat' if n_tiles > 1 else 'ctx'}",
    )(*args, cw, cp, wp, ps, inv)
    return uo.reshape(b * t, CONV_CH), do.reshape(b * t, POOL_CH)


def _att_in_weights(w_att_in, with_v_tok):
    o = 2 * DIFF_QK + DIFF_V
    q_d, k_d, v_d = w_att_in[:, :DIFF_QK], w_att_in[:, DIFF_QK:2 * DIFF_QK], w_att_in[:, 2 * DIFF_QK:o]
    cq, ckv = w_att_in[:, o:o + Q_LORA], w_att_in[:, o + Q_LORA:o + Q_LORA + KV_LORA]
    kpe = jnp.pad(w_att_in[:, o + Q_LORA + KV_LORA:], _ROPE_LANE_PAD)
    w_t = jnp.concatenate([q_d, v_d, cq, ckv], axis=1).T.astype(BF16)
    w_r = jnp.concatenate([k_d, ckv, kpe] + ([v_d] if with_v_tok else []), axis=1).astype(BF16)
    return w_t, w_r


def _mla_weights(w_uq, w_ukv):
    wq = w_uq.reshape(Q_LORA, H_MLA, QK_NOPE + QK_ROPE)
    wq = jnp.pad(wq, ((0, 0), (0, 0), (0, MLA_SLAB - QK_NOPE - QK_ROPE))).reshape(Q_LORA, MLA_WIDE)
    wkv = w_ukv.reshape(KV_LORA, H_MLA, QK_NOPE + V_MLA)
    wk = jnp.pad(wkv[:, :, :QK_NOPE], ((0, 0), (0, 0), (0, MLA_SLAB - QK_NOPE))).reshape(KV_LORA, MLA_WIDE)
    wv = wkv[:, :, QK_NOPE:].reshape(KV_LORA, H_MLA * V_MLA)
    return wq.T.astype(BF16), wk.astype(BF16), wv.T.astype(BF16)


def kernel(x_prompt, x_sample, cache_diff_k, cache_diff_v, cache_mla_ckv, cache_mla_krope, c, c_ctx,
           w_ada, b_ada, ln_mix_g, ln_mix_b, ln_mlp_g, ln_mlp_b, w_mlp_in, w_mlp_out,
           w_att_in, w_uq, w_ukv, q_norm_g, kv_norm_g, lam_q1, lam_k1, lam_q2, lam_k2, diff_norm_g, w_att_out,
           w_conv_in, conv_w, conv_b, conv_norm_g, conv_norm_b, w_pool, pool_scale, w_conv_out):
    bc, tc, d = x_prompt.shape
    bl, tl, _ = x_sample.shape
    past = cache_diff_k.shape[2]

    cond8 = jnp.zeros((8, d), F32).at[0].set(c_ctx).at[1:1 + bl].set(c)
    mods = _ada_mods(cond8, w_ada, b_ada).reshape(DEPTH, 8, 6, d)
    ctx_row = lambda i: 0
    lat_row = lambda i: 1 + i // (tl // TOKEN_TILE)

    wqt, wk, wvt = _mla_weights(w_uq[0], w_ukv[0])
    norm_w = (q_norm_g[0].reshape(Q_LORA, 1), kv_norm_g[0].reshape(1, KV_LORA), kv_norm_g[0].reshape(KV_LORA, 1))
    att_w = lambda with_v: _att_in_weights(w_att_in[0], with_v) + norm_w + (wqt, wk, wvt)
    lam4 = jnp.stack([lam_q1[0], lam_k1[0], lam_q2[0], lam_k2[0]])
    diff_g = diff_norm_g[0].reshape(1, 2 * DH_DIFF)
    lam_init = 0.8 - 0.6 * math.exp(-0.3 * 0)
    ln4 = [jnp.stack([ln_mix_g[l], ln_mix_b[l], ln_mlp_g[l], ln_mlp_b[l]]) for l in range(DEPTH)]
    mlp_w = [w_mlp_in, w_mlp_out]
    conv_p = jnp.stack([conv_b[0], conv_norm_g[0], conv_norm_b[0]])
    conv_w8 = jnp.broadcast_to(conv_w[0][:, None, :], (CONV_WIDTH, SUBLANES, CONV_CH))
    wp = w_pool[0].astype(BF16)
    ps = pool_scale[0].reshape(1, POOL_CH)

    xp = x_prompt.reshape(bc * tc, d)
    outs = _att_in(xp, mods, ctx_row, att_w(True), None, TOKEN_TILE,
                   cast=(0, mlp_w + [w_att_out, w_conv_in, w_conv_out]))
    kd_f, vd_f, ckv_f, kpe_f = outs[6:10]
    w1_0, w2_0, w_ao, w_ci, w_co = outs[10:]
    o_d, o_m, w1_1, w2_1 = _attention_ctx(outs[:6], lam4, diff_g, lam_init, bc, tc, cast=(1, mlp_w))
    xp, = _post(xp, o_d, o_m, mods, 0, ctx_row, w_ao, w1_0, w2_0, ln4[0], TOKEN_TILE)
    u, pz = _conv_in(xp, mods, 1, ctx_row, w_ci, TOKEN_TILE)
    uo, do = _conv_pool(u, pz, bc, tc, tc, conv_w8, conv_p, wp, ps)
    xp, = _post(xp, uo, do, mods, 1, ctx_row, w_co, w1_1, w2_1, ln4[1], TOKEN_TILE)

    xs = x_sample.reshape(bl * tl, d)
    outs = _att_in(xs, mods, lat_row, att_w(False), _rope_tables(tl), TOKEN_TILE)
    km_c, vmc_t = _cache_kv(cache_mla_ckv[:, 0].reshape(bl * past, KV_LORA),
                            jnp.pad(cache_mla_krope[:, 0].reshape(bl * past, QK_ROPE), _ROPE_LANE_PAD),
                            wk, wvt, bl)
    cache = (cache_diff_k[:, 0].reshape(bl * past, DIFF_QK), cache_diff_v[:, 0].reshape(bl * past, DIFF_V),
             km_c, vmc_t)
    o_d, o_m = _attention_lat(outs, cache, lam4, diff_g, lam_init, bl, tl)
    xs, = _post(xs, o_d, o_m, mods, 0, lat_row, w_ao, w1_0, w2_0, ln4[0], TOKEN_TILE)
    u, pz = _conv_in(xs, mods, 1, lat_row, w_ci, TOKEN_TILE)
    uo, do = _conv_pool(u, pz, bl, tl, CONV_TILE, conv_w8, conv_p, wp, ps)
    xs, = _post(xs, uo, do, mods, 1, lat_row, w_co, w1_1, w2_1, ln4[1], TOKEN_TILE)

    return (xp.reshape(bc, tc, d), xs.reshape(bl, tl, d),
            kd_f.reshape(bc, 1, tc, H_DIFF, 2 * DH_DIFF), vd_f.reshape(bc, 1, tc, H_DIFF, 2 * DH_DIFF),
            ckv_f.reshape(bc, 1, tc, KV_LORA), kpe_f.reshape(bc, 1, tc, QK_ROPE))
```
